```python
import jax, jax.numpy as jnp
from jax import lax
import numpy as np

D_MODEL = 1024
BATCH = 16
SEQ = 2048
DEPTH = 2

N_EVEN = (DEPTH + 1) // 2
N_ODD = DEPTH // 2

DN_HEADS = 4
DN_DK = 128
DN_DV = 128
DN_CHUNK = 64
CONV_WIDTH = 4
FOX_HEADS = 4
FOX_DH = 128
FOX_BLOCK = 128
EVEN_MIX = DN_HEADS * DN_DV + FOX_HEADS * FOX_DH
EVEN_SIZES = (2 * DN_HEADS * DN_DK + DN_HEADS * DN_DV,
              DN_HEADS * DN_DV,
              DN_HEADS,
              DN_HEADS,
              FOX_HEADS * FOX_DH, FOX_HEADS * FOX_DH, FOX_HEADS * FOX_DH,
              FOX_HEADS)
EVEN_IN = sum(EVEN_SIZES)

LRU_WIDTH = 512
LRU_BLOCKS = 4
LRU_C = 8.0
SGU_WIDTH = 512
SGU_GROUPS = 4
SGU_CHUNK = 128
ODD_MIX = LRU_WIDTH + SGU_WIDTH
ODD_SIZES = (LRU_WIDTH, LRU_WIDTH, SGU_WIDTH, SGU_WIDTH)
ODD_IN = sum(ODD_SIZES)

N_EXPERTS = 16
N_EXPERT_GROUPS = 4
EXPERTS_PER_GROUP = N_EXPERTS // N_EXPERT_GROUPS
TOP_K = 2
D_EXPERT = D_MODEL // 4

EPS = 1e-6

kernel_name = "hybrid_deltanet_fox_rglru_sgu_moe"


def split_cols(t, sizes):
    offs = np.cumsum(np.array(sizes))[:-1].tolist()
    return jnp.split(t, offs, axis=-1)


def rmsnorm(x, g):
    xf = x.astype(jnp.float32)
    y = xf * lax.rsqrt(jnp.mean(xf * xf, axis=-1, keepdims=True) + EPS)
    return (y * g.astype(jnp.float32)).astype(x.dtype)


def l2norm(x):
    xf = x.astype(jnp.float32)
    return xf * lax.rsqrt(jnp.sum(xf * xf, axis=-1, keepdims=True) + EPS)


def modulate(h, shift, scale):
    return h * (1 + scale[:, None, :]) + shift[:, None, :]


def causal_dwconv(x, w):
    k_w, ch = w.shape
    return lax.conv_general_dilated(x, w[:, None, :].astype(x.dtype), window_strides=(1,),
                                    padding=[(k_w - 1, 0)],
                                    dimension_numbers=('NWC', 'WIO', 'NWC'),
                                    feature_group_count=ch)


def gated_deltanet(q, k, v, log_decay, beta):
    f32 = jnp.float32
    b_, s_, h_, dk = q.shape
    dv = v.shape[-1]
    c_ = DN_CHUNK
    n_ = s_ // c_
    q = l2norm(q) * (dk ** -0.5)
    k = l2norm(k)
    v = v.astype(f32)

    def chunks(t):
        t = t.reshape((b_, n_, c_, h_) + t.shape[3:])
        return jnp.moveaxis(t, (1, 3), (0, 2))

    qc, kc, vc = chunks(q), chunks(k), chunks(v)
    gc = jnp.cumsum(chunks(log_decay.astype(f32)), axis=-1)
    bc = chunks(beta.astype(f32))
    idx = jnp.arange(c_)
    causal = idx[:, None] >= idx[None, :]
    strict = idx[:, None] > idx[None, :]
    decay = jnp.exp(jnp.where(causal, gc[..., :, None] - gc[..., None, :], -jnp.inf))
    kk = jnp.einsum('nbhcd,nbhsd->nbhcs', kc * bc[..., None], kc) * decay
    eye = jnp.eye(c_, dtype=f32)
    lower = eye + jnp.where(strict, kk, 0.0)
    t_inv = lax.linalg.triangular_solve(lower, jnp.broadcast_to(eye, lower.shape),
                                        left_side=True, lower=True)
    u = jnp.matmul(t_inv, vc * bc[..., None])
    w = jnp.matmul(t_inv, kc * (bc * jnp.exp(gc))[..., None])
    qk = jnp.einsum('nbhcd,nbhsd->nbhcs', qc, kc) * decay
    q_dec = qc * jnp.exp(gc)[..., None]
    k_dec = kc * jnp.exp(gc[..., -1:] - gc)[..., None]
    g_last = jnp.exp(gc[..., -1])

    def step(state, inp):
        u_i, w_i, qk_i, qd_i, kd_i, gl_i = inp
        v_new = u_i - jnp.matmul(w_i, state)
        o_i = jnp.matmul(qd_i, state) + jnp.matmul(qk_i, v_new)
        state = state * gl_i[..., None, None] + jnp.matmul(jnp.swapaxes(kd_i, -1, -2), v_new)
        return state, o_i

    s0 = jnp.zeros((b_, h_, dk, dv), f32)
    _, o = lax.scan(step, s0, (u, w, qk, q_dec, k_dec, g_last))
    return jnp.moveaxis(o, (0, 2), (1, 3)).reshape(b_, s_, h_, dv)


def forgetting_attention(q, k, v, log_f):
    f32 = jnp.float32
    s_ = q.shape[2]
    scale = q.shape[-1] ** -0.5
    cum = jnp.cumsum(log_f, axis=-1)
    outs = []
    for blk in range(s_ // FOX_BLOCK):
        q0 = blk * FOX_BLOCK
        q1 = q0 + FOX_BLOCK
        kb, vb = k[:, :, :q1], v[:, :, :q1]
        logits = (jnp.einsum('bhqd,bhkd->bhqk', q[:, :, q0:q1], kb).astype(f32) * scale
                  + cum[:, :, q0:q1, None] - cum[:, :, None, :q1])
        mask = (q0 + jnp.arange(FOX_BLOCK))[:, None] >= jnp.arange(q1)[None, :]
        p = jax.nn.softmax(jnp.where(mask, logits, -jnp.inf), axis=-1)
        outs.append(jnp.einsum('bhqk,bhkd->bhqd', p.astype(vb.dtype), vb))
    return jnp.concatenate(outs, axis=2)


def rg_lru(x, w_r, b_r, w_i, b_i, lam):
    f32 = jnp.float32
    b_, s_, wd = x.shape
    xh = x.reshape(b_, s_, LRU_BLOCKS, wd // LRU_BLOCKS)
    r = jax.nn.sigmoid(jnp.einsum('bshi,hij->bshj', xh, w_r).reshape(b_, s_, wd) + b_r)
    i = jax.nn.sigmoid(jnp.einsum('bshi,hij->bshj', xh, w_i).reshape(b_, s_, wd) + b_i)
    log_a = -LRU_C * r.astype(f32) * jax.nn.softplus(-lam.astype(f32))
    a = jnp.exp(log_a)
    b = jnp.sqrt(-jnp.expm1(2.0 * log_a)) * (i * x).astype(f32)

    def combine(left, right):
        a1, b1 = left
        a2, b2 = right
        return a1 * a2, a2 * b1 + b2

    _, h = lax.associative_scan(combine, (a, b), axis=1)
    return h.astype(x.dtype)


def chunked_sgu(u, v, g_norm, w_s, b_s):
    b_, s_, wd = u.shape
    n_ = s_ // SGU_CHUNK
    v = rmsnorm(v, g_norm).reshape(b_, n_, SGU_CHUNK, SGU_GROUPS, wd // SGU_GROUPS)
    tri = jnp.tril(jnp.ones((SGU_CHUNK, SGU_CHUNK), dtype=bool))
    w = jnp.where(tri, w_s, 0.0).astype(v.dtype)
    mixed = jnp.einsum('gts,bnsgd->bntgd', w, v) + b_s.T[None, None, :, :, None]
    return u * mixed.reshape(b_, s_, wd)


def even_mixer(h, w_in, conv_w, a_log, dt_bias, onorm_g, f_bias, qnorm_g, knorm_g, w_out):
    f32 = jnp.float32
    b_, s_, _ = h.shape
    proj = h @ w_in
    dn_qkv, dn_z, dn_a, dn_b, fq, fk, fv, ff = split_cols(proj, EVEN_SIZES)
    dn_qkv = jax.nn.silu(causal_dwconv(dn_qkv, conv_w))
    dq, dk, dv = split_cols(dn_qkv, (DN_HEADS * DN_DK, DN_HEADS * DN_DK, DN_HEADS * DN_DV))
    dq = dq.reshape(b_, s_, DN_HEADS, DN_DK)
    dk = dk.reshape(b_, s_, DN_HEADS, DN_DK)
    dv = dv.reshape(b_, s_, DN_HEADS, DN_DV)
    log_decay = -jnp.exp(a_log.astype(f32)) * jax.nn.softplus(dn_a.astype(f32) + dt_bias.astype(f32))
    beta = jax.nn.sigmoid(dn_b.astype(f32))
    o_dn = gated_deltanet(dq, dk, dv, log_decay, beta)
    o_dn = rmsnorm(o_dn, onorm_g) * jax.nn.silu(dn_z.reshape(b_, s_, DN_HEADS, DN_DV).astype(f32))
    o_dn = o_dn.reshape(b_, s_, DN_HEADS * DN_DV).astype(h.dtype)
    fq = rmsnorm(fq.reshape(b_, s_, FOX_HEADS, FOX_DH), qnorm_g).transpose(0, 2, 1, 3)
    fk = rmsnorm(fk.reshape(b_, s_, FOX_HEADS, FOX_DH), knorm_g).transpose(0, 2, 1, 3)
    fv = fv.reshape(b_, s_, FOX_HEADS, FOX_DH).transpose(0, 2, 1, 3)
    log_f = jax.nn.log_sigmoid(ff.astype(f32) + f_bias.astype(f32)).transpose(0, 2, 1)
    o_fox = forgetting_attention(fq, fk, fv, log_f)
    o_fox = o_fox.transpose(0, 2, 1, 3).reshape(b_, s_, FOX_HEADS * FOX_DH).astype(h.dtype)
    return jnp.concatenate([o_dn, o_fox], axis=-1) @ w_out


def odd_mixer(h, w_in, conv_w, conv_b, wr, br, wi, bi, lam, sgu_g, sgu_w, sgu_b, w_out):
    proj = h @ w_in
    lx, lg, su, sv = split_cols(proj, ODD_SIZES)
    lx = causal_dwconv(lx, conv_w) + conv_b
    o_lru = rg_lru(lx, wr, br, wi, bi, lam) * jax.nn.gelu(lg)
    o_sgu = chunked_sgu(jax.nn.gelu(su), jax.nn.gelu(sv), sgu_g, sgu_w, sgu_b)
    return jnp.concatenate([o_lru, o_sgu], axis=-1) @ w_out


def moe(h, router_w, router_b, w_gate, w_up, w_down):
    f32 = jnp.float32
    b_, s_, _ = h.shape
    probs = jax.nn.softmax(jnp.einsum('bsd,de->bse', h, router_w).astype(f32), axis=-1)
    sel = (probs + router_b.astype(f32)).reshape(b_, s_, N_EXPERT_GROUPS, EXPERTS_PER_GROUP)
    group_score = jnp.sum(lax.top_k(sel, TOP_K)[0], axis=-1)
    g_idx = jnp.argmax(group_score, axis=-1)
    in_group = jnp.sum(sel * jax.nn.one_hot(g_idx, N_EXPERT_GROUPS, dtype=f32)[..., None], axis=2)
    _, local = lax.top_k(in_group, TOP_K)
    e_idx = g_idx[..., None] * EXPERTS_PER_GROUP + local
    wts = jnp.take_along_axis(probs, e_idx, axis=-1)
    wts = wts / jnp.sum(wts, axis=-1, keepdims=True)
    gates = jnp.sum(jax.nn.one_hot(e_idx, N_EXPERTS, dtype=f32) * wts[..., None], axis=-2)
    act = (jax.nn.silu(jnp.einsum('bsd,edf->bsef', h, w_gate))
           * jnp.einsum('bsd,edf->bsef', h, w_up)) * gates.astype(h.dtype)[..., None]
    return jnp.einsum('bsef,efd->bsd', act, w_down)


def setup_inputs(seed: int = 0) -> dict:
    key = jax.random.key(seed)
    ks = list(jax.random.split(key, 40))
    f32 = jnp.float32
    d = D_MODEL

    def nrm(shape, scale):
        return jax.random.normal(ks.pop(), shape, f32) * scale

    def gain(shape):
        return 1.0 + nrm(shape, 0.02)

    x = nrm((BATCH, SEQ, d), 1.0)
    c = nrm((BATCH, d), 1.0)
    ada_w = nrm((DEPTH, d, 6 * d), 0.5 * d ** -0.5)
    ada_b = nrm((DEPTH, 6 * d), 0.02)
    norm1_g = gain((DEPTH, d))
    norm2_g = gain((DEPTH, d))
    ev_w_in = nrm((N_EVEN, d, EVEN_IN), d ** -0.5)
    ev_conv_w = nrm((N_EVEN, CONV_WIDTH, EVEN_SIZES[0]), CONV_WIDTH ** -0.5)
    ev_dn_a_log = jnp.log(jax.random.uniform(ks.pop(), (N_EVEN, DN_HEADS), f32, 1.0, 16.0))
    dt = jnp.exp(jax.random.uniform(ks.pop(), (N_EVEN, DN_HEADS), f32,
                                    float(np.log(1e-3)), float(np.log(1e-1))))
    ev_dn_dt_bias = dt + jnp.log(-jnp.expm1(-dt))
    ev_dn_onorm_g = gain((N_EVEN, DN_DV))
    ev_fox_f_bias = 2.0 + nrm((N_EVEN, FOX_HEADS), 0.5)
    ev_fox_qnorm_g = gain((N_EVEN, FOX_DH))
    ev_fox_knorm_g = gain((N_EVEN, FOX_DH))
    ev_w_out = nrm((N_EVEN, EVEN_MIX, d), EVEN_MIX ** -0.5)
    blk = LRU_WIDTH // LRU_BLOCKS
    od_w_in = nrm((N_ODD, d, ODD_IN), d ** -0.5)
    od_conv_w = nrm((N_ODD, CONV_WIDTH, LRU_WIDTH), CONV_WIDTH ** -0.5)
    od_conv_b = nrm((N_ODD, LRU_WIDTH), 0.02)
    od_lru_wr = nrm((N_ODD, LRU_BLOCKS, blk, blk), blk ** -0.5)
    od_lru_br = nrm((N_ODD, LRU_WIDTH), 0.02)
    od_lru_wi = nrm((N_ODD, LRU_BLOCKS, blk, blk), blk ** -0.5)
    od_lru_bi = nrm((N_ODD, LRU_WIDTH), 0.02)
    a0 = jax.random.uniform(ks.pop(), (N_ODD, LRU_WIDTH), f32, 0.9, 0.999) ** (1.0 / LRU_C)
    od_lru_lambda = jnp.log(a0) - jnp.log1p(-a0)
    od_sgu_norm_g = gain((N_ODD, SGU_WIDTH))
    od_sgu_w = nrm((N_ODD, SGU_GROUPS, SGU_CHUNK, SGU_CHUNK), SGU_CHUNK ** -0.5)
    od_sgu_b = 1.0 + nrm((N_ODD, SGU_GROUPS, SGU_CHUNK), 0.02)
    od_w_out = nrm((N_ODD, ODD_MIX, d), ODD_MIX ** -0.5)
    router_w = nrm((d, N_EXPERTS), d ** -0.5)
    router_b = nrm((N_EXPERTS,), 0.01)
    moe_w_gate = nrm((DEPTH, N_EXPERTS, d, D_EXPERT), d ** -0.5)
    moe_w_up = nrm((DEPTH, N_EXPERTS, d, D_EXPERT), d ** -0.5)
    moe_w_down = nrm((DEPTH, N_EXPERTS, D_EXPERT, d), D_EXPERT ** -0.5)
    return {"x": x, "c": c, "ada_w": ada_w, "ada_b": ada_b, "norm1_g": norm1_g, "norm2_g": norm2_g,
            "ev_w_in": ev_w_in, "ev_conv_w": ev_conv_w, "ev_dn_a_log": ev_dn_a_log,
            "ev_dn_dt_bias": ev_dn_dt_bias, "ev_dn_onorm_g": ev_dn_onorm_g, "ev_fox_f_bias": ev_fox_f_bias,
            "ev_fox_qnorm_g": ev_fox_qnorm_g, "ev_fox_knorm_g": ev_fox_knorm_g, "ev_w_out": ev_w_out,
            "od_w_in": od_w_in, "od_conv_w": od_conv_w, "od_conv_b": od_conv_b, "od_lru_wr": od_lru_wr,
            "od_lru_br": od_lru_br, "od_lru_wi": od_lru_wi, "od_lru_bi": od_lru_bi,
            "od_lru_lambda": od_lru_lambda, "od_sgu_norm_g": od_sgu_norm_g, "od_sgu_w": od_sgu_w,
            "od_sgu_b": od_sgu_b, "od_w_out": od_w_out, "router_w": router_w, "router_b": router_b,
            "moe_w_gate": moe_w_gate, "moe_w_up": moe_w_up, "moe_w_down": moe_w_down}


def reference(x, c, ada_w, ada_b, norm1_g, norm2_g,
              ev_w_in, ev_conv_w, ev_dn_a_log, ev_dn_dt_bias, ev_dn_onorm_g, ev_fox_f_bias,
              ev_fox_qnorm_g, ev_fox_knorm_g, ev_w_out,
              od_w_in, od_conv_w, od_conv_b, od_lru_wr, od_lru_br, od_lru_wi, od_lru_bi,
              od_lru_lambda, od_sgu_norm_g, od_sgu_w, od_sgu_b, od_w_out,
              router_w, router_b, moe_w_gate, moe_w_up, moe_w_down):
    c_act = jax.nn.silu(c)
    for layer in range(DEPTH):
        mod = c_act @ ada_w[layer] + ada_b[layer]
        sh1, sc1, gt1, sh2, sc2, gt2 = jnp.split(mod, 6, axis=-1)
        h = modulate(rmsnorm(x, norm1_g[layer]), sh1, sc1)
        i = layer // 2
        if layer % 2 == 0:
            y = even_mixer(h, ev_w_in[i], ev_conv_w[i], ev_dn_a_log[i], ev_dn_dt_bias[i],
                           ev_dn_onorm_g[i], ev_fox_f_bias[i], ev_fox_qnorm_g[i], ev_fox_knorm_g[i],
                           ev_w_out[i])
        else:
            y = odd_mixer(h, od_w_in[i], od_conv_w[i], od_conv_b[i], od_lru_wr[i], od_lru_br[i],
                          od_lru_wi[i], od_lru_bi[i], od_lru_lambda[i], od_sgu_norm_g[i],
                          od_sgu_w[i], od_sgu_b[i], od_w_out[i])
        x = x + gt1[:, None, :] * y
        h = modulate(rmsnorm(x, norm2_g[layer]), sh2, sc2)
        x = x + gt2[:, None, :] * moe(h, router_w, router_b, moe_w_gate[layer], moe_w_up[layer],
                                      moe_w_down[layer])
    return x
```

```python
import functools

import jax
import jax.numpy as jnp
from jax import lax
from jax.experimental import pallas as pl
from jax.experimental.pallas import tpu as pltpu

F32 = jnp.float32
BF16 = jnp.bfloat16
EPS = 1e-6
NEG_INF = float("-inf")

DN_HEADS = 4
DN_DK = 128
DN_CHUNK = 64
CONV_WIDTH = 4
FOX_HEADS = 4
FOX_DH = 128
LRU_BLOCKS = 4
LRU_C = 8.0
SGU_GROUPS = 4
SGU_CHUNK = 128
N_EXPERTS = 16
EXPERTS_PER_GROUP = 4
LANES = 128

VMEM_LIMIT = 48 * 1024 * 1024


def _cparams(sem):
    return pltpu.CompilerParams(dimension_semantics=sem, vmem_limit_bytes=VMEM_LIMIT)


def _dot(a, b):
    return jnp.dot(a, b, preferred_element_type=F32)


def _dot_nt(a, b):
    return lax.dot_general(a, b, (((1,), (1,)), ((), ())), preferred_element_type=F32)


def _dot_tn(a, b):
    return lax.dot_general(a, b, (((0,), (0,)), ((), ())), preferred_element_type=F32)


def _dot_exact(a, b):
    return jnp.dot(a, b, preferred_element_type=F32, precision=lax.Precision.HIGHEST)


def _sigmoid(x):
    return 1.0 / (1.0 + jnp.exp(-x))


def _silu(x):
    return x * _sigmoid(x)


def _softplus(x):
    return jnp.maximum(x, 0.0) + jnp.log(1.0 + jnp.exp(-jnp.abs(x)))


def _gelu_tanh(x):
    c = 0.7978845608028654
    return 0.5 * x * (1.0 + jnp.tanh(c * (x + 0.044715 * (x * x * x))))


def _prenorm(x, g, scale, shift):
    ms = jnp.mean(x * x, axis=-1, keepdims=True)
    return (x * lax.rsqrt(ms + EPS) * g) * (1.0 + scale) + shift


def _shift_rows(x, d, fill=0.0):
    rows = lax.broadcasted_iota(jnp.int32, x.shape, 0)
    return jnp.where(rows >= d, pltpu.roll(x, d, axis=0), fill)


def _causal_conv(x, w_ref):
    k_w = w_ref.shape[0]
    acc = x * w_ref[k_w - 1:k_w, :]
    for d in range(1, k_w):
        acc = acc + _shift_rows(x, d) * w_ref[k_w - 1 - d:k_w - d, :]
    return acc


def _adaln_kernel(c_ref, w_ref, b_ref, o_ref):
    c = c_ref[...]
    ca = _silu(c).astype(BF16)
    o_ref[0] = _dot(ca, w_ref[0].astype(BF16)) + b_ref[0]


def _adaln(c, ada_w, ada_b):
    depth, d, n = ada_w.shape
    b = c.shape[0]
    tn = 1536
    return pl.pallas_call(
        _adaln_kernel,
        grid=(depth, n // tn),
        in_specs=[pl.BlockSpec((b, d), lambda l, j: (0, 0)),
                  pl.BlockSpec((1, d, tn), lambda l, j: (l, 0, j)),
                  pl.BlockSpec((1, 1, tn), lambda l, j: (l, 0, j))],
        out_specs=pl.BlockSpec((1, b, tn), lambda l, j: (l, 0, j)),
        out_shape=jax.ShapeDtypeStruct((depth, b, n), F32),
        compiler_params=_cparams(("arbitrary", "arbitrary")),
        name="adaln",
    )(c, ada_w, ada_b.reshape(depth, 1, n))


def _inproj_kernel(x_ref, g_ref, sc_ref, sh_ref, w_ref, *out_refs, col_splits, acts):
    h = _prenorm(x_ref[0], g_ref[...], sc_ref[0], sh_ref[0]).astype(BF16)
    for o_ref, (c0, c1), act in zip(out_refs, col_splits, acts):
        p = _dot(h, w_ref[:, c0:c1])
        if act == "gelu":
            p = _gelu_tanh(p)
        o_ref[0] = p.astype(o_ref.dtype)


def _inproj(x, g, scale, shift, w, col_splits, out_dtypes, acts, tm):
    b, s, d = x.shape
    n = w.shape[1]
    outs = tuple(jax.ShapeDtypeStruct((b, s, c1 - c0), dt) for (c0, c1), dt in zip(col_splits, out_dtypes))
    row = lambda i, j: (i, 0, 0)
    return pl.pallas_call(
        functools.partial(_inproj_kernel, col_splits=col_splits, acts=acts),
        grid=(b, s // tm),
        in_specs=[pl.BlockSpec((1, tm, d), lambda i, j: (i, j, 0)),
                  pl.BlockSpec((1, d), lambda i, j: (0, 0)),
                  pl.BlockSpec((1, 1, d), row),
                  pl.BlockSpec((1, 1, d), row),
                  pl.BlockSpec((d, n), lambda i, j: (0, 0))],
        out_specs=tuple(pl.BlockSpec((1, tm, c1 - c0), lambda i, j: (i, j, 0)) for (c0, c1) in col_splits),
        out_shape=outs,
        compiler_params=_cparams(("arbitrary", "arbitrary")),
        name="inproj",
    )(x, g.reshape(1, d), scale, shift, w)


def _gates_kernel(sm_ref, smt_ref, pc_ref, pr_ref, col_ref, row_ref, cc_sc, cr_sc):
    blk = sm_ref.shape[1]

    @pl.when(pl.program_id(1) == 0)
    def _():
        cc_sc[...] = jnp.zeros_like(cc_sc)
        cr_sc[...] = jnp.zeros_like(cr_sc)

    ri = lax.broadcasted_iota(jnp.int32, (blk, blk), 0)
    ci = lax.broadcasted_iota(jnp.int32, (blk, blk), 1)
    same_chunk = (ri // DN_CHUNK) == (ci // DN_CHUNK)
    tril = jnp.where(ri >= ci, 1.0, 0.0)
    tril_loc = jnp.where(same_chunk, tril, 0.0)
    triu = jnp.where(ri <= ci, 1.0, 0.0)
    triu_loc = jnp.where(same_chunk, triu, 0.0)

    lane = lax.broadcasted_iota(jnp.int32, (blk, LANES), 1)
    xc = sm_ref[0] + pc_ref[1:2, :]
    dec = pc_ref[0:1, :] * _softplus(jnp.where(lane < 4, xc, -xc))
    beta = _sigmoid(xc)
    cum_loc = _dot_exact(tril_loc, dec)
    cum_glb = _dot_exact(tril, dec) + cc_sc[0:1, :]
    col_ref[0] = jnp.where(lane < 4, cum_loc, jnp.where(lane < 8, beta, cum_glb))
    cc_sc[...] = jnp.broadcast_to(cum_glb[blk - 1:blk, :], cc_sc.shape)

    rowi = lax.broadcasted_iota(jnp.int32, (16, blk), 0)
    xr = smt_ref[0] + pr_ref[:, 1:2]
    decr = pr_ref[:, 0:1] * _softplus(jnp.where(rowi < 4, xr, -xr))
    cumr_loc = _dot_exact(decr, triu_loc)
    cumr_glb = _dot_exact(decr, triu) + cr_sc[:, 0:1]
    row_ref[0] = jnp.where(rowi < 4, cumr_loc, cumr_glb)
    cr_sc[...] = jnp.broadcast_to(cumr_glb[:, blk - 1:blk], cr_sc.shape)


def _gates(small, small_t, pc, pr):
    b, s, _ = small.shape
    return pl.pallas_call(
        _gates_kernel,
        grid=(b, s // LANES),
        in_specs=[pl.BlockSpec((1, LANES, LANES), lambda i, j: (i, j, 0)),
                  pl.BlockSpec((1, 16, LANES), lambda i, j: (i, 0, j)),
                  pl.BlockSpec((2, LANES), lambda i, j: (0, 0)),
                  pl.BlockSpec((16, 2), lambda i, j: (0, 0))],
        out_specs=(pl.BlockSpec((1, LANES, LANES), lambda i, j: (i, j, 0)),
                   pl.BlockSpec((1, 16, LANES), lambda i, j: (i, 0, j))),
        out_shape=(jax.ShapeDtypeStruct((b, s, LANES), F32), jax.ShapeDtypeStruct((b, 16, s), F32)),
        scratch_shapes=[pltpu.VMEM((8, LANES), F32), pltpu.VMEM((16, LANES), F32)],
        compiler_params=_cparams(("arbitrary", "arbitrary")),
        name="gates",
    )(small, small_t, pc, pr)


def _unit_lower_inverse(low):
    c = low.shape[0]
    ri = lax.broadcasted_iota(jnp.int32, (c, c), 0)
    ci = lax.broadcasted_iota(jnp.int32, (c, c), 1)
    eye = (ri == ci).astype(F32)
    d = jnp.where((ri // 16) == (ci // 16), low, 0.0)
    x = eye - d
    p = d
    for _ in range(3):
        p = _dot(p.astype(BF16), p.astype(BF16))
        x = x + _dot(x.astype(BF16), p.astype(BF16))
    for width in (16, 32):
        off = jnp.where(((ri // (2 * width)) == (ci // (2 * width))) & ((ri // width) != (ci // width)), low, 0.0)
        xb = x.astype(BF16)
        x = x - _dot(_dot(xb, off.astype(BF16)).astype(BF16), xb)
    return x


def _deltanet_kernel(dn_ref, z_ref, col_ref, row_ref, cw_ref, og_ref, o_ref, q_sc, k_sc, v_sc, st_sc):
    s = dn_ref.shape[1]
    c = DN_CHUNK
    hd = DN_HEADS * DN_DK
    for j in range(3 * DN_HEADS):
        cols = slice(j * DN_DK, (j + 1) * DN_DK)
        xj = _silu(_causal_conv(dn_ref[0, :, cols].astype(F32), cw_ref.at[:, cols]))
        if j < 2 * DN_HEADS:
            xj = xj * lax.rsqrt(jnp.sum(xj * xj, axis=-1, keepdims=True) + EPS)
        if j < DN_HEADS:
            q_sc[:, cols] = xj * (DN_DK ** -0.5)
        elif j < 2 * DN_HEADS:
            k_sc[:, j * DN_DK - hd:(j + 1) * DN_DK - hd] = xj
        else:
            v_sc[:, j * DN_DK - 2 * hd:(j + 1) * DN_DK - 2 * hd] = xj
    st_sc[...] = jnp.zeros_like(st_sc)

    ri = lax.broadcasted_iota(jnp.int32, (c, c), 0)
    ci = lax.broadcasted_iota(jnp.int32, (c, c), 1)
    causal = ri >= ci
    strict = ri > ci
    og = og_ref[...]

    def chunk(n, carry):
        r0 = pl.multiple_of(n * c, c)
        colg = col_ref[0, pl.ds(r0, c), :]
        rowg = row_ref[0, n]
        for h in range(DN_HEADS):
            cols = slice(h * DN_DK, (h + 1) * DN_DK)
            q = q_sc[pl.ds(r0, c), cols]
            k = k_sc[pl.ds(r0, c), cols]
            v = v_sc[pl.ds(r0, c), cols]
            gc = colg[:, h:h + 1]
            beta = colg[:, 4 + h:5 + h]
            gr = rowg[h:h + 1, :]
            g_last = gc[c - 1:c, :]
            decay = jnp.exp(jnp.where(causal, gc - gr, NEG_INF))
            eg = jnp.exp(gc)
            kb = (k * beta).astype(BF16)
            kbf = k.astype(BF16)
            both = _dot_nt(jnp.concatenate([kb, q.astype(BF16)], axis=0), kbf)
            kk = both[:c] * decay
            qk = both[c:] * decay
            t_inv = _unit_lower_inverse(jnp.where(strict, kk, 0.0))
            rhs = jnp.concatenate([v * beta, k * (beta * eg)], axis=1).astype(BF16)
            uw = _dot(t_inv.astype(BF16), rhs)
            u = uw[:, :DN_DK]
            w = uw[:, DN_DK:]
            state = st_sc[h]
            sb = state.astype(BF16)
            ws = _dot(jnp.concatenate([w.astype(BF16), (q * eg).astype(BF16)], axis=0), sb)
            v_new = u - ws[:c]
            o = ws[c:] + _dot(qk.astype(BF16), v_new.astype(BF16))
            k_dec = k * jnp.exp(g_last - gc)
            st_sc[h] = state * jnp.exp(g_last) + _dot_tn(k_dec.astype(BF16), v_new.astype(BF16))
            on = o * lax.rsqrt(jnp.mean(o * o, axis=-1, keepdims=True) + EPS) * og
            zz = z_ref[0, pl.ds(r0, c), cols].astype(F32)
            o_ref[0, pl.ds(r0, c), cols] = (on * _silu(zz)).astype(o_ref.dtype)
        return carry

    lax.fori_loop(0, s // c, chunk, 0)


def _deltanet(dn, z, col, row, conv_w, onorm_g):
    b, s, w3 = dn.shape
    hd = DN_HEADS * DN_DK
    return pl.pallas_call(
        _deltanet_kernel,
        grid=(b,),
        in_specs=[pl.BlockSpec((1, s, w3), lambda i: (i, 0, 0)),
                  pl.BlockSpec((1, s, hd), lambda i: (i, 0, 0)),
                  pl.BlockSpec((1, s, LANES), lambda i: (i, 0, 0)),
                  pl.BlockSpec((1, s // DN_CHUNK, DN_HEADS, DN_CHUNK), lambda i: (i, 0, 0, 0)),
                  pl.BlockSpec((CONV_WIDTH, w3), lambda i: (0, 0)),
                  pl.BlockSpec((1, DN_DK), lambda i: (0, 0))],
        out_specs=pl.BlockSpec((1, s, hd), lambda i: (i, 0, 0)),
        out_shape=jax.ShapeDtypeStruct((b, s, hd), BF16),
        scratch_shapes=[pltpu.VMEM((s, hd), F32), pltpu.VMEM((s, hd), F32), pltpu.VMEM((s, hd), F32),
                        pltpu.VMEM((DN_HEADS, DN_DK, DN_DK), F32)],
        compiler_params=_cparams(("arbitrary",)),
        name="deltanet",
    )(dn, z, col, row, conv_w, onorm_g.reshape(1, DN_DK))


def _fox_kernel(q_ref, k_ref, v_ref, col_ref, row_ref, qg_ref, kg_ref, o_ref, kn_sc, *, tq, tk):
    h = pl.program_id(1)
    qi = pl.program_id(2)

    @pl.when(qi == 0)
    def _():
        kf = k_ref[0].astype(F32)
        kn_sc[...] = (kf * lax.rsqrt(jnp.mean(kf * kf, axis=-1, keepdims=True) + EPS) * kg_ref[...]).astype(BF16)

    qf = q_ref[0].astype(F32)
    qn = qf * lax.rsqrt(jnp.mean(qf * qf, axis=-1, keepdims=True) + EPS) * qg_ref[...]
    qn = (qn * (FOX_DH ** -0.5)).astype(BF16)
    lane = lax.broadcasted_iota(jnp.int32, (tq, LANES), 1)
    cq = jnp.sum(jnp.where(lane == 8 + h, col_ref[0], 0.0), axis=1, keepdims=True)
    q_pos = qi * tq + lax.broadcasted_iota(jnp.int32, (tq, tk), 0)
    k_off = lax.broadcasted_iota(jnp.int32, (tq, tk), 1)

    def body(j, carry):
        m, l, acc = carry
        k0 = pl.multiple_of(j * tk, tk)
        kb = kn_sc[pl.ds(k0, tk), :]
        vb = v_ref[0, pl.ds(k0, tk), :]
        ck = row_ref[0, 0, :, pl.ds(k0, tk)]
        logits = _dot_nt(qn, kb) + cq - ck
        logits = jnp.where(q_pos >= k0 + k_off, logits, NEG_INF)
        m_new = jnp.maximum(m, jnp.max(logits, axis=1, keepdims=True))
        alpha = jnp.exp(m - m_new)
        p = jnp.exp(logits - m_new)
        l = alpha * l + jnp.sum(p, axis=1, keepdims=True)
        acc = alpha * acc + _dot(p.astype(BF16), vb)
        return m_new, l, acc

    n_kv = (qi * tq + tq + tk - 1) // tk
    m0 = jnp.full((tq, 1), NEG_INF, F32)
    _, l, acc = lax.fori_loop(0, n_kv, body, (m0, jnp.zeros((tq, 1), F32), jnp.zeros((tq, FOX_DH), F32)))
    o_ref[0] = (acc / l).astype(o_ref.dtype)


def _fox(fox, col, row4, qg, kg, tq, tk):
    b, s, _ = fox.shape
    hd = FOX_HEADS * FOX_DH
    return pl.pallas_call(
        functools.partial(_fox_kernel, tq=tq, tk=tk),
        grid=(b, FOX_HEADS, s // tq),
        in_specs=[pl.BlockSpec((1, tq, FOX_DH), lambda i, h, j: (i, j, h)),
                  pl.BlockSpec((1, s, FOX_DH), lambda i, h, j: (i, 0, FOX_HEADS + h)),
                  pl.BlockSpec((1, s, FOX_DH), lambda i, h, j: (i, 0, 2 * FOX_HEADS + h)),
                  pl.BlockSpec((1, tq, LANES), lambda i, h, j: (i, j, 0)),
                  pl.BlockSpec((1, 1, 1, s), lambda i, h, j: (i, h, 0, 0)),
                  pl.BlockSpec((1, FOX_DH), lambda i, h, j: (0, 0)),
                  pl.BlockSpec((1, FOX_DH), lambda i, h, j: (0, 0))],
        out_specs=pl.BlockSpec((1, tq, FOX_DH), lambda i, h, j: (i, j, h)),
        out_shape=jax.ShapeDtypeStruct((b, s, hd), BF16),
        scratch_shapes=[pltpu.VMEM((s, FOX_DH), BF16)],
        compiler_params=_cparams(("arbitrary", "arbitrary", "arbitrary")),
        name="fox",
    )(fox, fox, fox, col, row4, qg.reshape(1, FOX_DH), kg.reshape(1, FOX_DH))


def _lru_kernel(x_ref, gate_ref, cw_ref, cb_ref, wr_ref, br_ref, wi_ref, bi_ref, lam_ref, o_ref):
    s = x_ref.shape[1]
    x = _causal_conv(x_ref[0].astype(F32), cw_ref) + cb_ref[...]
    xb = x.astype(BF16)
    r = _sigmoid(_dot(xb, wr_ref[0].astype(BF16)) + br_ref[...])
    i = _sigmoid(_dot(xb, wi_ref[0].astype(BF16)) + bi_ref[...])
    log_a = (-LRU_C) * r * _softplus(-lam_ref[...])
    a = jnp.exp(log_a)
    bb = jnp.sqrt(1.0 - jnp.exp(2.0 * log_a)) * (i * x)
    d = 1
    while d < s:
        bb = a * _shift_rows(bb, d, 0.0) + bb
        a = a * _shift_rows(a, d, 1.0)
        d *= 2
    o_ref[0] = (bb * gate_ref[0].astype(F32)).astype(o_ref.dtype)


def _lru(lx, lg, conv_w, conv_b, wr, br, wi, bi, lam):
    b, s, wd = lx.shape
    blk = wd // LRU_BLOCKS
    vec = lambda i, j: (0, j)
    return pl.pallas_call(
        _lru_kernel,
        grid=(b, LRU_BLOCKS),
        in_specs=[pl.BlockSpec((1, s, blk), lambda i, j: (i, 0, j)),
                  pl.BlockSpec((1, s, blk), lambda i, j: (i, 0, j)),
                  pl.BlockSpec((CONV_WIDTH, blk), vec),
                  pl.BlockSpec((1, blk), vec),
                  pl.BlockSpec((1, blk, blk), lambda i, j: (j, 0, 0)),
                  pl.BlockSpec((1, blk), vec),
                  pl.BlockSpec((1, blk, blk), lambda i, j: (j, 0, 0)),
                  pl.BlockSpec((1, blk), vec),
                  pl.BlockSpec((1, blk), vec)],
        out_specs=pl.BlockSpec((1, s, blk), lambda i, j: (i, 0, j)),
        out_shape=jax.ShapeDtypeStruct((b, s, wd), BF16),
        compiler_params=_cparams(("arbitrary", "arbitrary")),
        name="rglru",
    )(lx, lg, conv_w, conv_b.reshape(1, wd), wr, br.reshape(1, wd), wi, bi.reshape(1, wd), lam.reshape(1, wd))


def _sgu_kernel(u_ref, v_ref, g_ref, w_ref, bt_ref, o_ref):
    tm = u_ref.shape[1]
    wd = u_ref.shape[2]
    gw = wd // SGU_GROUPS
    c = SGU_CHUNK
    v = v_ref[0].astype(F32)
    vn = (v * lax.rsqrt(jnp.mean(v * v, axis=-1, keepdims=True) + EPS) * g_ref[...]).astype(BF16)
    ri = lax.broadcasted_iota(jnp.int32, (c, c), 0)
    ci = lax.broadcasted_iota(jnp.int32, (c, c), 1)
    for g in range(SGU_GROUPS):
        wg = jnp.where(ri >= ci, w_ref[g], 0.0).astype(BF16)
        bcol = bt_ref[:, g:g + 1]
        for n in range(tm // c):
            rows = slice(n * c, (n + 1) * c)
            cols = slice(g * gw, (g + 1) * gw)
            mixed = _dot(wg, vn[rows, cols]) + bcol
            o_ref[0, rows, cols] = (u_ref[0, rows, cols].astype(F32) * mixed).astype(o_ref.dtype)


def _sgu(su, sv, g_norm, w_s, b_s, tm):
    b, s, wd = su.shape
    return pl.pallas_call(
        _sgu_kernel,
        grid=(b, s // tm),
        in_specs=[pl.BlockSpec((1, tm, wd), lambda i, j: (i, j, 0)),
                  pl.BlockSpec((1, tm, wd), lambda i, j: (i, j, 0)),
                  pl.BlockSpec((1, wd), lambda i, j: (0, 0)),
                  pl.BlockSpec((SGU_GROUPS, SGU_CHUNK, SGU_CHUNK), lambda i, j: (0, 0, 0)),
                  pl.BlockSpec((SGU_CHUNK, SGU_GROUPS), lambda i, j: (0, 0))],
        out_specs=pl.BlockSpec((1, tm, wd), lambda i, j: (i, j, 0)),
        out_shape=jax.ShapeDtypeStruct((b, s, wd), BF16),
        compiler_params=_cparams(("arbitrary", "arbitrary")),
        name="sgu",
    )(su, sv, g_norm.reshape(1, wd), w_s, b_s.T)


def _routing_gates(logits, rb):
    tm = logits.shape[0]
    lane = lax.broadcasted_iota(jnp.int32, (tm, LANES), 1)
    lane_f = lane.astype(F32)
    valid = lane < N_EXPERTS
    lg = jnp.where(valid, logits, NEG_INF)
    ex = jnp.exp(lg - jnp.max(lg, axis=1, keepdims=True))
    probs = ex / jnp.sum(ex, axis=1, keepdims=True)
    sel = jnp.where(valid, probs + rb, NEG_INF)
    grp = lane // EXPERTS_PER_GROUP

    def top2(vals):
        m1 = jnp.max(vals, axis=1, keepdims=True)
        i1 = jnp.min(jnp.where(vals == m1, lane_f, float(LANES)), axis=1, keepdims=True)
        rest = jnp.where(lane_f == i1, NEG_INF, vals)
        m2 = jnp.max(rest, axis=1, keepdims=True)
        i2 = jnp.min(jnp.where(rest == m2, lane_f, float(LANES)), axis=1, keepdims=True)
        return m1, i1, m2, i2

    best = None
    g_idx = None
    for g in range(N_EXPERTS // EXPERTS_PER_GROUP):
        m1, _, m2, _ = top2(jnp.where(grp == g, sel, NEG_INF))
        score = m1 + m2
        if g == 0:
            best, g_idx = score, jnp.zeros((tm, 1), jnp.int32)
        else:
            upd = score > best
            best = jnp.where(upd, score, best)
            g_idx = jnp.where(upd, g, g_idx)
    _, i1, _, i2 = top2(jnp.where(grp == g_idx, sel, NEG_INF))
    hit1 = lane_f == i1
    hit2 = lane_f == i2
    p1 = jnp.sum(jnp.where(hit1, probs, 0.0), axis=1, keepdims=True)
    p2 = jnp.sum(jnp.where(hit2, probs, 0.0), axis=1, keepdims=True)
    den = p1 + p2
    return jnp.where(hit1, p1 / den, 0.0) + jnp.where(hit2, p2 / den, 0.0)


def _outproj_kernel(a_ref, b_ref, x_ref, w_ref, gt_ref, g2_ref, sc_ref, sh_ref, rw_ref, rb_ref,
                    x1_ref, h2_ref, gates_ref):
    half = a_ref.shape[2]
    y = _dot(a_ref[0], w_ref[:half, :]) + _dot(b_ref[0], w_ref[half:, :])
    x1 = x_ref[0] + gt_ref[0] * y
    x1_ref[0] = x1
    h2 = _prenorm(x1, g2_ref[...], sc_ref[0], sh_ref[0]).astype(BF16)
    h2_ref[0] = h2
    gates = _routing_gates(_dot(h2, rw_ref[...]), rb_ref[...])
    gates_ref[0] = gates[:, :N_EXPERTS]


def _outproj(a, bb, x, w, gt, g2, sc2, sh2, rw, rb, tm):
    b, s, d = x.shape
    half = a.shape[2]
    row = lambda i, j: (i, 0, 0)
    tok = lambda i, j: (i, j, 0)
    const = lambda i, j: (0, 0)
    return pl.pallas_call(
        _outproj_kernel,
        grid=(b, s // tm),
        in_specs=[pl.BlockSpec((1, tm, half), tok),
                  pl.BlockSpec((1, tm, half), tok),
                  pl.BlockSpec((1, tm, d), tok),
                  pl.BlockSpec((2 * half, d), const),
                  pl.BlockSpec((1, 1, d), row),
                  pl.BlockSpec((1, d), const),
                  pl.BlockSpec((1, 1, d), row),
                  pl.BlockSpec((1, 1, d), row),
                  pl.BlockSpec((d, LANES), const),
                  pl.BlockSpec((1, LANES), const)],
        out_specs=(pl.BlockSpec((1, tm, d), tok),
                   pl.BlockSpec((1, tm, d), tok),
                   pl.BlockSpec((1, tm, N_EXPERTS), tok)),
        out_shape=(jax.ShapeDtypeStruct((b, s, d), F32),
                   jax.ShapeDtypeStruct((b, s, d), BF16),
                   jax.ShapeDtypeStruct((b, s, N_EXPERTS), F32)),
        compiler_params=_cparams(("arbitrary", "arbitrary")),
        name="outproj_router",
    )(a, bb, x, w, gt, g2.reshape(1, d), sc2, sh2, rw, rb)


def _moe_kernel(h_ref, gates_ref, wg_ref, wu_ref, wd_ref, x_ref, gt_ref, o_ref, acc_ref):
    e = pl.program_id(2)

    @pl.when(e == 0)
    def _():
        acc_ref[...] = jnp.zeros_like(acc_ref)

    h = h_ref[0]
    tm = h.shape[0]
    lane = lax.broadcasted_iota(jnp.int32, (tm, N_EXPERTS), 1)
    gate = jnp.sum(jnp.where(lane == e, gates_ref[0], 0.0), axis=1, keepdims=True)
    act = _silu(_dot(h, wg_ref[0])) * _dot(h, wu_ref[0]) * gate
    acc_ref[...] += _dot(act.astype(BF16), wd_ref[0])

    @pl.when(e == pl.num_programs(2) - 1)
    def _():
        o_ref[0] = x_ref[0] + gt_ref[0] * acc_ref[...]


def _moe(h2, gates, wg, wu, wd, x, gt, tm):
    b, s, d = x.shape
    ne, _, f = wg.shape
    tok = lambda i, j, e: (i, j, 0)
    return pl.pallas_call(
        _moe_kernel,
        grid=(b, s // tm, ne),
        in_specs=[pl.BlockSpec((1, tm, d), tok),
                  pl.BlockSpec((1, tm, N_EXPERTS), tok),
                  pl.BlockSpec((1, d, f), lambda i, j, e: (e, 0, 0)),
                  pl.BlockSpec((1, d, f), lambda i, j, e: (e, 0, 0)),
                  pl.BlockSpec((1, f, d), lambda i, j, e: (e, 0, 0)),
                  pl.BlockSpec((1, tm, d), tok),
                  pl.BlockSpec((1, 1, d), lambda i, j, e: (i, 0, 0))],
        out_specs=pl.BlockSpec((1, tm, d), tok),
        out_shape=jax.ShapeDtypeStruct((b, s, d), F32),
        scratch_shapes=[pltpu.VMEM((tm, d), F32)],
        compiler_params=_cparams(("arbitrary", "arbitrary", "arbitrary")),
        name="moe",
    )(h2, gates, wg, wu, wd, x, gt)


def kernel(x, c, ada_w, ada_b, norm1_g, norm2_g, ev_w_in, ev_conv_w, ev_dn_a_log, ev_dn_dt_bias, ev_dn_onorm_g, ev_fox_f_bias, ev_fox_qnorm_g, ev_fox_knorm_g, ev_w_out, od_w_in, od_conv_w, od_conv_b, od_lru_wr, od_lru_br, od_lru_wi, od_lru_bi, od_lru_lambda, od_sgu_norm_g, od_sgu_w, od_sgu_b, od_w_out, router_w, router_b, moe_w_gate, moe_w_up, moe_w_down):
    b, s, d = x.shape
    depth = ada_w.shape[0]
    tm = min(512, s)
    mod = _adaln(c, ada_w, ada_b).reshape(depth, b, 6, 1, d)
    rw = jnp.pad(router_w, ((0, 0), (0, LANES - N_EXPERTS))).astype(BF16)
    rb = jnp.pad(router_b, (0, LANES - N_EXPERTS)).reshape(1, LANES)

    for layer in range(depth):
        sh1, sc1, gt1, sh2, sc2, gt2 = (mod[layer, :, k] for k in range(6))
        i = layer // 2
        if layer % 2 == 0:
            w = ev_w_in[i]
            nq = 3 * DN_HEADS * DN_DK
            nz = DN_HEADS * DN_DK
            nf = 3 * FOX_HEADS * FOX_DH
            o_a = nq + nz
            o_f = o_a + 2 * DN_HEADS
            o_ff = o_f + nf
            small_w = jnp.concatenate([w[:, o_a:o_f], w[:, o_ff:o_ff + FOX_HEADS]], axis=1)
            small_w = jnp.pad(small_w, ((0, 0), (0, LANES - small_w.shape[1])))
            w_all = jnp.concatenate([w[:, :o_a], w[:, o_f:o_ff], small_w], axis=1).astype(BF16)
            splits = ((0, nq), (nq, o_a), (o_a, o_a + nf), (o_a + nf, o_a + nf + LANES))
            dn, z, fox, small = _inproj(x, norm1_g[layer], sc1, sh1, w_all, splits,
                                        (BF16, BF16, BF16, F32), (None,) * 4, tm)
            small_t = jnp.swapaxes(small[:, :, :16], 1, 2)
            zeros4 = jnp.zeros((4,), F32)
            mul = jnp.concatenate([-jnp.exp(ev_dn_a_log[i]), zeros4, -jnp.ones((4,), F32), zeros4])
            bias = jnp.concatenate([ev_dn_dt_bias[i], zeros4, ev_fox_f_bias[i], zeros4])
            pr = jnp.stack([mul, bias], axis=1)
            pc = jnp.pad(jnp.stack([mul, bias], axis=0), ((0, 0), (0, LANES - 16)))
            col, row = _gates(small, small_t, pc, pr)
            row_dn = row[:, :DN_HEADS].reshape(b, DN_HEADS, s // DN_CHUNK, DN_CHUNK).transpose(0, 2, 1, 3)
            o_dn = _deltanet(dn, z, col, row_dn, ev_conv_w[i], ev_dn_onorm_g[i])
            row4 = row[:, 8:8 + FOX_HEADS].reshape(b, FOX_HEADS, 1, s)
            o_fox = _fox(fox, col, row4, ev_fox_qnorm_g[i], ev_fox_knorm_g[i], min(256, s), min(256, s))
            mix_a, mix_b, w_out = o_dn, o_fox, ev_w_out[i]
        else:
            lw = od_lru_wr.shape[-1] * LRU_BLOCKS
            splits = ((0, lw), (lw, 2 * lw), (2 * lw, 2 * lw + od_sgu_w.shape[-1] * SGU_GROUPS),
                      (2 * lw + od_sgu_w.shape[-1] * SGU_GROUPS, od_w_in.shape[-1]))
            lx, lg, su, sv = _inproj(x, norm1_g[layer], sc1, sh1, od_w_in[i].astype(BF16), splits,
                                     (BF16,) * 4, (None, "gelu", "gelu", "gelu"), tm)
            o_lru = _lru(lx, lg, od_conv_w[i], od_conv_b[i], od_lru_wr[i], od_lru_br[i], od_lru_wi[i],
                         od_lru_bi[i], od_lru_lambda[i])
            o_sgu = _sgu(su, sv, od_sgu_norm_g[i], od_sgu_w[i], od_sgu_b[i], tm)
            mix_a, mix_b, w_out = o_lru, o_sgu, od_w_out[i]
        x1, h2, gates = _outproj(mix_a, mix_b, x, w_out.astype(BF16), gt1, norm2_g[layer], sc2, sh2, rw, rb, tm)
        x = _moe(h2, gates, moe_w_gate[layer].astype(BF16), moe_w_up[layer].astype(BF16),
                 moe_w_down[layer].astype(BF16), x1, gt2, tm)
    return x
```

```python
import functools

import jax
import jax.numpy as jnp
from jax import lax
from jax.experimental import pallas as pl
from jax.experimental.pallas import tpu as pltpu

F32 = jnp.float32
BF16 = jnp.bfloat16
EPS = 1e-6
NEG_INF = float("-inf")

DN_HEADS = 4
DN_DK = 128
DN_CHUNK = 64
CONV_WIDTH = 4
FOX_HEADS = 4
FOX_DH = 128
LRU_BLOCKS = 4
LRU_C = 8.0
SGU_GROUPS = 4
SGU_CHUNK = 128
N_EXPERTS = 16
EXPERTS_PER_GROUP = 4
LANES = 128

VMEM_LIMIT = 48 * 1024 * 1024


def _cparams(sem):
    return pltpu.CompilerParams(dimension_semantics=sem, vmem_limit_bytes=VMEM_LIMIT)


def _dot(a, b):
    return jnp.dot(a, b, preferred_element_type=F32)


def _dot_nt(a, b):
    return lax.dot_general(a, b, (((1,), (1,)), ((), ())), preferred_element_type=F32)


def _dot_tn(a, b):
    return lax.dot_general(a, b, (((0,), (0,)), ((), ())), preferred_element_type=F32)


def _dot_exact(a, b):
    return jnp.dot(a, b, preferred_element_type=F32, precision=lax.Precision.HIGHEST)


def _sigmoid(x):
    return 1.0 / (1.0 + jnp.exp(-x))


def _silu(x):
    return x * _sigmoid(x)


def _softplus(x):
    return jnp.maximum(x, 0.0) + jnp.log(1.0 + jnp.exp(-jnp.abs(x)))


def _gelu_tanh(x):
    c = 0.7978845608028654
    return 0.5 * x * (1.0 + jnp.tanh(c * (x + 0.044715 * (x * x * x))))


def _prenorm(x, g, scale, shift):
    ms = jnp.mean(x * x, axis=-1, keepdims=True)
    return (x * lax.rsqrt(ms + EPS) * g) * (1.0 + scale) + shift


def _shift_rows(x, d, fill=0.0):
    rows = lax.broadcasted_iota(jnp.int32, x.shape, 0)
    return jnp.where(rows >= d, pltpu.roll(x, d, axis=0), fill)


def _causal_conv(x, w_ref):
    k_w = w_ref.shape[0]
    acc = x * w_ref[k_w - 1:k_w, :]
    for d in range(1, k_w):
        acc = acc + _shift_rows(x, d) * w_ref[k_w - 1 - d:k_w - d, :]
    return acc


def _adaln_kernel(c_ref, w_ref, b_ref, o_ref):
    c = c_ref[...]
    ca = _silu(c).astype(BF16)
    o_ref[0] = _dot(ca, w_ref[0].astype(BF16)) + b_ref[0]


def _adaln(c, ada_w, ada_b):
    depth, d, n = ada_w.shape
    b = c.shape[0]
    tn = 1536
    return pl.pallas_call(
        _adaln_kernel,
        grid=(depth, n // tn),
        in_specs=[pl.BlockSpec((b, d), lambda l, j: (0, 0)),
                  pl.BlockSpec((1, d, tn), lambda l, j: (l, 0, j)),
                  pl.BlockSpec((1, 1, tn), lambda l, j: (l, 0, j))],
        out_specs=pl.BlockSpec((1, b, tn), lambda l, j: (l, 0, j)),
        out_shape=jax.ShapeDtypeStruct((depth, b, n), F32),
        compiler_params=_cparams(("arbitrary", "arbitrary")),
        name="adaln",
    )(c, ada_w, ada_b.reshape(depth, 1, n))


def _inproj_kernel(x_ref, g_ref, sc_ref, sh_ref, w_ref, *out_refs, col_splits, acts):
    h = _prenorm(x_ref[0], g_ref[...], sc_ref[0], sh_ref[0]).astype(BF16)
    for o_ref, (c0, c1), act in zip(out_refs, col_splits, acts):
        p = _dot(h, w_ref[:, c0:c1])
        if act == "gelu":
            p = _gelu_tanh(p)
        o_ref[0] = p.astype(o_ref.dtype)


def _inproj(x, g, scale, shift, w, col_splits, out_dtypes, acts, tm):
    b, s, d = x.shape
    n = w.shape[1]
    outs = tuple(jax.ShapeDtypeStruct((b, s, c1 - c0), dt) for (c0, c1), dt in zip(col_splits, out_dtypes))
    row = lambda i, j: (i, 0, 0)
    return pl.pallas_call(
        functools.partial(_inproj_kernel, col_splits=col_splits, acts=acts),
        grid=(b, s // tm),
        in_specs=[pl.BlockSpec((1, tm, d), lambda i, j: (i, j, 0)),
                  pl.BlockSpec((1, d), lambda i, j: (0, 0)),
                  pl.BlockSpec((1, 1, d), row),
                  pl.BlockSpec((1, 1, d), row),
                  pl.BlockSpec((d, n), lambda i, j: (0, 0))],
        out_specs=tuple(pl.BlockSpec((1, tm, c1 - c0), lambda i, j: (i, j, 0)) for (c0, c1) in col_splits),
        out_shape=outs,
        compiler_params=_cparams(("arbitrary", "arbitrary")),
        name="inproj",
    )(x, g.reshape(1, d), scale, shift, w)


def _gates_kernel(sm_ref, smt_ref, pc_ref, pr_ref, col_ref, row_ref, cc_sc, cr_sc):
    blk = sm_ref.shape[1]

    @pl.when(pl.program_id(1) == 0)
    def _():
        cc_sc[...] = jnp.zeros_like(cc_sc)
        cr_sc[...] = jnp.zeros_like(cr_sc)

    ri = lax.broadcasted_iota(jnp.int32, (blk, blk), 0)
    ci = lax.broadcasted_iota(jnp.int32, (blk, blk), 1)
    same_chunk = (ri // DN_CHUNK) == (ci // DN_CHUNK)
    tril = jnp.where(ri >= ci, 1.0, 0.0)
    tril_loc = jnp.where(same_chunk, tril, 0.0)
    triu = jnp.where(ri <= ci, 1.0, 0.0)
    triu_loc = jnp.where(same_chunk, triu, 0.0)

    lane = lax.broadcasted_iota(jnp.int32, (blk, LANES), 1)
    xc = sm_ref[0] + pc_ref[1:2, :]
    dec = pc_ref[0:1, :] * _softplus(jnp.where(lane < 4, xc, -xc))
    beta = _sigmoid(xc)
    cum_loc = _dot_exact(tril_loc, dec)
    cum_glb = _dot_exact(tril, dec) + cc_sc[0:1, :]
    col_ref[0] = jnp.where(lane < 4, cum_loc, jnp.where(lane < 8, beta, cum_glb))
    cc_sc[...] = jnp.broadcast_to(cum_glb[blk - 1:blk, :], cc_sc.shape)

    rowi = lax.broadcasted_iota(jnp.int32, (16, blk), 0)
    xr = smt_ref[0] + pr_ref[:, 1:2]
    decr = pr_ref[:, 0:1] * _softplus(jnp.where(rowi < 4, xr, -xr))
    cumr_loc = _dot_exact(decr, triu_loc)
    cumr_glb = _dot_exact(decr, triu) + cr_sc[:, 0:1]
    row_ref[0] = jnp.where(rowi < 4, cumr_loc, cumr_glb)
    cr_sc[...] = jnp.broadcast_to(cumr_glb[:, blk - 1:blk], cr_sc.shape)


def _gates(small, small_t, pc, pr):
    b, s, _ = small.shape
    return pl.pallas_call(
        _gates_kernel,
        grid=(b, s // LANES),
        in_specs=[pl.BlockSpec((1, LANES, LANES), lambda i, j: (i, j, 0)),
                  pl.BlockSpec((1, 16, LANES), lambda i, j: (i, 0, j)),
                  pl.BlockSpec((2, LANES), lambda i, j: (0, 0)),
                  pl.BlockSpec((16, 2), lambda i, j: (0, 0))],
        out_specs=(pl.BlockSpec((1, LANES, LANES), lambda i, j: (i, j, 0)),
                   pl.BlockSpec((1, 16, LANES), lambda i, j: (i, 0, j))),
        out_shape=(jax.ShapeDtypeStruct((b, s, LANES), F32), jax.ShapeDtypeStruct((b, 16, s), F32)),
        scratch_shapes=[pltpu.VMEM((8, LANES), F32), pltpu.VMEM((16, LANES), F32)],
        compiler_params=_cparams(("arbitrary", "arbitrary")),
        name="gates",
    )(small, small_t, pc, pr)


DN_PACK = 4
DN_GROUP = DN_PACK * DN_CHUNK


def _blockdiag(p):
    c, wide = p.shape
    t = jnp.concatenate([p] * (wide // c), axis=0)
    rb = lax.broadcasted_iota(jnp.int32, (wide, wide), 0) // c
    cb = lax.broadcasted_iota(jnp.int32, (wide, wide), 1) // c
    return jnp.where(rb == cb, t, 0.0).astype(BF16)


def _diag_blocks(m, c):
    wide = m.shape[1]
    cb = lax.broadcasted_iota(jnp.int32, (c, wide), 1) // c
    out = m[:c]
    for j in range(1, wide // c):
        out = jnp.where(cb == j, m[j * c:(j + 1) * c], out)
    return out


def _rows_to_blocks(col, c, wide):
    cb = lax.broadcasted_iota(jnp.int32, (c, wide), 1) // c
    out = jnp.broadcast_to(col[:c], (c, wide))
    for j in range(1, wide // c):
        out = jnp.where(cb == j, col[j * c:(j + 1) * c], out)
    return out


def _packed_unit_lower_inverse(lows):
    c, wide = lows[0].shape
    ri = lax.broadcasted_iota(jnp.int32, (c, wide), 0)
    ci = lax.broadcasted_iota(jnp.int32, (c, wide), 1) % c
    eye = jnp.where(ri == ci, 1.0, 0.0)
    n = range(len(lows))
    ds = [jnp.where((ri // 16) == (ci // 16), lo, 0.0) for lo in lows]
    xs = [eye - d for d in ds]
    ps = ds
    bds = [_blockdiag(p) for p in ps]
    for _ in range(3):
        ps = [_dot(ps[i].astype(BF16), bds[i]) for i in n]
        bds = [_blockdiag(p) for p in ps]
        xs = [xs[i] + _dot(xs[i].astype(BF16), bds[i]) for i in n]
    for width in (16, 32):
        sel = ((ri // (2 * width)) == (ci // (2 * width))) & ((ri // width) != (ci // width))
        offs = [_blockdiag(jnp.where(sel, lo, 0.0)) for lo in lows]
        ts = [_dot(xs[i].astype(BF16), offs[i]) for i in n]
        bdx = [_blockdiag(x) for x in xs]
        xs = [xs[i] - _dot(ts[i].astype(BF16), bdx[i]) for i in n]
    return xs


def _deltanet_kernel(dn_ref, z_ref, col_ref, row_ref, cw_ref, og_ref, o_ref, q_sc, k_sc, v_sc, st_sc):
    s = dn_ref.shape[1]
    c = DN_CHUNK
    gt = DN_GROUP
    dk = DN_DK
    hd = DN_HEADS * dk
    heads = range(DN_HEADS)
    for j in range(3 * DN_HEADS):
        cols = slice(j * dk, (j + 1) * dk)
        xj = _silu(_causal_conv(dn_ref[0, :, cols].astype(F32), cw_ref.at[:, cols]))
        if j < 2 * DN_HEADS:
            xj = xj * lax.rsqrt(jnp.sum(xj * xj, axis=-1, keepdims=True) + EPS)
        if j < DN_HEADS:
            q_sc[:, cols] = xj * (dk ** -0.5)
        elif j < 2 * DN_HEADS:
            k_sc[:, j * dk - hd:(j + 1) * dk - hd] = xj
        else:
            v_sc[:, j * dk - 2 * hd:(j + 1) * dk - 2 * hd] = xj
    st_sc[...] = jnp.zeros_like(st_sc)

    ri = lax.broadcasted_iota(jnp.int32, (c, gt), 0)
    ci = lax.broadcasted_iota(jnp.int32, (c, gt), 1) % c
    og = og_ref[...]

    def group(g, carry):
        r0 = pl.multiple_of(g * gt, gt)
        colg = col_ref[0, pl.ds(r0, gt), :]
        rowg = row_ref[0, g]
        hs = [slice(h * dk, (h + 1) * dk) for h in heads]
        q = [q_sc[pl.ds(r0, gt), hs[h]] for h in heads]
        k = [k_sc[pl.ds(r0, gt), hs[h]] for h in heads]
        v = [v_sc[pl.ds(r0, gt), hs[h]] for h in heads]
        gc = [colg[:, h:h + 1] for h in heads]
        beta = [colg[:, DN_HEADS + h:DN_HEADS + h + 1] for h in heads]
        glast = [jnp.concatenate([jnp.broadcast_to(gc[h][(j + 1) * c - 1:(j + 1) * c], (c, 1))
                                  for j in range(DN_PACK)], axis=0) for h in heads]
        eg = [jnp.exp(gc[h]) for h in heads]
        kbf = [k[h].astype(BF16) for h in heads]
        both = [_dot_nt(jnp.concatenate([(k[h] * beta[h]).astype(BF16), q[h].astype(BF16)], axis=0), kbf[h])
                for h in heads]
        decay = [jnp.exp(jnp.where(ri >= ci, _rows_to_blocks(gc[h], c, gt) - rowg[h:h + 1, :], NEG_INF))
                 for h in heads]
        kk = [_diag_blocks(both[h][:gt], c) * decay[h] for h in heads]
        qk = [_blockdiag(_diag_blocks(both[h][gt:], c) * decay[h]) for h in heads]
        t_inv = _packed_unit_lower_inverse([jnp.where(ri > ci, kk[h], 0.0) for h in heads])
        rhs = [jnp.concatenate([k[h] * (beta[h] * eg[h]), v[h] * beta[h]], axis=1).astype(BF16) for h in heads]
        wu = [_dot(_blockdiag(t_inv[h]), rhs[h]).astype(BF16) for h in heads]
        qwu = [_dot(qk[h], wu[h]) for h in heads]
        qp = [(q[h] * eg[h] - qwu[h][:, :dk]).astype(BF16) for h in heads]
        kdec = [(k[h] * jnp.exp(glast[h] - gc[h])).astype(BF16) for h in heads]
        mb = [[_dot_tn(kdec[h][j * c:(j + 1) * c], wu[h][j * c:(j + 1) * c]) for h in heads]
              for j in range(DN_PACK)]
        outs = [[] for _ in heads]
        for j in range(DN_PACK):
            rows = slice(j * c, (j + 1) * c)
            state = [st_sc[h] for h in heads]
            lhs = [jnp.concatenate([qp[h][rows], mb[j][h][:, :dk].astype(BF16)], axis=0) for h in heads]
            r = [_dot(lhs[h], state[h].astype(BF16)) for h in heads]
            for h in heads:
                gl = jnp.exp(glast[h][j * c:j * c + 1])
                st_sc[h] = state[h] * gl - r[h][c:] + mb[j][h][:, dk:]
                outs[h].append(r[h][:c] + qwu[h][rows, dk:])
        for h in heads:
            o = jnp.concatenate(outs[h], axis=0)
            on = o * lax.rsqrt(jnp.mean(o * o, axis=-1, keepdims=True) + EPS) * og
            zz = z_ref[0, pl.ds(r0, gt), hs[h]].astype(F32)
            o_ref[0, pl.ds(r0, gt), hs[h]] = (on * _silu(zz)).astype(o_ref.dtype)
        return carry

    lax.fori_loop(0, s // gt, group, 0)


def _deltanet(dn, z, col, row, conv_w, onorm_g):
    b, s, w3 = dn.shape
    hd = DN_HEADS * DN_DK
    return pl.pallas_call(
        _deltanet_kernel,
        grid=(b,),
        in_specs=[pl.BlockSpec((1, s, w3), lambda i: (i, 0, 0)),
                  pl.BlockSpec((1, s, hd), lambda i: (i, 0, 0)),
                  pl.BlockSpec((1, s, LANES), lambda i: (i, 0, 0)),
                  pl.BlockSpec((1, s // DN_GROUP, DN_HEADS, DN_GROUP), lambda i: (i, 0, 0, 0)),
                  pl.BlockSpec((CONV_WIDTH, w3), lambda i: (0, 0)),
                  pl.BlockSpec((1, DN_DK), lambda i: (0, 0))],
        out_specs=pl.BlockSpec((1, s, hd), lambda i: (i, 0, 0)),
        out_shape=jax.ShapeDtypeStruct((b, s, hd), BF16),
        scratch_shapes=[pltpu.VMEM((s, hd), F32), pltpu.VMEM((s, hd), F32), pltpu.VMEM((s, hd), F32),
                        pltpu.VMEM((DN_HEADS, DN_DK, DN_DK), F32)],
        compiler_params=_cparams(("arbitrary",)),
        name="deltanet",
    )(dn, z, col, row, conv_w, onorm_g.reshape(1, DN_DK))


def _split3(x):
    hi = x.astype(BF16).astype(F32)
    r = x - hi
    mid = r.astype(BF16).astype(F32)
    return hi, mid, r - mid


def _fox_kernel(q_ref, k_ref, v_ref, colq_ref, colk_ref, qg_ref, kg_ref, o_ref, ka_sc, va_sc, m_sc, acc_sc, *, tq):
    qi = pl.program_id(1)
    s = k_ref.shape[1]
    dh = FOX_DH

    @pl.when(qi == 0)
    def _():
        lane = lax.broadcasted_iota(jnp.int32, (s, LANES), 1)
        ones_col = jnp.where(lane == 0, 1.0, 0.0).astype(BF16)
        for h in range(FOX_HEADS):
            cols = slice(h * dh, (h + 1) * dh)
            kf = k_ref[0, :, cols].astype(F32)
            kn = kf * lax.rsqrt(jnp.mean(kf * kf, axis=-1, keepdims=True) + EPS) * kg_ref[...]
            ka_sc[h, :, :dh] = kn.astype(BF16)
            hi, mid, lo = _split3(colk_ref[0, :, 8 + h:9 + h])
            ext = jnp.where(lane == 0, -hi, jnp.where(lane == 1, -mid, jnp.where(lane == 2, -lo,
                            jnp.where(lane < 6, 1.0, 0.0))))
            ka_sc[h, :, dh:] = ext.astype(BF16)
            va_sc[h, :, :dh] = v_ref[0, :, cols]
            va_sc[h, :, dh:] = ones_col

    lane = lax.broadcasted_iota(jnp.int32, (tq, LANES), 1)
    qa = []
    for h in range(FOX_HEADS):
        cols = slice(h * dh, (h + 1) * dh)
        qf = q_ref[0, :, cols].astype(F32)
        qn = qf * lax.rsqrt(jnp.mean(qf * qf, axis=-1, keepdims=True) + EPS) * qg_ref[...] * (dh ** -0.5)
        hi, mid, lo = _split3(colq_ref[0, :, 8 + h:9 + h])
        ext = jnp.where(lane < 3, 1.0, jnp.where(lane == 3, hi, jnp.where(lane == 4, mid,
                        jnp.where(lane == 5, lo, 0.0))))
        qa.append(jnp.concatenate([qn.astype(BF16), ext.astype(BF16)], axis=1))
    m_sc[...] = jnp.full(m_sc.shape, NEG_INF, F32)
    acc_sc[...] = jnp.zeros_like(acc_sc)
    causal = lax.broadcasted_iota(jnp.int32, (tq, tq), 0) >= lax.broadcasted_iota(jnp.int32, (tq, tq), 1)

    def step(k0, masked):
        heads = range(FOX_HEADS)
        logits = [_dot_nt(qa[h], ka_sc[h, pl.ds(k0, tq), :]) for h in heads]
        if masked:
            logits = [jnp.where(causal, lg, NEG_INF) for lg in logits]
        ps, alphas = [], []
        for h in heads:
            m_old = m_sc[h]
            m_new = jnp.maximum(m_old, jnp.max(logits[h], axis=1, keepdims=True))
            m_sc[h] = m_new
            alphas.append(jnp.exp(m_old - m_new))
            ps.append(jnp.exp(logits[h] - jnp.concatenate([m_new] * (tq // LANES), axis=1)).astype(BF16))
        for h in heads:
            pv = _dot(ps[h], va_sc[h, pl.ds(k0, tq), :])
            acc_sc[h] = acc_sc[h] * jnp.concatenate([alphas[h], alphas[h]], axis=1) + pv

    def body(j, carry):
        step(pl.multiple_of(j * tq, tq), False)
        return carry

    lax.fori_loop(0, qi, body, 0)
    step(pl.multiple_of(qi * tq, tq), True)
    for h in range(FOX_HEADS):
        acc = acc_sc[h]
        o_ref[0, :, h * dh:(h + 1) * dh] = (acc[:, :dh] / acc[:, dh:dh + 1]).astype(o_ref.dtype)


def _fox(fox, col, qg, kg, tq):
    b, s, _ = fox.shape
    hd = FOX_HEADS * FOX_DH
    return pl.pallas_call(
        functools.partial(_fox_kernel, tq=tq),
        grid=(b, s // tq),
        in_specs=[pl.BlockSpec((1, tq, hd), lambda i, j: (i, j, 0)),
                  pl.BlockSpec((1, s, hd), lambda i, j: (i, 0, 1)),
                  pl.BlockSpec((1, s, hd), lambda i, j: (i, 0, 2)),
                  pl.BlockSpec((1, tq, LANES), lambda i, j: (i, j, 0)),
                  pl.BlockSpec((1, s, LANES), lambda i, j: (i, 0, 0)),
                  pl.BlockSpec((1, FOX_DH), lambda i, j: (0, 0)),
                  pl.BlockSpec((1, FOX_DH), lambda i, j: (0, 0))],
        out_specs=pl.BlockSpec((1, tq, hd), lambda i, j: (i, j, 0)),
        out_shape=jax.ShapeDtypeStruct((b, s, hd), BF16),
        scratch_shapes=[pltpu.VMEM((FOX_HEADS, s, 2 * FOX_DH), BF16), pltpu.VMEM((FOX_HEADS, s, 2 * FOX_DH), BF16),
                        pltpu.VMEM((FOX_HEADS, tq, LANES), F32), pltpu.VMEM((FOX_HEADS, tq, 2 * FOX_DH), F32)],
        compiler_params=_cparams(("arbitrary", "arbitrary")),
        name="fox",
    )(fox, fox, fox, col, col, qg.reshape(1, FOX_DH), kg.reshape(1, FOX_DH))


def _lru_kernel(x_ref, gate_ref, cw_ref, cb_ref, wr_ref, br_ref, wi_ref, bi_ref, lam_ref, o_ref):
    s = x_ref.shape[1]
    x = _causal_conv(x_ref[0].astype(F32), cw_ref) + cb_ref[...]
    xb = x.astype(BF16)
    r = _sigmoid(_dot(xb, wr_ref[0].astype(BF16)) + br_ref[...])
    i = _sigmoid(_dot(xb, wi_ref[0].astype(BF16)) + bi_ref[...])
    log_a = (-LRU_C) * r * _softplus(-lam_ref[...])
    a = jnp.exp(log_a)
    bb = jnp.sqrt(1.0 - jnp.exp(2.0 * log_a)) * (i * x)
    d = 1
    while d < s:
        bb = a * _shift_rows(bb, d, 0.0) + bb
        a = a * _shift_rows(a, d, 1.0)
        d *= 2
    o_ref[0] = (bb * gate_ref[0].astype(F32)).astype(o_ref.dtype)


def _lru(lx, lg, conv_w, conv_b, wr, br, wi, bi, lam):
    b, s, wd = lx.shape
    blk = wd // LRU_BLOCKS
    vec = lambda i, j: (0, j)
    return pl.pallas_call(
        _lru_kernel,
        grid=(b, LRU_BLOCKS),
        in_specs=[pl.BlockSpec((1, s, blk), lambda i, j: (i, 0, j)),
                  pl.BlockSpec((1, s, blk), lambda i, j: (i, 0, j)),
                  pl.BlockSpec((CONV_WIDTH, blk), vec),
                  pl.BlockSpec((1, blk), vec),
                  pl.BlockSpec((1, blk, blk), lambda i, j: (j, 0, 0)),
                  pl.BlockSpec((1, blk), vec),
                  pl.BlockSpec((1, blk, blk), lambda i, j: (j, 0, 0)),
                  pl.BlockSpec((1, blk), vec),
                  pl.BlockSpec((1, blk), vec)],
        out_specs=pl.BlockSpec((1, s, blk), lambda i, j: (i, 0, j)),
        out_shape=jax.ShapeDtypeStruct((b, s, wd), BF16),
        compiler_params=_cparams(("arbitrary", "arbitrary")),
        name="rglru",
    )(lx, lg, conv_w, conv_b.reshape(1, wd), wr, br.reshape(1, wd), wi, bi.reshape(1, wd), lam.reshape(1, wd))


def _sgu_kernel(u_ref, v_ref, g_ref, w_ref, bt_ref, o_ref):
    tm = u_ref.shape[1]
    wd = u_ref.shape[2]
    gw = wd // SGU_GROUPS
    c = SGU_CHUNK
    v = v_ref[0].astype(F32)
    vn = (v * lax.rsqrt(jnp.mean(v * v, axis=-1, keepdims=True) + EPS) * g_ref[...]).astype(BF16)
    ri = lax.broadcasted_iota(jnp.int32, (c, c), 0)
    ci = lax.broadcasted_iota(jnp.int32, (c, c), 1)
    for g in range(SGU_GROUPS):
        wg = jnp.where(ri >= ci, w_ref[g], 0.0).astype(BF16)
        bcol = bt_ref[:, g:g + 1]
        for n in range(tm // c):
            rows = slice(n * c, (n + 1) * c)
            cols = slice(g * gw, (g + 1) * gw)
            mixed = _dot(wg, vn[rows, cols]) + bcol
            o_ref[0, rows, cols] = (u_ref[0, rows, cols].astype(F32) * mixed).astype(o_ref.dtype)


def _sgu(su, sv, g_norm, w_s, b_s, tm):
    b, s, wd = su.shape
    return pl.pallas_call(
        _sgu_kernel,
        grid=(b, s // tm),
        in_specs=[pl.BlockSpec((1, tm, wd), lambda i, j: (i, j, 0)),
                  pl.BlockSpec((1, tm, wd), lambda i, j: (i, j, 0)),
                  pl.BlockSpec((1, wd), lambda i, j: (0, 0)),
                  pl.BlockSpec((SGU_GROUPS, SGU_CHUNK, SGU_CHUNK), lambda i, j: (0, 0, 0)),
                  pl.BlockSpec((SGU_CHUNK, SGU_GROUPS), lambda i, j: (0, 0))],
        out_specs=pl.BlockSpec((1, tm, wd), lambda i, j: (i, j, 0)),
        out_shape=jax.ShapeDtypeStruct((b, s, wd), BF16),
        compiler_params=_cparams(("arbitrary", "arbitrary")),
        name="sgu",
    )(su, sv, g_norm.reshape(1, wd), w_s, b_s.T)


def _routing_gates(logits, rb):
    tm = logits.shape[0]
    lane = lax.broadcasted_iota(jnp.int32, (tm, LANES), 1)
    lane_f = lane.astype(F32)
    valid = lane < N_EXPERTS
    lg = jnp.where(valid, logits, NEG_INF)
    ex = jnp.exp(lg - jnp.max(lg, axis=1, keepdims=True))
    probs = ex / jnp.sum(ex, axis=1, keepdims=True)
    sel = jnp.where(valid, probs + rb, NEG_INF)
    grp = lane // EXPERTS_PER_GROUP

    def top2(vals):
        m1 = jnp.max(vals, axis=1, keepdims=True)
        i1 = jnp.min(jnp.where(vals == m1, lane_f, float(LANES)), axis=1, keepdims=True)
        rest = jnp.where(lane_f == i1, NEG_INF, vals)
        m2 = jnp.max(rest, axis=1, keepdims=True)
        i2 = jnp.min(jnp.where(rest == m2, lane_f, float(LANES)), axis=1, keepdims=True)
        return m1, i1, m2, i2

    best = None
    g_idx = None
    for g in range(N_EXPERTS // EXPERTS_PER_GROUP):
        m1, _, m2, _ = top2(jnp.where(grp == g, sel, NEG_INF))
        score = m1 + m2
        if g == 0:
            best, g_idx = score, jnp.zeros((tm, 1), jnp.int32)
        else:
            upd = score > best
            best = jnp.where(upd, score, best)
            g_idx = jnp.where(upd, g, g_idx)
    _, i1, _, i2 = top2(jnp.where(grp == g_idx, sel, NEG_INF))
    hit1 = lane_f == i1
    hit2 = lane_f == i2
    p1 = jnp.sum(jnp.where(hit1, probs, 0.0), axis=1, keepdims=True)
    p2 = jnp.sum(jnp.where(hit2, probs, 0.0), axis=1, keepdims=True)
    den = p1 + p2
    return jnp.where(hit1, p1 / den, 0.0) + jnp.where(hit2, p2 / den, 0.0)


def _outproj_kernel(a_ref, b_ref, x_ref, w_ref, gt_ref, g2_ref, sc_ref, sh_ref, rw_ref, rb_ref,
                    x1_ref, h2_ref, gates_ref):
    half = a_ref.shape[2]
    y = _dot(a_ref[0], w_ref[:half, :]) + _dot(b_ref[0], w_ref[half:, :])
    x1 = x_ref[0] + gt_ref[0] * y
    x1_ref[0] = x1
    h2 = _prenorm(x1, g2_ref[...], sc_ref[0], sh_ref[0]).astype(BF16)
    h2_ref[0] = h2
    gates = _routing_gates(_dot(h2, rw_ref[...]), rb_ref[...])
    gates_ref[0] = gates[:, :N_EXPERTS]


def _outproj(a, bb, x, w, gt, g2, sc2, sh2, rw, rb, tm):
    b, s, d = x.shape
    half = a.shape[2]
    row = lambda i, j: (i, 0, 0)
    tok = lambda i, j: (i, j, 0)
    const = lambda i, j: (0, 0)
    return pl.pallas_call(
        _outproj_kernel,
        grid=(b, s // tm),
        in_specs=[pl.BlockSpec((1, tm, half), tok),
                  pl.BlockSpec((1, tm, half), tok),
                  pl.BlockSpec((1, tm, d), tok),
                  pl.BlockSpec((2 * half, d), const),
                  pl.BlockSpec((1, 1, d), row),
                  pl.BlockSpec((1, d), const),
                  pl.BlockSpec((1, 1, d), row),
                  pl.BlockSpec((1, 1, d), row),
                  pl.BlockSpec((d, LANES), const),
                  pl.BlockSpec((1, LANES), const)],
        out_specs=(pl.BlockSpec((1, tm, d), tok),
                   pl.BlockSpec((1, tm, d), tok),
                   pl.BlockSpec((1, tm, N_EXPERTS), tok)),
        out_shape=(jax.ShapeDtypeStruct((b, s, d), F32),
                   jax.ShapeDtypeStruct((b, s, d), BF16),
                   jax.ShapeDtypeStruct((b, s, N_EXPERTS), F32)),
        compiler_params=_cparams(("arbitrary", "arbitrary")),
        name="outproj_router",
    )(a, bb, x, w, gt, g2.reshape(1, d), sc2, sh2, rw, rb)


def _moe_kernel(h_ref, gates_ref, wg_ref, wu_ref, wd_ref, x_ref, gt_ref, o_ref, acc_ref):
    e = pl.program_id(2)

    @pl.when(e == 0)
    def _():
        acc_ref[...] = jnp.zeros_like(acc_ref)

    h = h_ref[0]
    tm = h.shape[0]
    lane = lax.broadcasted_iota(jnp.int32, (tm, N_EXPERTS), 1)
    gate = jnp.sum(jnp.where(lane == e, gates_ref[0], 0.0), axis=1, keepdims=True)
    act = _silu(_dot(h, wg_ref[0])) * _dot(h, wu_ref[0]) * gate
    acc_ref[...] += _dot(act.astype(BF16), wd_ref[0])

    @pl.when(e == pl.num_programs(2) - 1)
    def _():
        o_ref[0] = x_ref[0] + gt_ref[0] * acc_ref[...]


def _moe(h2, gates, wg, wu, wd, x, gt, tm):
    b, s, d = x.shape
    ne, _, f = wg.shape
    tok = lambda i, j, e: (i, j, 0)
    return pl.pallas_call(
        _moe_kernel,
        grid=(b, s // tm, ne),
        in_specs=[pl.BlockSpec((1, tm, d), tok),
                  pl.BlockSpec((1, tm, N_EXPERTS), tok),
                  pl.BlockSpec((1, d, f), lambda i, j, e: (e, 0, 0)),
                  pl.BlockSpec((1, d, f), lambda i, j, e: (e, 0, 0)),
                  pl.BlockSpec((1, f, d), lambda i, j, e: (e, 0, 0)),
                  pl.BlockSpec((1, tm, d), tok),
                  pl.BlockSpec((1, 1, d), lambda i, j, e: (i, 0, 0))],
        out_specs=pl.BlockSpec((1, tm, d), tok),
        out_shape=jax.ShapeDtypeStruct((b, s, d), F32),
        scratch_shapes=[pltpu.VMEM((tm, d), F32)],
        compiler_params=_cparams(("arbitrary", "arbitrary", "arbitrary")),
        name="moe",
    )(h2, gates, wg, wu, wd, x, gt)


def kernel(x, c, ada_w, ada_b, norm1_g, norm2_g, ev_w_in, ev_conv_w, ev_dn_a_log, ev_dn_dt_bias, ev_dn_onorm_g, ev_fox_f_bias, ev_fox_qnorm_g, ev_fox_knorm_g, ev_w_out, od_w_in, od_conv_w, od_conv_b, od_lru_wr, od_lru_br, od_lru_wi, od_lru_bi, od_lru_lambda, od_sgu_norm_g, od_sgu_w, od_sgu_b, od_w_out, router_w, router_b, moe_w_gate, moe_w_up, moe_w_down):
    b, s, d = x.shape
    depth = ada_w.shape[0]
    tm = min(512, s)
    mod = _adaln(c, ada_w, ada_b).reshape(depth, b, 6, 1, d)
    rw = jnp.pad(router_w, ((0, 0), (0, LANES - N_EXPERTS))).astype(BF16)
    rb = jnp.pad(router_b, (0, LANES - N_EXPERTS)).reshape(1, LANES)

    for layer in range(depth):
        sh1, sc1, gt1, sh2, sc2, gt2 = (mod[layer, :, k] for k in range(6))
        i = layer // 2
        if layer % 2 == 0:
            w = ev_w_in[i]
            nq = 3 * DN_HEADS * DN_DK
            nz = DN_HEADS * DN_DK
            nf = 3 * FOX_HEADS * FOX_DH
            o_a = nq + nz
            o_f = o_a + 2 * DN_HEADS
            o_ff = o_f + nf
            small_w = jnp.concatenate([w[:, o_a:o_f], w[:, o_ff:o_ff + FOX_HEADS]], axis=1)
            small_w = jnp.pad(small_w, ((0, 0), (0, LANES - small_w.shape[1])))
            w_all = jnp.concatenate([w[:, :o_a], w[:, o_f:o_ff], small_w], axis=1).astype(BF16)
            splits = ((0, nq), (nq, o_a), (o_a, o_a + nf), (o_a + nf, o_a + nf + LANES))
            dn, z, fox, small = _inproj(x, norm1_g[layer], sc1, sh1, w_all, splits,
                                        (BF16, BF16, BF16, F32), (None,) * 4, tm)
            small_t = jnp.swapaxes(small[:, :, :16], 1, 2)
            zeros4 = jnp.zeros((4,), F32)
            mul = jnp.concatenate([-jnp.exp(ev_dn_a_log[i]), zeros4, -jnp.ones((4,), F32), zeros4])
            bias = jnp.concatenate([ev_dn_dt_bias[i], zeros4, ev_fox_f_bias[i], zeros4])
            pr = jnp.stack([mul, bias], axis=1)
            pc = jnp.pad(jnp.stack([mul, bias], axis=0), ((0, 0), (0, LANES - 16)))
            col, row = _gates(small, small_t, pc, pr)
            row_dn = row[:, :DN_HEADS].reshape(b, DN_HEADS, s // DN_GROUP, DN_GROUP).transpose(0, 2, 1, 3)
            o_dn = _deltanet(dn, z, col, row_dn, ev_conv_w[i], ev_dn_onorm_g[i])
            o_fox = _fox(fox, col, ev_fox_qnorm_g[i], ev_fox_knorm_g[i], min(512, s))
            mix_a, mix_b, w_out = o_dn, o_fox, ev_w_out[i]
        else:
            lw = od_lru_wr.shape[-1] * LRU_BLOCKS
            splits = ((0, lw), (lw, 2 * lw), (2 * lw, 2 * lw + od_sgu_w.shape[-1] * SGU_GROUPS),
                      (2 * lw + od_sgu_w.shape[-1] * SGU_GROUPS, od_w_in.shape[-1]))
            lx, lg, su, sv = _inproj(x, norm1_g[layer], sc1, sh1, od_w_in[i].astype(BF16), splits,
                                     (BF16,) * 4, (None, "gelu", "gelu", "gelu"), tm)
            o_lru = _lru(lx, lg, od_conv_w[i], od_conv_b[i], od_lru_wr[i], od_lru_br[i], od_lru_wi[i],
                         od_lru_bi[i], od_lru_lambda[i])
            o_sgu = _sgu(su, sv, od_sgu_norm_g[i], od_sgu_w[i], od_sgu_b[i], tm)
            mix_a, mix_b, w_out = o_lru, o_sgu, od_w_out[i]
        x1, h2, gates = _outproj(mix_a, mix_b, x, w_out.astype(BF16), gt1, norm2_g[layer], sc2, sh2, rw, rb, tm)
        x = _moe(h2, gates, moe_w_gate[layer].astype(BF16), moe_w_up[layer].astype(BF16),
                 moe_w_down[layer].astype(BF16), x1, gt2, tm)
    return x
```

```python
import functools

import jax
import jax.numpy as jnp
from jax import lax
from jax.experimental import pallas as pl
from jax.experimental.pallas import tpu as pltpu
from jax.experimental.pallas import tpu_sc as plsc

F32 = jnp.float32
BF16 = jnp.bfloat16
EPS = 1e-6
NEG_INF = float("-inf")

DN_HEADS = 4
DN_DK = 128
DN_CHUNK = 64
CONV_WIDTH = 4
FOX_HEADS = 4
FOX_DH = 128
LRU_BLOCKS = 4
LRU_C = 8.0
SGU_GROUPS = 4
SGU_CHUNK = 128
N_EXPERTS = 16
EXPERTS_PER_GROUP = 4
LANES = 128

VMEM_LIMIT = 48 * 1024 * 1024


def _cparams(sem):
    return pltpu.CompilerParams(dimension_semantics=sem, vmem_limit_bytes=VMEM_LIMIT)


def _dot(a, b):
    return jnp.dot(a, b, preferred_element_type=F32)


def _dot_nt(a, b):
    return lax.dot_general(a, b, (((1,), (1,)), ((), ())), preferred_element_type=F32)


def _dot_tn(a, b):
    return lax.dot_general(a, b, (((0,), (0,)), ((), ())), preferred_element_type=F32)


def _dot_exact(a, b):
    return jnp.dot(a, b, preferred_element_type=F32, precision=lax.Precision.HIGHEST)


def _sigmoid(x):
    return 1.0 / (1.0 + jnp.exp(-x))


def _silu(x):
    return x * _sigmoid(x)


def _softplus(x):
    return jnp.maximum(x, 0.0) + jnp.log(1.0 + jnp.exp(-jnp.abs(x)))


def _gelu_tanh(x):
    c = 0.7978845608028654
    return 0.5 * x * (1.0 + jnp.tanh(c * (x + 0.044715 * (x * x * x))))


def _prenorm(x, g, scale, shift):
    ms = jnp.mean(x * x, axis=-1, keepdims=True)
    return (x * lax.rsqrt(ms + EPS) * g) * (1.0 + scale) + shift


def _shift_rows(x, d, fill=0.0):
    rows = lax.broadcasted_iota(jnp.int32, x.shape, 0)
    return jnp.where(rows >= d, pltpu.roll(x, d, axis=0), fill)


def _causal_conv(x, w_ref):
    k_w = w_ref.shape[0]
    acc = x * w_ref[k_w - 1:k_w, :]
    for d in range(1, k_w):
        acc = acc + _shift_rows(x, d) * w_ref[k_w - 1 - d:k_w - d, :]
    return acc


def _adaln_kernel(c_ref, w_ref, b_ref, o_ref):
    c = c_ref[...]
    ca = _silu(c).astype(BF16)
    o_ref[0] = _dot(ca, w_ref[0].astype(BF16)) + b_ref[0]


def _adaln(c, ada_w, ada_b):
    depth, d, n = ada_w.shape
    b = c.shape[0]
    tn = 1536
    return pl.pallas_call(
        _adaln_kernel,
        grid=(depth, n // tn),
        in_specs=[pl.BlockSpec((b, d), lambda l, j: (0, 0)),
                  pl.BlockSpec((1, d, tn), lambda l, j: (l, 0, j)),
                  pl.BlockSpec((1, 1, tn), lambda l, j: (l, 0, j))],
        out_specs=pl.BlockSpec((1, b, tn), lambda l, j: (l, 0, j)),
        out_shape=jax.ShapeDtypeStruct((depth, b, n), F32),
        compiler_params=_cparams(("arbitrary", "arbitrary")),
        name="adaln",
    )(c, ada_w, ada_b.reshape(depth, 1, n))


def _inproj_kernel(x_ref, g_ref, sc_ref, sh_ref, w_ref, *out_refs, col_splits, acts):
    h = _prenorm(x_ref[0], g_ref[...], sc_ref[0], sh_ref[0]).astype(BF16)
    for o_ref, (c0, c1), act in zip(out_refs, col_splits, acts):
        p = _dot(h, w_ref[:, c0:c1])
        if act == "gelu":
            p = _gelu_tanh(p)
        o_ref[0] = p.astype(o_ref.dtype)


def _inproj(x, g, scale, shift, w, col_splits, out_dtypes, acts, tm):
    b, s, d = x.shape
    n = w.shape[1]
    outs = tuple(jax.ShapeDtypeStruct((b, s, c1 - c0), dt) for (c0, c1), dt in zip(col_splits, out_dtypes))
    row = lambda i, j: (i, 0, 0)
    return pl.pallas_call(
        functools.partial(_inproj_kernel, col_splits=col_splits, acts=acts),
        grid=(b, s // tm),
        in_specs=[pl.BlockSpec((1, tm, d), lambda i, j: (i, j, 0)),
                  pl.BlockSpec((1, d), lambda i, j: (0, 0)),
                  pl.BlockSpec((1, 1, d), row),
                  pl.BlockSpec((1, 1, d), row),
                  pl.BlockSpec((d, n), lambda i, j: (0, 0))],
        out_specs=tuple(pl.BlockSpec((1, tm, c1 - c0), lambda i, j: (i, j, 0)) for (c0, c1) in col_splits),
        out_shape=outs,
        compiler_params=_cparams(("arbitrary", "arbitrary")),
        name="inproj",
    )(x, g.reshape(1, d), scale, shift, w)


def _gates_kernel(sm_ref, smt_ref, pc_ref, pr_ref, col_ref, row_ref, cc_sc, cr_sc):
    blk = sm_ref.shape[1]

    @pl.when(pl.program_id(1) == 0)
    def _():
        cc_sc[...] = jnp.zeros_like(cc_sc)
        cr_sc[...] = jnp.zeros_like(cr_sc)

    ri = lax.broadcasted_iota(jnp.int32, (blk, blk), 0)
    ci = lax.broadcasted_iota(jnp.int32, (blk, blk), 1)
    same_chunk = (ri // DN_CHUNK) == (ci // DN_CHUNK)
    tril = jnp.where(ri >= ci, 1.0, 0.0)
    tril_loc = jnp.where(same_chunk, tril, 0.0)
    triu = jnp.where(ri <= ci, 1.0, 0.0)
    triu_loc = jnp.where(same_chunk, triu, 0.0)

    lane = lax.broadcasted_iota(jnp.int32, (blk, LANES), 1)
    xc = sm_ref[0] + pc_ref[1:2, :]
    dec = pc_ref[0:1, :] * _softplus(jnp.where(lane < 4, xc, -xc))
    beta = _sigmoid(xc)
    cum_loc = _dot_exact(tril_loc, dec)
    cum_glb = _dot_exact(tril, dec) + cc_sc[0:1, :]
    col_ref[0] = jnp.where(lane < 4, cum_loc, jnp.where(lane < 8, beta, cum_glb))
    cc_sc[...] = jnp.broadcast_to(cum_glb[blk - 1:blk, :], cc_sc.shape)

    rowi = lax.broadcasted_iota(jnp.int32, (16, blk), 0)
    xr = smt_ref[0] + pr_ref[:, 1:2]
    decr = pr_ref[:, 0:1] * _softplus(jnp.where(rowi < 4, xr, -xr))
    cumr_loc = _dot_exact(decr, triu_loc)
    cumr_glb = _dot_exact(decr, triu) + cr_sc[:, 0:1]
    row_ref[0] = jnp.where(rowi < 4, cumr_loc, cumr_glb)
    cr_sc[...] = jnp.broadcast_to(cumr_glb[:, blk - 1:blk], cr_sc.shape)


def _gates(small, small_t, pc, pr):
    b, s, _ = small.shape
    return pl.pallas_call(
        _gates_kernel,
        grid=(b, s // LANES),
        in_specs=[pl.BlockSpec((1, LANES, LANES), lambda i, j: (i, j, 0)),
                  pl.BlockSpec((1, 16, LANES), lambda i, j: (i, 0, j)),
                  pl.BlockSpec((2, LANES), lambda i, j: (0, 0)),
                  pl.BlockSpec((16, 2), lambda i, j: (0, 0))],
        out_specs=(pl.BlockSpec((1, LANES, LANES), lambda i, j: (i, j, 0)),
                   pl.BlockSpec((1, 16, LANES), lambda i, j: (i, 0, j))),
        out_shape=(jax.ShapeDtypeStruct((b, s, LANES), F32), jax.ShapeDtypeStruct((b, 16, s), F32)),
        scratch_shapes=[pltpu.VMEM((8, LANES), F32), pltpu.VMEM((16, LANES), F32)],
        compiler_params=_cparams(("arbitrary", "arbitrary")),
        name="gates",
    )(small, small_t, pc, pr)


DN_PACK = 4
DN_GROUP = DN_PACK * DN_CHUNK


def _blockdiag(p):
    c, wide = p.shape
    t = jnp.concatenate([p] * (wide // c), axis=0)
    rb = lax.broadcasted_iota(jnp.int32, (wide, wide), 0) // c
    cb = lax.broadcasted_iota(jnp.int32, (wide, wide), 1) // c
    return jnp.where(rb == cb, t, 0.0).astype(BF16)


def _diag_blocks(m, c):
    wide = m.shape[1]
    cb = lax.broadcasted_iota(jnp.int32, (c, wide), 1) // c
    out = m[:c]
    for j in range(1, wide // c):
        out = jnp.where(cb == j, m[j * c:(j + 1) * c], out)
    return out


def _rows_to_blocks(col, c, wide):
    cb = lax.broadcasted_iota(jnp.int32, (c, wide), 1) // c
    out = jnp.broadcast_to(col[:c], (c, wide))
    for j in range(1, wide // c):
        out = jnp.where(cb == j, col[j * c:(j + 1) * c], out)
    return out


def _packed_unit_lower_inverse(lows):
    c, wide = lows[0].shape
    ri = lax.broadcasted_iota(jnp.int32, (c, wide), 0)
    ci = lax.broadcasted_iota(jnp.int32, (c, wide), 1) % c
    eye = jnp.where(ri == ci, 1.0, 0.0)
    n = range(len(lows))
    ds = [jnp.where((ri // 16) == (ci // 16), lo, 0.0) for lo in lows]
    xs = [eye - d for d in ds]
    ps = ds
    bds = [_blockdiag(p) for p in ps]
    for _ in range(3):
        ps = [_dot(ps[i].astype(BF16), bds[i]) for i in n]
        bds = [_blockdiag(p) for p in ps]
        xs = [xs[i] + _dot(xs[i].astype(BF16), bds[i]) for i in n]
    for width in (16, 32):
        sel = ((ri // (2 * width)) == (ci // (2 * width))) & ((ri // width) != (ci // width))
        offs = [_blockdiag(jnp.where(sel, lo, 0.0)) for lo in lows]
        ts = [_dot(xs[i].astype(BF16), offs[i]) for i in n]
        bdx = [_blockdiag(x) for x in xs]
        xs = [xs[i] - _dot(ts[i].astype(BF16), bdx[i]) for i in n]
    return xs


def _deltanet_kernel(dn_ref, z_ref, col_ref, row_ref, cw_ref, og_ref, o_ref, q_sc, k_sc, v_sc, st_sc):
    s = dn_ref.shape[1]
    c = DN_CHUNK
    gt = DN_GROUP
    dk = DN_DK
    hd = DN_HEADS * dk
    heads = range(DN_HEADS)
    for j in range(3 * DN_HEADS):
        cols = slice(j * dk, (j + 1) * dk)
        xj = _silu(_causal_conv(dn_ref[0, :, cols].astype(F32), cw_ref.at[:, cols]))
        if j < 2 * DN_HEADS:
            xj = xj * lax.rsqrt(jnp.sum(xj * xj, axis=-1, keepdims=True) + EPS)
        if j < DN_HEADS:
            q_sc[:, cols] = xj * (dk ** -0.5)
        elif j < 2 * DN_HEADS:
            k_sc[:, j * dk - hd:(j + 1) * dk - hd] = xj
        else:
            v_sc[:, j * dk - 2 * hd:(j + 1) * dk - 2 * hd] = xj
    st_sc[...] = jnp.zeros_like(st_sc)

    ri = lax.broadcasted_iota(jnp.int32, (c, gt), 0)
    ci = lax.broadcasted_iota(jnp.int32, (c, gt), 1) % c
    og = og_ref[...]

    def group(g, carry):
        r0 = pl.multiple_of(g * gt, gt)
        colg = col_ref[0, pl.ds(r0, gt), :]
        rowg = row_ref[0, g]
        hs = [slice(h * dk, (h + 1) * dk) for h in heads]
        q = [q_sc[pl.ds(r0, gt), hs[h]] for h in heads]
        k = [k_sc[pl.ds(r0, gt), hs[h]] for h in heads]
        v = [v_sc[pl.ds(r0, gt), hs[h]] for h in heads]
        gc = [colg[:, h:h + 1] for h in heads]
        beta = [colg[:, DN_HEADS + h:DN_HEADS + h + 1] for h in heads]
        glast = [jnp.concatenate([jnp.broadcast_to(gc[h][(j + 1) * c - 1:(j + 1) * c], (c, 1))
                                  for j in range(DN_PACK)], axis=0) for h in heads]
        eg = [jnp.exp(gc[h]) for h in heads]
        kbf = [k[h].astype(BF16) for h in heads]
        both = [_dot_nt(jnp.concatenate([(k[h] * beta[h]).astype(BF16), q[h].astype(BF16)], axis=0), kbf[h])
                for h in heads]
        decay = [jnp.exp(jnp.where(ri >= ci, _rows_to_blocks(gc[h], c, gt) - rowg[h:h + 1, :], NEG_INF))
                 for h in heads]
        kk = [_diag_blocks(both[h][:gt], c) * decay[h] for h in heads]
        qk = [_blockdiag(_diag_blocks(both[h][gt:], c) * decay[h]) for h in heads]
        t_inv = _packed_unit_lower_inverse([jnp.where(ri > ci, kk[h], 0.0) for h in heads])
        rhs = [jnp.concatenate([k[h] * (beta[h] * eg[h]), v[h] * beta[h]], axis=1).astype(BF16) for h in heads]
        wu = [_dot(_blockdiag(t_inv[h]), rhs[h]).astype(BF16) for h in heads]
        qwu = [_dot(qk[h], wu[h]) for h in heads]
        qp = [(q[h] * eg[h] - qwu[h][:, :dk]).astype(BF16) for h in heads]
        kdec = [(k[h] * jnp.exp(glast[h] - gc[h])).astype(BF16) for h in heads]
        mb = [[_dot_tn(kdec[h][j * c:(j + 1) * c], wu[h][j * c:(j + 1) * c]) for h in heads]
              for j in range(DN_PACK)]
        outs = [[] for _ in heads]
        for j in range(DN_PACK):
            rows = slice(j * c, (j + 1) * c)
            state = [st_sc[h] for h in heads]
            lhs = [jnp.concatenate([qp[h][rows], mb[j][h][:, :dk].astype(BF16)], axis=0) for h in heads]
            r = [_dot(lhs[h], state[h].astype(BF16)) for h in heads]
            for h in heads:
                gl = jnp.exp(glast[h][j * c:j * c + 1])
                st_sc[h] = state[h] * gl - r[h][c:] + mb[j][h][:, dk:]
                outs[h].append(r[h][:c] + qwu[h][rows, dk:])
        for h in heads:
            o = jnp.concatenate(outs[h], axis=0)
            on = o * lax.rsqrt(jnp.mean(o * o, axis=-1, keepdims=True) + EPS) * og
            zz = z_ref[0, pl.ds(r0, gt), hs[h]].astype(F32)
            o_ref[0, pl.ds(r0, gt), hs[h]] = (on * _silu(zz)).astype(o_ref.dtype)
        return carry

    lax.fori_loop(0, s // gt, group, 0)


def _deltanet(dn, z, col, row, conv_w, onorm_g):
    b, s, w3 = dn.shape
    hd = DN_HEADS * DN_DK
    return pl.pallas_call(
        _deltanet_kernel,
        grid=(b,),
        in_specs=[pl.BlockSpec((1, s, w3), lambda i: (i, 0, 0)),
                  pl.BlockSpec((1, s, hd), lambda i: (i, 0, 0)),
                  pl.BlockSpec((1, s, LANES), lambda i: (i, 0, 0)),
                  pl.BlockSpec((1, s // DN_GROUP, DN_HEADS, DN_GROUP), lambda i: (i, 0, 0, 0)),
                  pl.BlockSpec((CONV_WIDTH, w3), lambda i: (0, 0)),
                  pl.BlockSpec((1, DN_DK), lambda i: (0, 0))],
        out_specs=pl.BlockSpec((1, s, hd), lambda i: (i, 0, 0)),
        out_shape=jax.ShapeDtypeStruct((b, s, hd), BF16),
        scratch_shapes=[pltpu.VMEM((s, hd), F32), pltpu.VMEM((s, hd), F32), pltpu.VMEM((s, hd), F32),
                        pltpu.VMEM((DN_HEADS, DN_DK, DN_DK), F32)],
        compiler_params=_cparams(("arbitrary",)),
        name="deltanet",
    )(dn, z, col, row, conv_w, onorm_g.reshape(1, DN_DK))


def _split3(x):
    hi = x.astype(BF16).astype(F32)
    r = x - hi
    mid = r.astype(BF16).astype(F32)
    return hi, mid, r - mid


def _fox_kernel(q_ref, k_ref, v_ref, colq_ref, colk_ref, qg_ref, kg_ref, o_ref, ka_sc, va_sc, m_sc, acc_sc, *, tq):
    qi = pl.program_id(1)
    s = k_ref.shape[1]
    dh = FOX_DH

    @pl.when(qi == 0)
    def _():
        lane = lax.broadcasted_iota(jnp.int32, (s, LANES), 1)
        ones_col = jnp.where(lane == 0, 1.0, 0.0).astype(BF16)
        for h in range(FOX_HEADS):
            cols = slice(h * dh, (h + 1) * dh)
            kf = k_ref[0, :, cols].astype(F32)
            kn = kf * lax.rsqrt(jnp.mean(kf * kf, axis=-1, keepdims=True) + EPS) * kg_ref[...]
            ka_sc[h, :, :dh] = kn.astype(BF16)
            hi, mid, lo = _split3(colk_ref[0, :, 8 + h:9 + h])
            ext = jnp.where(lane == 0, -hi, jnp.where(lane == 1, -mid, jnp.where(lane == 2, -lo,
                            jnp.where(lane < 6, 1.0, 0.0))))
            ka_sc[h, :, dh:] = ext.astype(BF16)
            va_sc[h, :, :dh] = v_ref[0, :, cols]
            va_sc[h, :, dh:] = ones_col

    lane = lax.broadcasted_iota(jnp.int32, (tq, LANES), 1)
    qa = []
    for h in range(FOX_HEADS):
        cols = slice(h * dh, (h + 1) * dh)
        qf = q_ref[0, :, cols].astype(F32)
        qn = qf * lax.rsqrt(jnp.mean(qf * qf, axis=-1, keepdims=True) + EPS) * qg_ref[...] * (dh ** -0.5)
        hi, mid, lo = _split3(colq_ref[0, :, 8 + h:9 + h])
        ext = jnp.where(lane < 3, 1.0, jnp.where(lane == 3, hi, jnp.where(lane == 4, mid,
                        jnp.where(lane == 5, lo, 0.0))))
        qa.append(jnp.concatenate([qn.astype(BF16), ext.astype(BF16)], axis=1))
    m_sc[...] = jnp.full(m_sc.shape, NEG_INF, F32)
    acc_sc[...] = jnp.zeros_like(acc_sc)
    causal = lax.broadcasted_iota(jnp.int32, (tq, tq), 0) >= lax.broadcasted_iota(jnp.int32, (tq, tq), 1)

    def step(k0, masked):
        heads = range(FOX_HEADS)
        logits = [_dot_nt(qa[h], ka_sc[h, pl.ds(k0, tq), :]) for h in heads]
        if masked:
            logits = [jnp.where(causal, lg, NEG_INF) for lg in logits]
        ps, alphas = [], []
        for h in heads:
            m_old = m_sc[h]
            m_new = jnp.maximum(m_old, jnp.max(logits[h], axis=1, keepdims=True))
            m_sc[h] = m_new
            alphas.append(jnp.exp(m_old - m_new))
            ps.append(jnp.exp(logits[h] - jnp.concatenate([m_new] * (tq // LANES), axis=1)).astype(BF16))
        for h in heads:
            pv = _dot(ps[h], va_sc[h, pl.ds(k0, tq), :])
            acc_sc[h] = acc_sc[h] * jnp.concatenate([alphas[h], alphas[h]], axis=1) + pv

    def body(j, carry):
        step(pl.multiple_of(j * tq, tq), False)
        return carry

    lax.fori_loop(0, qi, body, 0)
    step(pl.multiple_of(qi * tq, tq), True)
    for h in range(FOX_HEADS):
        acc = acc_sc[h]
        o_ref[0, :, h * dh:(h + 1) * dh] = (acc[:, :dh] / acc[:, dh:dh + 1]).astype(o_ref.dtype)


def _fox(fox, col, qg, kg, tq):
    b, s, _ = fox.shape
    hd = FOX_HEADS * FOX_DH
    return pl.pallas_call(
        functools.partial(_fox_kernel, tq=tq),
        grid=(b, s // tq),
        in_specs=[pl.BlockSpec((1, tq, hd), lambda i, j: (i, j, 0)),
                  pl.BlockSpec((1, s, hd), lambda i, j: (i, 0, 1)),
                  pl.BlockSpec((1, s, hd), lambda i, j: (i, 0, 2)),
                  pl.BlockSpec((1, tq, LANES), lambda i, j: (i, j, 0)),
                  pl.BlockSpec((1, s, LANES), lambda i, j: (i, 0, 0)),
                  pl.BlockSpec((1, FOX_DH), lambda i, j: (0, 0)),
                  pl.BlockSpec((1, FOX_DH), lambda i, j: (0, 0))],
        out_specs=pl.BlockSpec((1, tq, hd), lambda i, j: (i, j, 0)),
        out_shape=jax.ShapeDtypeStruct((b, s, hd), BF16),
        scratch_shapes=[pltpu.VMEM((FOX_HEADS, s, 2 * FOX_DH), BF16), pltpu.VMEM((FOX_HEADS, s, 2 * FOX_DH), BF16),
                        pltpu.VMEM((FOX_HEADS, tq, LANES), F32), pltpu.VMEM((FOX_HEADS, tq, 2 * FOX_DH), F32)],
        compiler_params=_cparams(("arbitrary", "arbitrary")),
        name="fox",
    )(fox, fox, fox, col, col, qg.reshape(1, FOX_DH), kg.reshape(1, FOX_DH))


def _lru_kernel(x_ref, gate_ref, cw_ref, cb_ref, wr_ref, br_ref, wi_ref, bi_ref, lam_ref, o_ref):
    s = x_ref.shape[1]
    x = _causal_conv(x_ref[0].astype(F32), cw_ref) + cb_ref[...]
    xb = x.astype(BF16)
    r = _sigmoid(_dot(xb, wr_ref[0].astype(BF16)) + br_ref[...])
    i = _sigmoid(_dot(xb, wi_ref[0].astype(BF16)) + bi_ref[...])
    log_a = (-LRU_C) * r * _softplus(-lam_ref[...])
    a = jnp.exp(log_a)
    bb = jnp.sqrt(1.0 - jnp.exp(2.0 * log_a)) * (i * x)
    d = 1
    while d < s:
        bb = a * _shift_rows(bb, d, 0.0) + bb
        a = a * _shift_rows(a, d, 1.0)
        d *= 2
    o_ref[0] = (bb * gate_ref[0].astype(F32)).astype(o_ref.dtype)


def _lru(lx, lg, conv_w, conv_b, wr, br, wi, bi, lam):
    b, s, wd = lx.shape
    blk = wd // LRU_BLOCKS
    vec = lambda i, j: (0, j)
    return pl.pallas_call(
        _lru_kernel,
        grid=(b, LRU_BLOCKS),
        in_specs=[pl.BlockSpec((1, s, blk), lambda i, j: (i, 0, j)),
                  pl.BlockSpec((1, s, blk), lambda i, j: (i, 0, j)),
                  pl.BlockSpec((CONV_WIDTH, blk), vec),
                  pl.BlockSpec((1, blk), vec),
                  pl.BlockSpec((1, blk, blk), lambda i, j: (j, 0, 0)),
                  pl.BlockSpec((1, blk), vec),
                  pl.BlockSpec((1, blk, blk), lambda i, j: (j, 0, 0)),
                  pl.BlockSpec((1, blk), vec),
                  pl.BlockSpec((1, blk), vec)],
        out_specs=pl.BlockSpec((1, s, blk), lambda i, j: (i, 0, j)),
        out_shape=jax.ShapeDtypeStruct((b, s, wd), BF16),
        compiler_params=_cparams(("arbitrary", "arbitrary")),
        name="rglru",
    )(lx, lg, conv_w, conv_b.reshape(1, wd), wr, br.reshape(1, wd), wi, bi.reshape(1, wd), lam.reshape(1, wd))


def _sgu_kernel(u_ref, v_ref, g_ref, w_ref, bt_ref, o_ref):
    tm = u_ref.shape[1]
    wd = u_ref.shape[2]
    gw = wd // SGU_GROUPS
    c = SGU_CHUNK
    v = v_ref[0].astype(F32)
    vn = (v * lax.rsqrt(jnp.mean(v * v, axis=-1, keepdims=True) + EPS) * g_ref[...]).astype(BF16)
    ri = lax.broadcasted_iota(jnp.int32, (c, c), 0)
    ci = lax.broadcasted_iota(jnp.int32, (c, c), 1)
    for g in range(SGU_GROUPS):
        wg = jnp.where(ri >= ci, w_ref[g], 0.0).astype(BF16)
        bcol = bt_ref[:, g:g + 1]
        for n in range(tm // c):
            rows = slice(n * c, (n + 1) * c)
            cols = slice(g * gw, (g + 1) * gw)
            mixed = _dot(wg, vn[rows, cols]) + bcol
            o_ref[0, rows, cols] = (u_ref[0, rows, cols].astype(F32) * mixed).astype(o_ref.dtype)


def _sgu(su, sv, g_norm, w_s, b_s, tm):
    b, s, wd = su.shape
    return pl.pallas_call(
        _sgu_kernel,
        grid=(b, s // tm),
        in_specs=[pl.BlockSpec((1, tm, wd), lambda i, j: (i, j, 0)),
                  pl.BlockSpec((1, tm, wd), lambda i, j: (i, j, 0)),
                  pl.BlockSpec((1, wd), lambda i, j: (0, 0)),
                  pl.BlockSpec((SGU_GROUPS, SGU_CHUNK, SGU_CHUNK), lambda i, j: (0, 0, 0)),
                  pl.BlockSpec((SGU_CHUNK, SGU_GROUPS), lambda i, j: (0, 0))],
        out_specs=pl.BlockSpec((1, tm, wd), lambda i, j: (i, j, 0)),
        out_shape=jax.ShapeDtypeStruct((b, s, wd), BF16),
        compiler_params=_cparams(("arbitrary", "arbitrary")),
        name="sgu",
    )(su, sv, g_norm.reshape(1, wd), w_s, b_s.T)


def _routing(logits, rb):
    tm = logits.shape[0]
    lane = lax.broadcasted_iota(jnp.int32, (tm, LANES), 1)
    lane_f = lane.astype(F32)
    valid = lane < N_EXPERTS
    lg = jnp.where(valid, logits, NEG_INF)
    ex = jnp.exp(lg - jnp.max(lg, axis=1, keepdims=True))
    probs = ex / jnp.sum(ex, axis=1, keepdims=True)
    sel = jnp.where(valid, probs + rb, NEG_INF)
    grp = lane // EXPERTS_PER_GROUP

    def top2(vals):
        m1 = jnp.max(vals, axis=1, keepdims=True)
        i1 = jnp.min(jnp.where(vals == m1, lane_f, float(LANES)), axis=1, keepdims=True)
        rest = jnp.where(lane_f == i1, NEG_INF, vals)
        m2 = jnp.max(rest, axis=1, keepdims=True)
        i2 = jnp.min(jnp.where(rest == m2, lane_f, float(LANES)), axis=1, keepdims=True)
        return m1, i1, m2, i2

    best = None
    g_idx = None
    for g in range(N_EXPERTS // EXPERTS_PER_GROUP):
        m1, _, m2, _ = top2(jnp.where(grp == g, sel, NEG_INF))
        score = m1 + m2
        if g == 0:
            best, g_idx = score, jnp.zeros((tm, 1), jnp.int32)
        else:
            upd = score > best
            best = jnp.where(upd, score, best)
            g_idx = jnp.where(upd, g, g_idx)
    _, i1, _, i2 = top2(jnp.where(grp == g_idx, sel, NEG_INF))
    p1 = jnp.sum(jnp.where(lane_f == i1, probs, 0.0), axis=1, keepdims=True)
    p2 = jnp.sum(jnp.where(lane_f == i2, probs, 0.0), axis=1, keepdims=True)
    den = p1 + p2
    first_lo = i1 < i2
    a = jnp.where(first_lo, i1, i2) - EXPERTS_PER_GROUP * g_idx.astype(F32)
    bhi = jnp.where(first_lo, i2, i1) - EXPERTS_PER_GROUP * g_idx.astype(F32)
    pair = a * (7.0 - a) * 0.5 + (bhi - a - 1.0)
    cls = PAIRS_PER_GROUP * g_idx.astype(F32) + pair
    w_lo = jnp.where(first_lo, p1, p2) / den
    w_hi = jnp.where(first_lo, p2, p1) / den
    return cls, w_lo, w_hi


PAIRS_PER_GROUP = EXPERTS_PER_GROUP * (EXPERTS_PER_GROUP - 1) // 2
N_CLASSES = (N_EXPERTS // EXPERTS_PER_GROUP) * PAIRS_PER_GROUP
_PAIRS = [(a, b) for a in range(EXPERTS_PER_GROUP) for b in range(a + 1, EXPERTS_PER_GROUP)]
CLASS_E_LO = [EXPERTS_PER_GROUP * (c // PAIRS_PER_GROUP) + _PAIRS[c % PAIRS_PER_GROUP][0] for c in range(N_CLASSES)]
CLASS_E_HI = [EXPERTS_PER_GROUP * (c // PAIRS_PER_GROUP) + _PAIRS[c % PAIRS_PER_GROUP][1] for c in range(N_CLASSES)]
ROW_WORDS = 640


def _pack_halves(x):
    kk = x.shape[1] // 2
    lo = lax.bitcast_convert_type(x[:, :kk].astype(BF16).astype(F32), jnp.uint32)
    hi = lax.bitcast_convert_type(x[:, kk:].astype(BF16).astype(F32), jnp.uint32)
    return lax.bitcast_convert_type((lo >> 16) | hi, jnp.int32)


def _unpack_halves(p):
    u = lax.bitcast_convert_type(p, jnp.uint32)
    lo = lax.bitcast_convert_type(u << 16, F32)
    hi = lax.bitcast_convert_type(u & jnp.uint32(0xFFFF0000), F32)
    return jnp.concatenate([lo.astype(BF16), hi.astype(BF16)], axis=1)


def _outproj_kernel(a_ref, b_ref, x_ref, w_ref, gt_ref, g2_ref, sc_ref, sh_ref, rw_ref, rb_ref,
                    x1_ref, rows_ref, route_ref, cnt_ref, carry_sc):
    first = (pl.program_id(0) == 0) & (pl.program_id(1) == 0)

    @pl.when(first)
    def _():
        carry_sc[...] = jnp.zeros_like(carry_sc)

    half = a_ref.shape[2]
    tm = a_ref.shape[1]
    y = _dot(a_ref[0], w_ref[:half, :]) + _dot(b_ref[0], w_ref[half:, :])
    x1 = x_ref[0] + gt_ref[0] * y
    x1_ref[0] = x1
    h2 = _prenorm(x1, g2_ref[...], sc_ref[0], sh_ref[0])
    cls, w_lo, w_hi = _routing(_dot(h2.astype(BF16), rw_ref[...]), rb_ref[...])
    lane = lax.broadcasted_iota(jnp.int32, (tm, LANES), 1)
    lane_f = lane.astype(F32)
    wts = jnp.where(lane == 0, w_lo, jnp.where(lane == 1, w_hi, 0.0))
    rows_ref[0, :, :ROW_WORDS - LANES] = _pack_halves(h2)
    rows_ref[0, :, ROW_WORDS - LANES:] = lax.bitcast_convert_type(wts, jnp.int32)
    onehot = jnp.where(lane_f == cls, 1.0, 0.0)
    before = lax.broadcasted_iota(jnp.int32, (tm, tm), 0) > lax.broadcasted_iota(jnp.int32, (tm, tm), 1)
    prefix = _dot(jnp.where(before, 1.0, 0.0).astype(BF16), onehot.astype(BF16)) + carry_sc[0:1, :]
    rank = jnp.sum(onehot * prefix, axis=1, keepdims=True)
    route_ref[0] = jnp.where(lane == 0, cls, jnp.where(lane == 1, rank, 0.0))
    carry_sc[...] = carry_sc[...] + jnp.sum(onehot, axis=0, keepdims=True)
    cnt_ref[...] = carry_sc[...]


def _outproj(a, bb, x, w, gt, g2, sc2, sh2, rw, rb, tm):
    b, s, d = x.shape
    half = a.shape[2]
    row = lambda i, j: (i, 0, 0)
    tok = lambda i, j: (i, j, 0)
    const = lambda i, j: (0, 0)
    return pl.pallas_call(
        _outproj_kernel,
        grid=(b, s // tm),
        in_specs=[pl.BlockSpec((1, tm, half), tok),
                  pl.BlockSpec((1, tm, half), tok),
                  pl.BlockSpec((1, tm, d), tok),
                  pl.BlockSpec((2 * half, d), const),
                  pl.BlockSpec((1, 1, d), row),
                  pl.BlockSpec((1, d), const),
                  pl.BlockSpec((1, 1, d), row),
                  pl.BlockSpec((1, 1, d), row),
                  pl.BlockSpec((d, LANES), const),
                  pl.BlockSpec((1, LANES), const)],
        out_specs=(pl.BlockSpec((1, tm, d), tok),
                   pl.BlockSpec((1, tm, ROW_WORDS), tok),
                   pl.BlockSpec((1, tm, LANES), tok),
                   pl.BlockSpec((8, LANES), const)),
        out_shape=(jax.ShapeDtypeStruct((b, s, d), F32),
                   jax.ShapeDtypeStruct((b, s, ROW_WORDS), jnp.int32),
                   jax.ShapeDtypeStruct((b, s, LANES), F32),
                   jax.ShapeDtypeStruct((8, LANES), F32)),
        scratch_shapes=[pltpu.VMEM((8, LANES), F32)],
        compiler_params=_cparams(("arbitrary", "arbitrary")),
        name="outproj_router",
    )(a, bb, x, w, gt, g2.reshape(1, d), sc2, sh2, rw, rb)


MOE_ROWS = 256
SC_CHUNK = 128


def _sc_workers():
    info = plsc.get_sparse_core_info()
    return info.num_cores, info.num_cores * info.num_subcores


def _sc_scatter_rows(rows, idx3, n_out):
    nw, k, ch = idx3.shape
    width = rows.shape[1]
    nc, _ = _sc_workers()
    mesh = plsc.VectorSubcoreMesh(core_axis_name="c", subcore_axis_name="s")

    @functools.partial(
        pl.kernel, mesh=mesh,
        out_type=jax.ShapeDtypeStruct((n_out, width), rows.dtype),
        scratch_types=[pltpu.VMEM((k, ch), jnp.int32), pltpu.VMEM((ch, width), rows.dtype), pltpu.SemaphoreType.DMA],
        name="moe_dispatch")
    def kern(rows_hbm, idx_hbm, out_hbm, idx_v, rows_v, sem):
        wid = lax.axis_index("s") * nc + lax.axis_index("c")
        pltpu.sync_copy(idx_hbm.at[wid], idx_v)

        @pl.loop(0, k)
        def _(j):
            pltpu.sync_copy(rows_hbm.at[pl.ds((wid * k + j) * ch, ch)], rows_v)
            pltpu.async_copy(rows_v, out_hbm.at[idx_v.at[j]], sem).wait()

    return kern(rows, idx3)


def _sc_gather_rows(table, idx3):
    nw, k, ch = idx3.shape
    width = table.shape[1]
    nc, _ = _sc_workers()
    mesh = plsc.VectorSubcoreMesh(core_axis_name="c", subcore_axis_name="s")

    @functools.partial(
        pl.kernel, mesh=mesh,
        out_type=jax.ShapeDtypeStruct((nw * k * ch, width), table.dtype),
        scratch_types=[pltpu.VMEM((k, ch), jnp.int32), pltpu.VMEM((ch, width), table.dtype), pltpu.SemaphoreType.DMA],
        name="moe_combine")
    def kern(table_hbm, idx_hbm, out_hbm, idx_v, rows_v, sem):
        wid = lax.axis_index("s") * nc + lax.axis_index("c")
        pltpu.sync_copy(idx_hbm.at[wid], idx_v)

        @pl.loop(0, k)
        def _(j):
            pltpu.async_copy(table_hbm.at[idx_v.at[j]], rows_v, sem).wait()
            pltpu.sync_copy(rows_v, out_hbm.at[pl.ds((wid * k + j) * ch, ch)])

    return kern(table, idx3)


def _experts_kernel(elo_ref, ehi_ref, nused_ref, x_ref, wg0_ref, wu0_ref, wd0_ref, wg1_ref, wu1_ref, wd1_ref, y_ref):
    @pl.when(pl.program_id(0) < nused_ref[0])
    def _():
        blk = x_ref[...]
        feat = ROW_WORDS - LANES
        h = _unpack_halves(blk[:, :feat])
        wts = lax.bitcast_convert_type(blk[:, feat:], F32)
        y = None
        for col, (wg_ref, wu_ref, wd_ref) in enumerate(((wg0_ref, wu0_ref, wd0_ref), (wg1_ref, wu1_ref, wd1_ref))):
            act = _silu(_dot(h, wg_ref[0])) * _dot(h, wu_ref[0]) * wts[:, col:col + 1]
            part = _dot(act.astype(BF16), wd_ref[0])
            y = part if y is None else y + part
        y_ref[...] = _pack_halves(y)


def _experts(xs, blk_elo, blk_ehi, nused, wg, wu, wd):
    n_rows = xs.shape[0]
    ne, d, f = wg.shape
    nblk = n_rows // MOE_ROWS
    rows = lambda i, elo, ehi, nu: (jnp.minimum(i, nu[0] - 1), 0)
    lo = lambda i, elo, ehi, nu: (elo[i], 0, 0)
    hi = lambda i, elo, ehi, nu: (ehi[i], 0, 0)
    return pl.pallas_call(
        _experts_kernel,
        grid_spec=pltpu.PrefetchScalarGridSpec(
            num_scalar_prefetch=3,
            grid=(nblk,),
            in_specs=[pl.BlockSpec((MOE_ROWS, ROW_WORDS), rows),
                      pl.BlockSpec((1, d, f), lo), pl.BlockSpec((1, d, f), lo), pl.BlockSpec((1, f, d), lo),
                      pl.BlockSpec((1, d, f), hi), pl.BlockSpec((1, d, f), hi), pl.BlockSpec((1, f, d), hi)],
            out_specs=pl.BlockSpec((MOE_ROWS, d // 2), lambda i, elo, ehi, nu: (i, 0))),
        out_shape=jax.ShapeDtypeStruct((n_rows, d // 2), jnp.int32),
        compiler_params=_cparams(("arbitrary",)),
        name="moe_experts",
    )(blk_elo, blk_ehi, nused, xs, wg, wu, wd, wg, wu, wd)


def _residual_kernel(x_ref, m_ref, gt_ref, o_ref):
    o_ref[0] = x_ref[0] + gt_ref[0] * _unpack_halves(m_ref[0]).astype(F32)


def _residual(x1, moe_p, gt, tm):
    b, s, d = x1.shape
    tok = lambda i, j: (i, j, 0)
    return pl.pallas_call(
        _residual_kernel,
        grid=(b, s // tm),
        in_specs=[pl.BlockSpec((1, tm, d), tok), pl.BlockSpec((1, tm, d // 2), tok),
                  pl.BlockSpec((1, 1, d), lambda i, j: (i, 0, 0))],
        out_specs=pl.BlockSpec((1, tm, d), tok),
        out_shape=jax.ShapeDtypeStruct((b, s, d), F32),
        compiler_params=_cparams(("arbitrary", "arbitrary")),
        name="moe_residual",
    )(x1, moe_p, gt)


def _moe(rows, route, counts, wg, wu, wd, x1, gt, tm):
    b, s, d = x1.shape
    t = b * s
    _, nw = _sc_workers()
    cnt = counts[0, :N_CLASSES].astype(jnp.int32)
    padded = ((cnt + MOE_ROWS - 1) // MOE_ROWS) * MOE_ROWS
    ends = jnp.cumsum(padded)
    n_rows = t + N_CLASSES * MOE_ROWS
    nblk = n_rows // MOE_ROWS
    cls = route[:, :, 0].reshape(t).astype(jnp.int32)
    rank = route[:, :, 1].reshape(t).astype(jnp.int32)
    pos = (ends - padded)[cls] + rank
    idx3 = pos.reshape(nw, t // (nw * SC_CHUNK), SC_CHUNK)
    nused = (ends[-1] // MOE_ROWS).reshape(1)
    blk_cls = jnp.sum((jnp.arange(nblk, dtype=jnp.int32)[:, None] * MOE_ROWS) >= ends[None, :], axis=1)
    blk_cls = jnp.minimum(blk_cls, blk_cls[jnp.maximum(nused[0] - 1, 0)])
    blk_elo = jnp.asarray(CLASS_E_LO, jnp.int32)[blk_cls]
    blk_ehi = jnp.asarray(CLASS_E_HI, jnp.int32)[blk_cls]
    xs = _sc_scatter_rows(rows.reshape(t, ROW_WORDS), idx3, n_rows)
    ys = _experts(xs, blk_elo, blk_ehi, nused, wg, wu, wd)
    moe_p = _sc_gather_rows(ys, idx3).reshape(b, s, d // 2)
    return _residual(x1, moe_p, gt, tm)


def kernel(x, c, ada_w, ada_b, norm1_g, norm2_g, ev_w_in, ev_conv_w, ev_dn_a_log, ev_dn_dt_bias, ev_dn_onorm_g, ev_fox_f_bias, ev_fox_qnorm_g, ev_fox_knorm_g, ev_w_out, od_w_in, od_conv_w, od_conv_b, od_lru_wr, od_lru_br, od_lru_wi, od_lru_bi, od_lru_lambda, od_sgu_norm_g, od_sgu_w, od_sgu_b, od_w_out, router_w, router_b, moe_w_gate, moe_w_up, moe_w_down):
    b, s, d = x.shape
    depth = ada_w.shape[0]
    tm = min(512, s)
    mod = _adaln(c, ada_w, ada_b).reshape(depth, b, 6, 1, d)
    rw = jnp.pad(router_w, ((0, 0), (0, LANES - N_EXPERTS))).astype(BF16)
    rb = jnp.pad(router_b, (0, LANES - N_EXPERTS)).reshape(1, LANES)

    for layer in range(depth):
        sh1, sc1, gt1, sh2, sc2, gt2 = (mod[layer, :, k] for k in range(6))
        i = layer // 2
        if layer % 2 == 0:
            w = ev_w_in[i]
            nq = 3 * DN_HEADS * DN_DK
            nz = DN_HEADS * DN_DK
            nf = 3 * FOX_HEADS * FOX_DH
            o_a = nq + nz
            o_f = o_a + 2 * DN_HEADS
            o_ff = o_f + nf
            small_w = jnp.concatenate([w[:, o_a:o_f], w[:, o_ff:o_ff + FOX_HEADS]], axis=1)
            small_w = jnp.pad(small_w, ((0, 0), (0, LANES - small_w.shape[1])))
            w_all = jnp.concatenate([w[:, :o_a], w[:, o_f:o_ff], small_w], axis=1).astype(BF16)
            splits = ((0, nq), (nq, o_a), (o_a, o_a + nf), (o_a + nf, o_a + nf + LANES))
            dn, z, fox, small = _inproj(x, norm1_g[layer], sc1, sh1, w_all, splits,
                                        (BF16, BF16, BF16, F32), (None,) * 4, tm)
            small_t = jnp.swapaxes(small[:, :, :16], 1, 2)
            zeros4 = jnp.zeros((4,), F32)
            mul = jnp.concatenate([-jnp.exp(ev_dn_a_log[i]), zeros4, -jnp.ones((4,), F32), zeros4])
            bias = jnp.concatenate([ev_dn_dt_bias[i], zeros4, ev_fox_f_bias[i], zeros4])
            pr = jnp.stack([mul, bias], axis=1)
            pc = jnp.pad(jnp.stack([mul, bias], axis=0), ((0, 0), (0, LANES - 16)))
            col, row = _gates(small, small_t, pc, pr)
            row_dn = row[:, :DN_HEADS].reshape(b, DN_HEADS, s // DN_GROUP, DN_GROUP).transpose(0, 2, 1, 3)
            o_dn = _deltanet(dn, z, col, row_dn, ev_conv_w[i], ev_dn_onorm_g[i])
            o_fox = _fox(fox, col, ev_fox_qnorm_g[i], ev_fox_knorm_g[i], min(512, s))
            mix_a, mix_b, w_out = o_dn, o_fox, ev_w_out[i]
        else:
            lw = od_lru_wr.shape[-1] * LRU_BLOCKS
            splits = ((0, lw), (lw, 2 * lw), (2 * lw, 2 * lw + od_sgu_w.shape[-1] * SGU_GROUPS),
                      (2 * lw + od_sgu_w.shape[-1] * SGU_GROUPS, od_w_in.shape[-1]))
            lx, lg, su, sv = _inproj(x, norm1_g[layer], sc1, sh1, od_w_in[i].astype(BF16), splits,
                                     (BF16,) * 4, (None, "gelu", "gelu", "gelu"), tm)
            o_lru = _lru(lx, lg, od_conv_w[i], od_conv_b[i], od_lru_wr[i], od_lru_br[i], od_lru_wi[i],
                         od_lru_bi[i], od_lru_lambda[i])
            o_sgu = _sgu(su, sv, od_sgu_norm_g[i], od_sgu_w[i], od_sgu_b[i], tm)
            mix_a, mix_b, w_out = o_lru, o_sgu, od_w_out[i]
        x1, rows, route, counts = _outproj(mix_a, mix_b, x, w_out.astype(BF16), gt1, norm2_g[layer], sc2, sh2,
                                           rw, rb, tm)
        x = _moe(rows, route, counts, moe_w_gate[layer].astype(BF16), moe_w_up[layer].astype(BF16),
                 moe_w_down[layer].astype(BF16), x1, gt2, tm)
    return x
```

```python
import functools

import jax
import jax.numpy as jnp
from jax import lax
from jax.experimental import pallas as pl
from jax.experimental.pallas import tpu as pltpu
from jax.experimental.pallas import tpu_sc as plsc

F32 = jnp.float32
BF16 = jnp.bfloat16
EPS = 1e-6
NEG_INF = float("-inf")

DN_HEADS = 4
DN_DK = 128
DN_CHUNK = 64
CONV_WIDTH = 4
FOX_HEADS = 4
FOX_DH = 128
LRU_BLOCKS = 4
LRU_C = 8.0
SGU_GROUPS = 4
SGU_CHUNK = 128
N_EXPERTS = 16
EXPERTS_PER_GROUP = 4
LANES = 128

VMEM_LIMIT = 48 * 1024 * 1024


def _cparams(sem):
    return pltpu.CompilerParams(dimension_semantics=sem, vmem_limit_bytes=VMEM_LIMIT)


def _dot(a, b):
    return jnp.dot(a, b, preferred_element_type=F32)


def _dot_nt(a, b):
    return lax.dot_general(a, b, (((1,), (1,)), ((), ())), preferred_element_type=F32)


def _dot_tn(a, b):
    return lax.dot_general(a, b, (((0,), (0,)), ((), ())), preferred_element_type=F32)


def _dot_exact(a, b):
    return jnp.dot(a, b, preferred_element_type=F32, precision=lax.Precision.HIGHEST)


def _sigmoid(x):
    return 1.0 / (1.0 + jnp.exp(-x))


def _silu(x):
    return x * _sigmoid(x)


def _softplus(x):
    return jnp.maximum(x, 0.0) + jnp.log(1.0 + jnp.exp(-jnp.abs(x)))


def _gelu_tanh(x):
    c = 0.7978845608028654
    return 0.5 * x * (1.0 + jnp.tanh(c * (x + 0.044715 * (x * x * x))))


def _prenorm(x, g, scale, shift):
    ms = jnp.mean(x * x, axis=-1, keepdims=True)
    return (x * lax.rsqrt(ms + EPS) * g) * (1.0 + scale) + shift


def _shift_rows(x, d, fill=0.0):
    rows = lax.broadcasted_iota(jnp.int32, x.shape, 0)
    return jnp.where(rows >= d, pltpu.roll(x, d, axis=0), fill)


def _causal_conv(x, w_ref):
    k_w = w_ref.shape[0]
    acc = x * w_ref[k_w - 1:k_w, :]
    for d in range(1, k_w):
        acc = acc + _shift_rows(x, d) * w_ref[k_w - 1 - d:k_w - d, :]
    return acc


def _adaln_kernel(c_ref, w_ref, b_ref, o_ref):
    c = c_ref[...]
    ca = _silu(c).astype(BF16)
    o_ref[0] = _dot(ca, w_ref[0].astype(BF16)) + b_ref[0]


def _adaln(c, ada_w, ada_b):
    depth, d, n = ada_w.shape
    b = c.shape[0]
    tn = 1536
    return pl.pallas_call(
        _adaln_kernel,
        grid=(depth, n // tn),
        in_specs=[pl.BlockSpec((b, d), lambda l, j: (0, 0)),
                  pl.BlockSpec((1, d, tn), lambda l, j: (l, 0, j)),
                  pl.BlockSpec((1, 1, tn), lambda l, j: (l, 0, j))],
        out_specs=pl.BlockSpec((1, b, tn), lambda l, j: (l, 0, j)),
        out_shape=jax.ShapeDtypeStruct((depth, b, n), F32),
        compiler_params=_cparams(("arbitrary", "arbitrary")),
        name="adaln",
    )(c, ada_w, ada_b.reshape(depth, 1, n))


def _inproj_kernel(*refs, col_splits, acts, pending):
    if pending:
        x_ref, m_ref, gtm_ref, g_ref, sc_ref, sh_ref, w_ref, xo_ref, *out_refs = refs
        x = x_ref[0] + gtm_ref[0] * _unpack_halves(m_ref[0]).astype(F32)
        xo_ref[0] = x
    else:
        x_ref, g_ref, sc_ref, sh_ref, w_ref, *out_refs = refs
        x = x_ref[0]
    h = _prenorm(x, g_ref[...], sc_ref[0], sh_ref[0]).astype(BF16)
    for o_ref, (c0, c1), act in zip(out_refs, col_splits, acts):
        p = _dot(h, w_ref[:, c0:c1])
        if act == "gelu":
            p = _gelu_tanh(p)
        o_ref[0] = p.astype(o_ref.dtype)


def _inproj(x, g, scale, shift, w, col_splits, out_dtypes, acts, tm, pending=None):
    b, s, d = x.shape
    n = w.shape[1]
    outs = tuple(jax.ShapeDtypeStruct((b, s, c1 - c0), dt) for (c0, c1), dt in zip(col_splits, out_dtypes))
    row = lambda i, j: (i, 0, 0)
    tok = lambda i, j: (i, j, 0)
    in_specs = [pl.BlockSpec((1, tm, d), tok)]
    out_specs = tuple(pl.BlockSpec((1, tm, c1 - c0), tok) for (c0, c1) in col_splits)
    args = (x,)
    if pending is not None:
        in_specs += [pl.BlockSpec((1, tm, d // 2), tok), pl.BlockSpec((1, 1, d), row)]
        out_specs = (pl.BlockSpec((1, tm, d), tok),) + out_specs
        outs = (jax.ShapeDtypeStruct((b, s, d), F32),) + outs
        args += tuple(pending)
    in_specs += [pl.BlockSpec((1, d), lambda i, j: (0, 0)),
                 pl.BlockSpec((1, 1, d), row),
                 pl.BlockSpec((1, 1, d), row),
                 pl.BlockSpec((d, n), lambda i, j: (0, 0))]
    return pl.pallas_call(
        functools.partial(_inproj_kernel, col_splits=col_splits, acts=acts, pending=pending is not None),
        grid=(b, s // tm),
        in_specs=in_specs,
        out_specs=out_specs,
        out_shape=outs,
        compiler_params=_cparams(("arbitrary", "arbitrary")),
        name="inproj",
    )(*args, g.reshape(1, d), scale, shift, w)


def _gates_kernel(sm_ref, smt_ref, pc_ref, pr_ref, col_ref, row_ref):
    s = sm_ref.shape[1]
    blk = LANES
    ri = lax.broadcasted_iota(jnp.int32, (blk, blk), 0)
    ci = lax.broadcasted_iota(jnp.int32, (blk, blk), 1)
    same_chunk = (ri // DN_CHUNK) == (ci // DN_CHUNK)
    tril = jnp.where(ri >= ci, 1.0, 0.0)
    tril_loc = jnp.where(same_chunk, tril, 0.0)
    triu_loc = jnp.where(same_chunk & (ri <= ci), 1.0, 0.0)
    lane = lax.broadcasted_iota(jnp.int32, (blk, LANES), 1)
    carry = jnp.zeros((1, LANES), F32)
    for j in range(s // blk):
        rows = slice(j * blk, (j + 1) * blk)
        xc = sm_ref[0, rows, :] + pc_ref[1:2, :]
        dec = pc_ref[0:1, :] * _softplus(jnp.where(lane < 4, xc, -xc))
        cum_glb = _dot_exact(tril, dec) + carry
        col_ref[0, rows, :] = jnp.where(lane < 4, _dot_exact(tril_loc, dec), jnp.where(lane < 8, _sigmoid(xc), cum_glb))
        carry = cum_glb[blk - 1:blk, :]
        decr = pr_ref[:, 0:1] * _softplus(smt_ref[0, :, rows] + pr_ref[:, 1:2])
        row_ref[0, :, rows] = _dot_exact(decr, triu_loc)


def _gates(small, small_t, pc, pr):
    b, s, _ = small.shape
    return pl.pallas_call(
        _gates_kernel,
        grid=(b,),
        in_specs=[pl.BlockSpec((1, s, LANES), lambda i: (i, 0, 0)),
                  pl.BlockSpec((1, 16, s), lambda i: (i, 0, 0)),
                  pl.BlockSpec((2, LANES), lambda i: (0, 0)),
                  pl.BlockSpec((16, 2), lambda i: (0, 0))],
        out_specs=(pl.BlockSpec((1, s, LANES), lambda i: (i, 0, 0)),
                   pl.BlockSpec((1, 16, s), lambda i: (i, 0, 0))),
        out_shape=(jax.ShapeDtypeStruct((b, s, LANES), F32), jax.ShapeDtypeStruct((b, 16, s), F32)),
        compiler_params=_cparams(("arbitrary",)),
        name="gates",
    )(small, small_t, pc, pr)


DN_PACK = 4
DN_GROUP = DN_PACK * DN_CHUNK


def _blockdiag(p):
    c, wide = p.shape
    t = jnp.concatenate([p] * (wide // c), axis=0)
    rb = lax.broadcasted_iota(jnp.int32, (wide, wide), 0) // c
    cb = lax.broadcasted_iota(jnp.int32, (wide, wide), 1) // c
    return jnp.where(rb == cb, t, 0.0).astype(BF16)


def _diag_blocks(m, c):
    wide = m.shape[1]
    cb = lax.broadcasted_iota(jnp.int32, (c, wide), 1) // c
    out = m[:c]
    for j in range(1, wide // c):
        out = jnp.where(cb == j, m[j * c:(j + 1) * c], out)
    return out


def _rows_to_blocks(col, c, wide):
    cb = lax.broadcasted_iota(jnp.int32, (c, wide), 1) // c
    out = jnp.broadcast_to(col[:c], (c, wide))
    for j in range(1, wide // c):
        out = jnp.where(cb == j, col[j * c:(j + 1) * c], out)
    return out


def _packed_unit_lower_inverse(lows):
    c, wide = lows[0].shape
    ri = lax.broadcasted_iota(jnp.int32, (c, wide), 0)
    ci = lax.broadcasted_iota(jnp.int32, (c, wide), 1) % c
    eye = jnp.where(ri == ci, 1.0, 0.0)
    n = range(len(lows))
    ds = [jnp.where((ri // 16) == (ci // 16), lo, 0.0) for lo in lows]
    xs = [eye - d for d in ds]
    ps = ds
    bds = [_blockdiag(p) for p in ps]
    for _ in range(3):
        ps = [_dot(ps[i].astype(BF16), bds[i]) for i in n]
        bds = [_blockdiag(p) for p in ps]
        xs = [xs[i] + _dot(xs[i].astype(BF16), bds[i]) for i in n]
    for width in (16, 32):
        sel = ((ri // (2 * width)) == (ci // (2 * width))) & ((ri // width) != (ci // width))
        offs = [_blockdiag(jnp.where(sel, lo, 0.0)) for lo in lows]
        ts = [_dot(xs[i].astype(BF16), offs[i]) for i in n]
        bdx = [_blockdiag(x) for x in xs]
        xs = [xs[i] - _dot(ts[i].astype(BF16), bdx[i]) for i in n]
    return xs


def _deltanet_kernel(dn_ref, z_ref, col_ref, row_ref, cw_ref, og_ref, o_ref, q_sc, k_sc, v_sc, st_sc):
    s = dn_ref.shape[1]
    c = DN_CHUNK
    gt = DN_GROUP
    dk = DN_DK
    hd = DN_HEADS * dk
    heads = range(DN_HEADS)
    for j in range(3 * DN_HEADS):
        cols = slice(j * dk, (j + 1) * dk)
        xj = _silu(_causal_conv(dn_ref[0, :, cols].astype(F32), cw_ref.at[:, cols]))
        if j < 2 * DN_HEADS:
            xj = xj * lax.rsqrt(jnp.sum(xj * xj, axis=-1, keepdims=True) + EPS)
        if j < DN_HEADS:
            q_sc[:, cols] = xj * (dk ** -0.5)
        elif j < 2 * DN_HEADS:
            k_sc[:, j * dk - hd:(j + 1) * dk - hd] = xj
        else:
            v_sc[:, j * dk - 2 * hd:(j + 1) * dk - 2 * hd] = xj
    st_sc[...] = jnp.zeros_like(st_sc)

    ri = lax.broadcasted_iota(jnp.int32, (c, gt), 0)
    ci = lax.broadcasted_iota(jnp.int32, (c, gt), 1) % c
    og = og_ref[...]

    def group(g, carry):
        r0 = pl.multiple_of(g * gt, gt)
        colg = col_ref[0, pl.ds(r0, gt), :]
        rowg = row_ref[0, g]
        hs = [slice(h * dk, (h + 1) * dk) for h in heads]
        q = [q_sc[pl.ds(r0, gt), hs[h]] for h in heads]
        k = [k_sc[pl.ds(r0, gt), hs[h]] for h in heads]
        v = [v_sc[pl.ds(r0, gt), hs[h]] for h in heads]
        gc = [colg[:, h:h + 1] for h in heads]
        beta = [colg[:, DN_HEADS + h:DN_HEADS + h + 1] for h in heads]
        glast = [jnp.concatenate([jnp.broadcast_to(gc[h][(j + 1) * c - 1:(j + 1) * c], (c, 1))
                                  for j in range(DN_PACK)], axis=0) for h in heads]
        eg = [jnp.exp(gc[h]) for h in heads]
        kbf = [k[h].astype(BF16) for h in heads]
        both = [_dot_nt(jnp.concatenate([(k[h] * beta[h]).astype(BF16), q[h].astype(BF16)], axis=0), kbf[h])
                for h in heads]
        decay = [jnp.exp(jnp.where(ri >= ci, _rows_to_blocks(gc[h], c, gt) - rowg[h:h + 1, :], NEG_INF))
                 for h in heads]
        kk = [_diag_blocks(both[h][:gt], c) * decay[h] for h in heads]
        qk = [_blockdiag(_diag_blocks(both[h][gt:], c) * decay[h]) for h in heads]
        t_inv = _packed_unit_lower_inverse([jnp.where(ri > ci, kk[h], 0.0) for h in heads])
        rhs = [jnp.concatenate([k[h] * (beta[h] * eg[h]), v[h] * beta[h]], axis=1).astype(BF16) for h in heads]
        wu = [_dot(_blockdiag(t_inv[h]), rhs[h]).astype(BF16) for h in heads]
        qwu = [_dot(qk[h], wu[h]) for h in heads]
        qp = [(q[h] * eg[h] - qwu[h][:, :dk]).astype(BF16) for h in heads]
        kdec = [(k[h] * jnp.exp(glast[h] - gc[h])).astype(BF16) for h in heads]
        mb = [[_dot_tn(kdec[h][j * c:(j + 1) * c], wu[h][j * c:(j + 1) * c]) for h in heads]
              for j in range(DN_PACK)]
        outs = [[] for _ in heads]
        for j in range(DN_PACK):
            rows = slice(j * c, (j + 1) * c)
            state = [st_sc[h] for h in heads]
            lhs = [jnp.concatenate([qp[h][rows], mb[j][h][:, :dk].astype(BF16)], axis=0) for h in heads]
            r = [_dot(lhs[h], state[h].astype(BF16)) for h in heads]
            for h in heads:
                gl = jnp.exp(glast[h][j * c:j * c + 1])
                st_sc[h] = state[h] * gl - r[h][c:] + mb[j][h][:, dk:]
                outs[h].append(r[h][:c] + qwu[h][rows, dk:])
        for h in heads:
            o = jnp.concatenate(outs[h], axis=0)
            on = o * lax.rsqrt(jnp.mean(o * o, axis=-1, keepdims=True) + EPS) * og
            zz = z_ref[0, pl.ds(r0, gt), hs[h]].astype(F32)
            o_ref[0, pl.ds(r0, gt), hs[h]] = (on * _silu(zz)).astype(o_ref.dtype)
        return carry

    lax.fori_loop(0, s // gt, group, 0)


def _deltanet(dn, z, col, row, conv_w, onorm_g):
    b, s, w3 = dn.shape
    hd = DN_HEADS * DN_DK
    return pl.pallas_call(
        _deltanet_kernel,
        grid=(b,),
        in_specs=[pl.BlockSpec((1, s, w3), lambda i: (i, 0, 0)),
                  pl.BlockSpec((1, s, hd), lambda i: (i, 0, 0)),
                  pl.BlockSpec((1, s, LANES), lambda i: (i, 0, 0)),
                  pl.BlockSpec((1, s // DN_GROUP, DN_HEADS, DN_GROUP), lambda i: (i, 0, 0, 0)),
                  pl.BlockSpec((CONV_WIDTH, w3), lambda i: (0, 0)),
                  pl.BlockSpec((1, DN_DK), lambda i: (0, 0))],
        out_specs=pl.BlockSpec((1, s, hd), lambda i: (i, 0, 0)),
        out_shape=jax.ShapeDtypeStruct((b, s, hd), BF16),
        scratch_shapes=[pltpu.VMEM((s, hd), F32), pltpu.VMEM((s, hd), F32), pltpu.VMEM((s, hd), F32),
                        pltpu.VMEM((DN_HEADS, DN_DK, DN_DK), F32)],
        compiler_params=_cparams(("arbitrary",)),
        name="deltanet",
    )(dn, z, col, row, conv_w, onorm_g.reshape(1, DN_DK))


def _split3(x):
    hi = x.astype(BF16).astype(F32)
    r = x - hi
    mid = r.astype(BF16).astype(F32)
    return hi, mid, r - mid


def _fox_kernel(q_ref, k_ref, v_ref, colq_ref, colk_ref, qg_ref, kg_ref, o_ref, ka_sc, va_sc, m_sc, acc_sc, *, tq):
    qi = pl.program_id(1)
    s = k_ref.shape[1]
    dh = FOX_DH

    @pl.when(qi == 0)
    def _():
        lane = lax.broadcasted_iota(jnp.int32, (s, LANES), 1)
        ones_col = jnp.where(lane == 0, 1.0, 0.0).astype(BF16)
        for h in range(FOX_HEADS):
            cols = slice(h * dh, (h + 1) * dh)
            kf = k_ref[0, :, cols].astype(F32)
            kn = kf * lax.rsqrt(jnp.mean(kf * kf, axis=-1, keepdims=True) + EPS) * kg_ref[...]
            ka_sc[h, :, :dh] = kn.astype(BF16)
            hi, mid, lo = _split3(colk_ref[0, :, 8 + h:9 + h])
            ext = jnp.where(lane == 0, -hi, jnp.where(lane == 1, -mid, jnp.where(lane == 2, -lo,
                            jnp.where(lane < 6, 1.0, 0.0))))
            ka_sc[h, :, dh:] = ext.astype(BF16)
            va_sc[h, :, :dh] = v_ref[0, :, cols]
            va_sc[h, :, dh:] = ones_col

    lane = lax.broadcasted_iota(jnp.int32, (tq, LANES), 1)
    qa = []
    for h in range(FOX_HEADS):
        cols = slice(h * dh, (h + 1) * dh)
        qf = q_ref[0, :, cols].astype(F32)
        qn = qf * lax.rsqrt(jnp.mean(qf * qf, axis=-1, keepdims=True) + EPS) * qg_ref[...] * (dh ** -0.5)
        hi, mid, lo = _split3(colq_ref[0, :, 8 + h:9 + h])
        ext = jnp.where(lane < 3, 1.0, jnp.where(lane == 3, hi, jnp.where(lane == 4, mid,
                        jnp.where(lane == 5, lo, 0.0))))
        qa.append(jnp.concatenate([qn.astype(BF16), ext.astype(BF16)], axis=1))
    m_sc[...] = jnp.full(m_sc.shape, NEG_INF, F32)
    acc_sc[...] = jnp.zeros_like(acc_sc)
    causal = lax.broadcasted_iota(jnp.int32, (tq, tq), 0) >= lax.broadcasted_iota(jnp.int32, (tq, tq), 1)

    def step(k0, masked):
        heads = range(FOX_HEADS)
        logits = [_dot_nt(qa[h], ka_sc[h, pl.ds(k0, tq), :]) for h in heads]
        if masked:
            logits = [jnp.where(causal, lg, NEG_INF) for lg in logits]
        ps, alphas = [], []
        for h in heads:
            m_old = m_sc[h]
            m_new = jnp.maximum(m_old, jnp.max(logits[h], axis=1, keepdims=True))
            m_sc[h] = m_new
            alphas.append(jnp.exp(m_old - m_new))
            ps.append(jnp.exp(logits[h] - jnp.concatenate([m_new] * (tq // LANES), axis=1)).astype(BF16))
        for h in heads:
            pv = _dot(ps[h], va_sc[h, pl.ds(k0, tq), :])
            acc_sc[h] = acc_sc[h] * jnp.concatenate([alphas[h], alphas[h]], axis=1) + pv

    def body(j, carry):
        step(pl.multiple_of(j * tq, tq), False)
        return carry

    lax.fori_loop(0, qi, body, 0)
    step(pl.multiple_of(qi * tq, tq), True)
    for h in range(FOX_HEADS):
        acc = acc_sc[h]
        o_ref[0, :, h * dh:(h + 1) * dh] = (acc[:, :dh] / acc[:, dh:dh + 1]).astype(o_ref.dtype)


def _fox(fox, col, qg, kg, tq):
    b, s, _ = fox.shape
    hd = FOX_HEADS * FOX_DH
    return pl.pallas_call(
        functools.partial(_fox_kernel, tq=tq),
        grid=(b, s // tq),
        in_specs=[pl.BlockSpec((1, tq, hd), lambda i, j: (i, j, 0)),
                  pl.BlockSpec((1, s, hd), lambda i, j: (i, 0, 1)),
                  pl.BlockSpec((1, s, hd), lambda i, j: (i, 0, 2)),
                  pl.BlockSpec((1, tq, LANES), lambda i, j: (i, j, 0)),
                  pl.BlockSpec((1, s, LANES), lambda i, j: (i, 0, 0)),
                  pl.BlockSpec((1, FOX_DH), lambda i, j: (0, 0)),
                  pl.BlockSpec((1, FOX_DH), lambda i, j: (0, 0))],
        out_specs=pl.BlockSpec((1, tq, hd), lambda i, j: (i, j, 0)),
        out_shape=jax.ShapeDtypeStruct((b, s, hd), BF16),
        scratch_shapes=[pltpu.VMEM((FOX_HEADS, s, 2 * FOX_DH), BF16), pltpu.VMEM((FOX_HEADS, s, 2 * FOX_DH), BF16),
                        pltpu.VMEM((FOX_HEADS, tq, LANES), F32), pltpu.VMEM((FOX_HEADS, tq, 2 * FOX_DH), F32)],
        compiler_params=_cparams(("arbitrary", "arbitrary")),
        name="fox",
    )(fox, fox, fox, col, col, qg.reshape(1, FOX_DH), kg.reshape(1, FOX_DH))


def _lru_kernel(x_ref, gate_ref, cw_ref, cb_ref, wr_ref, br_ref, wi_ref, bi_ref, lam_ref, o_ref):
    s = x_ref.shape[1]
    x = _causal_conv(x_ref[0].astype(F32), cw_ref) + cb_ref[...]
    xb = x.astype(BF16)
    r = _sigmoid(_dot(xb, wr_ref[0].astype(BF16)) + br_ref[...])
    i = _sigmoid(_dot(xb, wi_ref[0].astype(BF16)) + bi_ref[...])
    log_a = (-LRU_C) * r * _softplus(-lam_ref[...])
    a = jnp.exp(log_a)
    bb = jnp.sqrt(1.0 - jnp.exp(2.0 * log_a)) * (i * x)
    d = 1
    while d < s:
        bb = a * _shift_rows(bb, d, 0.0) + bb
        a = a * _shift_rows(a, d, 1.0)
        d *= 2
    o_ref[0] = (bb * gate_ref[0].astype(F32)).astype(o_ref.dtype)


def _lru(lx, lg, conv_w, conv_b, wr, br, wi, bi, lam):
    b, s, wd = lx.shape
    blk = wd // LRU_BLOCKS
    vec = lambda i, j: (0, j)
    return pl.pallas_call(
        _lru_kernel,
        grid=(b, LRU_BLOCKS),
        in_specs=[pl.BlockSpec((1, s, blk), lambda i, j: (i, 0, j)),
                  pl.BlockSpec((1, s, blk), lambda i, j: (i, 0, j)),
                  pl.BlockSpec((CONV_WIDTH, blk), vec),
                  pl.BlockSpec((1, blk), vec),
                  pl.BlockSpec((1, blk, blk), lambda i, j: (j, 0, 0)),
                  pl.BlockSpec((1, blk), vec),
                  pl.BlockSpec((1, blk, blk), lambda i, j: (j, 0, 0)),
                  pl.BlockSpec((1, blk), vec),
                  pl.BlockSpec((1, blk), vec)],
        out_specs=pl.BlockSpec((1, s, blk), lambda i, j: (i, 0, j)),
        out_shape=jax.ShapeDtypeStruct((b, s, wd), BF16),
        compiler_params=_cparams(("arbitrary", "arbitrary")),
        name="rglru",
    )(lx, lg, conv_w, conv_b.reshape(1, wd), wr, br.reshape(1, wd), wi, bi.reshape(1, wd), lam.reshape(1, wd))


def _sgu_kernel(u_ref, v_ref, g_ref, w_ref, bt_ref, o_ref):
    tm = u_ref.shape[1]
    wd = u_ref.shape[2]
    gw = wd // SGU_GROUPS
    c = SGU_CHUNK
    v = v_ref[0].astype(F32)
    vn = (v * lax.rsqrt(jnp.mean(v * v, axis=-1, keepdims=True) + EPS) * g_ref[...]).astype(BF16)
    ri = lax.broadcasted_iota(jnp.int32, (c, c), 0)
    ci = lax.broadcasted_iota(jnp.int32, (c, c), 1)
    for g in range(SGU_GROUPS):
        wg = jnp.where(ri >= ci, w_ref[g], 0.0).astype(BF16)
        bcol = bt_ref[:, g:g + 1]
        for n in range(tm // c):
            rows = slice(n * c, (n + 1) * c)
            cols = slice(g * gw, (g + 1) * gw)
            mixed = _dot(wg, vn[rows, cols]) + bcol
            o_ref[0, rows, cols] = (u_ref[0, rows, cols].astype(F32) * mixed).astype(o_ref.dtype)


def _sgu(su, sv, g_norm, w_s, b_s, tm):
    b, s, wd = su.shape
    return pl.pallas_call(
        _sgu_kernel,
        grid=(b, s // tm),
        in_specs=[pl.BlockSpec((1, tm, wd), lambda i, j: (i, j, 0)),
                  pl.BlockSpec((1, tm, wd), lambda i, j: (i, j, 0)),
                  pl.BlockSpec((1, wd), lambda i, j: (0, 0)),
                  pl.BlockSpec((SGU_GROUPS, SGU_CHUNK, SGU_CHUNK), lambda i, j: (0, 0, 0)),
                  pl.BlockSpec((SGU_CHUNK, SGU_GROUPS), lambda i, j: (0, 0))],
        out_specs=pl.BlockSpec((1, tm, wd), lambda i, j: (i, j, 0)),
        out_shape=jax.ShapeDtypeStruct((b, s, wd), BF16),
        compiler_params=_cparams(("arbitrary", "arbitrary")),
        name="sgu",
    )(su, sv, g_norm.reshape(1, wd), w_s, b_s.T)


def _routing(logits, rb):
    ne, tm = logits.shape
    row = lax.broadcasted_iota(jnp.int32, (ne, tm), 0)
    row_f = row.astype(F32)
    ex = jnp.exp(logits - jnp.max(logits, axis=0, keepdims=True))
    probs = ex / jnp.sum(ex, axis=0, keepdims=True)
    sel = probs + rb

    def top2(vals, idx):
        m1 = jnp.max(vals, axis=0, keepdims=True)
        i1 = jnp.min(jnp.where(vals == m1, idx, float(ne)), axis=0, keepdims=True)
        rest = jnp.where(idx == i1, NEG_INF, vals)
        m2 = jnp.max(rest, axis=0, keepdims=True)
        return m1, i1, m2, rest

    best = None
    g_idx = None
    grp = row // EXPERTS_PER_GROUP
    for g in range(ne // EXPERTS_PER_GROUP):
        m1, _, m2, _ = top2(jnp.where(grp == g, sel, NEG_INF), row_f)
        score = m1 + m2
        if g == 0:
            best, g_idx = score, jnp.zeros((1, tm), jnp.int32)
        else:
            upd = score > best
            best = jnp.where(upd, score, best)
            g_idx = jnp.where(upd, g, g_idx)
    _, i1, m2, rest = top2(jnp.where(grp == g_idx, sel, NEG_INF), row_f)
    i2 = jnp.min(jnp.where(rest == m2, row_f, float(ne)), axis=0, keepdims=True)
    p1 = jnp.sum(jnp.where(row_f == i1, probs, 0.0), axis=0, keepdims=True)
    p2 = jnp.sum(jnp.where(row_f == i2, probs, 0.0), axis=0, keepdims=True)
    den = p1 + p2
    first_lo = i1 < i2
    a = jnp.where(first_lo, i1, i2) - EXPERTS_PER_GROUP * g_idx.astype(F32)
    bhi = jnp.where(first_lo, i2, i1) - EXPERTS_PER_GROUP * g_idx.astype(F32)
    pair = a * (7.0 - a) * 0.5 + (bhi - a - 1.0)
    cls = PAIRS_PER_GROUP * g_idx.astype(F32) + pair
    w_lo = jnp.where(first_lo, p1, p2) / den
    w_hi = jnp.where(first_lo, p2, p1) / den
    return cls, w_lo, w_hi


PAIRS_PER_GROUP = EXPERTS_PER_GROUP * (EXPERTS_PER_GROUP - 1) // 2
N_CLASSES = (N_EXPERTS // EXPERTS_PER_GROUP) * PAIRS_PER_GROUP
_PAIRS = [(a, b) for a in range(EXPERTS_PER_GROUP) for b in range(a + 1, EXPERTS_PER_GROUP)]
CLASS_E_LO = [EXPERTS_PER_GROUP * (c // PAIRS_PER_GROUP) + _PAIRS[c % PAIRS_PER_GROUP][0] for c in range(N_CLASSES)]
CLASS_E_HI = [EXPERTS_PER_GROUP * (c // PAIRS_PER_GROUP) + _PAIRS[c % PAIRS_PER_GROUP][1] for c in range(N_CLASSES)]
CLASS_ROWS = 32
ROW_WORDS = 640


def _pack_halves(x):
    kk = x.shape[1] // 2
    lo = lax.bitcast_convert_type(x[:, :kk].astype(BF16).astype(F32), jnp.uint32)
    hi = lax.bitcast_convert_type(x[:, kk:].astype(BF16).astype(F32), jnp.uint32)
    return lax.bitcast_convert_type((lo >> 16) | hi, jnp.int32)


def _unpack_halves(p):
    u = lax.bitcast_convert_type(p, jnp.uint32)
    lo = lax.bitcast_convert_type(u << 16, F32)
    hi = lax.bitcast_convert_type(u & jnp.uint32(0xFFFF0000), F32)
    return jnp.concatenate([lo.astype(BF16), hi.astype(BF16)], axis=1)


def _outproj_kernel(a_ref, b_ref, x_ref, w_ref, gt_ref, g2_ref, sc_ref, sh_ref, rwt_ref, rb_ref,
                    x1_ref, rows_ref, route_ref, cnt_ref, carry_sc):
    first = (pl.program_id(0) == 0) & (pl.program_id(1) == 0)

    @pl.when(first)
    def _():
        carry_sc[...] = jnp.zeros_like(carry_sc)

    half = a_ref.shape[2]
    tm = a_ref.shape[1]
    y = _dot(a_ref[0], w_ref[:half, :]) + _dot(b_ref[0], w_ref[half:, :])
    x1 = x_ref[0] + gt_ref[0] * y
    x1_ref[0] = x1
    h2 = _prenorm(x1, g2_ref[...], sc_ref[0], sh_ref[0])
    rows_ref[0, :, :ROW_WORDS - LANES] = _pack_halves(h2)
    cls, w_lo, w_hi = _routing(_dot_nt(rwt_ref[...], h2.astype(BF16)), rb_ref[...])
    r128 = lax.broadcasted_iota(jnp.int32, (LANES, tm), 0)
    wts = jnp.where(r128 == 0, w_lo, jnp.where(r128 == 1, w_hi, 0.0))
    rows_ref[0, :, ROW_WORDS - LANES:] = lax.bitcast_convert_type(wts.T, jnp.int32)
    crow = lax.broadcasted_iota(jnp.int32, (carry_sc.shape[0], tm), 0).astype(F32)
    onehot = jnp.where(crow == cls, 1.0, 0.0)
    earlier = lax.broadcasted_iota(jnp.int32, (tm, tm), 0) < lax.broadcasted_iota(jnp.int32, (tm, tm), 1)
    prefix = _dot(onehot.astype(BF16), jnp.where(earlier, 1.0, 0.0).astype(BF16)) + carry_sc[:, 0:1]
    rank = jnp.sum(onehot * prefix, axis=0, keepdims=True)
    r8 = lax.broadcasted_iota(jnp.int32, (8, tm), 0)
    route_ref[0] = jnp.where(r8 == 0, cls, jnp.where(r8 == 1, rank, 0.0))
    carry_sc[...] = carry_sc[...] + jnp.sum(onehot, axis=1, keepdims=True)
    cnt_ref[...] = carry_sc[...]


def _outproj(a, bb, x, w, gt, g2, sc2, sh2, rw, rb, tm):
    b, s, d = x.shape
    half = a.shape[2]
    row = lambda i, j: (i, 0, 0)
    tok = lambda i, j: (i, j, 0)
    const = lambda i, j: (0, 0)
    return pl.pallas_call(
        _outproj_kernel,
        grid=(b, s // tm),
        in_specs=[pl.BlockSpec((1, tm, half), tok),
                  pl.BlockSpec((1, tm, half), tok),
                  pl.BlockSpec((1, tm, d), tok),
                  pl.BlockSpec((2 * half, d), const),
                  pl.BlockSpec((1, 1, d), row),
                  pl.BlockSpec((1, d), const),
                  pl.BlockSpec((1, 1, d), row),
                  pl.BlockSpec((1, 1, d), row),
                  pl.BlockSpec((N_EXPERTS, d), const),
                  pl.BlockSpec((N_EXPERTS, 1), const)],
        out_specs=(pl.BlockSpec((1, tm, d), tok),
                   pl.BlockSpec((1, tm, ROW_WORDS), tok),
                   pl.BlockSpec((1, 8, tm), lambda i, j: (i, 0, j)),
                   pl.BlockSpec((CLASS_ROWS, LANES), const)),
        out_shape=(jax.ShapeDtypeStruct((b, s, d), F32),
                   jax.ShapeDtypeStruct((b, s, ROW_WORDS), jnp.int32),
                   jax.ShapeDtypeStruct((b, 8, s), F32),
                   jax.ShapeDtypeStruct((CLASS_ROWS, LANES), F32)),
        scratch_shapes=[pltpu.VMEM((CLASS_ROWS, LANES), F32)],
        compiler_params=_cparams(("arbitrary", "arbitrary")),
        name="outproj_router",
    )(a, bb, x, w, gt, g2.reshape(1, d), sc2, sh2, rw, rb)


MOE_ROWS = 256
SC_CHUNK = 128


def _sc_workers():
    info = plsc.get_sparse_core_info()
    return info.num_cores, info.num_cores * info.num_subcores


def _sc_scatter_rows(rows, idx3, n_out):
    nw, k, ch = idx3.shape
    width = rows.shape[1]
    nc, _ = _sc_workers()
    mesh = plsc.VectorSubcoreMesh(core_axis_name="c", subcore_axis_name="s")

    @functools.partial(
        pl.kernel, mesh=mesh,
        out_type=jax.ShapeDtypeStruct((n_out, width), rows.dtype),
        scratch_types=[pltpu.VMEM((k, ch), jnp.int32), pltpu.VMEM((ch, width), rows.dtype), pltpu.SemaphoreType.DMA],
        name="moe_dispatch")
    def kern(rows_hbm, idx_hbm, out_hbm, idx_v, rows_v, sem):
        wid = lax.axis_index("s") * nc + lax.axis_index("c")
        pltpu.sync_copy(idx_hbm.at[wid], idx_v)

        @pl.loop(0, k)
        def _(j):
            pltpu.sync_copy(rows_hbm.at[pl.ds((wid * k + j) * ch, ch)], rows_v)
            pltpu.async_copy(rows_v, out_hbm.at[idx_v.at[j]], sem).wait()

    return kern(rows, idx3)


def _sc_gather_rows(table, idx3):
    nw, k, ch = idx3.shape
    width = table.shape[1]
    nc, _ = _sc_workers()
    mesh = plsc.VectorSubcoreMesh(core_axis_name="c", subcore_axis_name="s")

    @functools.partial(
        pl.kernel, mesh=mesh,
        out_type=jax.ShapeDtypeStruct((nw * k * ch, width), table.dtype),
        scratch_types=[pltpu.VMEM((k, ch), jnp.int32), pltpu.VMEM((ch, width), table.dtype), pltpu.SemaphoreType.DMA],
        name="moe_combine")
    def kern(table_hbm, idx_hbm, out_hbm, idx_v, rows_v, sem):
        wid = lax.axis_index("s") * nc + lax.axis_index("c")
        pltpu.sync_copy(idx_hbm.at[wid], idx_v)

        @pl.loop(0, k)
        def _(j):
            pltpu.async_copy(table_hbm.at[idx_v.at[j]], rows_v, sem).wait()
            pltpu.sync_copy(rows_v, out_hbm.at[pl.ds((wid * k + j) * ch, ch)])

    return kern(table, idx3)


def _experts_kernel(elo_ref, ehi_ref, nused_ref, x_ref, wg0_ref, wu0_ref, wd0_ref, wg1_ref, wu1_ref, wd1_ref, y_ref):
    @pl.when(pl.program_id(0) < nused_ref[0])
    def _():
        blk = x_ref[...]
        feat = ROW_WORDS - LANES
        h = _unpack_halves(blk[:, :feat])
        wts = lax.bitcast_convert_type(blk[:, feat:], F32)
        y = None
        for col, (wg_ref, wu_ref, wd_ref) in enumerate(((wg0_ref, wu0_ref, wd0_ref), (wg1_ref, wu1_ref, wd1_ref))):
            act = (_silu(_dot(h, wg_ref[0].astype(BF16))) * _dot(h, wu_ref[0].astype(BF16))
                   * wts[:, col:col + 1])
            part = _dot(act.astype(BF16), wd_ref[0].astype(BF16))
            y = part if y is None else y + part
        y_ref[...] = _pack_halves(y)


def _experts(xs, blk_elo, blk_ehi, nused, wg, wu, wd):
    n_rows = xs.shape[0]
    ne, d, f = wg.shape
    nblk = n_rows // MOE_ROWS
    rows = lambda i, elo, ehi, nu: (jnp.minimum(i, nu[0] - 1), 0)
    lo = lambda i, elo, ehi, nu: (elo[i], 0, 0)
    hi = lambda i, elo, ehi, nu: (ehi[i], 0, 0)
    return pl.pallas_call(
        _experts_kernel,
        grid_spec=pltpu.PrefetchScalarGridSpec(
            num_scalar_prefetch=3,
            grid=(nblk,),
            in_specs=[pl.BlockSpec((MOE_ROWS, ROW_WORDS), rows),
                      pl.BlockSpec((1, d, f), lo), pl.BlockSpec((1, d, f), lo), pl.BlockSpec((1, f, d), lo),
                      pl.BlockSpec((1, d, f), hi), pl.BlockSpec((1, d, f), hi), pl.BlockSpec((1, f, d), hi)],
            out_specs=pl.BlockSpec((MOE_ROWS, d // 2), lambda i, elo, ehi, nu: (i, 0))),
        out_shape=jax.ShapeDtypeStruct((n_rows, d // 2), jnp.int32),
        compiler_params=_cparams(("arbitrary",)),
        name="moe_experts",
    )(blk_elo, blk_ehi, nused, xs, wg, wu, wd, wg, wu, wd)


def _residual_kernel(x_ref, m_ref, gt_ref, o_ref):
    o_ref[0] = x_ref[0] + gt_ref[0] * _unpack_halves(m_ref[0]).astype(F32)


def _residual(x1, moe_p, gt, tm):
    b, s, d = x1.shape
    tok = lambda i, j: (i, j, 0)
    return pl.pallas_call(
        _residual_kernel,
        grid=(b, s // tm),
        in_specs=[pl.BlockSpec((1, tm, d), tok), pl.BlockSpec((1, tm, d // 2), tok),
                  pl.BlockSpec((1, 1, d), lambda i, j: (i, 0, 0))],
        out_specs=pl.BlockSpec((1, tm, d), tok),
        out_shape=jax.ShapeDtypeStruct((b, s, d), F32),
        compiler_params=_cparams(("arbitrary", "arbitrary")),
        name="moe_residual",
    )(x1, moe_p, gt)


def _moe(rows, route, counts, wg, wu, wd):
    b, s, _ = rows.shape
    d = wg.shape[1]
    t = b * s
    _, nw = _sc_workers()
    cnt = counts[:N_CLASSES, 0].astype(jnp.int32)
    padded = ((cnt + MOE_ROWS - 1) // MOE_ROWS) * MOE_ROWS
    ends = jnp.cumsum(padded)
    n_rows = t + N_CLASSES * MOE_ROWS
    nblk = n_rows // MOE_ROWS
    cls = route[:, 0, :].reshape(t).astype(jnp.int32)
    rank = route[:, 1, :].reshape(t).astype(jnp.int32)
    pos = (ends - padded)[cls] + rank
    idx3 = pos.reshape(nw, t // (nw * SC_CHUNK), SC_CHUNK)
    nused = (ends[-1] // MOE_ROWS).reshape(1)
    blk_cls = jnp.sum((jnp.arange(nblk, dtype=jnp.int32)[:, None] * MOE_ROWS) >= ends[None, :], axis=1)
    blk_cls = jnp.minimum(blk_cls, blk_cls[jnp.maximum(nused[0] - 1, 0)])
    blk_elo = jnp.asarray(CLASS_E_LO, jnp.int32)[blk_cls]
    blk_ehi = jnp.asarray(CLASS_E_HI, jnp.int32)[blk_cls]
    xs = _sc_scatter_rows(rows.reshape(t, ROW_WORDS), idx3, n_rows)
    ys = _experts(xs, blk_elo, blk_ehi, nused, wg, wu, wd)
    return _sc_gather_rows(ys, idx3).reshape(b, s, d // 2)


def kernel(x, c, ada_w, ada_b, norm1_g, norm2_g, ev_w_in, ev_conv_w, ev_dn_a_log, ev_dn_dt_bias, ev_dn_onorm_g, ev_fox_f_bias, ev_fox_qnorm_g, ev_fox_knorm_g, ev_w_out, od_w_in, od_conv_w, od_conv_b, od_lru_wr, od_lru_br, od_lru_wi, od_lru_bi, od_lru_lambda, od_sgu_norm_g, od_sgu_w, od_sgu_b, od_w_out, router_w, router_b, moe_w_gate, moe_w_up, moe_w_down):
    b, s, d = x.shape
    depth = ada_w.shape[0]
    tm = min(512, s)
    mod = _adaln(c, ada_w, ada_b).reshape(depth, b, 6, 1, d)
    rw = router_w.T.astype(BF16)
    rb = router_b.reshape(N_EXPERTS, 1)

    pending = None
    for layer in range(depth):
        sh1, sc1, gt1, sh2, sc2, gt2 = (mod[layer, :, k] for k in range(6))
        i = layer // 2
        if layer % 2 == 0:
            w = ev_w_in[i]
            nq = 3 * DN_HEADS * DN_DK
            nz = DN_HEADS * DN_DK
            nf = 3 * FOX_HEADS * FOX_DH
            o_a = nq + nz
            o_f = o_a + 2 * DN_HEADS
            o_ff = o_f + nf
            small_w = jnp.concatenate([w[:, o_a:o_f], w[:, o_ff:o_ff + FOX_HEADS]], axis=1)
            small_w = jnp.pad(small_w, ((0, 0), (0, LANES - small_w.shape[1])))
            w_all = jnp.concatenate([w[:, :o_a], w[:, o_f:o_ff], small_w], axis=1).astype(BF16)
            splits = ((0, nq), (nq, o_a), (o_a, o_a + nf), (o_a + nf, o_a + nf + LANES))
            res = _inproj(x, norm1_g[layer], sc1, sh1, w_all, splits,
                          (BF16, BF16, BF16, F32), (None,) * 4, tm, pending)
            if pending is not None:
                x, res = res[0], res[1:]
            dn, z, fox, small = res
            small_t = jnp.swapaxes(small[:, :, :16], 1, 2)
            zeros4 = jnp.zeros((4,), F32)
            mul = jnp.concatenate([-jnp.exp(ev_dn_a_log[i]), zeros4, -jnp.ones((4,), F32), zeros4])
            bias = jnp.concatenate([ev_dn_dt_bias[i], zeros4, ev_fox_f_bias[i], zeros4])
            pr = jnp.stack([mul, bias], axis=1)
            pc = jnp.pad(jnp.stack([mul, bias], axis=0), ((0, 0), (0, LANES - 16)))
            col, row = _gates(small, small_t, pc, pr)
            row_dn = row[:, :DN_HEADS].reshape(b, DN_HEADS, s // DN_GROUP, DN_GROUP).transpose(0, 2, 1, 3)
            o_dn = _deltanet(dn, z, col, row_dn, ev_conv_w[i], ev_dn_onorm_g[i])
            o_fox = _fox(fox, col, ev_fox_qnorm_g[i], ev_fox_knorm_g[i], min(512, s))
            mix_a, mix_b, w_out = o_dn, o_fox, ev_w_out[i]
        else:
            lw = od_lru_wr.shape[-1] * LRU_BLOCKS
            splits = ((0, lw), (lw, 2 * lw), (2 * lw, 2 * lw + od_sgu_w.shape[-1] * SGU_GROUPS),
                      (2 * lw + od_sgu_w.shape[-1] * SGU_GROUPS, od_w_in.shape[-1]))
            res = _inproj(x, norm1_g[layer], sc1, sh1, od_w_in[i].astype(BF16), splits,
                          (BF16,) * 4, (None, "gelu", "gelu", "gelu"), tm, pending)
            if pending is not None:
                x, res = res[0], res[1:]
            lx, lg, su, sv = res
            o_lru = _lru(lx, lg, od_conv_w[i], od_conv_b[i], od_lru_wr[i], od_lru_br[i], od_lru_wi[i],
                         od_lru_bi[i], od_lru_lambda[i])
            o_sgu = _sgu(su, sv, od_sgu_norm_g[i], od_sgu_w[i], od_sgu_b[i], tm)
            mix_a, mix_b, w_out = o_lru, o_sgu, od_w_out[i]
        x1, rows, route, counts = _outproj(mix_a, mix_b, x, w_out.astype(BF16), gt1, norm2_g[layer], sc2, sh2,
                                           rw, rb, tm)
        x = x1
        pending = (_moe(rows, route, counts, moe_w_gate[layer], moe_w_up[layer], moe_w_down[layer]), gt2)
    return _residual(x, *pending, tm)
```

```python
import functools

import jax
import jax.numpy as jnp
from jax import lax
from jax.experimental import pallas as pl
from jax.experimental.pallas import tpu as pltpu
from jax.experimental.pallas import tpu_sc as plsc

F32 = jnp.float32
BF16 = jnp.bfloat16
EPS = 1e-6
NEG_INF = float("-inf")

DN_HEADS = 4
DN_DK = 128
DN_CHUNK = 64
CONV_WIDTH = 4
FOX_HEADS = 4
FOX_DH = 128
LRU_BLOCKS = 4
LRU_C = 8.0
SGU_GROUPS = 4
SGU_CHUNK = 128
N_EXPERTS = 16
EXPERTS_PER_GROUP = 4
LANES = 128
SUBLANES = 8

VMEM_LIMIT = 48 * 1024 * 1024


def _cparams(sem):
    return pltpu.CompilerParams(dimension_semantics=sem, vmem_limit_bytes=VMEM_LIMIT)


def _dot(a, b):
    return jnp.dot(a, b, preferred_element_type=F32)


def _dot_nt(a, b):
    return lax.dot_general(a, b, (((1,), (1,)), ((), ())), preferred_element_type=F32)


def _dot_tn(a, b):
    return lax.dot_general(a, b, (((0,), (0,)), ((), ())), preferred_element_type=F32)


def _dot_exact(a, b):
    return jnp.dot(a, b, preferred_element_type=F32, precision=lax.Precision.HIGHEST)


def _sigmoid(x):
    return 1.0 / (1.0 + jnp.exp(-x))


def _silu(x):
    return x * _sigmoid(x)


def _softplus(x):
    return jnp.maximum(x, 0.0) + jnp.log(1.0 + jnp.exp(-jnp.abs(x)))


def _gelu_tanh(x):
    c = 0.7978845608028654
    return 0.5 * x * (1.0 + jnp.tanh(c * (x + 0.044715 * (x * x * x))))


def _prenorm(x, g, scale, shift):
    ms = jnp.mean(x * x, axis=-1, keepdims=True)
    return (x * lax.rsqrt(ms + EPS) * g) * (1.0 + scale) + shift


def _shift_rows(x, d, fill=0.0):
    rows = lax.broadcasted_iota(jnp.int32, x.shape, 0)
    return jnp.where(rows >= d, pltpu.roll(x, d, axis=0), fill)


def _causal_conv(x, w_ref):
    k_w = w_ref.shape[0]
    acc = x * w_ref[k_w - 1:k_w, :]
    for d in range(1, k_w):
        acc = acc + _shift_rows(x, d) * w_ref[k_w - 1 - d:k_w - d, :]
    return acc


def _adaln_kernel(c_ref, w_ref, b_ref, o_ref):
    c = c_ref[...]
    ca = _silu(c).astype(BF16)
    o_ref[0] = _dot(ca, w_ref[0].astype(BF16)) + b_ref[0]


def _adaln(c, ada_w, ada_b):
    depth, d, n = ada_w.shape
    b = c.shape[0]
    tn = 1536
    return pl.pallas_call(
        _adaln_kernel,
        grid=(depth, n // tn),
        in_specs=[pl.BlockSpec((b, d), lambda l, j: (0, 0)),
                  pl.BlockSpec((1, d, tn), lambda l, j: (l, 0, j)),
                  pl.BlockSpec((1, 1, tn), lambda l, j: (l, 0, j))],
        out_specs=pl.BlockSpec((1, b, tn), lambda l, j: (l, 0, j)),
        out_shape=jax.ShapeDtypeStruct((depth, b, n), F32),
        compiler_params=_cparams(("arbitrary", "arbitrary")),
        name="adaln",
    )(c, ada_w, ada_b.reshape(depth, 1, n))


CONV_HALO = 8
CONV_SLAB = 512


def _inproj_kernel(*refs, col_splits, acts, pending, conv):
    refs = list(refs)
    x_ref = refs.pop(0)
    if pending:
        m_ref, gtm_ref = refs.pop(0), refs.pop(0)
    g_ref, sc_ref, sh_ref, w_ref = (refs.pop(0) for _ in range(4))
    if conv:
        cw_ref = refs.pop(0)
        halo_sc = refs.pop()
    if pending:
        xo_ref = refs.pop(0)
        x = x_ref[0] + gtm_ref[0] * _unpack_halves(m_ref[0]).astype(F32)
        xo_ref[0] = x
    else:
        x = x_ref[0]
    out_refs = refs
    h = _prenorm(x, g_ref[...], sc_ref[0], sh_ref[0]).astype(BF16)
    tm = x.shape[0]
    if conv:
        @pl.when(pl.program_id(1) == 0)
        def _():
            halo_sc[0:CONV_HALO, :] = jnp.zeros((CONV_HALO, halo_sc.shape[1]), F32)

        c0, c1 = col_splits[0]
        for s0 in range(c0, c1, CONV_SLAB):
            halo_sc[CONV_HALO:, s0 - c0:s0 - c0 + CONV_SLAB] = _dot(h, w_ref[:, s0:s0 + CONV_SLAB])
    for k, (o_ref, (c0, c1), act) in enumerate(zip(out_refs, col_splits, acts)):
        if conv and k == 0:
            continue
        p = _dot(h, w_ref[:, c0:c1])
        if act == "gelu":
            p = _gelu_tanh(p)
        o_ref[0] = p.astype(o_ref.dtype)
    if conv:
        o_ref = out_refs[0]
        for t0 in range(0, halo_sc.shape[1], DN_DK):
            cols = slice(t0, t0 + DN_DK)
            acc = halo_sc[CONV_HALO:, cols] * cw_ref[CONV_WIDTH - 1:CONV_WIDTH, cols]
            for dd in range(1, CONV_WIDTH):
                acc = acc + halo_sc[pl.ds(CONV_HALO - dd, tm), cols] * cw_ref[CONV_WIDTH - 1 - dd:CONV_WIDTH - dd, cols]
            halo_sc[0:CONV_HALO, cols] = halo_sc[tm:tm + CONV_HALO, cols]
            yj = _silu(acc)
            if t0 < 2 * DN_HEADS * DN_DK:
                yj = yj * lax.rsqrt(jnp.sum(yj * yj, axis=-1, keepdims=True) + EPS)
            if t0 < DN_HEADS * DN_DK:
                yj = yj * (DN_DK ** -0.5)
            o_ref[0, :, cols] = yj.astype(o_ref.dtype)


def _inproj(x, g, scale, shift, w, col_splits, out_dtypes, acts, tm, pending=None, conv_w=None):
    b, s, d = x.shape
    n = w.shape[1]
    outs = tuple(jax.ShapeDtypeStruct((b, s, c1 - c0), dt) for (c0, c1), dt in zip(col_splits, out_dtypes))
    row = lambda i, j: (i, 0, 0)
    tok = lambda i, j: (i, j, 0)
    const = lambda i, j: (0, 0)
    in_specs = [pl.BlockSpec((1, tm, d), tok)]
    out_specs = tuple(pl.BlockSpec((1, tm, c1 - c0), tok) for (c0, c1) in col_splits)
    args = (x,)
    if pending is not None:
        in_specs += [pl.BlockSpec((1, tm, d // 2), tok), pl.BlockSpec((1, 1, d), row)]
        out_specs = (pl.BlockSpec((1, tm, d), tok),) + out_specs
        outs = (jax.ShapeDtypeStruct((b, s, d), F32),) + outs
        args += tuple(pending)
    in_specs += [pl.BlockSpec((1, d), const), pl.BlockSpec((1, 1, d), row), pl.BlockSpec((1, 1, d), row),
                 pl.BlockSpec((d, n), const)]
    args += (g.reshape(1, d), scale, shift, w)
    scratch = []
    if conv_w is not None:
        in_specs.append(pl.BlockSpec(conv_w.shape, const))
        args += (conv_w,)
        scratch.append(pltpu.VMEM((CONV_HALO + tm, conv_w.shape[1]), F32))
    return pl.pallas_call(
        functools.partial(_inproj_kernel, col_splits=col_splits, acts=acts, pending=pending is not None,
                          conv=conv_w is not None),
        grid=(b, s // tm),
        in_specs=in_specs,
        out_specs=out_specs,
        out_shape=outs,
        scratch_shapes=scratch,
        compiler_params=_cparams(("arbitrary", "arbitrary")),
        name="inproj",
    )(*args)


def _gates_kernel(sm_ref, smt_ref, pc_ref, pr_ref, col_ref, row_ref):
    s = sm_ref.shape[1]
    blk = LANES
    ri = lax.broadcasted_iota(jnp.int32, (blk, blk), 0)
    ci = lax.broadcasted_iota(jnp.int32, (blk, blk), 1)
    same_chunk = (ri // DN_CHUNK) == (ci // DN_CHUNK)
    tril = jnp.where(ri >= ci, 1.0, 0.0)
    tril_loc = jnp.where(same_chunk, tril, 0.0)
    triu_loc = jnp.where(same_chunk & (ri <= ci), 1.0, 0.0)
    lane = lax.broadcasted_iota(jnp.int32, (blk, LANES), 1)
    carry = jnp.zeros((1, LANES), F32)
    for j in range(s // blk):
        rows = slice(j * blk, (j + 1) * blk)
        xc = sm_ref[0, rows, :] + pc_ref[1:2, :]
        dec = pc_ref[0:1, :] * _softplus(jnp.where(lane < 4, xc, -xc))
        cum_glb = _dot_exact(tril, dec) + carry
        col_ref[0, rows, :] = jnp.where(lane < 4, _dot_exact(tril_loc, dec), jnp.where(lane < 8, _sigmoid(xc), cum_glb))
        carry = cum_glb[blk - 1:blk, :]
        decr = pr_ref[:, 0:1] * _softplus(smt_ref[0, :, rows] + pr_ref[:, 1:2])
        row_ref[0, :, rows] = _dot_exact(decr, triu_loc)


def _gates(small, small_t, pc, pr):
    b, s, _ = small.shape
    return pl.pallas_call(
        _gates_kernel,
        grid=(b,),
        in_specs=[pl.BlockSpec((1, s, LANES), lambda i: (i, 0, 0)),
                  pl.BlockSpec((1, 16, s), lambda i: (i, 0, 0)),
                  pl.BlockSpec((2, LANES), lambda i: (0, 0)),
                  pl.BlockSpec((16, 2), lambda i: (0, 0))],
        out_specs=(pl.BlockSpec((1, s, LANES), lambda i: (i, 0, 0)),
                   pl.BlockSpec((1, 16, s), lambda i: (i, 0, 0))),
        out_shape=(jax.ShapeDtypeStruct((b, s, LANES), F32), jax.ShapeDtypeStruct((b, 16, s), F32)),
        compiler_params=_cparams(("arbitrary",)),
        name="gates",
    )(small, small_t, pc, pr)


DN_PACK = 4
DN_GROUP = DN_PACK * DN_CHUNK


def _blockdiag(p):
    c, wide = p.shape
    t = jnp.concatenate([p] * (wide // c), axis=0)
    rb = lax.broadcasted_iota(jnp.int32, (wide, wide), 0) // c
    cb = lax.broadcasted_iota(jnp.int32, (wide, wide), 1) // c
    return jnp.where(rb == cb, t, 0.0).astype(BF16)


def _diag_blocks(m, c):
    wide = m.shape[1]
    cb = lax.broadcasted_iota(jnp.int32, (c, wide), 1) // c
    out = m[:c]
    for j in range(1, wide // c):
        out = jnp.where(cb == j, m[j * c:(j + 1) * c], out)
    return out


def _rows_to_blocks(col, c, wide):
    cb = lax.broadcasted_iota(jnp.int32, (c, wide), 1) // c
    out = jnp.broadcast_to(col[:c], (c, wide))
    for j in range(1, wide // c):
        out = jnp.where(cb == j, col[j * c:(j + 1) * c], out)
    return out


def _packed_unit_lower_inverse(lows):
    c, wide = lows[0].shape
    ri = lax.broadcasted_iota(jnp.int32, (c, wide), 0)
    ci = lax.broadcasted_iota(jnp.int32, (c, wide), 1) % c
    eye = jnp.where(ri == ci, 1.0, 0.0)
    n = range(len(lows))
    ds = [jnp.where((ri // 16) == (ci // 16), lo, 0.0) for lo in lows]
    xs = [eye - d for d in ds]
    ps = ds
    bds = [_blockdiag(p) for p in ps]
    for _ in range(3):
        ps = [_dot(ps[i].astype(BF16), bds[i]) for i in n]
        bds = [_blockdiag(p) for p in ps]
        xs = [xs[i] + _dot(xs[i].astype(BF16), bds[i]) for i in n]
    for width in (16, 32):
        sel = ((ri // (2 * width)) == (ci // (2 * width))) & ((ri // width) != (ci // width))
        offs = [_blockdiag(jnp.where(sel, lo, 0.0)) for lo in lows]
        ts = [_dot(xs[i].astype(BF16), offs[i]) for i in n]
        bdx = [_blockdiag(x) for x in xs]
        xs = [xs[i] - _dot(ts[i].astype(BF16), bdx[i]) for i in n]
    return xs


def _deltanet_kernel(dn_ref, z_ref, col_ref, row_ref, og_ref, o_ref, st_sc):
    s = dn_ref.shape[1]
    c = DN_CHUNK
    gt = DN_GROUP
    dk = DN_DK
    hd = DN_HEADS * dk
    heads = range(DN_HEADS)
    st_sc[...] = jnp.zeros_like(st_sc)

    ri = lax.broadcasted_iota(jnp.int32, (c, gt), 0)
    ci = lax.broadcasted_iota(jnp.int32, (c, gt), 1) % c
    og = og_ref[...]

    def group(g, carry):
        r0 = pl.multiple_of(g * gt, gt)
        colg = col_ref[0, pl.ds(r0, gt), :]
        rowg = row_ref[0, g]
        hs = [slice(h * dk, (h + 1) * dk) for h in heads]
        q = [dn_ref[0, pl.ds(r0, gt), h * dk:(h + 1) * dk].astype(F32) for h in heads]
        k = [dn_ref[0, pl.ds(r0, gt), hd + h * dk:hd + (h + 1) * dk].astype(F32) for h in heads]
        v = [dn_ref[0, pl.ds(r0, gt), 2 * hd + h * dk:2 * hd + (h + 1) * dk].astype(F32) for h in heads]
        gc = [colg[:, h:h + 1] for h in heads]
        beta = [colg[:, DN_HEADS + h:DN_HEADS + h + 1] for h in heads]
        glast = [jnp.concatenate([jnp.broadcast_to(gc[h][(j + 1) * c - 1:(j + 1) * c], (c, 1))
                                  for j in range(DN_PACK)], axis=0) for h in heads]
        eg = [jnp.exp(gc[h]) for h in heads]
        kbf = [k[h].astype(BF16) for h in heads]
        both = [_dot_nt(jnp.concatenate([(k[h] * beta[h]).astype(BF16), q[h].astype(BF16)], axis=0), kbf[h])
                for h in heads]
        decay = [jnp.exp(jnp.where(ri >= ci, _rows_to_blocks(gc[h], c, gt) - rowg[h:h + 1, :], NEG_INF))
                 for h in heads]
        kk = [_diag_blocks(both[h][:gt], c) * decay[h] for h in heads]
        qk = [_blockdiag(_diag_blocks(both[h][gt:], c) * decay[h]) for h in heads]
        t_inv = _packed_unit_lower_inverse([jnp.where(ri > ci, kk[h], 0.0) for h in heads])
        rhs = [jnp.concatenate([k[h] * (beta[h] * eg[h]), v[h] * beta[h]], axis=1).astype(BF16) for h in heads]
        wu = [_dot(_blockdiag(t_inv[h]), rhs[h]).astype(BF16) for h in heads]
        qwu = [_dot(qk[h], wu[h]) for h in heads]
        qp = [(q[h] * eg[h] - qwu[h][:, :dk]).astype(BF16) for h in heads]
        kdec = [(k[h] * jnp.exp(glast[h] - gc[h])).astype(BF16) for h in heads]
        mb = [[_dot_tn(kdec[h][j * c:(j + 1) * c], wu[h][j * c:(j + 1) * c]) for h in heads]
              for j in range(DN_PACK)]
        outs = [[] for _ in heads]
        for j in range(DN_PACK):
            rows = slice(j * c, (j + 1) * c)
            state = [st_sc[h] for h in heads]
            lhs = [jnp.concatenate([qp[h][rows], mb[j][h][:, :dk].astype(BF16)], axis=0) for h in heads]
            r = [_dot(lhs[h], state[h].astype(BF16)) for h in heads]
            for h in heads:
                gl = jnp.exp(glast[h][j * c:j * c + 1])
                st_sc[h] = state[h] * gl - r[h][c:] + mb[j][h][:, dk:]
                outs[h].append(r[h][:c] + qwu[h][rows, dk:])
        for h in heads:
            o = jnp.concatenate(outs[h], axis=0)
            on = o * lax.rsqrt(jnp.mean(o * o, axis=-1, keepdims=True) + EPS) * og
            zz = z_ref[0, pl.ds(r0, gt), hs[h]].astype(F32)
            o_ref[0, pl.ds(r0, gt), hs[h]] = (on * _silu(zz)).astype(o_ref.dtype)
        return carry

    lax.fori_loop(0, s // gt, group, 0)


def _deltanet(dn, z, col, row, onorm_g):
    b, s, w3 = dn.shape
    hd = DN_HEADS * DN_DK
    return pl.pallas_call(
        _deltanet_kernel,
        grid=(b,),
        in_specs=[pl.BlockSpec((1, s, w3), lambda i: (i, 0, 0)),
                  pl.BlockSpec((1, s, hd), lambda i: (i, 0, 0)),
                  pl.BlockSpec((1, s, LANES), lambda i: (i, 0, 0)),
                  pl.BlockSpec((1, s // DN_GROUP, DN_HEADS, DN_GROUP), lambda i: (i, 0, 0, 0)),
                  pl.BlockSpec((1, DN_DK), lambda i: (0, 0))],
        out_specs=pl.BlockSpec((1, s, hd), lambda i: (i, 0, 0)),
        out_shape=jax.ShapeDtypeStruct((b, s, hd), BF16),
        scratch_shapes=[pltpu.VMEM((DN_HEADS, DN_DK, DN_DK), F32)],
        compiler_params=_cparams(("arbitrary",)),
        name="deltanet",
    )(dn, z, col, row, onorm_g.reshape(1, DN_DK))


def _split3(x):
    hi = x.astype(BF16).astype(F32)
    r = x - hi
    mid = r.astype(BF16).astype(F32)
    return hi, mid, r - mid


def _fox_kernel(q_ref, k_ref, v_ref, colq_ref, colk_ref, qg_ref, kg_ref, o_ref, ka_sc, va_sc, m_sc, acc_sc, *, tq):
    qi = pl.program_id(1)
    s = k_ref.shape[1]
    dh = FOX_DH

    @pl.when(qi == 0)
    def _():
        lane = lax.broadcasted_iota(jnp.int32, (s, LANES), 1)
        ones_col = jnp.where(lane == 0, 1.0, 0.0).astype(BF16)
        for h in range(FOX_HEADS):
            cols = slice(h * dh, (h + 1) * dh)
            kf = k_ref[0, :, cols].astype(F32)
            kn = kf * lax.rsqrt(jnp.mean(kf * kf, axis=-1, keepdims=True) + EPS) * kg_ref[...]
            ka_sc[h, :, :dh] = kn.astype(BF16)
            hi, mid, lo = _split3(colk_ref[0, :, 8 + h:9 + h])
            ext = jnp.where(lane == 0, -hi, jnp.where(lane == 1, -mid, jnp.where(lane == 2, -lo,
                            jnp.where(lane < 6, 1.0, 0.0))))
            ka_sc[h, :, dh:] = ext.astype(BF16)
            va_sc[h, :, :dh] = v_ref[0, :, cols]
            va_sc[h, :, dh:] = ones_col

    lane = lax.broadcasted_iota(jnp.int32, (tq, LANES), 1)
    qa = []
    for h in range(FOX_HEADS):
        cols = slice(h * dh, (h + 1) * dh)
        qf = q_ref[0, :, cols].astype(F32)
        qn = qf * lax.rsqrt(jnp.mean(qf * qf, axis=-1, keepdims=True) + EPS) * qg_ref[...] * (dh ** -0.5)
        hi, mid, lo = _split3(colq_ref[0, :, 8 + h:9 + h])
        ext = jnp.where(lane < 3, 1.0, jnp.where(lane == 3, hi, jnp.where(lane == 4, mid,
                        jnp.where(lane == 5, lo, 0.0))))
        qa.append(jnp.concatenate([qn.astype(BF16), ext.astype(BF16)], axis=1))
    m_sc[...] = jnp.full(m_sc.shape, NEG_INF, F32)
    acc_sc[...] = jnp.zeros_like(acc_sc)
    causal = lax.broadcasted_iota(jnp.int32, (tq, tq), 0) >= lax.broadcasted_iota(jnp.int32, (tq, tq), 1)

    def step(k0, masked):
        heads = range(FOX_HEADS)
        logits = [_dot_nt(qa[h], ka_sc[h, pl.ds(k0, tq), :]) for h in heads]
        if masked:
            logits = [jnp.where(causal, lg, NEG_INF) for lg in logits]
        ps, alphas = [], []
        for h in heads:
            m_old = m_sc[h]
            m_new = jnp.maximum(m_old, jnp.max(logits[h], axis=1, keepdims=True))
            m_sc[h] = m_new
            alphas.append(jnp.exp(m_old - m_new))
            ps.append(jnp.exp(logits[h] - jnp.concatenate([m_new] * (tq // LANES), axis=1)).astype(BF16))
        for h in heads:
            pv = _dot(ps[h], va_sc[h, pl.ds(k0, tq), :])
            acc_sc[h] = acc_sc[h] * jnp.concatenate([alphas[h], alphas[h]], axis=1) + pv

    def body(j, carry):
        step(pl.multiple_of(j * tq, tq), False)
        return carry

    lax.fori_loop(0, qi, body, 0)
    step(pl.multiple_of(qi * tq, tq), True)
    for h in range(FOX_HEADS):
        acc = acc_sc[h]
        o_ref[0, :, h * dh:(h + 1) * dh] = (acc[:, :dh] / acc[:, dh:dh + 1]).astype(o_ref.dtype)


def _fox(fox, col, qg, kg, tq):
    b, s, _ = fox.shape
    hd = FOX_HEADS * FOX_DH
    return pl.pallas_call(
        functools.partial(_fox_kernel, tq=tq),
        grid=(b, s // tq),
        in_specs=[pl.BlockSpec((1, tq, hd), lambda i, j: (i, j, 0)),
                  pl.BlockSpec((1, s, hd), lambda i, j: (i, 0, 1)),
                  pl.BlockSpec((1, s, hd), lambda i, j: (i, 0, 2)),
                  pl.BlockSpec((1, tq, LANES), lambda i, j: (i, j, 0)),
                  pl.BlockSpec((1, s, LANES), lambda i, j: (i, 0, 0)),
                  pl.BlockSpec((1, FOX_DH), lambda i, j: (0, 0)),
                  pl.BlockSpec((1, FOX_DH), lambda i, j: (0, 0))],
        out_specs=pl.BlockSpec((1, tq, hd), lambda i, j: (i, j, 0)),
        out_shape=jax.ShapeDtypeStruct((b, s, hd), BF16),
        scratch_shapes=[pltpu.VMEM((FOX_HEADS, s, 2 * FOX_DH), BF16), pltpu.VMEM((FOX_HEADS, s, 2 * FOX_DH), BF16),
                        pltpu.VMEM((FOX_HEADS, tq, LANES), F32), pltpu.VMEM((FOX_HEADS, tq, 2 * FOX_DH), F32)],
        compiler_params=_cparams(("arbitrary", "arbitrary")),
        name="fox",
    )(fox, fox, fox, col, col, qg.reshape(1, FOX_DH), kg.reshape(1, FOX_DH))


def _lru_kernel(x_ref, gate_ref, cw_ref, cb_ref, wr_ref, br_ref, wi_ref, bi_ref, lam_ref, o_ref, a_sc, b_sc):
    s = x_ref.shape[1]
    x = _causal_conv(x_ref[0].astype(F32), cw_ref) + cb_ref[...]
    xb = x.astype(BF16)
    r = _sigmoid(_dot(xb, wr_ref[0].astype(BF16)) + br_ref[...])
    i = _sigmoid(_dot(xb, wi_ref[0].astype(BF16)) + bi_ref[...])
    log_a = (-LRU_C) * r * _softplus(-lam_ref[...])
    a = jnp.exp(log_a)
    bb = jnp.sqrt(1.0 - a * a) * (i * x)
    nt = s // SUBLANES
    a3 = a.reshape(nt, SUBLANES, a.shape[1])
    b3 = bb.reshape(nt, SUBLANES, a.shape[1])
    sub = lax.broadcasted_iota(jnp.int32, a3.shape, 1)
    d = 1
    while d < SUBLANES:
        keep = sub >= d
        b3 = a3 * jnp.where(keep, pltpu.roll(b3, d, axis=1), 0.0) + b3
        a3 = a3 * jnp.where(keep, pltpu.roll(a3, d, axis=1), 1.0)
        d *= 2
    a_sc[...] = a3.reshape(s, a.shape[1])
    b_sc[...] = b3.reshape(s, a.shape[1])
    at = a_sc[pl.ds(SUBLANES - 1, nt, stride=SUBLANES), :]
    bt = b_sc[pl.ds(SUBLANES - 1, nt, stride=SUBLANES), :]
    d = 1
    while d < nt:
        bt = at * _shift_rows(bt, d, 0.0) + bt
        at = at * _shift_rows(at, d, 1.0)
        d *= 2
    h_prev = _shift_rows(bt, 1, 0.0)
    bb = (a3 * h_prev[:, None, :] + b3).reshape(s, a.shape[1])
    o_ref[0] = (bb * gate_ref[0].astype(F32)).astype(o_ref.dtype)


def _lru(lx, lg, conv_w, conv_b, wr, br, wi, bi, lam):
    b, s, wd = lx.shape
    blk = wd // LRU_BLOCKS
    vec = lambda i, j: (0, j)
    return pl.pallas_call(
        _lru_kernel,
        grid=(b, LRU_BLOCKS),
        in_specs=[pl.BlockSpec((1, s, blk), lambda i, j: (i, 0, j)),
                  pl.BlockSpec((1, s, blk), lambda i, j: (i, 0, j)),
                  pl.BlockSpec((CONV_WIDTH, blk), vec),
                  pl.BlockSpec((1, blk), vec),
                  pl.BlockSpec((1, blk, blk), lambda i, j: (j, 0, 0)),
                  pl.BlockSpec((1, blk), vec),
                  pl.BlockSpec((1, blk, blk), lambda i, j: (j, 0, 0)),
                  pl.BlockSpec((1, blk), vec),
                  pl.BlockSpec((1, blk), vec)],
        out_specs=pl.BlockSpec((1, s, blk), lambda i, j: (i, 0, j)),
        out_shape=jax.ShapeDtypeStruct((b, s, wd), BF16),
        scratch_shapes=[pltpu.VMEM((s, blk), F32), pltpu.VMEM((s, blk), F32)],
        compiler_params=_cparams(("arbitrary", "arbitrary")),
        name="rglru",
    )(lx, lg, conv_w, conv_b.reshape(1, wd), wr, br.reshape(1, wd), wi, bi.reshape(1, wd), lam.reshape(1, wd))


def _sgu_kernel(u_ref, v_ref, g_ref, w_ref, bt_ref, o_ref):
    tm = u_ref.shape[1]
    wd = u_ref.shape[2]
    gw = wd // SGU_GROUPS
    c = SGU_CHUNK
    v = v_ref[0].astype(F32)
    vn = (v * lax.rsqrt(jnp.mean(v * v, axis=-1, keepdims=True) + EPS) * g_ref[...]).astype(BF16)
    ri = lax.broadcasted_iota(jnp.int32, (c, c), 0)
    ci = lax.broadcasted_iota(jnp.int32, (c, c), 1)
    for g in range(SGU_GROUPS):
        wg = jnp.where(ri >= ci, w_ref[g], 0.0).astype(BF16)
        bcol = bt_ref[:, g:g + 1]
        for n in range(tm // c):
            rows = slice(n * c, (n + 1) * c)
            cols = slice(g * gw, (g + 1) * gw)
            mixed = _dot(wg, vn[rows, cols]) + bcol
            o_ref[0, rows, cols] = (u_ref[0, rows, cols].astype(F32) * mixed).astype(o_ref.dtype)


def _sgu(su, sv, g_norm, w_s, b_s, tm):
    b, s, wd = su.shape
    return pl.pallas_call(
        _sgu_kernel,
        grid=(b, s // tm),
        in_specs=[pl.BlockSpec((1, tm, wd), lambda i, j: (i, j, 0)),
                  pl.BlockSpec((1, tm, wd), lambda i, j: (i, j, 0)),
                  pl.BlockSpec((1, wd), lambda i, j: (0, 0)),
                  pl.BlockSpec((SGU_GROUPS, SGU_CHUNK, SGU_CHUNK), lambda i, j: (0, 0, 0)),
                  pl.BlockSpec((SGU_CHUNK, SGU_GROUPS), lambda i, j: (0, 0))],
        out_specs=pl.BlockSpec((1, tm, wd), lambda i, j: (i, j, 0)),
        out_shape=jax.ShapeDtypeStruct((b, s, wd), BF16),
        compiler_params=_cparams(("arbitrary", "arbitrary")),
        name="sgu",
    )(su, sv, g_norm.reshape(1, wd), w_s, b_s.T)


def _routing(logits, rb):
    ne, tm = logits.shape
    row = lax.broadcasted_iota(jnp.int32, (ne, tm), 0)
    row_f = row.astype(F32)
    ex = jnp.exp(logits - jnp.max(logits, axis=0, keepdims=True))
    probs = ex / jnp.sum(ex, axis=0, keepdims=True)
    sel = probs + rb

    def top2(vals, idx):
        m1 = jnp.max(vals, axis=0, keepdims=True)
        i1 = jnp.min(jnp.where(vals == m1, idx, float(ne)), axis=0, keepdims=True)
        rest = jnp.where(idx == i1, NEG_INF, vals)
        m2 = jnp.max(rest, axis=0, keepdims=True)
        return m1, i1, m2, rest

    best = None
    g_idx = None
    grp = row // EXPERTS_PER_GROUP
    for g in range(ne // EXPERTS_PER_GROUP):
        m1, _, m2, _ = top2(jnp.where(grp == g, sel, NEG_INF), row_f)
        score = m1 + m2
        if g == 0:
            best, g_idx = score, jnp.zeros((1, tm), jnp.int32)
        else:
            upd = score > best
            best = jnp.where(upd, score, best)
            g_idx = jnp.where(upd, g, g_idx)
    _, i1, m2, rest = top2(jnp.where(grp == g_idx, sel, NEG_INF), row_f)
    i2 = jnp.min(jnp.where(rest == m2, row_f, float(ne)), axis=0, keepdims=True)
    p1 = jnp.sum(jnp.where(row_f == i1, probs, 0.0), axis=0, keepdims=True)
    p2 = jnp.sum(jnp.where(row_f == i2, probs, 0.0), axis=0, keepdims=True)
    den = p1 + p2
    first_lo = i1 < i2
    a = jnp.where(first_lo, i1, i2) - EXPERTS_PER_GROUP * g_idx.astype(F32)
    bhi = jnp.where(first_lo, i2, i1) - EXPERTS_PER_GROUP * g_idx.astype(F32)
    pair = a * (7.0 - a) * 0.5 + (bhi - a - 1.0)
    cls = PAIRS_PER_GROUP * g_idx.astype(F32) + pair
    w_lo = jnp.where(first_lo, p1, p2) / den
    w_hi = jnp.where(first_lo, p2, p1) / den
    return cls, w_lo, w_hi


PAIRS_PER_GROUP = EXPERTS_PER_GROUP * (EXPERTS_PER_GROUP - 1) // 2
N_CLASSES = (N_EXPERTS // EXPERTS_PER_GROUP) * PAIRS_PER_GROUP
_PAIRS = [(a, b) for a in range(EXPERTS_PER_GROUP) for b in range(a + 1, EXPERTS_PER_GROUP)]
CLASS_E_LO = [EXPERTS_PER_GROUP * (c // PAIRS_PER_GROUP) + _PAIRS[c % PAIRS_PER_GROUP][0] for c in range(N_CLASSES)]
CLASS_E_HI = [EXPERTS_PER_GROUP * (c // PAIRS_PER_GROUP) + _PAIRS[c % PAIRS_PER_GROUP][1] for c in range(N_CLASSES)]
CLASS_ROWS = 32
ROW_WORDS = 640


def _pack_halves(x):
    kk = x.shape[1] // 2
    lo = lax.bitcast_convert_type(x[:, :kk].astype(BF16).astype(F32), jnp.uint32)
    hi = lax.bitcast_convert_type(x[:, kk:].astype(BF16).astype(F32), jnp.uint32)
    return lax.bitcast_convert_type((lo >> 16) | hi, jnp.int32)


def _unpack_halves(p):
    u = lax.bitcast_convert_type(p, jnp.uint32)
    lo = lax.bitcast_convert_type(u << 16, F32)
    hi = lax.bitcast_convert_type(u & jnp.uint32(0xFFFF0000), F32)
    return jnp.concatenate([lo.astype(BF16), hi.astype(BF16)], axis=1)


def _outproj_kernel(a_ref, b_ref, x_ref, w_ref, gt_ref, g2_ref, sc_ref, sh_ref, rwt_ref, rb_ref,
                    x1_ref, rows_ref, route_ref, cnt_ref, carry_sc):
    first = (pl.program_id(0) == 0) & (pl.program_id(1) == 0)

    @pl.when(first)
    def _():
        carry_sc[...] = jnp.zeros_like(carry_sc)

    half = a_ref.shape[2]
    tm = a_ref.shape[1]
    y = _dot(a_ref[0], w_ref[:half, :]) + _dot(b_ref[0], w_ref[half:, :])
    x1 = x_ref[0] + gt_ref[0] * y
    x1_ref[0] = x1
    h2 = _prenorm(x1, g2_ref[...], sc_ref[0], sh_ref[0])
    rows_ref[0, :, :ROW_WORDS - LANES] = _pack_halves(h2)
    cls, w_lo, w_hi = _routing(_dot_nt(rwt_ref[...], h2.astype(BF16)), rb_ref[...])
    r128 = lax.broadcasted_iota(jnp.int32, (LANES, tm), 0)
    wts = jnp.where(r128 == 0, w_lo, jnp.where(r128 == 1, w_hi, 0.0))
    rows_ref[0, :, ROW_WORDS - LANES:] = lax.bitcast_convert_type(wts.T, jnp.int32)
    crow = lax.broadcasted_iota(jnp.int32, (carry_sc.shape[0], tm), 0).astype(F32)
    onehot = jnp.where(crow == cls, 1.0, 0.0)
    earlier = lax.broadcasted_iota(jnp.int32, (tm, tm), 0) < lax.broadcasted_iota(jnp.int32, (tm, tm), 1)
    prefix = _dot(onehot.astype(BF16), jnp.where(earlier, 1.0, 0.0).astype(BF16)) + carry_sc[:, 0:1]
    rank = jnp.sum(onehot * prefix, axis=0, keepdims=True)
    r8 = lax.broadcasted_iota(jnp.int32, (8, tm), 0)
    route_ref[0] = jnp.where(r8 == 0, cls, jnp.where(r8 == 1, rank, 0.0))
    carry_sc[...] = carry_sc[...] + jnp.sum(onehot, axis=1, keepdims=True)
    cnt_ref[...] = carry_sc[...]


def _outproj(a, bb, x, w, gt, g2, sc2, sh2, rw, rb, tm):
    b, s, d = x.shape
    half = a.shape[2]
    row = lambda i, j: (i, 0, 0)
    tok = lambda i, j: (i, j, 0)
    const = lambda i, j: (0, 0)
    return pl.pallas_call(
        _outproj_kernel,
        grid=(b, s // tm),
        in_specs=[pl.BlockSpec((1, tm, half), tok),
                  pl.BlockSpec((1, tm, half), tok),
                  pl.BlockSpec((1, tm, d), tok),
                  pl.BlockSpec((2 * half, d), const),
                  pl.BlockSpec((1, 1, d), row),
                  pl.BlockSpec((1, d), const),
                  pl.BlockSpec((1, 1, d), row),
                  pl.BlockSpec((1, 1, d), row),
                  pl.BlockSpec((N_EXPERTS, d), const),
                  pl.BlockSpec((N_EXPERTS, 1), const)],
        out_specs=(pl.BlockSpec((1, tm, d), tok),
                   pl.BlockSpec((1, tm, ROW_WORDS), tok),
                   pl.BlockSpec((1, 8, tm), lambda i, j: (i, 0, j)),
                   pl.BlockSpec((CLASS_ROWS, LANES), const)),
        out_shape=(jax.ShapeDtypeStruct((b, s, d), F32),
                   jax.ShapeDtypeStruct((b, s, ROW_WORDS), jnp.int32),
                   jax.ShapeDtypeStruct((b, 8, s), F32),
                   jax.ShapeDtypeStruct((CLASS_ROWS, LANES), F32)),
        scratch_shapes=[pltpu.VMEM((CLASS_ROWS, LANES), F32)],
        compiler_params=_cparams(("arbitrary", "arbitrary")),
        name="outproj_router",
    )(a, bb, x, w, gt, g2.reshape(1, d), sc2, sh2, rw, rb)


MOE_ROWS = 256
SC_CHUNK = 128


def _sc_workers():
    info = plsc.get_sparse_core_info()
    return info.num_cores, info.num_cores * info.num_subcores


def _sc_scatter_rows(rows, idx3, n_out):
    nw, k, ch = idx3.shape
    width = rows.shape[1]
    nc, _ = _sc_workers()
    mesh = plsc.VectorSubcoreMesh(core_axis_name="c", subcore_axis_name="s")

    @functools.partial(
        pl.kernel, mesh=mesh,
        out_type=jax.ShapeDtypeStruct((n_out, width), rows.dtype),
        scratch_types=[pltpu.VMEM((k, ch), jnp.int32), pltpu.VMEM((ch, width), rows.dtype), pltpu.SemaphoreType.DMA],
        name="moe_dispatch")
    def kern(rows_hbm, idx_hbm, out_hbm, idx_v, rows_v, sem):
        wid = lax.axis_index("s") * nc + lax.axis_index("c")
        pltpu.sync_copy(idx_hbm.at[wid], idx_v)

        @pl.loop(0, k)
        def _(j):
            pltpu.sync_copy(rows_hbm.at[pl.ds((wid * k + j) * ch, ch)], rows_v)
            pltpu.async_copy(rows_v, out_hbm.at[idx_v.at[j]], sem).wait()

    return kern(rows, idx3)


def _sc_gather_rows(table, idx3):
    nw, k, ch = idx3.shape
    width = table.shape[1]
    nc, _ = _sc_workers()
    mesh = plsc.VectorSubcoreMesh(core_axis_name="c", subcore_axis_name="s")

    @functools.partial(
        pl.kernel, mesh=mesh,
        out_type=jax.ShapeDtypeStruct((nw * k * ch, width), table.dtype),
        scratch_types=[pltpu.VMEM((k, ch), jnp.int32), pltpu.VMEM((ch, width), table.dtype), pltpu.SemaphoreType.DMA],
        name="moe_combine")
    def kern(table_hbm, idx_hbm, out_hbm, idx_v, rows_v, sem):
        wid = lax.axis_index("s") * nc + lax.axis_index("c")
        pltpu.sync_copy(idx_hbm.at[wid], idx_v)

        @pl.loop(0, k)
        def _(j):
            pltpu.async_copy(table_hbm.at[idx_v.at[j]], rows_v, sem).wait()
            pltpu.sync_copy(rows_v, out_hbm.at[pl.ds((wid * k + j) * ch, ch)])

    return kern(table, idx3)


def _experts_kernel(elo_ref, ehi_ref, nused_ref, x_ref, wg0_ref, wu0_ref, wd0_ref, wg1_ref, wu1_ref, wd1_ref, y_ref):
    @pl.when(pl.program_id(0) < nused_ref[0])
    def _():
        blk = x_ref[...]
        feat = ROW_WORDS - LANES
        h = _unpack_halves(blk[:, :feat])
        wts = lax.bitcast_convert_type(blk[:, feat:], F32)
        y = None
        for col, (wg_ref, wu_ref, wd_ref) in enumerate(((wg0_ref, wu0_ref, wd0_ref), (wg1_ref, wu1_ref, wd1_ref))):
            act = (_silu(_dot(h, wg_ref[0, 0].astype(BF16))) * _dot(h, wu_ref[0, 0].astype(BF16))
                   * wts[:, col:col + 1])
            part = _dot(act.astype(BF16), wd_ref[0, 0].astype(BF16))
            y = part if y is None else y + part
        y_ref[...] = _pack_halves(y)


def _experts(xs, blk_elo, blk_ehi, nused, wg, wu, wd, layer):
    n_rows = xs.shape[0]
    _, ne, d, f = wg.shape
    nblk = n_rows // MOE_ROWS
    rows = lambda i, elo, ehi, nu: (jnp.minimum(i, nu[0] - 1), 0)
    lo = lambda i, elo, ehi, nu: (layer, elo[i], 0, 0)
    hi = lambda i, elo, ehi, nu: (layer, ehi[i], 0, 0)
    return pl.pallas_call(
        _experts_kernel,
        grid_spec=pltpu.PrefetchScalarGridSpec(
            num_scalar_prefetch=3,
            grid=(nblk,),
            in_specs=[pl.BlockSpec((MOE_ROWS, ROW_WORDS), rows),
                      pl.BlockSpec((1, 1, d, f), lo), pl.BlockSpec((1, 1, d, f), lo), pl.BlockSpec((1, 1, f, d), lo),
                      pl.BlockSpec((1, 1, d, f), hi), pl.BlockSpec((1, 1, d, f), hi), pl.BlockSpec((1, 1, f, d), hi)],
            out_specs=pl.BlockSpec((MOE_ROWS, d // 2), lambda i, elo, ehi, nu: (i, 0))),
        out_shape=jax.ShapeDtypeStruct((n_rows, d // 2), jnp.int32),
        compiler_params=_cparams(("arbitrary",)),
        name="moe_experts",
    )(blk_elo, blk_ehi, nused, xs, wg, wu, wd, wg, wu, wd)


def _residual_kernel(x_ref, m_ref, gt_ref, o_ref):
    o_ref[0] = x_ref[0] + gt_ref[0] * _unpack_halves(m_ref[0]).astype(F32)


def _residual(x1, moe_p, gt, tm):
    b, s, d = x1.shape
    tok = lambda i, j: (i, j, 0)
    return pl.pallas_call(
        _residual_kernel,
        grid=(b, s // tm),
        in_specs=[pl.BlockSpec((1, tm, d), tok), pl.BlockSpec((1, tm, d // 2), tok),
                  pl.BlockSpec((1, 1, d), lambda i, j: (i, 0, 0))],
        out_specs=pl.BlockSpec((1, tm, d), tok),
        out_shape=jax.ShapeDtypeStruct((b, s, d), F32),
        compiler_params=_cparams(("arbitrary", "arbitrary")),
        name="moe_residual",
    )(x1, moe_p, gt)


def _moe(rows, route, counts, wg, wu, wd, layer):
    b, s, _ = rows.shape
    d = wg.shape[2]
    t = b * s
    _, nw = _sc_workers()
    cnt = counts[:N_CLASSES, 0].astype(jnp.int32)
    padded = ((cnt + MOE_ROWS - 1) // MOE_ROWS) * MOE_ROWS
    ends = jnp.cumsum(padded)
    n_rows = t + N_CLASSES * MOE_ROWS
    nblk = n_rows // MOE_ROWS
    cls = route[:, 0, :].reshape(t).astype(jnp.int32)
    rank = route[:, 1, :].reshape(t).astype(jnp.int32)
    pos = (ends - padded)[cls] + rank
    idx3 = pos.reshape(nw, t // (nw * SC_CHUNK), SC_CHUNK)
    nused = (ends[-1] // MOE_ROWS).reshape(1)
    blk_cls = jnp.sum((jnp.arange(nblk, dtype=jnp.int32)[:, None] * MOE_ROWS) >= ends[None, :], axis=1)
    blk_cls = jnp.minimum(blk_cls, blk_cls[jnp.maximum(nused[0] - 1, 0)])
    blk_elo = jnp.asarray(CLASS_E_LO, jnp.int32)[blk_cls]
    blk_ehi = jnp.asarray(CLASS_E_HI, jnp.int32)[blk_cls]
    xs = _sc_scatter_rows(rows.reshape(t, ROW_WORDS), idx3, n_rows)
    ys = _experts(xs, blk_elo, blk_ehi, nused, wg, wu, wd, layer)
    return _sc_gather_rows(ys, idx3).reshape(b, s, d // 2)


def kernel(x, c, ada_w, ada_b, norm1_g, norm2_g, ev_w_in, ev_conv_w, ev_dn_a_log, ev_dn_dt_bias, ev_dn_onorm_g, ev_fox_f_bias, ev_fox_qnorm_g, ev_fox_knorm_g, ev_w_out, od_w_in, od_conv_w, od_conv_b, od_lru_wr, od_lru_br, od_lru_wi, od_lru_bi, od_lru_lambda, od_sgu_norm_g, od_sgu_w, od_sgu_b, od_w_out, router_w, router_b, moe_w_gate, moe_w_up, moe_w_down):
    b, s, d = x.shape
    depth = ada_w.shape[0]
    tm = min(512, s)
    mod = _adaln(c, ada_w, ada_b).reshape(depth, b, 6, 1, d)
    rw = router_w.T.astype(BF16)
    rb = router_b.reshape(N_EXPERTS, 1)

    pending = None
    for layer in range(depth):
        sh1, sc1, gt1, sh2, sc2, gt2 = (mod[layer, :, k] for k in range(6))
        i = layer // 2
        if layer % 2 == 0:
            w = ev_w_in[i]
            nq = 3 * DN_HEADS * DN_DK
            nz = DN_HEADS * DN_DK
            nf = 3 * FOX_HEADS * FOX_DH
            o_a = nq + nz
            o_f = o_a + 2 * DN_HEADS
            o_ff = o_f + nf
            small_w = jnp.concatenate([w[:, o_a:o_f], w[:, o_ff:o_ff + FOX_HEADS]], axis=1)
            small_w = jnp.pad(small_w, ((0, 0), (0, LANES - small_w.shape[1])))
            w_all = jnp.concatenate([w[:, :o_a], w[:, o_f:o_ff], small_w], axis=1).astype(BF16)
            splits = ((0, nq), (nq, o_a), (o_a, o_a + nf), (o_a + nf, o_a + nf + LANES))
            res = _inproj(x, norm1_g[layer], sc1, sh1, w_all, splits,
                          (BF16, BF16, BF16, F32), (None,) * 4, tm, pending, ev_conv_w[i])
            if pending is not None:
                x, res = res[0], res[1:]
            dn, z, fox, small = res
            small_t = jnp.swapaxes(small[:, :, :16], 1, 2)
            zeros4 = jnp.zeros((4,), F32)
            mul = jnp.concatenate([-jnp.exp(ev_dn_a_log[i]), zeros4, -jnp.ones((4,), F32), zeros4])
            bias = jnp.concatenate([ev_dn_dt_bias[i], zeros4, ev_fox_f_bias[i], zeros4])
            pr = jnp.stack([mul, bias], axis=1)
            pc = jnp.pad(jnp.stack([mul, bias], axis=0), ((0, 0), (0, LANES - 16)))
            col, row = _gates(small, small_t, pc, pr)
            row_dn = row[:, :DN_HEADS].reshape(b, DN_HEADS, s // DN_GROUP, DN_GROUP).transpose(0, 2, 1, 3)
            o_dn = _deltanet(dn, z, col, row_dn, ev_dn_onorm_g[i])
            o_fox = _fox(fox, col, ev_fox_qnorm_g[i], ev_fox_knorm_g[i], min(512, s))
            mix_a, mix_b, w_out = o_dn, o_fox, ev_w_out[i]
        else:
            lw = od_lru_wr.shape[-1] * LRU_BLOCKS
            splits = ((0, lw), (lw, 2 * lw), (2 * lw, 2 * lw + od_sgu_w.shape[-1] * SGU_GROUPS),
                      (2 * lw + od_sgu_w.shape[-1] * SGU_GROUPS, od_w_in.shape[-1]))
            res = _inproj(x, norm1_g[layer], sc1, sh1, od_w_in[i].astype(BF16), splits,
                          (BF16,) * 4, (None, "gelu", "gelu", "gelu"), tm, pending)
            if pending is not None:
                x, res = res[0], res[1:]
            lx, lg, su, sv = res
            o_lru = _lru(lx, lg, od_conv_w[i], od_conv_b[i], od_lru_wr[i], od_lru_br[i], od_lru_wi[i],
                         od_lru_bi[i], od_lru_lambda[i])
            o_sgu = _sgu(su, sv, od_sgu_norm_g[i], od_sgu_w[i], od_sgu_b[i], tm)
            mix_a, mix_b, w_out = o_lru, o_sgu, od_w_out[i]
        x1, rows, route, counts = _outproj(mix_a, mix_b, x, w_out.astype(BF16), gt1, norm2_g[layer], sc2, sh2,
                                           rw, rb, tm)
        x = x1
        pending = (_moe(rows, route, counts, moe_w_gate, moe_w_up, moe_w_down, layer), gt2)
    return _residual(x, *pending, tm)
```

```python
import functools

import jax
import jax.numpy as jnp
from jax import lax
from jax.experimental import pallas as pl
from jax.experimental.pallas import tpu as pltpu
from jax.experimental.pallas import tpu_sc as plsc

F32 = jnp.float32
BF16 = jnp.bfloat16
EPS = 1e-6
NEG_INF = float("-inf")

DN_HEADS = 4
DN_DK = 128
DN_CHUNK = 64
CONV_WIDTH = 4
FOX_HEADS = 4
FOX_DH = 128
LRU_BLOCKS = 4
LRU_C = 8.0
SGU_GROUPS = 4
SGU_CHUNK = 128
N_EXPERTS = 16
EXPERTS_PER_GROUP = 4
LANES = 128
SUBLANES = 8

VMEM_LIMIT = 48 * 1024 * 1024


def _cparams(sem):
    return pltpu.CompilerParams(dimension_semantics=sem, vmem_limit_bytes=VMEM_LIMIT)


def _dot(a, b):
    return jnp.dot(a, b, preferred_element_type=F32)


def _dot_nt(a, b):
    return lax.dot_general(a, b, (((1,), (1,)), ((), ())), preferred_element_type=F32)


def _dot_tn(a, b):
    return lax.dot_general(a, b, (((0,), (0,)), ((), ())), preferred_element_type=F32)


def _dot_exact(a, b):
    return jnp.dot(a, b, preferred_element_type=F32, precision=lax.Precision.HIGHEST)


def _sigmoid(x):
    return 1.0 / (1.0 + jnp.exp(-x))


def _silu(x):
    return x * _sigmoid(x)


def _softplus(x):
    return jnp.maximum(x, 0.0) + jnp.log(1.0 + jnp.exp(-jnp.abs(x)))


def _gelu_tanh(x):
    c = 0.7978845608028654
    return 0.5 * x * (1.0 + jnp.tanh(c * (x + 0.044715 * (x * x * x))))


def _prenorm(x, g, scale, shift):
    ms = jnp.mean(x * x, axis=-1, keepdims=True)
    return (x * lax.rsqrt(ms + EPS) * g) * (1.0 + scale) + shift


def _shift_rows(x, d, fill=0.0):
    rows = lax.broadcasted_iota(jnp.int32, x.shape, 0)
    return jnp.where(rows >= d, pltpu.roll(x, d, axis=0), fill)


def _causal_conv(x, w_ref):
    k_w = w_ref.shape[0]
    acc = x * w_ref[k_w - 1:k_w, :]
    for d in range(1, k_w):
        acc = acc + _shift_rows(x, d) * w_ref[k_w - 1 - d:k_w - d, :]
    return acc


def _adaln_kernel(c_ref, w_ref, b_ref, o_ref):
    c = c_ref[...]
    ca = _silu(c).astype(BF16)
    o_ref[0] = _dot(ca, w_ref[0].astype(BF16)) + b_ref[0]


def _adaln(c, ada_w, ada_b):
    depth, d, n = ada_w.shape
    b = c.shape[0]
    tn = 1536
    return pl.pallas_call(
        _adaln_kernel,
        grid=(depth, n // tn),
        in_specs=[pl.BlockSpec((b, d), lambda l, j: (0, 0)),
                  pl.BlockSpec((1, d, tn), lambda l, j: (l, 0, j)),
                  pl.BlockSpec((1, 1, tn), lambda l, j: (l, 0, j))],
        out_specs=pl.BlockSpec((1, b, tn), lambda l, j: (l, 0, j)),
        out_shape=jax.ShapeDtypeStruct((depth, b, n), F32),
        compiler_params=_cparams(("arbitrary", "arbitrary")),
        name="adaln",
    )(c, ada_w, ada_b.reshape(depth, 1, n))


CONV_HALO = 8
CONV_SLAB = 512


def _inproj_kernel(*refs, col_splits, acts, pending, conv, sgu):
    refs = list(refs)
    x_ref = refs.pop(0)
    if pending:
        m_ref, gtm_ref = refs.pop(0), refs.pop(0)
    g_ref, sc_ref, sh_ref, w_ref = (refs.pop(0) for _ in range(4))
    if conv:
        cw_ref = refs.pop(0)
        halo_sc = refs.pop()
    if sgu:
        sg_ref, sw_ref, sbt_ref = (refs.pop(0) for _ in range(3))
    if pending:
        xo_ref = refs.pop(0)
        x = x_ref[0] + gtm_ref[0] * _unpack_halves(m_ref[0]).astype(F32)
        xo_ref[0] = x
    else:
        x = x_ref[0]
    out_refs = refs
    h = _prenorm(x, g_ref[...], sc_ref[0], sh_ref[0]).astype(BF16)
    tm = x.shape[0]
    if conv:
        @pl.when(pl.program_id(1) == 0)
        def _():
            halo_sc[0:CONV_HALO, :] = jnp.zeros((CONV_HALO, halo_sc.shape[1]), F32)

        c0, c1 = col_splits[0]
        for s0 in range(c0, c1, CONV_SLAB):
            halo_sc[CONV_HALO:, s0 - c0:s0 - c0 + CONV_SLAB] = _dot(h, w_ref[:, s0:s0 + CONV_SLAB])
    n_plain = len(col_splits) - (2 if sgu else 0)
    for k, (o_ref, (c0, c1), act) in enumerate(zip(out_refs[:n_plain], col_splits[:n_plain], acts)):
        if conv and k == 0:
            continue
        p = _dot(h, w_ref[:, c0:c1])
        if act == "gelu":
            p = _gelu_tanh(p)
        o_ref[0] = p.astype(o_ref.dtype)
    if sgu:
        (u0, u1), (v0, v1) = col_splits[n_plain:]
        o_ref = out_refs[n_plain]
        u = _gelu_tanh(_dot(h, w_ref[:, u0:u1]))
        v = _gelu_tanh(_dot(h, w_ref[:, v0:v1]))
        vn = (v * lax.rsqrt(jnp.mean(v * v, axis=-1, keepdims=True) + EPS) * sg_ref[...]).astype(BF16)
        c = SGU_CHUNK
        gw = (u1 - u0) // SGU_GROUPS
        ri = lax.broadcasted_iota(jnp.int32, (c, c), 0)
        ci = lax.broadcasted_iota(jnp.int32, (c, c), 1)
        for g in range(SGU_GROUPS):
            wg = jnp.where(ri >= ci, sw_ref[g], 0.0).astype(BF16)
            bcol = sbt_ref[:, g:g + 1]
            for n in range(tm // c):
                rows = slice(n * c, (n + 1) * c)
                cols = slice(g * gw, (g + 1) * gw)
                mixed = _dot(wg, vn[rows, cols]) + bcol
                o_ref[0, rows, cols] = (u[rows, cols] * mixed).astype(o_ref.dtype)
    if conv:
        o_ref = out_refs[0]
        for t0 in range(0, halo_sc.shape[1], DN_DK):
            cols = slice(t0, t0 + DN_DK)
            acc = halo_sc[CONV_HALO:, cols] * cw_ref[CONV_WIDTH - 1:CONV_WIDTH, cols]
            for dd in range(1, CONV_WIDTH):
                acc = acc + halo_sc[pl.ds(CONV_HALO - dd, tm), cols] * cw_ref[CONV_WIDTH - 1 - dd:CONV_WIDTH - dd, cols]
            halo_sc[0:CONV_HALO, cols] = halo_sc[tm:tm + CONV_HALO, cols]
            yj = _silu(acc)
            if t0 < 2 * DN_HEADS * DN_DK:
                yj = yj * lax.rsqrt(jnp.sum(yj * yj, axis=-1, keepdims=True) + EPS)
            if t0 < DN_HEADS * DN_DK:
                yj = yj * (DN_DK ** -0.5)
            o_ref[0, :, cols] = yj.astype(o_ref.dtype)


def _inproj(x, g, scale, shift, w, col_splits, out_dtypes, acts, tm, pending=None, conv_w=None, sgu=None):
    b, s, d = x.shape
    n = w.shape[1]
    out_cols = list(col_splits) if sgu is None else list(col_splits[:-2]) + [col_splits[-2]]
    outs = tuple(jax.ShapeDtypeStruct((b, s, c1 - c0), dt) for (c0, c1), dt in zip(out_cols, out_dtypes))
    row = lambda i, j: (i, 0, 0)
    tok = lambda i, j: (i, j, 0)
    const = lambda i, j: (0, 0)
    in_specs = [pl.BlockSpec((1, tm, d), tok)]
    out_specs = tuple(pl.BlockSpec((1, tm, c1 - c0), tok) for (c0, c1) in out_cols)
    args = (x,)
    if pending is not None:
        in_specs += [pl.BlockSpec((1, tm, d // 2), tok), pl.BlockSpec((1, 1, d), row)]
        out_specs = (pl.BlockSpec((1, tm, d), tok),) + out_specs
        outs = (jax.ShapeDtypeStruct((b, s, d), F32),) + outs
        args += tuple(pending)
    in_specs += [pl.BlockSpec((1, d), const), pl.BlockSpec((1, 1, d), row), pl.BlockSpec((1, 1, d), row),
                 pl.BlockSpec((d, n), const)]
    args += (g.reshape(1, d), scale, shift, w)
    scratch = []
    if conv_w is not None:
        in_specs.append(pl.BlockSpec(conv_w.shape, const))
        args += (conv_w,)
        scratch.append(pltpu.VMEM((CONV_HALO + tm, conv_w.shape[1]), F32))
    if sgu is not None:
        g_norm, w_s, b_s = sgu
        in_specs += [pl.BlockSpec((1, g_norm.shape[0]), const), pl.BlockSpec(w_s.shape, lambda i, j: (0, 0, 0)),
                     pl.BlockSpec((b_s.shape[1], b_s.shape[0]), const)]
        args += (g_norm.reshape(1, -1), w_s, b_s.T)
    return pl.pallas_call(
        functools.partial(_inproj_kernel, col_splits=col_splits, acts=acts, pending=pending is not None,
                          conv=conv_w is not None, sgu=sgu is not None),
        grid=(b, s // tm),
        in_specs=in_specs,
        out_specs=out_specs,
        out_shape=outs,
        scratch_shapes=scratch,
        compiler_params=_cparams(("arbitrary", "arbitrary")),
        name="inproj",
    )(*args)


def _gates_kernel(sm_ref, smt_ref, pc_ref, pr_ref, col_ref, row_ref):
    s = sm_ref.shape[1]
    blk = LANES
    ri = lax.broadcasted_iota(jnp.int32, (blk, blk), 0)
    ci = lax.broadcasted_iota(jnp.int32, (blk, blk), 1)
    same_chunk = (ri // DN_CHUNK) == (ci // DN_CHUNK)
    tril = jnp.where(ri >= ci, 1.0, 0.0)
    tril_loc = jnp.where(same_chunk, tril, 0.0)
    triu_loc = jnp.where(same_chunk & (ri <= ci), 1.0, 0.0)
    lane = lax.broadcasted_iota(jnp.int32, (blk, LANES), 1)
    carry = jnp.zeros((1, LANES), F32)
    for j in range(s // blk):
        rows = slice(j * blk, (j + 1) * blk)
        xc = sm_ref[0, rows, :] + pc_ref[1:2, :]
        dec = pc_ref[0:1, :] * _softplus(jnp.where(lane < 4, xc, -xc))
        cum_glb = _dot_exact(tril, dec) + carry
        col_ref[0, rows, :] = jnp.where(lane < 4, _dot_exact(tril_loc, dec), jnp.where(lane < 8, _sigmoid(xc), cum_glb))
        carry = cum_glb[blk - 1:blk, :]
        decr = pr_ref[:, 0:1] * _softplus(smt_ref[0, :, rows] + pr_ref[:, 1:2])
        row_ref[0, :, rows] = _dot_exact(decr, triu_loc)


def _gates(small, small_t, pc, pr):
    b, s, _ = small.shape
    return pl.pallas_call(
        _gates_kernel,
        grid=(b,),
        in_specs=[pl.BlockSpec((1, s, LANES), lambda i: (i, 0, 0)),
                  pl.BlockSpec((1, 16, s), lambda i: (i, 0, 0)),
                  pl.BlockSpec((2, LANES), lambda i: (0, 0)),
                  pl.BlockSpec((16, 2), lambda i: (0, 0))],
        out_specs=(pl.BlockSpec((1, s, LANES), lambda i: (i, 0, 0)),
                   pl.BlockSpec((1, 16, s), lambda i: (i, 0, 0))),
        out_shape=(jax.ShapeDtypeStruct((b, s, LANES), F32), jax.ShapeDtypeStruct((b, 16, s), F32)),
        compiler_params=_cparams(("arbitrary",)),
        name="gates",
    )(small, small_t, pc, pr)


DN_PACK = 4
DN_GROUP = DN_PACK * DN_CHUNK


def _blockdiag(p):
    c, wide = p.shape
    t = jnp.concatenate([p] * (wide // c), axis=0)
    rb = lax.broadcasted_iota(jnp.int32, (wide, wide), 0) // c
    cb = lax.broadcasted_iota(jnp.int32, (wide, wide), 1) // c
    return jnp.where(rb == cb, t, 0.0).astype(BF16)


def _diag_blocks(m, c):
    wide = m.shape[1]
    cb = lax.broadcasted_iota(jnp.int32, (c, wide), 1) // c
    out = m[:c]
    for j in range(1, wide // c):
        out = jnp.where(cb == j, m[j * c:(j + 1) * c], out)
    return out


def _rows_to_blocks(col, c, wide):
    cb = lax.broadcasted_iota(jnp.int32, (c, wide), 1) // c
    out = jnp.broadcast_to(col[:c], (c, wide))
    for j in range(1, wide // c):
        out = jnp.where(cb == j, col[j * c:(j + 1) * c], out)
    return out


def _packed_unit_lower_inverse(lows):
    c, wide = lows[0].shape
    ri = lax.broadcasted_iota(jnp.int32, (c, wide), 0)
    ci = lax.broadcasted_iota(jnp.int32, (c, wide), 1) % c
    eye = jnp.where(ri == ci, 1.0, 0.0)
    n = range(len(lows))
    ds = [jnp.where((ri // 16) == (ci // 16), lo, 0.0) for lo in lows]
    xs = [eye - d for d in ds]
    ps = ds
    bds = [_blockdiag(p) for p in ps]
    for _ in range(3):
        ps = [_dot(ps[i].astype(BF16), bds[i]) for i in n]
        bds = [_blockdiag(p) for p in ps]
        xs = [xs[i] + _dot(xs[i].astype(BF16), bds[i]) for i in n]
    for width in (16, 32):
        sel = ((ri // (2 * width)) == (ci // (2 * width))) & ((ri // width) != (ci // width))
        offs = [_blockdiag(jnp.where(sel, lo, 0.0)) for lo in lows]
        ts = [_dot(xs[i].astype(BF16), offs[i]) for i in n]
        bdx = [_blockdiag(x) for x in xs]
        xs = [xs[i] - _dot(ts[i].astype(BF16), bdx[i]) for i in n]
    return xs


def _deltanet_kernel(dn_ref, z_ref, col_ref, row_ref, og_ref, o_ref, st_sc):
    s = dn_ref.shape[1]
    c = DN_CHUNK
    gt = DN_GROUP
    dk = DN_DK
    hd = DN_HEADS * dk
    heads = range(DN_HEADS)
    st_sc[...] = jnp.zeros_like(st_sc)

    ri = lax.broadcasted_iota(jnp.int32, (c, gt), 0)
    ci = lax.broadcasted_iota(jnp.int32, (c, gt), 1) % c
    og = og_ref[...]

    def group(g, carry):
        r0 = pl.multiple_of(g * gt, gt)
        colg = col_ref[0, pl.ds(r0, gt), :]
        rowg = row_ref[0, g]
        hs = [slice(h * dk, (h + 1) * dk) for h in heads]
        q = [dn_ref[0, pl.ds(r0, gt), h * dk:(h + 1) * dk].astype(F32) for h in heads]
        k = [dn_ref[0, pl.ds(r0, gt), hd + h * dk:hd + (h + 1) * dk].astype(F32) for h in heads]
        v = [dn_ref[0, pl.ds(r0, gt), 2 * hd + h * dk:2 * hd + (h + 1) * dk].astype(F32) for h in heads]
        gc = [colg[:, h:h + 1] for h in heads]
        beta = [colg[:, DN_HEADS + h:DN_HEADS + h + 1] for h in heads]
        glast = [jnp.concatenate([jnp.broadcast_to(gc[h][(j + 1) * c - 1:(j + 1) * c], (c, 1))
                                  for j in range(DN_PACK)], axis=0) for h in heads]
        eg = [jnp.exp(gc[h]) for h in heads]
        kbf = [k[h].astype(BF16) for h in heads]
        both = [_dot_nt(jnp.concatenate([(k[h] * beta[h]).astype(BF16), q[h].astype(BF16)], axis=0), kbf[h])
                for h in heads]
        decay = [jnp.exp(jnp.where(ri >= ci, _rows_to_blocks(gc[h], c, gt) - rowg[h:h + 1, :], NEG_INF))
                 for h in heads]
        kk = [_diag_blocks(both[h][:gt], c) * decay[h] for h in heads]
        qk = [_blockdiag(_diag_blocks(both[h][gt:], c) * decay[h]) for h in heads]
        t_inv = _packed_unit_lower_inverse([jnp.where(ri > ci, kk[h], 0.0) for h in heads])
        rhs = [jnp.concatenate([k[h] * (beta[h] * eg[h]), v[h] * beta[h]], axis=1).astype(BF16) for h in heads]
        wu = [_dot(_blockdiag(t_inv[h]), rhs[h]).astype(BF16) for h in heads]
        qwu = [_dot(qk[h], wu[h]) for h in heads]
        qp = [(q[h] * eg[h] - qwu[h][:, :dk]).astype(BF16) for h in heads]
        kdec = [(k[h] * jnp.exp(glast[h] - gc[h])).astype(BF16) for h in heads]
        mb = [[_dot_tn(kdec[h][j * c:(j + 1) * c], wu[h][j * c:(j + 1) * c]) for h in heads]
              for j in range(DN_PACK)]
        outs = [[] for _ in heads]
        for j in range(DN_PACK):
            rows = slice(j * c, (j + 1) * c)
            state = [st_sc[h] for h in heads]
            lhs = [jnp.concatenate([qp[h][rows], mb[j][h][:, :dk].astype(BF16)], axis=0) for h in heads]
            r = [_dot(lhs[h], state[h].astype(BF16)) for h in heads]
            for h in heads:
                gl = jnp.exp(glast[h][j * c:j * c + 1])
                st_sc[h] = state[h] * gl - r[h][c:] + mb[j][h][:, dk:]
                outs[h].append(r[h][:c] + qwu[h][rows, dk:])
        for h in heads:
            o = jnp.concatenate(outs[h], axis=0)
            on = o * lax.rsqrt(jnp.mean(o * o, axis=-1, keepdims=True) + EPS) * og
            zz = z_ref[0, pl.ds(r0, gt), hs[h]].astype(F32)
            o_ref[0, pl.ds(r0, gt), hs[h]] = (on * _silu(zz)).astype(o_ref.dtype)
        return carry

    lax.fori_loop(0, s // gt, group, 0)


def _deltanet(dn, z, col, row, onorm_g):
    b, s, w3 = dn.shape
    hd = DN_HEADS * DN_DK
    return pl.pallas_call(
        _deltanet_kernel,
        grid=(b,),
        in_specs=[pl.BlockSpec((1, s, w3), lambda i: (i, 0, 0)),
                  pl.BlockSpec((1, s, hd), lambda i: (i, 0, 0)),
                  pl.BlockSpec((1, s, LANES), lambda i: (i, 0, 0)),
                  pl.BlockSpec((1, s // DN_GROUP, DN_HEADS, DN_GROUP), lambda i: (i, 0, 0, 0)),
                  pl.BlockSpec((1, DN_DK), lambda i: (0, 0))],
        out_specs=pl.BlockSpec((1, s, hd), lambda i: (i, 0, 0)),
        out_shape=jax.ShapeDtypeStruct((b, s, hd), BF16),
        scratch_shapes=[pltpu.VMEM((DN_HEADS, DN_DK, DN_DK), F32)],
        compiler_params=_cparams(("arbitrary",)),
        name="deltanet",
    )(dn, z, col, row, onorm_g.reshape(1, DN_DK))


def _split3(x):
    hi = x.astype(BF16).astype(F32)
    r = x - hi
    mid = r.astype(BF16).astype(F32)
    return hi, mid, r - mid


def _fox_kernel(q_ref, k_ref, v_ref, colq_ref, colk_ref, qg_ref, kg_ref, o_ref, ka_sc, va_sc, m_sc, acc_sc, *, tq):
    qi = pl.program_id(1)
    s = k_ref.shape[1]
    dh = FOX_DH

    def bias_lanes(col):
        lane = lax.broadcasted_iota(jnp.int32, col.shape, 1)
        hi, mid, lo = _split3(col)
        return jnp.where((lane >= 8) & (lane < 8 + FOX_HEADS), hi,
                         jnp.where((lane >= 24) & (lane < 24 + FOX_HEADS), pltpu.roll(mid, 16, axis=1),
                                   jnp.where((lane >= 40) & (lane < 40 + FOX_HEADS), pltpu.roll(lo, 32, axis=1), 0.0)))

    @pl.when(qi == 0)
    def _():
        lane = lax.broadcasted_iota(jnp.int32, (s, LANES), 1)
        ones_col = jnp.where(lane == 0, 1.0, 0.0).astype(BF16)
        ones_k = (lane >= 56) & (lane < 96) & ((lane % 16) >= 8) & ((lane % 16) < 8 + FOX_HEADS)
        ext_k = (jnp.where(ones_k, 1.0, 0.0) - bias_lanes(colk_ref[0])).astype(BF16)
        for h in range(FOX_HEADS):
            cols = slice(h * dh, (h + 1) * dh)
            kf = k_ref[0, :, cols].astype(F32)
            kn = kf * lax.rsqrt(jnp.mean(kf * kf, axis=-1, keepdims=True) + EPS) * kg_ref[...]
            ka_sc[h, :, :dh] = kn.astype(BF16)
            ka_sc[h, :, dh:] = ext_k
            va_sc[h, :, :dh] = v_ref[0, :, cols]
            va_sc[h, :, dh:] = ones_col

    lane = lax.broadcasted_iota(jnp.int32, (tq, LANES), 1)
    cq = pltpu.roll(bias_lanes(colq_ref[0]), 48, axis=1)
    qa = []
    for h in range(FOX_HEADS):
        cols = slice(h * dh, (h + 1) * dh)
        qf = q_ref[0, :, cols].astype(F32)
        qn = qf * lax.rsqrt(jnp.mean(qf * qf, axis=-1, keepdims=True) + EPS) * qg_ref[...] * (dh ** -0.5)
        mine = ((lane % 16) == 8 + h) & (lane < 96)
        ext = jnp.where(mine, jnp.where(lane < 48, 1.0, cq), 0.0)
        qa.append(jnp.concatenate([qn.astype(BF16), ext.astype(BF16)], axis=1))
    m_sc[...] = jnp.full(m_sc.shape, NEG_INF, F32)
    acc_sc[...] = jnp.zeros_like(acc_sc)
    causal = lax.broadcasted_iota(jnp.int32, (tq, tq), 0) >= lax.broadcasted_iota(jnp.int32, (tq, tq), 1)

    def step(k0, masked):
        heads = range(FOX_HEADS)
        logits = [_dot_nt(qa[h], ka_sc[h, pl.ds(k0, tq), :]) for h in heads]
        if masked:
            logits = [jnp.where(causal, lg, NEG_INF) for lg in logits]
        ps, alphas = [], []
        for h in heads:
            m_old = m_sc[h]
            m_new = jnp.maximum(m_old, jnp.max(logits[h], axis=1, keepdims=True))
            m_sc[h] = m_new
            alphas.append(jnp.exp(m_old - m_new))
            ps.append(jnp.exp(logits[h] - jnp.concatenate([m_new] * (tq // LANES), axis=1)).astype(BF16))
        for h in heads:
            pv = _dot(ps[h], va_sc[h, pl.ds(k0, tq), :])
            acc_sc[h] = acc_sc[h] * jnp.concatenate([alphas[h], alphas[h]], axis=1) + pv

    def body(j, carry):
        step(pl.multiple_of(j * tq, tq), False)
        return carry

    lax.fori_loop(0, qi, body, 0)
    step(pl.multiple_of(qi * tq, tq), True)
    for h in range(FOX_HEADS):
        acc = acc_sc[h]
        o_ref[0, :, h * dh:(h + 1) * dh] = (acc[:, :dh] / acc[:, dh:dh + 1]).astype(o_ref.dtype)


def _fox(fox, col, qg, kg, tq):
    b, s, _ = fox.shape
    hd = FOX_HEADS * FOX_DH
    return pl.pallas_call(
        functools.partial(_fox_kernel, tq=tq),
        grid=(b, s // tq),
        in_specs=[pl.BlockSpec((1, tq, hd), lambda i, j: (i, j, 0)),
                  pl.BlockSpec((1, s, hd), lambda i, j: (i, 0, 1)),
                  pl.BlockSpec((1, s, hd), lambda i, j: (i, 0, 2)),
                  pl.BlockSpec((1, tq, LANES), lambda i, j: (i, j, 0)),
                  pl.BlockSpec((1, s, LANES), lambda i, j: (i, 0, 0)),
                  pl.BlockSpec((1, FOX_DH), lambda i, j: (0, 0)),
                  pl.BlockSpec((1, FOX_DH), lambda i, j: (0, 0))],
        out_specs=pl.BlockSpec((1, tq, hd), lambda i, j: (i, j, 0)),
        out_shape=jax.ShapeDtypeStruct((b, s, hd), BF16),
        scratch_shapes=[pltpu.VMEM((FOX_HEADS, s, 2 * FOX_DH), BF16), pltpu.VMEM((FOX_HEADS, s, 2 * FOX_DH), BF16),
                        pltpu.VMEM((FOX_HEADS, tq, LANES), F32), pltpu.VMEM((FOX_HEADS, tq, 2 * FOX_DH), F32)],
        compiler_params=_cparams(("arbitrary", "arbitrary")),
        name="fox",
    )(fox, fox, fox, col, col, qg.reshape(1, FOX_DH), kg.reshape(1, FOX_DH))


def _lru_kernel(x_ref, gate_ref, cw_ref, cb_ref, wr_ref, br_ref, wi_ref, bi_ref, lam_ref, o_ref, a_sc, b_sc):
    s = x_ref.shape[1]
    x = _causal_conv(x_ref[0].astype(F32), cw_ref) + cb_ref[...]
    xb = x.astype(BF16)
    r = _sigmoid(_dot(xb, wr_ref[0].astype(BF16)) + br_ref[...])
    i = _sigmoid(_dot(xb, wi_ref[0].astype(BF16)) + bi_ref[...])
    log_a = (-LRU_C) * r * _softplus(-lam_ref[...])
    a = jnp.exp(log_a)
    bb = jnp.sqrt(1.0 - a * a) * (i * x)
    nt = s // SUBLANES
    a3 = a.reshape(nt, SUBLANES, a.shape[1])
    b3 = bb.reshape(nt, SUBLANES, a.shape[1])
    sub = lax.broadcasted_iota(jnp.int32, a3.shape, 1)
    d = 1
    while d < SUBLANES:
        keep = sub >= d
        b3 = a3 * jnp.where(keep, pltpu.roll(b3, d, axis=1), 0.0) + b3
        a3 = a3 * jnp.where(keep, pltpu.roll(a3, d, axis=1), 1.0)
        d *= 2
    a_sc[...] = a3.reshape(s, a.shape[1])
    b_sc[...] = b3.reshape(s, a.shape[1])
    at = a_sc[pl.ds(SUBLANES - 1, nt, stride=SUBLANES), :]
    bt = b_sc[pl.ds(SUBLANES - 1, nt, stride=SUBLANES), :]
    d = 1
    while d < nt:
        bt = at * _shift_rows(bt, d, 0.0) + bt
        at = at * _shift_rows(at, d, 1.0)
        d *= 2
    h_prev = _shift_rows(bt, 1, 0.0)
    bb = (a3 * h_prev[:, None, :] + b3).reshape(s, a.shape[1])
    o_ref[0] = (bb * gate_ref[0].astype(F32)).astype(o_ref.dtype)


def _lru(lx, lg, conv_w, conv_b, wr, br, wi, bi, lam):
    b, s, wd = lx.shape
    blk = wd // LRU_BLOCKS
    vec = lambda i, j: (0, j)
    return pl.pallas_call(
        _lru_kernel,
        grid=(b, LRU_BLOCKS),
        in_specs=[pl.BlockSpec((1, s, blk), lambda i, j: (i, 0, j)),
                  pl.BlockSpec((1, s, blk), lambda i, j: (i, 0, j)),
                  pl.BlockSpec((CONV_WIDTH, blk), vec),
                  pl.BlockSpec((1, blk), vec),
                  pl.BlockSpec((1, blk, blk), lambda i, j: (j, 0, 0)),
                  pl.BlockSpec((1, blk), vec),
                  pl.BlockSpec((1, blk, blk), lambda i, j: (j, 0, 0)),
                  pl.BlockSpec((1, blk), vec),
                  pl.BlockSpec((1, blk), vec)],
        out_specs=pl.BlockSpec((1, s, blk), lambda i, j: (i, 0, j)),
        out_shape=jax.ShapeDtypeStruct((b, s, wd), BF16),
        scratch_shapes=[pltpu.VMEM((s, blk), F32), pltpu.VMEM((s, blk), F32)],
        compiler_params=_cparams(("arbitrary", "arbitrary")),
        name="rglru",
    )(lx, lg, conv_w, conv_b.reshape(1, wd), wr, br.reshape(1, wd), wi, bi.reshape(1, wd), lam.reshape(1, wd))


def _routing(logits, rb):
    ne, tm = logits.shape
    row = lax.broadcasted_iota(jnp.int32, (ne, tm), 0)
    row_f = row.astype(F32)
    ex = jnp.exp(logits - jnp.max(logits, axis=0, keepdims=True))
    probs = ex / jnp.sum(ex, axis=0, keepdims=True)
    sel = probs + rb

    def top2(vals, idx):
        m1 = jnp.max(vals, axis=0, keepdims=True)
        i1 = jnp.min(jnp.where(vals == m1, idx, float(ne)), axis=0, keepdims=True)
        rest = jnp.where(idx == i1, NEG_INF, vals)
        m2 = jnp.max(rest, axis=0, keepdims=True)
        return m1, i1, m2, rest

    best = None
    g_idx = None
    grp = row // EXPERTS_PER_GROUP
    for g in range(ne // EXPERTS_PER_GROUP):
        m1, _, m2, _ = top2(jnp.where(grp == g, sel, NEG_INF), row_f)
        score = m1 + m2
        if g == 0:
            best, g_idx = score, jnp.zeros((1, tm), jnp.int32)
        else:
            upd = score > best
            best = jnp.where(upd, score, best)
            g_idx = jnp.where(upd, g, g_idx)
    _, i1, m2, rest = top2(jnp.where(grp == g_idx, sel, NEG_INF), row_f)
    i2 = jnp.min(jnp.where(rest == m2, row_f, float(ne)), axis=0, keepdims=True)
    p1 = jnp.sum(jnp.where(row_f == i1, probs, 0.0), axis=0, keepdims=True)
    p2 = jnp.sum(jnp.where(row_f == i2, probs, 0.0), axis=0, keepdims=True)
    den = p1 + p2
    first_lo = i1 < i2
    a = jnp.where(first_lo, i1, i2) - EXPERTS_PER_GROUP * g_idx.astype(F32)
    bhi = jnp.where(first_lo, i2, i1) - EXPERTS_PER_GROUP * g_idx.astype(F32)
    pair = a * (7.0 - a) * 0.5 + (bhi - a - 1.0)
    cls = PAIRS_PER_GROUP * g_idx.astype(F32) + pair
    w_lo = jnp.where(first_lo, p1, p2) / den
    w_hi = jnp.where(first_lo, p2, p1) / den
    return cls, w_lo, w_hi


PAIRS_PER_GROUP = EXPERTS_PER_GROUP * (EXPERTS_PER_GROUP - 1) // 2
N_CLASSES = (N_EXPERTS // EXPERTS_PER_GROUP) * PAIRS_PER_GROUP
_PAIRS = [(a, b) for a in range(EXPERTS_PER_GROUP) for b in range(a + 1, EXPERTS_PER_GROUP)]
CLASS_E_LO = [EXPERTS_PER_GROUP * (c // PAIRS_PER_GROUP) + _PAIRS[c % PAIRS_PER_GROUP][0] for c in range(N_CLASSES)]
CLASS_E_HI = [EXPERTS_PER_GROUP * (c // PAIRS_PER_GROUP) + _PAIRS[c % PAIRS_PER_GROUP][1] for c in range(N_CLASSES)]
CLASS_ROWS = 32
ROW_WORDS = 640


def _pack_halves(x):
    kk = x.shape[1] // 2
    lo = lax.bitcast_convert_type(x[:, :kk].astype(BF16).astype(F32), jnp.uint32)
    hi = lax.bitcast_convert_type(x[:, kk:].astype(BF16).astype(F32), jnp.uint32)
    return lax.bitcast_convert_type((lo >> 16) | hi, jnp.int32)


def _unpack_halves(p):
    u = lax.bitcast_convert_type(p, jnp.uint32)
    lo = lax.bitcast_convert_type(u << 16, F32)
    hi = lax.bitcast_convert_type(u & jnp.uint32(0xFFFF0000), F32)
    return jnp.concatenate([lo.astype(BF16), hi.astype(BF16)], axis=1)


def _outproj_kernel(a_ref, b_ref, x_ref, w_ref, gt_ref, g2_ref, sc_ref, sh_ref, rwt_ref, rb_ref,
                    x1_ref, rows_ref, route_ref, cnt_ref, carry_sc):
    first = (pl.program_id(0) == 0) & (pl.program_id(1) == 0)

    @pl.when(first)
    def _():
        carry_sc[...] = jnp.zeros_like(carry_sc)

    half = a_ref.shape[2]
    tm = a_ref.shape[1]
    y = _dot(a_ref[0], w_ref[:half, :]) + _dot(b_ref[0], w_ref[half:, :])
    x1 = x_ref[0] + gt_ref[0] * y
    x1_ref[0] = x1
    h2 = _prenorm(x1, g2_ref[...], sc_ref[0], sh_ref[0])
    rows_ref[0, :, :ROW_WORDS - LANES] = _pack_halves(h2)
    cls, w_lo, w_hi = _routing(_dot_nt(rwt_ref[...], h2.astype(BF16)), rb_ref[...])
    r128 = lax.broadcasted_iota(jnp.int32, (LANES, tm), 0)
    wts = jnp.where(r128 == 0, w_lo, jnp.where(r128 == 1, w_hi, 0.0))
    rows_ref[0, :, ROW_WORDS - LANES:] = lax.bitcast_convert_type(wts.T, jnp.int32)
    crow = lax.broadcasted_iota(jnp.int32, (carry_sc.shape[0], tm), 0).astype(F32)
    onehot = jnp.where(crow == cls, 1.0, 0.0)
    earlier = lax.broadcasted_iota(jnp.int32, (tm, tm), 0) < lax.broadcasted_iota(jnp.int32, (tm, tm), 1)
    prefix = _dot(onehot.astype(BF16), jnp.where(earlier, 1.0, 0.0).astype(BF16)) + carry_sc[:, 0:1]
    rank = jnp.sum(onehot * prefix, axis=0, keepdims=True)
    r8 = lax.broadcasted_iota(jnp.int32, (8, tm), 0)
    route_ref[0] = jnp.where(r8 == 0, cls, jnp.where(r8 == 1, rank, 0.0))
    carry_sc[...] = carry_sc[...] + jnp.sum(onehot, axis=1, keepdims=True)
    cnt_ref[...] = carry_sc[...]


def _outproj(a, bb, x, w, gt, g2, sc2, sh2, rw, rb, tm):
    b, s, d = x.shape
    half = a.shape[2]
    row = lambda i, j: (i, 0, 0)
    tok = lambda i, j: (i, j, 0)
    const = lambda i, j: (0, 0)
    return pl.pallas_call(
        _outproj_kernel,
        grid=(b, s // tm),
        in_specs=[pl.BlockSpec((1, tm, half), tok),
                  pl.BlockSpec((1, tm, half), tok),
                  pl.BlockSpec((1, tm, d), tok),
                  pl.BlockSpec((2 * half, d), const),
                  pl.BlockSpec((1, 1, d), row),
                  pl.BlockSpec((1, d), const),
                  pl.BlockSpec((1, 1, d), row),
                  pl.BlockSpec((1, 1, d), row),
                  pl.BlockSpec((N_EXPERTS, d), const),
                  pl.BlockSpec((N_EXPERTS, 1), const)],
        out_specs=(pl.BlockSpec((1, tm, d), tok),
                   pl.BlockSpec((1, tm, ROW_WORDS), tok),
                   pl.BlockSpec((1, 8, tm), lambda i, j: (i, 0, j)),
                   pl.BlockSpec((CLASS_ROWS, LANES), const)),
        out_shape=(jax.ShapeDtypeStruct((b, s, d), F32),
                   jax.ShapeDtypeStruct((b, s, ROW_WORDS), jnp.int32),
                   jax.ShapeDtypeStruct((b, 8, s), F32),
                   jax.ShapeDtypeStruct((CLASS_ROWS, LANES), F32)),
        scratch_shapes=[pltpu.VMEM((CLASS_ROWS, LANES), F32)],
        compiler_params=_cparams(("arbitrary", "arbitrary")),
        name="outproj_router",
    )(a, bb, x, w, gt, g2.reshape(1, d), sc2, sh2, rw, rb)


MOE_ROWS = 512
SC_CHUNK = 128


def _sc_workers():
    info = plsc.get_sparse_core_info()
    return info.num_cores, info.num_cores * info.num_subcores


def _sc_scatter_rows(rows, idx3, n_out):
    nw, k, ch = idx3.shape
    width = rows.shape[1]
    nc, _ = _sc_workers()
    mesh = plsc.VectorSubcoreMesh(core_axis_name="c", subcore_axis_name="s")

    @functools.partial(
        pl.kernel, mesh=mesh,
        out_type=jax.ShapeDtypeStruct((n_out, width), rows.dtype),
        scratch_types=[pltpu.VMEM((k, ch), jnp.int32), pltpu.VMEM((ch, width), rows.dtype), pltpu.SemaphoreType.DMA],
        name="moe_dispatch")
    def kern(rows_hbm, idx_hbm, out_hbm, idx_v, rows_v, sem):
        wid = lax.axis_index("s") * nc + lax.axis_index("c")
        pltpu.sync_copy(idx_hbm.at[wid], idx_v)

        @pl.loop(0, k)
        def _(j):
            pltpu.sync_copy(rows_hbm.at[pl.ds((wid * k + j) * ch, ch)], rows_v)
            pltpu.async_copy(rows_v, out_hbm.at[idx_v.at[j]], sem).wait()

    return kern(rows, idx3)


def _sc_gather_rows(table, idx3):
    nw, k, ch = idx3.shape
    width = table.shape[1]
    nc, _ = _sc_workers()
    mesh = plsc.VectorSubcoreMesh(core_axis_name="c", subcore_axis_name="s")

    @functools.partial(
        pl.kernel, mesh=mesh,
        out_type=jax.ShapeDtypeStruct((nw * k * ch, width), table.dtype),
        scratch_types=[pltpu.VMEM((k, ch), jnp.int32), pltpu.VMEM((ch, width), table.dtype), pltpu.SemaphoreType.DMA],
        name="moe_combine")
    def kern(table_hbm, idx_hbm, out_hbm, idx_v, rows_v, sem):
        wid = lax.axis_index("s") * nc + lax.axis_index("c")
        pltpu.sync_copy(idx_hbm.at[wid], idx_v)

        @pl.loop(0, k)
        def _(j):
            pltpu.async_copy(table_hbm.at[idx_v.at[j]], rows_v, sem).wait()
            pltpu.sync_copy(rows_v, out_hbm.at[pl.ds((wid * k + j) * ch, ch)])

    return kern(table, idx3)


def _experts_kernel(elo_ref, ehi_ref, nused_ref, x_ref, wg0_ref, wu0_ref, wd0_ref, wg1_ref, wu1_ref, wd1_ref, y_ref):
    @pl.when(pl.program_id(0) < nused_ref[0])
    def _():
        blk = x_ref[...]
        feat = ROW_WORDS - LANES
        h = _unpack_halves(blk[:, :feat])
        wts = lax.bitcast_convert_type(blk[:, feat:], F32)
        y = None
        for col, (wg_ref, wu_ref, wd_ref) in enumerate(((wg0_ref, wu0_ref, wd0_ref), (wg1_ref, wu1_ref, wd1_ref))):
            act = (_silu(_dot(h, wg_ref[0, 0].astype(BF16))) * _dot(h, wu_ref[0, 0].astype(BF16))
                   * wts[:, col:col + 1])
            part = _dot(act.astype(BF16), wd_ref[0, 0].astype(BF16))
            y = part if y is None else y + part
        y_ref[...] = _pack_halves(y)


def _experts(xs, blk_elo, blk_ehi, nused, wg, wu, wd, layer):
    n_rows = xs.shape[0]
    _, ne, d, f = wg.shape
    nblk = n_rows // MOE_ROWS
    rows = lambda i, elo, ehi, nu: (jnp.minimum(i, nu[0] - 1), 0)
    lo = lambda i, elo, ehi, nu: (layer, elo[i], 0, 0)
    hi = lambda i, elo, ehi, nu: (layer, ehi[i], 0, 0)
    return pl.pallas_call(
        _experts_kernel,
        grid_spec=pltpu.PrefetchScalarGridSpec(
            num_scalar_prefetch=3,
            grid=(nblk,),
            in_specs=[pl.BlockSpec((MOE_ROWS, ROW_WORDS), rows),
                      pl.BlockSpec((1, 1, d, f), lo), pl.BlockSpec((1, 1, d, f), lo), pl.BlockSpec((1, 1, f, d), lo),
                      pl.BlockSpec((1, 1, d, f), hi), pl.BlockSpec((1, 1, d, f), hi), pl.BlockSpec((1, 1, f, d), hi)],
            out_specs=pl.BlockSpec((MOE_ROWS, d // 2), lambda i, elo, ehi, nu: (i, 0))),
        out_shape=jax.ShapeDtypeStruct((n_rows, d // 2), jnp.int32),
        compiler_params=_cparams(("arbitrary",)),
        name="moe_experts",
    )(blk_elo, blk_ehi, nused, xs, wg, wu, wd, wg, wu, wd)


def _residual_kernel(x_ref, m_ref, gt_ref, o_ref):
    o_ref[0] = x_ref[0] + gt_ref[0] * _unpack_halves(m_ref[0]).astype(F32)


def _residual(x1, moe_p, gt, tm):
    b, s, d = x1.shape
    tok = lambda i, j: (i, j, 0)
    return pl.pallas_call(
        _residual_kernel,
        grid=(b, s // tm),
        in_specs=[pl.BlockSpec((1, tm, d), tok), pl.BlockSpec((1, tm, d // 2), tok),
                  pl.BlockSpec((1, 1, d), lambda i, j: (i, 0, 0))],
        out_specs=pl.BlockSpec((1, tm, d), tok),
        out_shape=jax.ShapeDtypeStruct((b, s, d), F32),
        compiler_params=_cparams(("arbitrary", "arbitrary")),
        name="moe_residual",
    )(x1, moe_p, gt)


def _moe(rows, route, counts, wg, wu, wd, layer):
    b, s, _ = rows.shape
    d = wg.shape[2]
    t = b * s
    _, nw = _sc_workers()
    cnt = counts[:N_CLASSES, 0].astype(jnp.int32)
    padded = ((cnt + MOE_ROWS - 1) // MOE_ROWS) * MOE_ROWS
    ends = jnp.cumsum(padded)
    n_rows = t + N_CLASSES * MOE_ROWS
    nblk = n_rows // MOE_ROWS
    cls = route[:, 0, :].reshape(t).astype(jnp.int32)
    rank = route[:, 1, :].reshape(t).astype(jnp.int32)
    pos = (ends - padded)[cls] + rank
    idx3 = pos.reshape(nw, t // (nw * SC_CHUNK), SC_CHUNK)
    nused = (ends[-1] // MOE_ROWS).reshape(1)
    blk_cls = jnp.sum((jnp.arange(nblk, dtype=jnp.int32)[:, None] * MOE_ROWS) >= ends[None, :], axis=1)
    blk_cls = jnp.minimum(blk_cls, blk_cls[jnp.maximum(nused[0] - 1, 0)])
    blk_elo = jnp.asarray(CLASS_E_LO, jnp.int32)[blk_cls]
    blk_ehi = jnp.asarray(CLASS_E_HI, jnp.int32)[blk_cls]
    xs = _sc_scatter_rows(rows.reshape(t, ROW_WORDS), idx3, n_rows)
    ys = _experts(xs, blk_elo, blk_ehi, nused, wg, wu, wd, layer)
    return _sc_gather_rows(ys, idx3).reshape(b, s, d // 2)


def kernel(x, c, ada_w, ada_b, norm1_g, norm2_g, ev_w_in, ev_conv_w, ev_dn_a_log, ev_dn_dt_bias, ev_dn_onorm_g, ev_fox_f_bias, ev_fox_qnorm_g, ev_fox_knorm_g, ev_w_out, od_w_in, od_conv_w, od_conv_b, od_lru_wr, od_lru_br, od_lru_wi, od_lru_bi, od_lru_lambda, od_sgu_norm_g, od_sgu_w, od_sgu_b, od_w_out, router_w, router_b, moe_w_gate, moe_w_up, moe_w_down):
    b, s, d = x.shape
    depth = ada_w.shape[0]
    tm = min(512, s)
    mod = _adaln(c, ada_w, ada_b).reshape(depth, b, 6, 1, d)
    rw = router_w.T.astype(BF16)
    rb = router_b.reshape(N_EXPERTS, 1)

    pending = None
    for layer in range(depth):
        sh1, sc1, gt1, sh2, sc2, gt2 = (mod[layer, :, k] for k in range(6))
        i = layer // 2
        if layer % 2 == 0:
            w = ev_w_in[i]
            nq = 3 * DN_HEADS * DN_DK
            nz = DN_HEADS * DN_DK
            nf = 3 * FOX_HEADS * FOX_DH
            o_a = nq + nz
            o_f = o_a + 2 * DN_HEADS
            o_ff = o_f + nf
            small_w = jnp.concatenate([w[:, o_a:o_f], w[:, o_ff:o_ff + FOX_HEADS]], axis=1)
            small_w = jnp.pad(small_w, ((0, 0), (0, LANES - small_w.shape[1])))
            w_all = jnp.concatenate([w[:, :o_a], w[:, o_f:o_ff], small_w], axis=1).astype(BF16)
            splits = ((0, nq), (nq, o_a), (o_a, o_a + nf), (o_a + nf, o_a + nf + LANES))
            res = _inproj(x, norm1_g[layer], sc1, sh1, w_all, splits,
                          (BF16, BF16, BF16, F32), (None,) * 4, tm, pending, ev_conv_w[i])
            if pending is not None:
                x, res = res[0], res[1:]
            dn, z, fox, small = res
            small_t = jnp.swapaxes(small[:, :, :16], 1, 2)
            zeros4 = jnp.zeros((4,), F32)
            mul = jnp.concatenate([-jnp.exp(ev_dn_a_log[i]), zeros4, -jnp.ones((4,), F32), zeros4])
            bias = jnp.concatenate([ev_dn_dt_bias[i], zeros4, ev_fox_f_bias[i], zeros4])
            pr = jnp.stack([mul, bias], axis=1)
            pc = jnp.pad(jnp.stack([mul, bias], axis=0), ((0, 0), (0, LANES - 16)))
            col, row = _gates(small, small_t, pc, pr)
            row_dn = row[:, :DN_HEADS].reshape(b, DN_HEADS, s // DN_GROUP, DN_GROUP).transpose(0, 2, 1, 3)
            o_dn = _deltanet(dn, z, col, row_dn, ev_dn_onorm_g[i])
            o_fox = _fox(fox, col, ev_fox_qnorm_g[i], ev_fox_knorm_g[i], min(512, s))
            mix_a, mix_b, w_out = o_dn, o_fox, ev_w_out[i]
        else:
            lw = od_lru_wr.shape[-1] * LRU_BLOCKS
            splits = ((0, lw), (lw, 2 * lw), (2 * lw, 2 * lw + od_sgu_w.shape[-1] * SGU_GROUPS),
                      (2 * lw + od_sgu_w.shape[-1] * SGU_GROUPS, od_w_in.shape[-1]))
            res = _inproj(x, norm1_g[layer], sc1, sh1, od_w_in[i].astype(BF16), splits,
                          (BF16,) * 3, (None, "gelu"), tm, pending,
                          sgu=(od_sgu_norm_g[i], od_sgu_w[i], od_sgu_b[i]))
            if pending is not None:
                x, res = res[0], res[1:]
            lx, lg, o_sgu = res
            o_lru = _lru(lx, lg, od_conv_w[i], od_conv_b[i], od_lru_wr[i], od_lru_br[i], od_lru_wi[i],
                         od_lru_bi[i], od_lru_lambda[i])
            mix_a, mix_b, w_out = o_lru, o_sgu, od_w_out[i]
        x1, rows, route, counts = _outproj(mix_a, mix_b, x, w_out.astype(BF16), gt1, norm2_g[layer], sc2, sh2,
                                           rw, rb, tm)
        x = x1
        pending = (_moe(rows, route, counts, moe_w_gate, moe_w_up, moe_w_down, layer), gt2)
    return _residual(x, *pending, tm)
```

```python
import functools

import jax
import jax.numpy as jnp
from jax import lax
from jax.experimental import pallas as pl
from jax.experimental.pallas import tpu as pltpu
from jax.experimental.pallas import tpu_sc as plsc

F32 = jnp.float32
BF16 = jnp.bfloat16
EPS = 1e-6
NEG_INF = float("-inf")

DN_HEADS = 4
DN_DK = 128
DN_CHUNK = 64
CONV_WIDTH = 4
FOX_HEADS = 4
FOX_DH = 128
LRU_BLOCKS = 4
LRU_C = 8.0
SGU_GROUPS = 4
SGU_CHUNK = 128
N_EXPERTS = 16
EXPERTS_PER_GROUP = 4
LANES = 128
SUBLANES = 8

VMEM_LIMIT = 48 * 1024 * 1024


def _cparams(sem):
    return pltpu.CompilerParams(dimension_semantics=sem, vmem_limit_bytes=VMEM_LIMIT)


def _dot(a, b):
    return jnp.dot(a, b, preferred_element_type=F32)


def _dot_nt(a, b):
    return lax.dot_general(a, b, (((1,), (1,)), ((), ())), preferred_element_type=F32)


def _dot_tn(a, b):
    return lax.dot_general(a, b, (((0,), (0,)), ((), ())), preferred_element_type=F32)


def _dot_exact(a, b):
    return jnp.dot(a, b, preferred_element_type=F32, precision=lax.Precision.HIGHEST)


def _sigmoid(x):
    return 1.0 / (1.0 + jnp.exp(-x))


def _silu(x):
    return x * _sigmoid(x)


def _softplus(x):
    return jnp.maximum(x, 0.0) + jnp.log(1.0 + jnp.exp(-jnp.abs(x)))


def _gelu_tanh(x):
    c = 0.7978845608028654
    return 0.5 * x * (1.0 + jnp.tanh(c * (x + 0.044715 * (x * x * x))))


def _prenorm(x, g, scale, shift):
    ms = jnp.mean(x * x, axis=-1, keepdims=True)
    return (x * lax.rsqrt(ms + EPS) * g) * (1.0 + scale) + shift


def _shift_rows(x, d, fill=0.0):
    rows = lax.broadcasted_iota(jnp.int32, x.shape, 0)
    return jnp.where(rows >= d, pltpu.roll(x, d, axis=0), fill)


def _causal_conv(x, w_ref):
    k_w = w_ref.shape[0]
    acc = x * w_ref[k_w - 1:k_w, :]
    for d in range(1, k_w):
        acc = acc + _shift_rows(x, d) * w_ref[k_w - 1 - d:k_w - d, :]
    return acc


def _adaln_kernel(c_ref, w_ref, b_ref, o_ref):
    c = c_ref[...]
    ca = _silu(c).astype(BF16)
    o_ref[0] = _dot(ca, w_ref[0].astype(BF16)) + b_ref[0]


def _adaln(c, ada_w, ada_b):
    depth, d, n = ada_w.shape
    b = c.shape[0]
    tn = 1536
    return pl.pallas_call(
        _adaln_kernel,
        grid=(depth, n // tn),
        in_specs=[pl.BlockSpec((b, d), lambda l, j: (0, 0)),
                  pl.BlockSpec((1, d, tn), lambda l, j: (l, 0, j)),
                  pl.BlockSpec((1, 1, tn), lambda l, j: (l, 0, j))],
        out_specs=pl.BlockSpec((1, b, tn), lambda l, j: (l, 0, j)),
        out_shape=jax.ShapeDtypeStruct((depth, b, n), F32),
        compiler_params=_cparams(("arbitrary", "arbitrary")),
        name="adaln",
    )(c, ada_w, ada_b.reshape(depth, 1, n))


CONV_HALO = 8
CONV_SLAB = 512


def _inproj_kernel(*refs, col_splits, acts, pending, conv, sgu):
    refs = list(refs)
    x_ref = refs.pop(0)
    if pending:
        m_ref, gtm_ref = refs.pop(0), refs.pop(0)
    g_ref, sc_ref, sh_ref, w_ref = (refs.pop(0) for _ in range(4))
    if conv:
        cw_ref = refs.pop(0)
        halo_sc = refs.pop()
    if sgu:
        sg_ref, sw_ref, sbt_ref = (refs.pop(0) for _ in range(3))
    if pending:
        xo_ref = refs.pop(0)
        x = x_ref[0] + gtm_ref[0] * _unpack_halves(m_ref[0]).astype(F32)
        xo_ref[0] = x
    else:
        x = x_ref[0]
    out_refs = refs
    h = _prenorm(x, g_ref[...], sc_ref[0], sh_ref[0]).astype(BF16)
    tm = x.shape[0]
    if conv:
        @pl.when(pl.program_id(1) == 0)
        def _():
            halo_sc[0:CONV_HALO, :] = jnp.zeros((CONV_HALO, halo_sc.shape[1]), F32)

        c0, c1 = col_splits[0]
        for s0 in range(c0, c1, CONV_SLAB):
            halo_sc[CONV_HALO:, s0 - c0:s0 - c0 + CONV_SLAB] = _dot(h, w_ref[:, s0:s0 + CONV_SLAB])
    n_plain = len(col_splits) - (2 if sgu else 0)
    for k, (o_ref, (c0, c1), act) in enumerate(zip(out_refs[:n_plain], col_splits[:n_plain], acts)):
        if conv and k == 0:
            continue
        p = _dot(h, w_ref[:, c0:c1])
        if act == "gelu":
            p = _gelu_tanh(p)
        o_ref[0] = p.astype(o_ref.dtype)
    if sgu:
        (u0, u1), (v0, v1) = col_splits[n_plain:]
        o_ref = out_refs[n_plain]
        u = _gelu_tanh(_dot(h, w_ref[:, u0:u1]))
        v = _gelu_tanh(_dot(h, w_ref[:, v0:v1]))
        vn = (v * lax.rsqrt(jnp.mean(v * v, axis=-1, keepdims=True) + EPS) * sg_ref[...]).astype(BF16)
        c = SGU_CHUNK
        gw = (u1 - u0) // SGU_GROUPS
        ri = lax.broadcasted_iota(jnp.int32, (c, c), 0)
        ci = lax.broadcasted_iota(jnp.int32, (c, c), 1)
        for g in range(SGU_GROUPS):
            wg = jnp.where(ri >= ci, sw_ref[g], 0.0).astype(BF16)
            bcol = sbt_ref[:, g:g + 1]
            for n in range(tm // c):
                rows = slice(n * c, (n + 1) * c)
                cols = slice(g * gw, (g + 1) * gw)
                mixed = _dot(wg, vn[rows, cols]) + bcol
                o_ref[0, rows, cols] = (u[rows, cols] * mixed).astype(o_ref.dtype)
    if conv:
        o_ref = out_refs[0]
        for t0 in range(0, halo_sc.shape[1], DN_DK):
            cols = slice(t0, t0 + DN_DK)
            acc = halo_sc[CONV_HALO:, cols] * cw_ref[CONV_WIDTH - 1:CONV_WIDTH, cols]
            for dd in range(1, CONV_WIDTH):
                acc = acc + halo_sc[pl.ds(CONV_HALO - dd, tm), cols] * cw_ref[CONV_WIDTH - 1 - dd:CONV_WIDTH - dd, cols]
            halo_sc[0:CONV_HALO, cols] = halo_sc[tm:tm + CONV_HALO, cols]
            yj = _silu(acc)
            if t0 < 2 * DN_HEADS * DN_DK:
                yj = yj * lax.rsqrt(jnp.sum(yj * yj, axis=-1, keepdims=True) + EPS)
            if t0 < DN_HEADS * DN_DK:
                yj = yj * (DN_DK ** -0.5)
            o_ref[0, :, cols] = yj.astype(o_ref.dtype)


def _inproj(x, g, scale, shift, w, col_splits, out_dtypes, acts, tm, pending=None, conv_w=None, sgu=None):
    b, s, d = x.shape
    n = w.shape[1]
    out_cols = list(col_splits) if sgu is None else list(col_splits[:-2]) + [col_splits[-2]]
    outs = tuple(jax.ShapeDtypeStruct((b, s, c1 - c0), dt) for (c0, c1), dt in zip(out_cols, out_dtypes))
    row = lambda i, j: (i, 0, 0)
    tok = lambda i, j: (i, j, 0)
    const = lambda i, j: (0, 0)
    in_specs = [pl.BlockSpec((1, tm, d), tok)]
    out_specs = tuple(pl.BlockSpec((1, tm, c1 - c0), tok) for (c0, c1) in out_cols)
    args = (x,)
    if pending is not None:
        in_specs += [pl.BlockSpec((1, tm, d // 2), tok), pl.BlockSpec((1, 1, d), row)]
        out_specs = (pl.BlockSpec((1, tm, d), tok),) + out_specs
        outs = (jax.ShapeDtypeStruct((b, s, d), F32),) + outs
        args += tuple(pending)
    in_specs += [pl.BlockSpec((1, d), const), pl.BlockSpec((1, 1, d), row), pl.BlockSpec((1, 1, d), row),
                 pl.BlockSpec((d, n), const)]
    args += (g.reshape(1, d), scale, shift, w)
    scratch = []
    if conv_w is not None:
        in_specs.append(pl.BlockSpec(conv_w.shape, const))
        args += (conv_w,)
        scratch.append(pltpu.VMEM((CONV_HALO + tm, conv_w.shape[1]), F32))
    if sgu is not None:
        g_norm, w_s, b_s = sgu
        in_specs += [pl.BlockSpec((1, g_norm.shape[0]), const), pl.BlockSpec(w_s.shape, lambda i, j: (0, 0, 0)),
                     pl.BlockSpec((b_s.shape[1], b_s.shape[0]), const)]
        args += (g_norm.reshape(1, -1), w_s, b_s.T)
    return pl.pallas_call(
        functools.partial(_inproj_kernel, col_splits=col_splits, acts=acts, pending=pending is not None,
                          conv=conv_w is not None, sgu=sgu is not None),
        grid=(b, s // tm),
        in_specs=in_specs,
        out_specs=out_specs,
        out_shape=outs,
        scratch_shapes=scratch,
        compiler_params=_cparams(("arbitrary", "arbitrary")),
        name="inproj",
    )(*args)


def _gates_kernel(sm_ref, smt_ref, pc_ref, pr_ref, col_ref, row_ref):
    s = sm_ref.shape[1]
    blk = LANES
    ri = lax.broadcasted_iota(jnp.int32, (blk, blk), 0)
    ci = lax.broadcasted_iota(jnp.int32, (blk, blk), 1)
    same_chunk = (ri // DN_CHUNK) == (ci // DN_CHUNK)
    tril = jnp.where(ri >= ci, 1.0, 0.0)
    tril_loc = jnp.where(same_chunk, tril, 0.0)
    triu_loc = jnp.where(same_chunk & (ri <= ci), 1.0, 0.0)
    lane = lax.broadcasted_iota(jnp.int32, (blk, LANES), 1)
    carry = jnp.zeros((1, LANES), F32)
    for j in range(s // blk):
        rows = slice(j * blk, (j + 1) * blk)
        xc = sm_ref[0, rows, :] + pc_ref[1:2, :]
        dec = pc_ref[0:1, :] * _softplus(jnp.where(lane < 4, xc, -xc))
        cum_glb = _dot_exact(tril, dec) + carry
        col_ref[0, rows, :] = jnp.where(lane < 4, _dot_exact(tril_loc, dec), jnp.where(lane < 8, _sigmoid(xc), cum_glb))
        carry = cum_glb[blk - 1:blk, :]
        decr = pr_ref[:, 0:1] * _softplus(smt_ref[0, :, rows] + pr_ref[:, 1:2])
        row_ref[0, :, rows] = _dot_exact(decr, triu_loc)


def _gates(small, small_t, pc, pr):
    b, s, _ = small.shape
    return pl.pallas_call(
        _gates_kernel,
        grid=(b,),
        in_specs=[pl.BlockSpec((1, s, LANES), lambda i: (i, 0, 0)),
                  pl.BlockSpec((1, 16, s), lambda i: (i, 0, 0)),
                  pl.BlockSpec((2, LANES), lambda i: (0, 0)),
                  pl.BlockSpec((16, 2), lambda i: (0, 0))],
        out_specs=(pl.BlockSpec((1, s, LANES), lambda i: (i, 0, 0)),
                   pl.BlockSpec((1, 16, s), lambda i: (i, 0, 0))),
        out_shape=(jax.ShapeDtypeStruct((b, s, LANES), F32), jax.ShapeDtypeStruct((b, 16, s), F32)),
        compiler_params=_cparams(("arbitrary",)),
        name="gates",
    )(small, small_t, pc, pr)


DN_PACK = 4
DN_GROUP = DN_PACK * DN_CHUNK
DN_ITER_GROUPS = 2


def _blockdiag(p):
    c, wide = p.shape
    t = jnp.concatenate([p] * (wide // c), axis=0)
    rb = lax.broadcasted_iota(jnp.int32, (wide, wide), 0) // c
    cb = lax.broadcasted_iota(jnp.int32, (wide, wide), 1) // c
    return jnp.where(rb == cb, t, 0.0).astype(BF16)


def _diag_blocks(m, c):
    wide = m.shape[1]
    cb = lax.broadcasted_iota(jnp.int32, (c, wide), 1) // c
    out = m[:c]
    for j in range(1, wide // c):
        out = jnp.where(cb == j, m[j * c:(j + 1) * c], out)
    return out


def _rows_to_blocks(col, c, wide):
    cb = lax.broadcasted_iota(jnp.int32, (c, wide), 1) // c
    out = jnp.broadcast_to(col[:c], (c, wide))
    for j in range(1, wide // c):
        out = jnp.where(cb == j, col[j * c:(j + 1) * c], out)
    return out


def _packed_unit_lower_inverse(lows):
    c, wide = lows[0].shape
    ri = lax.broadcasted_iota(jnp.int32, (c, wide), 0)
    ci = lax.broadcasted_iota(jnp.int32, (c, wide), 1) % c
    eye = jnp.where(ri == ci, 1.0, 0.0)
    n = range(len(lows))
    ds = [jnp.where((ri // 16) == (ci // 16), lo, 0.0) for lo in lows]
    xs = [eye - d for d in ds]
    ps = ds
    bds = [_blockdiag(p) for p in ps]
    for _ in range(3):
        ps = [_dot(ps[i].astype(BF16), bds[i]) for i in n]
        bds = [_blockdiag(p) for p in ps]
        xs = [xs[i] + _dot(xs[i].astype(BF16), bds[i]) for i in n]
    for width in (16, 32):
        sel = ((ri // (2 * width)) == (ci // (2 * width))) & ((ri // width) != (ci // width))
        offs = [_blockdiag(jnp.where(sel, lo, 0.0)) for lo in lows]
        ts = [_dot(xs[i].astype(BF16), offs[i]) for i in n]
        bdx = [_blockdiag(x) for x in xs]
        xs = [xs[i] - _dot(ts[i].astype(BF16), bdx[i]) for i in n]
    return xs


def _deltanet_kernel(dn_ref, z_ref, col_ref, row_ref, og_ref, o_ref, st_sc):
    s = dn_ref.shape[1]
    c = DN_CHUNK
    gt = DN_GROUP
    dk = DN_DK
    hd = DN_HEADS * dk
    heads = range(DN_HEADS)
    st_sc[...] = jnp.zeros_like(st_sc)

    ri = lax.broadcasted_iota(jnp.int32, (c, gt), 0)
    ci = lax.broadcasted_iota(jnp.int32, (c, gt), 1) % c
    og = og_ref[...]

    per_iter = DN_ITER_GROUPS if (s // gt) % DN_ITER_GROUPS == 0 else 1
    insts = [(gg, h) for gg in range(per_iter) for h in heads]
    ins = range(len(insts))

    def group(g, carry):
        r0 = [pl.multiple_of((g * per_iter + gg) * gt, gt) for gg in range(per_iter)]
        colg = [col_ref[0, pl.ds(r0[gg], gt), :] for gg in range(per_iter)]
        rowg = [row_ref[0, g * per_iter + gg] for gg in range(per_iter)]
        hs = [slice(h * dk, (h + 1) * dk) for h in heads]
        q = [dn_ref[0, pl.ds(r0[gg], gt), h * dk:(h + 1) * dk].astype(F32) for gg, h in insts]
        k = [dn_ref[0, pl.ds(r0[gg], gt), hd + h * dk:hd + (h + 1) * dk].astype(F32) for gg, h in insts]
        v = [dn_ref[0, pl.ds(r0[gg], gt), 2 * hd + h * dk:2 * hd + (h + 1) * dk].astype(F32) for gg, h in insts]
        gc = [colg[gg][:, h:h + 1] for gg, h in insts]
        beta = [colg[gg][:, DN_HEADS + h:DN_HEADS + h + 1] for gg, h in insts]
        grow = [rowg[gg][h:h + 1, :] for gg, h in insts]
        glast = [jnp.concatenate([jnp.broadcast_to(gc[i][(j + 1) * c - 1:(j + 1) * c], (c, 1))
                                  for j in range(DN_PACK)], axis=0) for i in ins]
        eg = [jnp.exp(gc[i]) for i in ins]
        kbf = [k[i].astype(BF16) for i in ins]
        both = [_dot_nt(jnp.concatenate([(k[i] * beta[i]).astype(BF16), q[i].astype(BF16)], axis=0), kbf[i])
                for i in ins]
        decay = [jnp.exp(jnp.where(ri >= ci, _rows_to_blocks(gc[i], c, gt) - grow[i], NEG_INF)) for i in ins]
        kk = [_diag_blocks(both[i][:gt], c) * decay[i] for i in ins]
        qk = [_blockdiag(_diag_blocks(both[i][gt:], c) * decay[i]) for i in ins]
        t_inv = _packed_unit_lower_inverse([jnp.where(ri > ci, kk[i], 0.0) for i in ins])
        rhs = [jnp.concatenate([k[i] * (beta[i] * eg[i]), v[i] * beta[i]], axis=1).astype(BF16) for i in ins]
        wu = [_dot(_blockdiag(t_inv[i]), rhs[i]).astype(BF16) for i in ins]
        qwu = [_dot(qk[i], wu[i]) for i in ins]
        qp = [(q[i] * eg[i] - qwu[i][:, :dk]).astype(BF16) for i in ins]
        kdec = [(k[i] * jnp.exp(glast[i] - gc[i])).astype(BF16) for i in ins]
        mb = [[_dot_tn(kdec[i][j * c:(j + 1) * c], wu[i][j * c:(j + 1) * c]) for i in ins]
              for j in range(DN_PACK)]
        for gg in range(per_iter):
            mine = [gg * DN_HEADS + h for h in heads]
            outs = [[] for _ in heads]
            for j in range(DN_PACK):
                rows = slice(j * c, (j + 1) * c)
                state = [st_sc[h] for h in heads]
                lhs = [jnp.concatenate([qp[i][rows], mb[j][i][:, :dk].astype(BF16)], axis=0) for i in mine]
                r = [_dot(lhs[h], state[h].astype(BF16)) for h in heads]
                for h, i in zip(heads, mine):
                    gl = jnp.exp(glast[i][j * c:j * c + 1])
                    st_sc[h] = state[h] * gl - r[h][c:] + mb[j][i][:, dk:]
                    outs[h].append(r[h][:c] + qwu[i][rows, dk:])
            for h in heads:
                o = jnp.concatenate(outs[h], axis=0)
                on = o * lax.rsqrt(jnp.mean(o * o, axis=-1, keepdims=True) + EPS) * og
                zz = z_ref[0, pl.ds(r0[gg], gt), hs[h]].astype(F32)
                o_ref[0, pl.ds(r0[gg], gt), hs[h]] = (on * _silu(zz)).astype(o_ref.dtype)
        return carry

    lax.fori_loop(0, s // (gt * per_iter), group, 0)


def _deltanet(dn, z, col, row, onorm_g):
    b, s, w3 = dn.shape
    hd = DN_HEADS * DN_DK
    return pl.pallas_call(
        _deltanet_kernel,
        grid=(b,),
        in_specs=[pl.BlockSpec((1, s, w3), lambda i: (i, 0, 0)),
                  pl.BlockSpec((1, s, hd), lambda i: (i, 0, 0)),
                  pl.BlockSpec((1, s, LANES), lambda i: (i, 0, 0)),
                  pl.BlockSpec((1, s // DN_GROUP, DN_HEADS, DN_GROUP), lambda i: (i, 0, 0, 0)),
                  pl.BlockSpec((1, DN_DK), lambda i: (0, 0))],
        out_specs=pl.BlockSpec((1, s, hd), lambda i: (i, 0, 0)),
        out_shape=jax.ShapeDtypeStruct((b, s, hd), BF16),
        scratch_shapes=[pltpu.VMEM((DN_HEADS, DN_DK, DN_DK), F32)],
        compiler_params=_cparams(("arbitrary",)),
        name="deltanet",
    )(dn, z, col, row, onorm_g.reshape(1, DN_DK))


def _split3(x):
    hi = x.astype(BF16).astype(F32)
    r = x - hi
    mid = r.astype(BF16).astype(F32)
    return hi, mid, r - mid


def _fox_kernel(q_ref, k_ref, v_ref, colq_ref, colk_ref, qg_ref, kg_ref, o_ref, ka_sc, va_sc, m_sc, acc_sc, *, tq):
    qi = pl.program_id(1)
    s = k_ref.shape[1]
    dh = FOX_DH

    def bias_lanes(col):
        lane = lax.broadcasted_iota(jnp.int32, col.shape, 1)
        hi, mid, lo = _split3(col)
        return jnp.where((lane >= 8) & (lane < 8 + FOX_HEADS), hi,
                         jnp.where((lane >= 24) & (lane < 24 + FOX_HEADS), pltpu.roll(mid, 16, axis=1),
                                   jnp.where((lane >= 40) & (lane < 40 + FOX_HEADS), pltpu.roll(lo, 32, axis=1), 0.0)))

    @pl.when(qi == 0)
    def _():
        lane = lax.broadcasted_iota(jnp.int32, (s, LANES), 1)
        ones_col = jnp.where(lane == 0, 1.0, 0.0).astype(BF16)
        ones_k = (lane >= 56) & (lane < 96) & ((lane % 16) >= 8) & ((lane % 16) < 8 + FOX_HEADS)
        ext_k = (jnp.where(ones_k, 1.0, 0.0) - bias_lanes(colk_ref[0])).astype(BF16)
        for h in range(FOX_HEADS):
            cols = slice(h * dh, (h + 1) * dh)
            kf = k_ref[0, :, cols].astype(F32)
            kn = kf * lax.rsqrt(jnp.mean(kf * kf, axis=-1, keepdims=True) + EPS) * kg_ref[...]
            ka_sc[h, :, :dh] = kn.astype(BF16)
            ka_sc[h, :, dh:] = ext_k
            va_sc[h, :, :dh] = v_ref[0, :, cols]
            va_sc[h, :, dh:] = ones_col

    lane = lax.broadcasted_iota(jnp.int32, (tq, LANES), 1)
    cq = pltpu.roll(bias_lanes(colq_ref[0]), 48, axis=1)
    qa = []
    for h in range(FOX_HEADS):
        cols = slice(h * dh, (h + 1) * dh)
        qf = q_ref[0, :, cols].astype(F32)
        qn = qf * lax.rsqrt(jnp.mean(qf * qf, axis=-1, keepdims=True) + EPS) * qg_ref[...] * (dh ** -0.5)
        mine = ((lane % 16) == 8 + h) & (lane < 96)
        ext = jnp.where(mine, jnp.where(lane < 48, 1.0, cq), 0.0)
        qa.append(jnp.concatenate([qn.astype(BF16), ext.astype(BF16)], axis=1))
    m_sc[...] = jnp.full(m_sc.shape, NEG_INF, F32)
    acc_sc[...] = jnp.zeros_like(acc_sc)
    causal = lax.broadcasted_iota(jnp.int32, (tq, tq), 0) >= lax.broadcasted_iota(jnp.int32, (tq, tq), 1)

    def step(k0, masked):
        heads = range(FOX_HEADS)
        logits = [_dot_nt(qa[h], ka_sc[h, pl.ds(k0, tq), :]) for h in heads]
        if masked:
            logits = [jnp.where(causal, lg, NEG_INF) for lg in logits]
        ps, alphas = [], []
        for h in heads:
            m_old = m_sc[h]
            m_new = jnp.maximum(m_old, jnp.max(logits[h], axis=1, keepdims=True))
            m_sc[h] = m_new
            alphas.append(jnp.exp(m_old - m_new))
            ps.append(jnp.exp(logits[h] - jnp.concatenate([m_new] * (tq // LANES), axis=1)).astype(BF16))
        for h in heads:
            pv = _dot(ps[h], va_sc[h, pl.ds(k0, tq), :])
            acc_sc[h] = acc_sc[h] * jnp.concatenate([alphas[h], alphas[h]], axis=1) + pv

    def body(j, carry):
        step(pl.multiple_of(j * tq, tq), False)
        return carry

    lax.fori_loop(0, qi, body, 0)
    step(pl.multiple_of(qi * tq, tq), True)
    for h in range(FOX_HEADS):
        acc = acc_sc[h]
        o_ref[0, :, h * dh:(h + 1) * dh] = (acc[:, :dh] / acc[:, dh:dh + 1]).astype(o_ref.dtype)


def _fox(fox, col, qg, kg, tq):
    b, s, _ = fox.shape
    hd = FOX_HEADS * FOX_DH
    return pl.pallas_call(
        functools.partial(_fox_kernel, tq=tq),
        grid=(b, s // tq),
        in_specs=[pl.BlockSpec((1, tq, hd), lambda i, j: (i, j, 0)),
                  pl.BlockSpec((1, s, hd), lambda i, j: (i, 0, 1)),
                  pl.BlockSpec((1, s, hd), lambda i, j: (i, 0, 2)),
                  pl.BlockSpec((1, tq, LANES), lambda i, j: (i, j, 0)),
                  pl.BlockSpec((1, s, LANES), lambda i, j: (i, 0, 0)),
                  pl.BlockSpec((1, FOX_DH), lambda i, j: (0, 0)),
                  pl.BlockSpec((1, FOX_DH), lambda i, j: (0, 0))],
        out_specs=pl.BlockSpec((1, tq, hd), lambda i, j: (i, j, 0)),
        out_shape=jax.ShapeDtypeStruct((b, s, hd), BF16),
        scratch_shapes=[pltpu.VMEM((FOX_HEADS, s, 2 * FOX_DH), BF16), pltpu.VMEM((FOX_HEADS, s, 2 * FOX_DH), BF16),
                        pltpu.VMEM((FOX_HEADS, tq, LANES), F32), pltpu.VMEM((FOX_HEADS, tq, 2 * FOX_DH), F32)],
        compiler_params=_cparams(("arbitrary", "arbitrary")),
        name="fox",
    )(fox, fox, fox, col, col, qg.reshape(1, FOX_DH), kg.reshape(1, FOX_DH))


def _lru_kernel(x_ref, gate_ref, cw_ref, cb_ref, wr_ref, br_ref, wi_ref, bi_ref, lam_ref, o_ref, a_sc, b_sc):
    s = x_ref.shape[1]
    x = _causal_conv(x_ref[0].astype(F32), cw_ref) + cb_ref[...]
    xb = x.astype(BF16)
    r = _sigmoid(_dot(xb, wr_ref[0].astype(BF16)) + br_ref[...])
    i = _sigmoid(_dot(xb, wi_ref[0].astype(BF16)) + bi_ref[...])
    log_a = (-LRU_C) * r * _softplus(-lam_ref[...])
    a = jnp.exp(log_a)
    bb = jnp.sqrt(1.0 - a * a) * (i * x)
    nt = s // SUBLANES
    a3 = a.reshape(nt, SUBLANES, a.shape[1])
    b3 = bb.reshape(nt, SUBLANES, a.shape[1])
    sub = lax.broadcasted_iota(jnp.int32, a3.shape, 1)
    d = 1
    while d < SUBLANES:
        keep = sub >= d
        b3 = a3 * jnp.where(keep, pltpu.roll(b3, d, axis=1), 0.0) + b3
        a3 = a3 * jnp.where(keep, pltpu.roll(a3, d, axis=1), 1.0)
        d *= 2
    a_sc[...] = a3.reshape(s, a.shape[1])
    b_sc[...] = b3.reshape(s, a.shape[1])
    at = a_sc[pl.ds(SUBLANES - 1, nt, stride=SUBLANES), :]
    bt = b_sc[pl.ds(SUBLANES - 1, nt, stride=SUBLANES), :]
    d = 1
    while d < nt:
        bt = at * _shift_rows(bt, d, 0.0) + bt
        at = at * _shift_rows(at, d, 1.0)
        d *= 2
    h_prev = _shift_rows(bt, 1, 0.0)
    bb = (a3 * h_prev[:, None, :] + b3).reshape(s, a.shape[1])
    o_ref[0] = (bb * gate_ref[0].astype(F32)).astype(o_ref.dtype)


def _lru(lx, lg, conv_w, conv_b, wr, br, wi, bi, lam):
    b, s, wd = lx.shape
    blk = wd // LRU_BLOCKS
    vec = lambda i, j: (0, j)
    return pl.pallas_call(
        _lru_kernel,
        grid=(b, LRU_BLOCKS),
        in_specs=[pl.BlockSpec((1, s, blk), lambda i, j: (i, 0, j)),
                  pl.BlockSpec((1, s, blk), lambda i, j: (i, 0, j)),
                  pl.BlockSpec((CONV_WIDTH, blk), vec),
                  pl.BlockSpec((1, blk), vec),
                  pl.BlockSpec((1, blk, blk), lambda i, j: (j, 0, 0)),
                  pl.BlockSpec((1, blk), vec),
                  pl.BlockSpec((1, blk, blk), lambda i, j: (j, 0, 0)),
                  pl.BlockSpec((1, blk), vec),
                  pl.BlockSpec((1, blk), vec)],
        out_specs=pl.BlockSpec((1, s, blk), lambda i, j: (i, 0, j)),
        out_shape=jax.ShapeDtypeStruct((b, s, wd), BF16),
        scratch_shapes=[pltpu.VMEM((s, blk), F32), pltpu.VMEM((s, blk), F32)],
        compiler_params=_cparams(("arbitrary", "arbitrary")),
        name="rglru",
    )(lx, lg, conv_w, conv_b.reshape(1, wd), wr, br.reshape(1, wd), wi, bi.reshape(1, wd), lam.reshape(1, wd))


def _routing(logits, rb):
    ne, tm = logits.shape
    row = lax.broadcasted_iota(jnp.int32, (ne, tm), 0)
    row_f = row.astype(F32)
    ex = jnp.exp(logits - jnp.max(logits, axis=0, keepdims=True))
    probs = ex / jnp.sum(ex, axis=0, keepdims=True)
    sel = probs + rb

    def top2(vals, idx):
        m1 = jnp.max(vals, axis=0, keepdims=True)
        i1 = jnp.min(jnp.where(vals == m1, idx, float(ne)), axis=0, keepdims=True)
        rest = jnp.where(idx == i1, NEG_INF, vals)
        m2 = jnp.max(rest, axis=0, keepdims=True)
        return m1, i1, m2, rest

    best = None
    g_idx = None
    grp = row // EXPERTS_PER_GROUP
    for g in range(ne // EXPERTS_PER_GROUP):
        m1, _, m2, _ = top2(jnp.where(grp == g, sel, NEG_INF), row_f)
        score = m1 + m2
        if g == 0:
            best, g_idx = score, jnp.zeros((1, tm), jnp.int32)
        else:
            upd = score > best
            best = jnp.where(upd, score, best)
            g_idx = jnp.where(upd, g, g_idx)
    _, i1, m2, rest = top2(jnp.where(grp == g_idx, sel, NEG_INF), row_f)
    i2 = jnp.min(jnp.where(rest == m2, row_f, float(ne)), axis=0, keepdims=True)
    p1 = jnp.sum(jnp.where(row_f == i1, probs, 0.0), axis=0, keepdims=True)
    p2 = jnp.sum(jnp.where(row_f == i2, probs, 0.0), axis=0, keepdims=True)
    den = p1 + p2
    first_lo = i1 < i2
    a = jnp.where(first_lo, i1, i2) - EXPERTS_PER_GROUP * g_idx.astype(F32)
    bhi = jnp.where(first_lo, i2, i1) - EXPERTS_PER_GROUP * g_idx.astype(F32)
    pair = a * (7.0 - a) * 0.5 + (bhi - a - 1.0)
    cls = PAIRS_PER_GROUP * g_idx.astype(F32) + pair
    w_lo = jnp.where(first_lo, p1, p2) / den
    w_hi = jnp.where(first_lo, p2, p1) / den
    return cls, w_lo, w_hi


PAIRS_PER_GROUP = EXPERTS_PER_GROUP * (EXPERTS_PER_GROUP - 1) // 2
N_CLASSES = (N_EXPERTS // EXPERTS_PER_GROUP) * PAIRS_PER_GROUP
_PAIRS = [(a, b) for a in range(EXPERTS_PER_GROUP) for b in range(a + 1, EXPERTS_PER_GROUP)]
CLASS_E_LO = [EXPERTS_PER_GROUP * (c // PAIRS_PER_GROUP) + _PAIRS[c % PAIRS_PER_GROUP][0] for c in range(N_CLASSES)]
CLASS_E_HI = [EXPERTS_PER_GROUP * (c // PAIRS_PER_GROUP) + _PAIRS[c % PAIRS_PER_GROUP][1] for c in range(N_CLASSES)]
CLASS_ROWS = 32
ROW_WORDS = 640


def _pack_halves(x):
    kk = x.shape[1] // 2
    lo = lax.bitcast_convert_type(x[:, :kk].astype(BF16).astype(F32), jnp.uint32)
    hi = lax.bitcast_convert_type(x[:, kk:].astype(BF16).astype(F32), jnp.uint32)
    return lax.bitcast_convert_type((lo >> 16) | hi, jnp.int32)


def _unpack_halves(p):
    u = lax.bitcast_convert_type(p, jnp.uint32)
    lo = lax.bitcast_convert_type(u << 16, F32)
    hi = lax.bitcast_convert_type(u & jnp.uint32(0xFFFF0000), F32)
    return jnp.concatenate([lo.astype(BF16), hi.astype(BF16)], axis=1)


def _outproj_kernel(a_ref, b_ref, x_ref, w_ref, gt_ref, g2_ref, sc_ref, sh_ref, rwt_ref, rb_ref,
                    x1_ref, rows_ref, route_ref, cnt_ref, carry_sc):
    first = (pl.program_id(0) == 0) & (pl.program_id(1) == 0)

    @pl.when(first)
    def _():
        carry_sc[...] = jnp.zeros_like(carry_sc)

    half = a_ref.shape[2]
    tm = a_ref.shape[1]
    y = _dot(a_ref[0], w_ref[:half, :]) + _dot(b_ref[0], w_ref[half:, :])
    x1 = x_ref[0] + gt_ref[0] * y
    x1_ref[0] = x1
    h2 = _prenorm(x1, g2_ref[...], sc_ref[0], sh_ref[0])
    rows_ref[0, :, :ROW_WORDS - LANES] = _pack_halves(h2)
    cls, w_lo, w_hi = _routing(_dot_nt(rwt_ref[...], h2.astype(BF16)), rb_ref[...])
    r128 = lax.broadcasted_iota(jnp.int32, (LANES, tm), 0)
    wts = jnp.where(r128 == 0, w_lo, jnp.where(r128 == 1, w_hi, 0.0))
    rows_ref[0, :, ROW_WORDS - LANES:] = lax.bitcast_convert_type(wts.T, jnp.int32)
    crow = lax.broadcasted_iota(jnp.int32, (carry_sc.shape[0], tm), 0).astype(F32)
    onehot = jnp.where(crow == cls, 1.0, 0.0)
    earlier = lax.broadcasted_iota(jnp.int32, (tm, tm), 0) < lax.broadcasted_iota(jnp.int32, (tm, tm), 1)
    prefix = _dot(onehot.astype(BF16), jnp.where(earlier, 1.0, 0.0).astype(BF16)) + carry_sc[:, 0:1]
    rank = jnp.sum(onehot * prefix, axis=0, keepdims=True)
    r8 = lax.broadcasted_iota(jnp.int32, (8, tm), 0)
    route_ref[0] = jnp.where(r8 == 0, cls, jnp.where(r8 == 1, rank, 0.0))
    carry_sc[...] = carry_sc[...] + jnp.sum(onehot, axis=1, keepdims=True)
    cnt_ref[...] = carry_sc[...]


def _outproj(a, bb, x, w, gt, g2, sc2, sh2, rw, rb, tm):
    b, s, d = x.shape
    half = a.shape[2]
    row = lambda i, j: (i, 0, 0)
    tok = lambda i, j: (i, j, 0)
    const = lambda i, j: (0, 0)
    return pl.pallas_call(
        _outproj_kernel,
        grid=(b, s // tm),
        in_specs=[pl.BlockSpec((1, tm, half), tok),
                  pl.BlockSpec((1, tm, half), tok),
                  pl.BlockSpec((1, tm, d), tok),
                  pl.BlockSpec((2 * half, d), const),
                  pl.BlockSpec((1, 1, d), row),
                  pl.BlockSpec((1, d), const),
                  pl.BlockSpec((1, 1, d), row),
                  pl.BlockSpec((1, 1, d), row),
                  pl.BlockSpec((N_EXPERTS, d), const),
                  pl.BlockSpec((N_EXPERTS, 1), const)],
        out_specs=(pl.BlockSpec((1, tm, d), tok),
                   pl.BlockSpec((1, tm, ROW_WORDS), tok),
                   pl.BlockSpec((1, 8, tm), lambda i, j: (i, 0, j)),
                   pl.BlockSpec((CLASS_ROWS, LANES), const)),
        out_shape=(jax.ShapeDtypeStruct((b, s, d), F32),
                   jax.ShapeDtypeStruct((b, s, ROW_WORDS), jnp.int32),
                   jax.ShapeDtypeStruct((b, 8, s), F32),
                   jax.ShapeDtypeStruct((CLASS_ROWS, LANES), F32)),
        scratch_shapes=[pltpu.VMEM((CLASS_ROWS, LANES), F32)],
        compiler_params=_cparams(("arbitrary", "arbitrary")),
        name="outproj_router",
    )(a, bb, x, w, gt, g2.reshape(1, d), sc2, sh2, rw, rb)


MOE_ROWS = 512
SC_CHUNK = 128


def _sc_workers():
    info = plsc.get_sparse_core_info()
    return info.num_cores, info.num_cores * info.num_subcores


def _sc_scatter_rows(rows, idx3, n_out):
    nw, k, ch = idx3.shape
    width = rows.shape[1]
    nc, _ = _sc_workers()
    mesh = plsc.VectorSubcoreMesh(core_axis_name="c", subcore_axis_name="s")

    @functools.partial(
        pl.kernel, mesh=mesh,
        out_type=jax.ShapeDtypeStruct((n_out, width), rows.dtype),
        scratch_types=[pltpu.VMEM((k, ch), jnp.int32), pltpu.VMEM((ch, width), rows.dtype), pltpu.SemaphoreType.DMA],
        name="moe_dispatch")
    def kern(rows_hbm, idx_hbm, out_hbm, idx_v, rows_v, sem):
        wid = lax.axis_index("s") * nc + lax.axis_index("c")
        pltpu.sync_copy(idx_hbm.at[wid], idx_v)

        @pl.loop(0, k)
        def _(j):
            pltpu.sync_copy(rows_hbm.at[pl.ds((wid * k + j) * ch, ch)], rows_v)
            pltpu.async_copy(rows_v, out_hbm.at[idx_v.at[j]], sem).wait()

    return kern(rows, idx3)


def _sc_gather_rows(table, idx3):
    nw, k, ch = idx3.shape
    width = table.shape[1]
    nc, _ = _sc_workers()
    mesh = plsc.VectorSubcoreMesh(core_axis_name="c", subcore_axis_name="s")

    @functools.partial(
        pl.kernel, mesh=mesh,
        out_type=jax.ShapeDtypeStruct((nw * k * ch, width), table.dtype),
        scratch_types=[pltpu.VMEM((k, ch), jnp.int32), pltpu.VMEM((ch, width), table.dtype), pltpu.SemaphoreType.DMA],
        name="moe_combine")
    def kern(table_hbm, idx_hbm, out_hbm, idx_v, rows_v, sem):
        wid = lax.axis_index("s") * nc + lax.axis_index("c")
        pltpu.sync_copy(idx_hbm.at[wid], idx_v)

        @pl.loop(0, k)
        def _(j):
            pltpu.async_copy(table_hbm.at[idx_v.at[j]], rows_v, sem).wait()
            pltpu.sync_copy(rows_v, out_hbm.at[pl.ds((wid * k + j) * ch, ch)])

    return kern(table, idx3)


def _experts_kernel(elo_ref, ehi_ref, nused_ref, x_ref, wg0_ref, wu0_ref, wd0_ref, wg1_ref, wu1_ref, wd1_ref, y_ref):
    @pl.when(pl.program_id(0) < nused_ref[0])
    def _():
        blk = x_ref[...]
        feat = ROW_WORDS - LANES
        h = _unpack_halves(blk[:, :feat])
        wts = lax.bitcast_convert_type(blk[:, feat:], F32)
        y = None
        for col, (wg_ref, wu_ref, wd_ref) in enumerate(((wg0_ref, wu0_ref, wd0_ref), (wg1_ref, wu1_ref, wd1_ref))):
            act = (_silu(_dot(h, wg_ref[0, 0].astype(BF16))) * _dot(h, wu_ref[0, 0].astype(BF16))
                   * wts[:, col:col + 1])
            part = _dot(act.astype(BF16), wd_ref[0, 0].astype(BF16))
            y = part if y is None else y + part
        y_ref[...] = _pack_halves(y)


def _experts(xs, blk_elo, blk_ehi, nused, wg, wu, wd, layer):
    n_rows = xs.shape[0]
    _, ne, d, f = wg.shape
    nblk = n_rows // MOE_ROWS
    rows = lambda i, elo, ehi, nu: (jnp.minimum(i, nu[0] - 1), 0)
    lo = lambda i, elo, ehi, nu: (layer, elo[i], 0, 0)
    hi = lambda i, elo, ehi, nu: (layer, ehi[i], 0, 0)
    return pl.pallas_call(
        _experts_kernel,
        grid_spec=pltpu.PrefetchScalarGridSpec(
            num_scalar_prefetch=3,
            grid=(nblk,),
            in_specs=[pl.BlockSpec((MOE_ROWS, ROW_WORDS), rows),
                      pl.BlockSpec((1, 1, d, f), lo), pl.BlockSpec((1, 1, d, f), lo), pl.BlockSpec((1, 1, f, d), lo),
                      pl.BlockSpec((1, 1, d, f), hi), pl.BlockSpec((1, 1, d, f), hi), pl.BlockSpec((1, 1, f, d), hi)],
            out_specs=pl.BlockSpec((MOE_ROWS, d // 2), lambda i, elo, ehi, nu: (i, 0))),
        out_shape=jax.ShapeDtypeStruct((n_rows, d // 2), jnp.int32),
        compiler_params=_cparams(("arbitrary",)),
        name="moe_experts",
    )(blk_elo, blk_ehi, nused, xs, wg, wu, wd, wg, wu, wd)


def _residual_kernel(x_ref, m_ref, gt_ref, o_ref):
    o_ref[0] = x_ref[0] + gt_ref[0] * _unpack_halves(m_ref[0]).astype(F32)


def _residual(x1, moe_p, gt, tm):
    b, s, d = x1.shape
    tok = lambda i, j: (i, j, 0)
    return pl.pallas_call(
        _residual_kernel,
        grid=(b, s // tm),
        in_specs=[pl.BlockSpec((1, tm, d), tok), pl.BlockSpec((1, tm, d // 2), tok),
                  pl.BlockSpec((1, 1, d), lambda i, j: (i, 0, 0))],
        out_specs=pl.BlockSpec((1, tm, d), tok),
        out_shape=jax.ShapeDtypeStruct((b, s, d), F32),
        compiler_params=_cparams(("arbitrary", "arbitrary")),
        name="moe_residual",
    )(x1, moe_p, gt)


def _moe(rows, route, counts, wg, wu, wd, layer):
    b, s, _ = rows.shape
    d = wg.shape[2]
    t = b * s
    _, nw = _sc_workers()
    cnt = counts[:N_CLASSES, 0].astype(jnp.int32)
    padded = ((cnt + MOE_ROWS - 1) // MOE_ROWS) * MOE_ROWS
    ends = jnp.cumsum(padded)
    n_rows = t + N_CLASSES * MOE_ROWS
    nblk = n_rows // MOE_ROWS
    cls = route[:, 0, :].reshape(t).astype(jnp.int32)
    rank = route[:, 1, :].reshape(t).astype(jnp.int32)
    pos = (ends - padded)[cls] + rank
    idx3 = pos.reshape(nw, t // (nw * SC_CHUNK), SC_CHUNK)
    nused = (ends[-1] // MOE_ROWS).reshape(1)
    blk_cls = jnp.sum((jnp.arange(nblk, dtype=jnp.int32)[:, None] * MOE_ROWS) >= ends[None, :], axis=1)
    blk_cls = jnp.minimum(blk_cls, blk_cls[jnp.maximum(nused[0] - 1, 0)])
    blk_elo = jnp.asarray(CLASS_E_LO, jnp.int32)[blk_cls]
    blk_ehi = jnp.asarray(CLASS_E_HI, jnp.int32)[blk_cls]
    xs = _sc_scatter_rows(rows.reshape(t, ROW_WORDS), idx3, n_rows)
    ys = _experts(xs, blk_elo, blk_ehi, nused, wg, wu, wd, layer)
    return _sc_gather_rows(ys, idx3).reshape(b, s, d // 2)


def kernel(x, c, ada_w, ada_b, norm1_g, norm2_g, ev_w_in, ev_conv_w, ev_dn_a_log, ev_dn_dt_bias, ev_dn_onorm_g, ev_fox_f_bias, ev_fox_qnorm_g, ev_fox_knorm_g, ev_w_out, od_w_in, od_conv_w, od_conv_b, od_lru_wr, od_lru_br, od_lru_wi, od_lru_bi, od_lru_lambda, od_sgu_norm_g, od_sgu_w, od_sgu_b, od_w_out, router_w, router_b, moe_w_gate, moe_w_up, moe_w_down):
    b, s, d = x.shape
    depth = ada_w.shape[0]
    tm = min(512, s)
    mod = _adaln(c, ada_w, ada_b).reshape(depth, b, 6, 1, d)
    rw = router_w.T.astype(BF16)
    rb = router_b.reshape(N_EXPERTS, 1)

    pending = None
    for layer in range(depth):
        sh1, sc1, gt1, sh2, sc2, gt2 = (mod[layer, :, k] for k in range(6))
        i = layer // 2
        if layer % 2 == 0:
            w = ev_w_in[i]
            nq = 3 * DN_HEADS * DN_DK
            nz = DN_HEADS * DN_DK
            nf = 3 * FOX_HEADS * FOX_DH
            o_a = nq + nz
            o_f = o_a + 2 * DN_HEADS
            o_ff = o_f + nf
            small_w = jnp.concatenate([w[:, o_a:o_f], w[:, o_ff:o_ff + FOX_HEADS]], axis=1)
            small_w = jnp.pad(small_w, ((0, 0), (0, LANES - small_w.shape[1])))
            w_all = jnp.concatenate([w[:, :o_a], w[:, o_f:o_ff], small_w], axis=1).astype(BF16)
            splits = ((0, nq), (nq, o_a), (o_a, o_a + nf), (o_a + nf, o_a + nf + LANES))
            res = _inproj(x, norm1_g[layer], sc1, sh1, w_all, splits,
                          (BF16, BF16, BF16, F32), (None,) * 4, tm, pending, ev_conv_w[i])
            if pending is not None:
                x, res = res[0], res[1:]
            dn, z, fox, small = res
            small_t = jnp.swapaxes(small[:, :, :16], 1, 2)
            zeros4 = jnp.zeros((4,), F32)
            mul = jnp.concatenate([-jnp.exp(ev_dn_a_log[i]), zeros4, -jnp.ones((4,), F32), zeros4])
            bias = jnp.concatenate([ev_dn_dt_bias[i], zeros4, ev_fox_f_bias[i], zeros4])
            pr = jnp.stack([mul, bias], axis=1)
            pc = jnp.pad(jnp.stack([mul, bias], axis=0), ((0, 0), (0, LANES - 16)))
            col, row = _gates(small, small_t, pc, pr)
            row_dn = row[:, :DN_HEADS].reshape(b, DN_HEADS, s // DN_GROUP, DN_GROUP).transpose(0, 2, 1, 3)
            o_dn = _deltanet(dn, z, col, row_dn, ev_dn_onorm_g[i])
            o_fox = _fox(fox, col, ev_fox_qnorm_g[i], ev_fox_knorm_g[i], min(512, s))
            mix_a, mix_b, w_out = o_dn, o_fox, ev_w_out[i]
        else:
            lw = od_lru_wr.shape[-1] * LRU_BLOCKS
            splits = ((0, lw), (lw, 2 * lw), (2 * lw, 2 * lw + od_sgu_w.shape[-1] * SGU_GROUPS),
                      (2 * lw + od_sgu_w.shape[-1] * SGU_GROUPS, od_w_in.shape[-1]))
            res = _inproj(x, norm1_g[layer], sc1, sh1, od_w_in[i].astype(BF16), splits,
                          (BF16,) * 3, (None, "gelu"), tm, pending,
                          sgu=(od_sgu_norm_g[i], od_sgu_w[i], od_sgu_b[i]))
            if pending is not None:
                x, res = res[0], res[1:]
            lx, lg, o_sgu = res
            o_lru = _lru(lx, lg, od_conv_w[i], od_conv_b[i], od_lru_wr[i], od_lru_br[i], od_lru_wi[i],
                         od_lru_bi[i], od_lru_lambda[i])
            mix_a, mix_b, w_out = o_lru, o_sgu, od_w_out[i]
        x1, rows, route, counts = _outproj(mix_a, mix_b, x, w_out.astype(BF16), gt1, norm2_g[layer], sc2, sh2,
                                           rw, rb, tm)
        x = x1
        pending = (_moe(rows, route, counts, moe_w_gate, moe_w_up, moe_w_down, layer), gt2)
    return _residual(x, *pending, tm)
```

```python
import functools

import jax
import jax.numpy as jnp
from jax import lax
from jax.experimental import pallas as pl
from jax.experimental.pallas import tpu as pltpu
from jax.experimental.pallas import tpu_sc as plsc

F32 = jnp.float32
BF16 = jnp.bfloat16
EPS = 1e-6
NEG_INF = float("-inf")

DN_HEADS = 4
DN_DK = 128
DN_CHUNK = 64
CONV_WIDTH = 4
FOX_HEADS = 4
FOX_DH = 128
LRU_BLOCKS = 4
LRU_C = 8.0
SGU_GROUPS = 4
SGU_CHUNK = 128
N_EXPERTS = 16
EXPERTS_PER_GROUP = 4
LANES = 128
SUBLANES = 8

VMEM_LIMIT = 48 * 1024 * 1024


def _cparams(sem):
    return pltpu.CompilerParams(dimension_semantics=sem, vmem_limit_bytes=VMEM_LIMIT)


def _dot(a, b):
    return jnp.dot(a, b, preferred_element_type=F32)


def _dot_nt(a, b):
    return lax.dot_general(a, b, (((1,), (1,)), ((), ())), preferred_element_type=F32)


def _dot_tn(a, b):
    return lax.dot_general(a, b, (((0,), (0,)), ((), ())), preferred_element_type=F32)


def _dot_exact(a, b):
    return jnp.dot(a, b, preferred_element_type=F32, precision=lax.Precision.HIGHEST)


def _sigmoid(x):
    return 0.5 + 0.5 * jnp.tanh(0.5 * x)


def _silu(x):
    hx = 0.5 * x
    return hx + hx * jnp.tanh(hx)


def _softplus(x):
    return jnp.maximum(x, 0.0) + jnp.log(1.0 + jnp.exp(-jnp.abs(x)))


def _gelu_tanh(x):
    c = 0.7978845608028654
    return 0.5 * x * (1.0 + jnp.tanh(c * (x + 0.044715 * (x * x * x))))


def _prenorm(x, g, scale, shift):
    ms = jnp.mean(x * x, axis=-1, keepdims=True)
    return (x * lax.rsqrt(ms + EPS) * g) * (1.0 + scale) + shift


def _shift_rows(x, d, fill=0.0):
    rows = lax.broadcasted_iota(jnp.int32, x.shape, 0)
    return jnp.where(rows >= d, pltpu.roll(x, d, axis=0), fill)


def _causal_conv(x, w_ref):
    k_w = w_ref.shape[0]
    acc = x * w_ref[k_w - 1:k_w, :]
    for d in range(1, k_w):
        acc = acc + _shift_rows(x, d) * w_ref[k_w - 1 - d:k_w - d, :]
    return acc


def _adaln_kernel(c_ref, w_ref, b_ref, o_ref):
    c = c_ref[...]
    ca = _silu(c).astype(BF16)
    o_ref[0] = _dot(ca, w_ref[0].astype(BF16)) + b_ref[0]


def _adaln(c, ada_w, ada_b):
    depth, d, n = ada_w.shape
    b = c.shape[0]
    tn = 1536
    return pl.pallas_call(
        _adaln_kernel,
        grid=(depth, n // tn),
        in_specs=[pl.BlockSpec((b, d), lambda l, j: (0, 0)),
                  pl.BlockSpec((1, d, tn), lambda l, j: (l, 0, j)),
                  pl.BlockSpec((1, 1, tn), lambda l, j: (l, 0, j))],
        out_specs=pl.BlockSpec((1, b, tn), lambda l, j: (l, 0, j)),
        out_shape=jax.ShapeDtypeStruct((depth, b, n), F32),
        compiler_params=_cparams(("arbitrary", "arbitrary")),
        name="adaln",
    )(c, ada_w, ada_b.reshape(depth, 1, n))


CONV_HALO = 8
CONV_SLAB = 512


def _inproj_kernel(*refs, col_splits, acts, pending, conv, sgu, cast_w):
    refs = list(refs)
    x_ref = refs.pop(0)
    if pending:
        m_ref, gtm_ref = refs.pop(0), refs.pop(0)
    g_ref, sc_ref, sh_ref, w_ref = (refs.pop(0) for _ in range(4))
    if cast_w:
        w_f32, w_ref = w_ref, refs.pop()

        @pl.when((pl.program_id(0) == 0) & (pl.program_id(1) == 0))
        def _():
            w_ref[...] = w_f32[...].astype(BF16)
    if conv:
        cw_ref = refs.pop(0)
        halo_sc = refs.pop()
    if sgu:
        sg_ref, sw_ref, sbt_ref = (refs.pop(0) for _ in range(3))
    if pending:
        xo_ref = refs.pop(0)
        x = x_ref[0] + gtm_ref[0] * _unpack_halves(m_ref[0]).astype(F32)
        xo_ref[0] = x
    else:
        x = x_ref[0]
    out_refs = refs
    h = _prenorm(x, g_ref[...], sc_ref[0], sh_ref[0]).astype(BF16)
    tm = x.shape[0]
    if conv:
        @pl.when(pl.program_id(1) == 0)
        def _():
            halo_sc[0:CONV_HALO, :] = jnp.zeros((CONV_HALO, halo_sc.shape[1]), F32)

        c0, c1 = col_splits[0]
        for s0 in range(c0, c1, CONV_SLAB):
            halo_sc[CONV_HALO:, s0 - c0:s0 - c0 + CONV_SLAB] = _dot(h, w_ref[:, s0:s0 + CONV_SLAB])
    n_plain = len(col_splits) - (2 if sgu else 0)
    for k, (o_ref, (c0, c1), act) in enumerate(zip(out_refs[:n_plain], col_splits[:n_plain], acts)):
        if conv and k == 0:
            continue
        p = _dot(h, w_ref[:, c0:c1])
        if act == "gelu":
            p = _gelu_tanh(p)
        o_ref[0] = p.astype(o_ref.dtype)
    if sgu:
        (u0, u1), (v0, v1) = col_splits[n_plain:]
        o_ref = out_refs[n_plain]
        u = _gelu_tanh(_dot(h, w_ref[:, u0:u1]))
        v = _gelu_tanh(_dot(h, w_ref[:, v0:v1]))
        vn = (v * lax.rsqrt(jnp.mean(v * v, axis=-1, keepdims=True) + EPS) * sg_ref[...]).astype(BF16)
        c = SGU_CHUNK
        gw = (u1 - u0) // SGU_GROUPS
        ri = lax.broadcasted_iota(jnp.int32, (c, c), 0)
        ci = lax.broadcasted_iota(jnp.int32, (c, c), 1)
        for g in range(SGU_GROUPS):
            wg = jnp.where(ri >= ci, sw_ref[g], 0.0).astype(BF16)
            bcol = sbt_ref[:, g:g + 1]
            for n in range(tm // c):
                rows = slice(n * c, (n + 1) * c)
                cols = slice(g * gw, (g + 1) * gw)
                mixed = _dot(wg, vn[rows, cols]) + bcol
                o_ref[0, rows, cols] = (u[rows, cols] * mixed).astype(o_ref.dtype)
    if conv:
        o_ref = out_refs[0]
        for t0 in range(0, halo_sc.shape[1], DN_DK):
            cols = slice(t0, t0 + DN_DK)
            acc = halo_sc[CONV_HALO:, cols] * cw_ref[CONV_WIDTH - 1:CONV_WIDTH, cols]
            for dd in range(1, CONV_WIDTH):
                acc = acc + halo_sc[pl.ds(CONV_HALO - dd, tm), cols] * cw_ref[CONV_WIDTH - 1 - dd:CONV_WIDTH - dd, cols]
            halo_sc[0:CONV_HALO, cols] = halo_sc[tm:tm + CONV_HALO, cols]
            yj = _silu(acc)
            if t0 < 2 * DN_HEADS * DN_DK:
                yj = yj * lax.rsqrt(jnp.sum(yj * yj, axis=-1, keepdims=True) + EPS)
            if t0 < DN_HEADS * DN_DK:
                yj = yj * (DN_DK ** -0.5)
            o_ref[0, :, cols] = yj.astype(o_ref.dtype)


def _inproj(x, g, scale, shift, w, col_splits, out_dtypes, acts, tm, pending=None, conv_w=None, sgu=None):
    b, s, d = x.shape
    n = w.shape[1]
    out_cols = list(col_splits) if sgu is None else list(col_splits[:-2]) + [col_splits[-2]]
    outs = tuple(jax.ShapeDtypeStruct((b, s, c1 - c0), dt) for (c0, c1), dt in zip(out_cols, out_dtypes))
    row = lambda i, j: (i, 0, 0)
    tok = lambda i, j: (i, j, 0)
    const = lambda i, j: (0, 0)
    in_specs = [pl.BlockSpec((1, tm, d), tok)]
    out_specs = tuple(pl.BlockSpec((1, tm, c1 - c0), tok) for (c0, c1) in out_cols)
    args = (x,)
    if pending is not None:
        in_specs += [pl.BlockSpec((1, tm, d // 2), tok), pl.BlockSpec((1, 1, d), row)]
        out_specs = (pl.BlockSpec((1, tm, d), tok),) + out_specs
        outs = (jax.ShapeDtypeStruct((b, s, d), F32),) + outs
        args += tuple(pending)
    in_specs += [pl.BlockSpec((1, d), const), pl.BlockSpec((1, 1, d), row), pl.BlockSpec((1, 1, d), row),
                 pl.BlockSpec((d, n), const)]
    args += (g.reshape(1, d), scale, shift, w)
    scratch = []
    if conv_w is not None:
        in_specs.append(pl.BlockSpec(conv_w.shape, const))
        args += (conv_w,)
        scratch.append(pltpu.VMEM((CONV_HALO + tm, conv_w.shape[1]), F32))
    if w.dtype != BF16:
        scratch.append(pltpu.VMEM((d, n), BF16))
    if sgu is not None:
        g_norm, w_s, b_s = sgu
        in_specs += [pl.BlockSpec((1, g_norm.shape[0]), const), pl.BlockSpec(w_s.shape, lambda i, j: (0, 0, 0)),
                     pl.BlockSpec((b_s.shape[1], b_s.shape[0]), const)]
        args += (g_norm.reshape(1, -1), w_s, b_s.T)
    return pl.pallas_call(
        functools.partial(_inproj_kernel, col_splits=col_splits, acts=acts, pending=pending is not None,
                          conv=conv_w is not None, sgu=sgu is not None, cast_w=w.dtype != BF16),
        grid=(b, s // tm),
        in_specs=in_specs,
        out_specs=out_specs,
        out_shape=outs,
        scratch_shapes=scratch,
        compiler_params=_cparams(("arbitrary", "arbitrary")),
        name="inproj",
    )(*args)


def _gates_kernel(sm_ref, smt_ref, pc_ref, pr_ref, col_ref, row_ref):
    s = sm_ref.shape[1]
    blk = LANES
    ri = lax.broadcasted_iota(jnp.int32, (blk, blk), 0)
    ci = lax.broadcasted_iota(jnp.int32, (blk, blk), 1)
    same_chunk = (ri // DN_CHUNK) == (ci // DN_CHUNK)
    tril = jnp.where(ri >= ci, 1.0, 0.0)
    tril_loc = jnp.where(same_chunk, tril, 0.0)
    triu_loc = jnp.where(same_chunk & (ri <= ci), 1.0, 0.0)
    lane = lax.broadcasted_iota(jnp.int32, (blk, LANES), 1)
    carry = jnp.zeros((1, LANES), F32)
    for j in range(s // blk):
        rows = slice(j * blk, (j + 1) * blk)
        xc = sm_ref[0, rows, :] + pc_ref[1:2, :]
        dec = pc_ref[0:1, :] * _softplus(jnp.where(lane < 4, xc, -xc))
        cum_glb = _dot_exact(tril, dec) + carry
        col_ref[0, rows, :] = jnp.where(lane < 4, _dot_exact(tril_loc, dec), jnp.where(lane < 8, _sigmoid(xc), cum_glb))
        carry = cum_glb[blk - 1:blk, :]
        decr = pr_ref[:, 0:1] * _softplus(smt_ref[0, :, rows] + pr_ref[:, 1:2])
        row_ref[0, :, rows] = _dot_exact(decr, triu_loc)


def _gates(small, small_t, pc, pr):
    b, s, _ = small.shape
    return pl.pallas_call(
        _gates_kernel,
        grid=(b,),
        in_specs=[pl.BlockSpec((1, s, LANES), lambda i: (i, 0, 0)),
                  pl.BlockSpec((1, 16, s), lambda i: (i, 0, 0)),
                  pl.BlockSpec((2, LANES), lambda i: (0, 0)),
                  pl.BlockSpec((16, 2), lambda i: (0, 0))],
        out_specs=(pl.BlockSpec((1, s, LANES), lambda i: (i, 0, 0)),
                   pl.BlockSpec((1, 16, s), lambda i: (i, 0, 0))),
        out_shape=(jax.ShapeDtypeStruct((b, s, LANES), F32), jax.ShapeDtypeStruct((b, 16, s), F32)),
        compiler_params=_cparams(("arbitrary",)),
        name="gates",
    )(small, small_t, pc, pr)


DN_PACK = 4
DN_GROUP = DN_PACK * DN_CHUNK
DN_ITER_GROUPS = 2


def _blockdiag(p):
    c, wide = p.shape
    t = jnp.concatenate([p] * (wide // c), axis=0)
    rb = lax.broadcasted_iota(jnp.int32, (wide, wide), 0) // c
    cb = lax.broadcasted_iota(jnp.int32, (wide, wide), 1) // c
    return jnp.where(rb == cb, t, 0.0).astype(BF16)


def _diag_blocks(m, c):
    wide = m.shape[1]
    cb = lax.broadcasted_iota(jnp.int32, (c, wide), 1) // c
    out = m[:c]
    for j in range(1, wide // c):
        out = jnp.where(cb == j, m[j * c:(j + 1) * c], out)
    return out


def _rows_to_blocks(col, c, wide):
    cb = lax.broadcasted_iota(jnp.int32, (c, wide), 1) // c
    out = jnp.broadcast_to(col[:c], (c, wide))
    for j in range(1, wide // c):
        out = jnp.where(cb == j, col[j * c:(j + 1) * c], out)
    return out


def _packed_unit_lower_inverse(lows):
    c, wide = lows[0].shape
    ri = lax.broadcasted_iota(jnp.int32, (c, wide), 0)
    ci = lax.broadcasted_iota(jnp.int32, (c, wide), 1) % c
    eye = jnp.where(ri == ci, 1.0, 0.0)
    n = range(len(lows))
    ds = [jnp.where((ri // 16) == (ci // 16), lo, 0.0) for lo in lows]
    xs = [eye - d for d in ds]
    ps = ds
    bds = [_blockdiag(p) for p in ps]
    for _ in range(3):
        ps = [_dot(ps[i].astype(BF16), bds[i]) for i in n]
        bds = [_blockdiag(p) for p in ps]
        xs = [xs[i] + _dot(xs[i].astype(BF16), bds[i]) for i in n]
    for width in (16, 32):
        sel = ((ri // (2 * width)) == (ci // (2 * width))) & ((ri // width) != (ci // width))
        offs = [_blockdiag(jnp.where(sel, lo, 0.0)) for lo in lows]
        ts = [_dot(xs[i].astype(BF16), offs[i]) for i in n]
        bdx = [_blockdiag(x) for x in xs]
        xs = [xs[i] - _dot(ts[i].astype(BF16), bdx[i]) for i in n]
    return xs


def _deltanet_kernel(dn_ref, z_ref, col_ref, row_ref, og_ref, o_ref, st_sc):
    s = dn_ref.shape[1]
    c = DN_CHUNK
    gt = DN_GROUP
    dk = DN_DK
    hd = DN_HEADS * dk
    heads = range(DN_HEADS)
    st_sc[...] = jnp.zeros_like(st_sc)

    ri = lax.broadcasted_iota(jnp.int32, (c, gt), 0)
    ci = lax.broadcasted_iota(jnp.int32, (c, gt), 1) % c
    og = og_ref[...]

    per_iter = DN_ITER_GROUPS if (s // gt) % DN_ITER_GROUPS == 0 else 1
    insts = [(gg, h) for gg in range(per_iter) for h in heads]
    ins = range(len(insts))

    def group(g, carry):
        r0 = [pl.multiple_of((g * per_iter + gg) * gt, gt) for gg in range(per_iter)]
        colg = [col_ref[0, pl.ds(r0[gg], gt), :] for gg in range(per_iter)]
        rowg = [row_ref[0, g * per_iter + gg] for gg in range(per_iter)]
        hs = [slice(h * dk, (h + 1) * dk) for h in heads]
        q = [dn_ref[0, pl.ds(r0[gg], gt), h * dk:(h + 1) * dk].astype(F32) for gg, h in insts]
        k = [dn_ref[0, pl.ds(r0[gg], gt), hd + h * dk:hd + (h + 1) * dk].astype(F32) for gg, h in insts]
        v = [dn_ref[0, pl.ds(r0[gg], gt), 2 * hd + h * dk:2 * hd + (h + 1) * dk].astype(F32) for gg, h in insts]
        gc = [colg[gg][:, h:h + 1] for gg, h in insts]
        beta = [colg[gg][:, DN_HEADS + h:DN_HEADS + h + 1] for gg, h in insts]
        grow = [rowg[gg][h:h + 1, :] for gg, h in insts]
        glast = [jnp.concatenate([jnp.broadcast_to(gc[i][(j + 1) * c - 1:(j + 1) * c], (c, 1))
                                  for j in range(DN_PACK)], axis=0) for i in ins]
        eg = [jnp.exp(gc[i]) for i in ins]
        kbf = [k[i].astype(BF16) for i in ins]
        both = [_dot_nt(jnp.concatenate([(k[i] * beta[i]).astype(BF16), q[i].astype(BF16)], axis=0), kbf[i])
                for i in ins]
        decay = [jnp.exp(jnp.where(ri >= ci, _rows_to_blocks(gc[i], c, gt) - grow[i], NEG_INF)) for i in ins]
        kk = [_diag_blocks(both[i][:gt], c) * decay[i] for i in ins]
        qk = [_blockdiag(_diag_blocks(both[i][gt:], c) * decay[i]) for i in ins]
        t_inv = _packed_unit_lower_inverse([jnp.where(ri > ci, kk[i], 0.0) for i in ins])
        rhs = [jnp.concatenate([k[i] * (beta[i] * eg[i]), v[i] * beta[i]], axis=1).astype(BF16) for i in ins]
        wu = [_dot(_blockdiag(t_inv[i]), rhs[i]).astype(BF16) for i in ins]
        qwu = [_dot(qk[i], wu[i]) for i in ins]
        qp = [(q[i] * eg[i] - qwu[i][:, :dk]).astype(BF16) for i in ins]
        kdec = [(k[i] * jnp.exp(glast[i] - gc[i])).astype(BF16) for i in ins]
        mb = [[_dot_tn(kdec[i][j * c:(j + 1) * c], wu[i][j * c:(j + 1) * c]) for i in ins]
              for j in range(DN_PACK)]
        for gg in range(per_iter):
            mine = [gg * DN_HEADS + h for h in heads]
            outs = [[] for _ in heads]
            for j in range(DN_PACK):
                rows = slice(j * c, (j + 1) * c)
                state = [st_sc[h] for h in heads]
                lhs = [jnp.concatenate([qp[i][rows], mb[j][i][:, :dk].astype(BF16)], axis=0) for i in mine]
                r = [_dot(lhs[h], state[h].astype(BF16)) for h in heads]
                for h, i in zip(heads, mine):
                    gl = jnp.exp(glast[i][j * c:j * c + 1])
                    st_sc[h] = state[h] * gl - r[h][c:] + mb[j][i][:, dk:]
                    outs[h].append(r[h][:c] + qwu[i][rows, dk:])
            for h in heads:
                o = jnp.concatenate(outs[h], axis=0)
                on = o * lax.rsqrt(jnp.mean(o * o, axis=-1, keepdims=True) + EPS) * og
                zz = z_ref[0, pl.ds(r0[gg], gt), hs[h]].astype(F32)
                o_ref[0, pl.ds(r0[gg], gt), hs[h]] = (on * _silu(zz)).astype(o_ref.dtype)
        return carry

    lax.fori_loop(0, s // (gt * per_iter), group, 0)


def _deltanet(dn, z, col, row, onorm_g):
    b, s, w3 = dn.shape
    hd = DN_HEADS * DN_DK
    return pl.pallas_call(
        _deltanet_kernel,
        grid=(b,),
        in_specs=[pl.BlockSpec((1, s, w3), lambda i: (i, 0, 0)),
                  pl.BlockSpec((1, s, hd), lambda i: (i, 0, 0)),
                  pl.BlockSpec((1, s, LANES), lambda i: (i, 0, 0)),
                  pl.BlockSpec((1, s // DN_GROUP, DN_HEADS, DN_GROUP), lambda i: (i, 0, 0, 0)),
                  pl.BlockSpec((1, DN_DK), lambda i: (0, 0))],
        out_specs=pl.BlockSpec((1, s, hd), lambda i: (i, 0, 0)),
        out_shape=jax.ShapeDtypeStruct((b, s, hd), BF16),
        scratch_shapes=[pltpu.VMEM((DN_HEADS, DN_DK, DN_DK), F32)],
        compiler_params=_cparams(("arbitrary",)),
        name="deltanet",
    )(dn, z, col, row, onorm_g.reshape(1, DN_DK))


def _split3(x):
    hi = x.astype(BF16).astype(F32)
    r = x - hi
    mid = r.astype(BF16).astype(F32)
    return hi, mid, r - mid


def _fox_kernel(q_ref, k_ref, v_ref, colq_ref, colk_ref, qg_ref, kg_ref, o_ref, ka_sc, va_sc, m_sc, acc_sc, *, tq):
    qi = pl.program_id(1)
    s = k_ref.shape[1]
    dh = FOX_DH

    def bias_lanes(col):
        lane = lax.broadcasted_iota(jnp.int32, col.shape, 1)
        hi, mid, lo = _split3(col)
        return jnp.where((lane >= 8) & (lane < 8 + FOX_HEADS), hi,
                         jnp.where((lane >= 24) & (lane < 24 + FOX_HEADS), pltpu.roll(mid, 16, axis=1),
                                   jnp.where((lane >= 40) & (lane < 40 + FOX_HEADS), pltpu.roll(lo, 32, axis=1), 0.0)))

    @pl.when(qi == 0)
    def _():
        lane = lax.broadcasted_iota(jnp.int32, (s, LANES), 1)
        ones_col = jnp.where(lane == 0, 1.0, 0.0).astype(BF16)
        ones_k = (lane >= 56) & (lane < 96) & ((lane % 16) >= 8) & ((lane % 16) < 8 + FOX_HEADS)
        ext_k = (jnp.where(ones_k, 1.0, 0.0) - bias_lanes(colk_ref[0])).astype(BF16)
        for h in range(FOX_HEADS):
            cols = slice(h * dh, (h + 1) * dh)
            kf = k_ref[0, :, cols].astype(F32)
            kn = kf * lax.rsqrt(jnp.mean(kf * kf, axis=-1, keepdims=True) + EPS) * kg_ref[...]
            ka_sc[h, :, :dh] = kn.astype(BF16)
            ka_sc[h, :, dh:] = ext_k
            va_sc[h, :, :dh] = v_ref[0, :, cols]
            va_sc[h, :, dh:] = ones_col

    lane = lax.broadcasted_iota(jnp.int32, (tq, LANES), 1)
    cq = pltpu.roll(bias_lanes(colq_ref[0]), 48, axis=1)
    qa = []
    for h in range(FOX_HEADS):
        cols = slice(h * dh, (h + 1) * dh)
        qf = q_ref[0, :, cols].astype(F32)
        qn = qf * lax.rsqrt(jnp.mean(qf * qf, axis=-1, keepdims=True) + EPS) * qg_ref[...] * (dh ** -0.5)
        mine = ((lane % 16) == 8 + h) & (lane < 96)
        ext = jnp.where(mine, jnp.where(lane < 48, 1.0, cq), 0.0)
        qa.append(jnp.concatenate([qn.astype(BF16), ext.astype(BF16)], axis=1))
    m_sc[...] = jnp.full(m_sc.shape, NEG_INF, F32)
    acc_sc[...] = jnp.zeros_like(acc_sc)
    causal = lax.broadcasted_iota(jnp.int32, (tq, tq), 0) >= lax.broadcasted_iota(jnp.int32, (tq, tq), 1)

    def step(k0, masked):
        heads = range(FOX_HEADS)
        logits = [_dot_nt(qa[h], ka_sc[h, pl.ds(k0, tq), :]) for h in heads]
        if masked:
            logits = [jnp.where(causal, lg, NEG_INF) for lg in logits]
        ps, alphas = [], []
        for h in heads:
            m_old = m_sc[h]
            m_new = jnp.maximum(m_old, jnp.max(logits[h], axis=1, keepdims=True))
            m_sc[h] = m_new
            alphas.append(jnp.exp(m_old - m_new))
            ps.append(jnp.exp(logits[h] - jnp.concatenate([m_new] * (tq // LANES), axis=1)).astype(BF16))
        for h in heads:
            pv = _dot(ps[h], va_sc[h, pl.ds(k0, tq), :])
            acc_sc[h] = acc_sc[h] * jnp.concatenate([alphas[h], alphas[h]], axis=1) + pv

    def body(j, carry):
        step(pl.multiple_of(j * tq, tq), False)
        return carry

    lax.fori_loop(0, qi, body, 0)
    step(pl.multiple_of(qi * tq, tq), True)
    for h in range(FOX_HEADS):
        acc = acc_sc[h]
        o_ref[0, :, h * dh:(h + 1) * dh] = (acc[:, :dh] / acc[:, dh:dh + 1]).astype(o_ref.dtype)


def _fox(fox, col, qg, kg, tq):
    b, s, _ = fox.shape
    hd = FOX_HEADS * FOX_DH
    return pl.pallas_call(
        functools.partial(_fox_kernel, tq=tq),
        grid=(b, s // tq),
        in_specs=[pl.BlockSpec((1, tq, hd), lambda i, j: (i, j, 0)),
                  pl.BlockSpec((1, s, hd), lambda i, j: (i, 0, 1)),
                  pl.BlockSpec((1, s, hd), lambda i, j: (i, 0, 2)),
                  pl.BlockSpec((1, tq, LANES), lambda i, j: (i, j, 0)),
                  pl.BlockSpec((1, s, LANES), lambda i, j: (i, 0, 0)),
                  pl.BlockSpec((1, FOX_DH), lambda i, j: (0, 0)),
                  pl.BlockSpec((1, FOX_DH), lambda i, j: (0, 0))],
        out_specs=pl.BlockSpec((1, tq, hd), lambda i, j: (i, j, 0)),
        out_shape=jax.ShapeDtypeStruct((b, s, hd), BF16),
        scratch_shapes=[pltpu.VMEM((FOX_HEADS, s, 2 * FOX_DH), BF16), pltpu.VMEM((FOX_HEADS, s, 2 * FOX_DH), BF16),
                        pltpu.VMEM((FOX_HEADS, tq, LANES), F32), pltpu.VMEM((FOX_HEADS, tq, 2 * FOX_DH), F32)],
        compiler_params=_cparams(("arbitrary", "arbitrary")),
        name="fox",
    )(fox, fox, fox, col, col, qg.reshape(1, FOX_DH), kg.reshape(1, FOX_DH))


def _lru_kernel(x_ref, gate_ref, cw_ref, cb_ref, wr_ref, br_ref, wi_ref, bi_ref, lam_ref, o_ref, a_sc, b_sc):
    s = x_ref.shape[1]
    x = _causal_conv(x_ref[0].astype(F32), cw_ref) + cb_ref[...]
    xb = x.astype(BF16)
    r = _sigmoid(_dot(xb, wr_ref[0].astype(BF16)) + br_ref[...])
    i = _sigmoid(_dot(xb, wi_ref[0].astype(BF16)) + bi_ref[...])
    log_a = (-LRU_C) * r * _softplus(-lam_ref[...])
    a = jnp.exp(log_a)
    bb = jnp.sqrt(1.0 - a * a) * (i * x)
    nt = s // SUBLANES
    a3 = a.reshape(nt, SUBLANES, a.shape[1])
    b3 = bb.reshape(nt, SUBLANES, a.shape[1])
    sub = lax.broadcasted_iota(jnp.int32, a3.shape, 1)
    d = 1
    while d < SUBLANES:
        keep = sub >= d
        b3 = a3 * jnp.where(keep, pltpu.roll(b3, d, axis=1), 0.0) + b3
        a3 = a3 * jnp.where(keep, pltpu.roll(a3, d, axis=1), 1.0)
        d *= 2
    a_sc[...] = a3.reshape(s, a.shape[1])
    b_sc[...] = b3.reshape(s, a.shape[1])
    at = a_sc[pl.ds(SUBLANES - 1, nt, stride=SUBLANES), :]
    bt = b_sc[pl.ds(SUBLANES - 1, nt, stride=SUBLANES), :]
    d = 1
    while d < nt:
        bt = at * _shift_rows(bt, d, 0.0) + bt
        at = at * _shift_rows(at, d, 1.0)
        d *= 2
    h_prev = _shift_rows(bt, 1, 0.0)
    bb = (a3 * h_prev[:, None, :] + b3).reshape(s, a.shape[1])
    o_ref[0] = (bb * gate_ref[0].astype(F32)).astype(o_ref.dtype)


def _lru(lx, lg, conv_w, conv_b, wr, br, wi, bi, lam):
    b, s, wd = lx.shape
    blk = wd // LRU_BLOCKS
    vec = lambda i, j: (0, j)
    return pl.pallas_call(
        _lru_kernel,
        grid=(b, LRU_BLOCKS),
        in_specs=[pl.BlockSpec((1, s, blk), lambda i, j: (i, 0, j)),
                  pl.BlockSpec((1, s, blk), lambda i, j: (i, 0, j)),
                  pl.BlockSpec((CONV_WIDTH, blk), vec),
                  pl.BlockSpec((1, blk), vec),
                  pl.BlockSpec((1, blk, blk), lambda i, j: (j, 0, 0)),
                  pl.BlockSpec((1, blk), vec),
                  pl.BlockSpec((1, blk, blk), lambda i, j: (j, 0, 0)),
                  pl.BlockSpec((1, blk), vec),
                  pl.BlockSpec((1, blk), vec)],
        out_specs=pl.BlockSpec((1, s, blk), lambda i, j: (i, 0, j)),
        out_shape=jax.ShapeDtypeStruct((b, s, wd), BF16),
        scratch_shapes=[pltpu.VMEM((s, blk), F32), pltpu.VMEM((s, blk), F32)],
        compiler_params=_cparams(("arbitrary", "arbitrary")),
        name="rglru",
    )(lx, lg, conv_w, conv_b.reshape(1, wd), wr, br.reshape(1, wd), wi, bi.reshape(1, wd), lam.reshape(1, wd))


def _routing(logits, rb):
    ne, tm = logits.shape
    row = lax.broadcasted_iota(jnp.int32, (ne, tm), 0)
    row_f = row.astype(F32)
    ex = jnp.exp(logits - jnp.max(logits, axis=0, keepdims=True))
    probs = ex / jnp.sum(ex, axis=0, keepdims=True)
    sel = probs + rb

    def top2(vals, idx):
        m1 = jnp.max(vals, axis=0, keepdims=True)
        i1 = jnp.min(jnp.where(vals == m1, idx, float(ne)), axis=0, keepdims=True)
        rest = jnp.where(idx == i1, NEG_INF, vals)
        m2 = jnp.max(rest, axis=0, keepdims=True)
        return m1, i1, m2, rest

    best = None
    g_idx = None
    grp = row // EXPERTS_PER_GROUP
    for g in range(ne // EXPERTS_PER_GROUP):
        m1, _, m2, _ = top2(jnp.where(grp == g, sel, NEG_INF), row_f)
        score = m1 + m2
        if g == 0:
            best, g_idx = score, jnp.zeros((1, tm), jnp.int32)
        else:
            upd = score > best
            best = jnp.where(upd, score, best)
            g_idx = jnp.where(upd, g, g_idx)
    _, i1, m2, rest = top2(jnp.where(grp == g_idx, sel, NEG_INF), row_f)
    i2 = jnp.min(jnp.where(rest == m2, row_f, float(ne)), axis=0, keepdims=True)
    p1 = jnp.sum(jnp.where(row_f == i1, probs, 0.0), axis=0, keepdims=True)
    p2 = jnp.sum(jnp.where(row_f == i2, probs, 0.0), axis=0, keepdims=True)
    den = p1 + p2
    first_lo = i1 < i2
    a = jnp.where(first_lo, i1, i2) - EXPERTS_PER_GROUP * g_idx.astype(F32)
    bhi = jnp.where(first_lo, i2, i1) - EXPERTS_PER_GROUP * g_idx.astype(F32)
    pair = a * (7.0 - a) * 0.5 + (bhi - a - 1.0)
    cls = PAIRS_PER_GROUP * g_idx.astype(F32) + pair
    w_lo = jnp.where(first_lo, p1, p2) / den
    w_hi = jnp.where(first_lo, p2, p1) / den
    return cls, w_lo, w_hi


PAIRS_PER_GROUP = EXPERTS_PER_GROUP * (EXPERTS_PER_GROUP - 1) // 2
N_CLASSES = (N_EXPERTS // EXPERTS_PER_GROUP) * PAIRS_PER_GROUP
_PAIRS = [(a, b) for a in range(EXPERTS_PER_GROUP) for b in range(a + 1, EXPERTS_PER_GROUP)]
CLASS_E_LO = [EXPERTS_PER_GROUP * (c // PAIRS_PER_GROUP) + _PAIRS[c % PAIRS_PER_GROUP][0] for c in range(N_CLASSES)]
CLASS_E_HI = [EXPERTS_PER_GROUP * (c // PAIRS_PER_GROUP) + _PAIRS[c % PAIRS_PER_GROUP][1] for c in range(N_CLASSES)]
CLASS_ROWS = 32
ROW_FEAT = 512
ROW_WORDS = 640


def _pack_halves(x):
    kk = x.shape[1] // 2
    lo = lax.bitcast_convert_type(x[:, :kk].astype(BF16).astype(F32), jnp.uint32)
    hi = lax.bitcast_convert_type(x[:, kk:].astype(BF16).astype(F32), jnp.uint32)
    return lax.bitcast_convert_type((lo >> 16) | hi, jnp.int32)


def _unpack_halves(p):
    u = lax.bitcast_convert_type(p, jnp.uint32)
    lo = lax.bitcast_convert_type(u << 16, F32)
    hi = lax.bitcast_convert_type(u & jnp.uint32(0xFFFF0000), F32)
    return jnp.concatenate([lo.astype(BF16), hi.astype(BF16)], axis=1)


def _outproj_kernel(a_ref, b_ref, x_ref, wf_ref, gt_ref, g2_ref, sc_ref, sh_ref, rwt_ref, rb_ref,
                    x1_ref, rows_ref, route_ref, cnt_ref, carry_sc, w_ref):
    first = (pl.program_id(0) == 0) & (pl.program_id(1) == 0)

    @pl.when(first)
    def _():
        carry_sc[...] = jnp.zeros_like(carry_sc)
        w_ref[...] = wf_ref[...].astype(BF16)

    half = a_ref.shape[2]
    tm = a_ref.shape[1]
    y = _dot(a_ref[0], w_ref[:half, :]) + _dot(b_ref[0], w_ref[half:, :])
    x1 = x_ref[0] + gt_ref[0] * y
    x1_ref[0] = x1
    h2 = _prenorm(x1, g2_ref[...], sc_ref[0], sh_ref[0])
    rows_ref[0, :, :ROW_FEAT] = _pack_halves(h2)
    cls, w_lo, w_hi = _routing(_dot_nt(rwt_ref[...], h2.astype(BF16)), rb_ref[...])
    r128 = lax.broadcasted_iota(jnp.int32, (LANES, tm), 0)
    wts = jnp.where(r128 == 0, w_lo, jnp.where(r128 == 1, w_hi, 0.0))
    rows_ref[0, :, ROW_FEAT:] = lax.bitcast_convert_type(wts.T[:, :ROW_WORDS - ROW_FEAT], jnp.int32)
    crow = lax.broadcasted_iota(jnp.int32, (carry_sc.shape[0], tm), 0).astype(F32)
    onehot = jnp.where(crow == cls, 1.0, 0.0)
    earlier = lax.broadcasted_iota(jnp.int32, (tm, tm), 0) < lax.broadcasted_iota(jnp.int32, (tm, tm), 1)
    prefix = _dot(onehot.astype(BF16), jnp.where(earlier, 1.0, 0.0).astype(BF16)) + carry_sc[:, 0:1]
    rank = jnp.sum(onehot * prefix, axis=0, keepdims=True)
    r8 = lax.broadcasted_iota(jnp.int32, (8, tm), 0)
    route_ref[0] = jnp.where(r8 == 0, cls, jnp.where(r8 == 1, rank, 0.0))
    carry_sc[...] = carry_sc[...] + jnp.sum(onehot, axis=1, keepdims=True)
    cnt_ref[...] = carry_sc[...]


def _outproj(a, bb, x, w, gt, g2, sc2, sh2, rw, rb, tm):
    b, s, d = x.shape
    half = a.shape[2]
    row = lambda i, j: (i, 0, 0)
    tok = lambda i, j: (i, j, 0)
    const = lambda i, j: (0, 0)
    return pl.pallas_call(
        _outproj_kernel,
        grid=(b, s // tm),
        in_specs=[pl.BlockSpec((1, tm, half), tok),
                  pl.BlockSpec((1, tm, half), tok),
                  pl.BlockSpec((1, tm, d), tok),
                  pl.BlockSpec((2 * half, d), const),
                  pl.BlockSpec((1, 1, d), row),
                  pl.BlockSpec((1, d), const),
                  pl.BlockSpec((1, 1, d), row),
                  pl.BlockSpec((1, 1, d), row),
                  pl.BlockSpec((N_EXPERTS, d), const),
                  pl.BlockSpec((N_EXPERTS, 1), const)],
        out_specs=(pl.BlockSpec((1, tm, d), tok),
                   pl.BlockSpec((1, tm, ROW_WORDS), tok),
                   pl.BlockSpec((1, 8, tm), lambda i, j: (i, 0, j)),
                   pl.BlockSpec((CLASS_ROWS, LANES), const)),
        out_shape=(jax.ShapeDtypeStruct((b, s, d), F32),
                   jax.ShapeDtypeStruct((b, s, ROW_WORDS), jnp.int32),
                   jax.ShapeDtypeStruct((b, 8, s), F32),
                   jax.ShapeDtypeStruct((CLASS_ROWS, LANES), F32)),
        scratch_shapes=[pltpu.VMEM((CLASS_ROWS, LANES), F32), pltpu.VMEM((2 * half, d), BF16)],
        compiler_params=_cparams(("arbitrary", "arbitrary")),
        name="outproj_router",
    )(a, bb, x, w, gt, g2.reshape(1, d), sc2, sh2, rw, rb)


MOE_ROWS = 512
SC_CHUNK = 128


def _sc_workers():
    info = plsc.get_sparse_core_info()
    return info.num_cores, info.num_cores * info.num_subcores


def _sc_scatter_rows(rows, idx3, n_out):
    nw, k, ch = idx3.shape
    width = rows.shape[1]
    nc, _ = _sc_workers()
    mesh = plsc.VectorSubcoreMesh(core_axis_name="c", subcore_axis_name="s")

    @functools.partial(
        pl.kernel, mesh=mesh,
        out_type=jax.ShapeDtypeStruct((n_out, width), rows.dtype),
        scratch_types=[pltpu.VMEM((k, ch), jnp.int32), pltpu.VMEM((ch, width), rows.dtype), pltpu.SemaphoreType.DMA],
        name="moe_dispatch")
    def kern(rows_hbm, idx_hbm, out_hbm, idx_v, rows_v, sem):
        wid = lax.axis_index("s") * nc + lax.axis_index("c")
        pltpu.sync_copy(idx_hbm.at[wid], idx_v)

        @pl.loop(0, k)
        def _(j):
            pltpu.sync_copy(rows_hbm.at[pl.ds((wid * k + j) * ch, ch)], rows_v)
            pltpu.async_copy(rows_v, out_hbm.at[idx_v.at[j]], sem).wait()

    return kern(rows, idx3)


def _sc_gather_rows(table, idx3):
    nw, k, ch = idx3.shape
    width = table.shape[1]
    nc, _ = _sc_workers()
    mesh = plsc.VectorSubcoreMesh(core_axis_name="c", subcore_axis_name="s")

    @functools.partial(
        pl.kernel, mesh=mesh,
        out_type=jax.ShapeDtypeStruct((nw * k * ch, width), table.dtype),
        scratch_types=[pltpu.VMEM((k, ch), jnp.int32), pltpu.VMEM((ch, width), table.dtype), pltpu.SemaphoreType.DMA],
        name="moe_combine")
    def kern(table_hbm, idx_hbm, out_hbm, idx_v, rows_v, sem):
        wid = lax.axis_index("s") * nc + lax.axis_index("c")
        pltpu.sync_copy(idx_hbm.at[wid], idx_v)

        @pl.loop(0, k)
        def _(j):
            pltpu.async_copy(table_hbm.at[idx_v.at[j]], rows_v, sem).wait()
            pltpu.sync_copy(rows_v, out_hbm.at[pl.ds((wid * k + j) * ch, ch)])

    return kern(table, idx3)


def _experts_kernel(elo_ref, ehi_ref, nused_ref, x_ref, wg0_ref, wu0_ref, wd0_ref, wg1_ref, wu1_ref, wd1_ref, y_ref):
    @pl.when(pl.program_id(0) < nused_ref[0])
    def _():
        blk = x_ref[...]
        feat = ROW_FEAT
        h = _unpack_halves(blk[:, :feat])
        wts = lax.bitcast_convert_type(blk[:, feat:], F32)
        y = None
        for col, (wg_ref, wu_ref, wd_ref) in enumerate(((wg0_ref, wu0_ref, wd0_ref), (wg1_ref, wu1_ref, wd1_ref))):
            act = (_silu(_dot(h, wg_ref[0, 0].astype(BF16))) * _dot(h, wu_ref[0, 0].astype(BF16))
                   * wts[:, col:col + 1])
            part = _dot(act.astype(BF16), wd_ref[0, 0].astype(BF16))
            y = part if y is None else y + part
        y_ref[...] = _pack_halves(y)


def _experts(xs, blk_elo, blk_ehi, nused, wg, wu, wd, layer):
    n_rows = xs.shape[0]
    _, ne, d, f = wg.shape
    nblk = n_rows // MOE_ROWS
    rows = lambda i, elo, ehi, nu: (jnp.minimum(i, nu[0] - 1), 0)
    lo = lambda i, elo, ehi, nu: (layer, elo[i], 0, 0)
    hi = lambda i, elo, ehi, nu: (layer, ehi[i], 0, 0)
    return pl.pallas_call(
        _experts_kernel,
        grid_spec=pltpu.PrefetchScalarGridSpec(
            num_scalar_prefetch=3,
            grid=(nblk,),
            in_specs=[pl.BlockSpec((MOE_ROWS, ROW_WORDS), rows),
                      pl.BlockSpec((1, 1, d, f), lo), pl.BlockSpec((1, 1, d, f), lo), pl.BlockSpec((1, 1, f, d), lo),
                      pl.BlockSpec((1, 1, d, f), hi), pl.BlockSpec((1, 1, d, f), hi), pl.BlockSpec((1, 1, f, d), hi)],
            out_specs=pl.BlockSpec((MOE_ROWS, d // 2), lambda i, elo, ehi, nu: (i, 0))),
        out_shape=jax.ShapeDtypeStruct((n_rows, d // 2), jnp.int32),
        compiler_params=_cparams(("arbitrary",)),
        name="moe_experts",
    )(blk_elo, blk_ehi, nused, xs, wg, wu, wd, wg, wu, wd)


def _residual_kernel(x_ref, m_ref, gt_ref, o_ref):
    o_ref[0] = x_ref[0] + gt_ref[0] * _unpack_halves(m_ref[0]).astype(F32)


def _residual(x1, moe_p, gt, tm):
    b, s, d = x1.shape
    tok = lambda i, j: (i, j, 0)
    return pl.pallas_call(
        _residual_kernel,
        grid=(b, s // tm),
        in_specs=[pl.BlockSpec((1, tm, d), tok), pl.BlockSpec((1, tm, d // 2), tok),
                  pl.BlockSpec((1, 1, d), lambda i, j: (i, 0, 0))],
        out_specs=pl.BlockSpec((1, tm, d), tok),
        out_shape=jax.ShapeDtypeStruct((b, s, d), F32),
        compiler_params=_cparams(("arbitrary", "arbitrary")),
        name="moe_residual",
    )(x1, moe_p, gt)


def _moe(rows, route, counts, wg, wu, wd, layer):
    b, s, _ = rows.shape
    d = wg.shape[2]
    t = b * s
    _, nw = _sc_workers()
    cnt = counts[:N_CLASSES, 0].astype(jnp.int32)
    padded = ((cnt + MOE_ROWS - 1) // MOE_ROWS) * MOE_ROWS
    ends = jnp.cumsum(padded)
    n_rows = t + N_CLASSES * MOE_ROWS
    nblk = n_rows // MOE_ROWS
    cls = route[:, 0, :].reshape(t).astype(jnp.int32)
    rank = route[:, 1, :].reshape(t).astype(jnp.int32)
    pos = (ends - padded)[cls] + rank
    idx3 = pos.reshape(nw, t // (nw * SC_CHUNK), SC_CHUNK)
    nused = (ends[-1] // MOE_ROWS).reshape(1)
    blk_cls = jnp.sum((jnp.arange(nblk, dtype=jnp.int32)[:, None] * MOE_ROWS) >= ends[None, :], axis=1)
    blk_cls = jnp.minimum(blk_cls, blk_cls[jnp.maximum(nused[0] - 1, 0)])
    blk_elo = jnp.asarray(CLASS_E_LO, jnp.int32)[blk_cls]
    blk_ehi = jnp.asarray(CLASS_E_HI, jnp.int32)[blk_cls]
    xs = _sc_scatter_rows(rows.reshape(t, ROW_WORDS), idx3, n_rows)
    ys = _experts(xs, blk_elo, blk_ehi, nused, wg, wu, wd, layer)
    return _sc_gather_rows(ys, idx3).reshape(b, s, d // 2)


def kernel(x, c, ada_w, ada_b, norm1_g, norm2_g, ev_w_in, ev_conv_w, ev_dn_a_log, ev_dn_dt_bias, ev_dn_onorm_g, ev_fox_f_bias, ev_fox_qnorm_g, ev_fox_knorm_g, ev_w_out, od_w_in, od_conv_w, od_conv_b, od_lru_wr, od_lru_br, od_lru_wi, od_lru_bi, od_lru_lambda, od_sgu_norm_g, od_sgu_w, od_sgu_b, od_w_out, router_w, router_b, moe_w_gate, moe_w_up, moe_w_down):
    b, s, d = x.shape
    depth = ada_w.shape[0]
    tm = min(512, s)
    mod = _adaln(c, ada_w, ada_b).reshape(depth, b, 6, 1, d)
    rw = router_w.T.astype(BF16)
    rb = router_b.reshape(N_EXPERTS, 1)

    pending = None
    for layer in range(depth):
        sh1, sc1, gt1, sh2, sc2, gt2 = (mod[layer, :, k] for k in range(6))
        i = layer // 2
        if layer % 2 == 0:
            w = ev_w_in[i]
            nq = 3 * DN_HEADS * DN_DK
            nz = DN_HEADS * DN_DK
            nf = 3 * FOX_HEADS * FOX_DH
            o_a = nq + nz
            o_f = o_a + 2 * DN_HEADS
            o_ff = o_f + nf
            small_w = jnp.concatenate([w[:, o_a:o_f], w[:, o_ff:o_ff + FOX_HEADS]], axis=1)
            small_w = jnp.pad(small_w, ((0, 0), (0, LANES - small_w.shape[1])))
            w_all = jnp.concatenate([w[:, :o_a], w[:, o_f:o_ff], small_w], axis=1).astype(BF16)
            splits = ((0, nq), (nq, o_a), (o_a, o_a + nf), (o_a + nf, o_a + nf + LANES))
            res = _inproj(x, norm1_g[layer], sc1, sh1, w_all, splits,
                          (BF16, BF16, BF16, F32), (None,) * 4, tm, pending, ev_conv_w[i])
            if pending is not None:
                x, res = res[0], res[1:]
            dn, z, fox, small = res
            small_t = jnp.swapaxes(small[:, :, :16], 1, 2)
            zeros4 = jnp.zeros((4,), F32)
            mul = jnp.concatenate([-jnp.exp(ev_dn_a_log[i]), zeros4, -jnp.ones((4,), F32), zeros4])
            bias = jnp.concatenate([ev_dn_dt_bias[i], zeros4, ev_fox_f_bias[i], zeros4])
            pr = jnp.stack([mul, bias], axis=1)
            pc = jnp.pad(jnp.stack([mul, bias], axis=0), ((0, 0), (0, LANES - 16)))
            col, row = _gates(small, small_t, pc, pr)
            row_dn = row[:, :DN_HEADS].reshape(b, DN_HEADS, s // DN_GROUP, DN_GROUP).transpose(0, 2, 1, 3)
            o_dn = _deltanet(dn, z, col, row_dn, ev_dn_onorm_g[i])
            o_fox = _fox(fox, col, ev_fox_qnorm_g[i], ev_fox_knorm_g[i], min(512, s))
            mix_a, mix_b, w_out = o_dn, o_fox, ev_w_out[i]
        else:
            lw = od_lru_wr.shape[-1] * LRU_BLOCKS
            splits = ((0, lw), (lw, 2 * lw), (2 * lw, 2 * lw + od_sgu_w.shape[-1] * SGU_GROUPS),
                      (2 * lw + od_sgu_w.shape[-1] * SGU_GROUPS, od_w_in.shape[-1]))
            res = _inproj(x, norm1_g[layer], sc1, sh1, od_w_in[i], splits,
                          (BF16,) * 3, (None, "gelu"), tm, pending,
                          sgu=(od_sgu_norm_g[i], od_sgu_w[i], od_sgu_b[i]))
            if pending is not None:
                x, res = res[0], res[1:]
            lx, lg, o_sgu = res
            o_lru = _lru(lx, lg, od_conv_w[i], od_conv_b[i], od_lru_wr[i], od_lru_br[i], od_lru_wi[i],
                         od_lru_bi[i], od_lru_lambda[i])
            mix_a, mix_b, w_out = o_lru, o_sgu, od_w_out[i]
        x1, rows, route, counts = _outproj(mix_a, mix_b, x, w_out, gt1, norm2_g[layer], sc2, sh2,
                                           rw, rb, tm)
        x = x1
        pending = (_moe(rows, route, counts, moe_w_gate, moe_w_up, moe_w_down, layer), gt2)
    return _residual(x, *pending, tm)
```

```python
import functools

import jax
import jax.numpy as jnp
from jax import lax
from jax.experimental import pallas as pl
from jax.experimental.pallas import tpu as pltpu
from jax.experimental.pallas import tpu_sc as plsc

F32 = jnp.float32
BF16 = jnp.bfloat16
EPS = 1e-6
NEG_INF = float("-inf")

DN_HEADS = 4
DN_DK = 128
DN_CHUNK = 64
CONV_WIDTH = 4
FOX_HEADS = 4
FOX_DH = 128
LRU_BLOCKS = 4
LRU_C = 8.0
SGU_GROUPS = 4
SGU_CHUNK = 128
N_EXPERTS = 16
EXPERTS_PER_GROUP = 4
LANES = 128
SUBLANES = 8

VMEM_LIMIT = 48 * 1024 * 1024


def _cparams(sem):
    return pltpu.CompilerParams(dimension_semantics=sem, vmem_limit_bytes=VMEM_LIMIT)


def _dot(a, b):
    return jnp.dot(a, b, preferred_element_type=F32)


def _dot_nt(a, b):
    return lax.dot_general(a, b, (((1,), (1,)), ((), ())), preferred_element_type=F32)


def _dot_tn(a, b):
    return lax.dot_general(a, b, (((0,), (0,)), ((), ())), preferred_element_type=F32)


def _dot_ones(a, b):
    if a.dtype == BF16:
        return sum(_dot(a, t.astype(BF16)) for t in _split3(b))
    return sum(_dot(t.astype(BF16), b) for t in _split3(a))


def _sigmoid(x):
    return 0.5 + 0.5 * jnp.tanh(0.5 * x)


def _silu(x):
    hx = 0.5 * x
    return hx + hx * jnp.tanh(hx)


def _softplus(x):
    return jnp.maximum(x, 0.0) + jnp.log(1.0 + jnp.exp(-jnp.abs(x)))


def _gelu_tanh(x):
    c = 0.7978845608028654
    return 0.5 * x * (1.0 + jnp.tanh(c * (x + 0.044715 * (x * x * x))))


def _prenorm(x, g, scale, shift):
    ms = jnp.mean(x * x, axis=-1, keepdims=True)
    return (x * lax.rsqrt(ms + EPS) * g) * (1.0 + scale) + shift


def _shift_rows(x, d, fill=0.0):
    rows = lax.broadcasted_iota(jnp.int32, x.shape, 0)
    return jnp.where(rows >= d, pltpu.roll(x, d, axis=0), fill)


def _causal_conv(x, w_ref):
    k_w = w_ref.shape[0]
    acc = x * w_ref[k_w - 1:k_w, :]
    for d in range(1, k_w):
        acc = acc + _shift_rows(x, d) * w_ref[k_w - 1 - d:k_w - d, :]
    return acc


def _adaln_kernel(c_ref, w_ref, b_ref, o_ref):
    c = c_ref[...]
    ca = _silu(c).astype(BF16)
    o_ref[0] = _dot(ca, w_ref[0].astype(BF16)) + b_ref[0]


def _adaln(c, ada_w, ada_b):
    depth, d, n = ada_w.shape
    b = c.shape[0]
    tn = 1536
    return pl.pallas_call(
        _adaln_kernel,
        grid=(depth, n // tn),
        in_specs=[pl.BlockSpec((b, d), lambda l, j: (0, 0)),
                  pl.BlockSpec((1, d, tn), lambda l, j: (l, 0, j)),
                  pl.BlockSpec((1, 1, tn), lambda l, j: (l, 0, j))],
        out_specs=pl.BlockSpec((1, b, tn), lambda l, j: (l, 0, j)),
        out_shape=jax.ShapeDtypeStruct((depth, b, n), F32),
        compiler_params=_cparams(("arbitrary", "arbitrary")),
        name="adaln",
    )(c, ada_w, ada_b.reshape(depth, 1, n))


CONV_HALO = 8
CONV_SLAB = 512


def _inproj_kernel(*refs, col_splits, acts, pending, conv, sgu, cast_w):
    refs = list(refs)
    x_ref = refs.pop(0)
    if pending:
        m_ref, gtm_ref = refs.pop(0), refs.pop(0)
    g_ref, sc_ref, sh_ref, w_ref = (refs.pop(0) for _ in range(4))
    if cast_w:
        w_f32, w_ref = w_ref, refs.pop()

        @pl.when((pl.program_id(0) == 0) & (pl.program_id(1) == 0))
        def _():
            w_ref[...] = w_f32[...].astype(BF16)
    if conv:
        cw_ref = refs.pop(0)
        halo_sc = refs.pop()
    if sgu:
        sg_ref, sw_ref, sbt_ref = (refs.pop(0) for _ in range(3))
    if pending:
        xo_ref = refs.pop(0)
        x = x_ref[0] + gtm_ref[0] * _unpack_halves(m_ref[0]).astype(F32)
        xo_ref[0] = x
    else:
        x = x_ref[0]
    out_refs = refs
    h = _prenorm(x, g_ref[...], sc_ref[0], sh_ref[0]).astype(BF16)
    tm = x.shape[0]
    if conv:
        @pl.when(pl.program_id(1) == 0)
        def _():
            halo_sc[0:CONV_HALO, :] = jnp.zeros((CONV_HALO, halo_sc.shape[1]), F32)

        c0, c1 = col_splits[0]
        for s0 in range(c0, c1, CONV_SLAB):
            halo_sc[CONV_HALO:, s0 - c0:s0 - c0 + CONV_SLAB] = _dot(h, w_ref[:, s0:s0 + CONV_SLAB])
    n_plain = len(col_splits) - (2 if sgu else 0)
    for k, (o_ref, (c0, c1), act) in enumerate(zip(out_refs[:n_plain], col_splits[:n_plain], acts)):
        if conv and k == 0:
            continue
        p = _dot(h, w_ref[:, c0:c1])
        if act == "gelu":
            p = _gelu_tanh(p)
        o_ref[0] = p.astype(o_ref.dtype)
    if sgu:
        (u0, u1), (v0, v1) = col_splits[n_plain:]
        o_ref = out_refs[n_plain]
        u = _gelu_tanh(_dot(h, w_ref[:, u0:u1]))
        v = _gelu_tanh(_dot(h, w_ref[:, v0:v1]))
        vn = (v * lax.rsqrt(jnp.mean(v * v, axis=-1, keepdims=True) + EPS) * sg_ref[...]).astype(BF16)
        c = SGU_CHUNK
        gw = (u1 - u0) // SGU_GROUPS
        ri = lax.broadcasted_iota(jnp.int32, (c, c), 0)
        ci = lax.broadcasted_iota(jnp.int32, (c, c), 1)
        for g in range(SGU_GROUPS):
            wg = jnp.where(ri >= ci, sw_ref[g], 0.0).astype(BF16)
            bcol = sbt_ref[:, g:g + 1]
            for n in range(tm // c):
                rows = slice(n * c, (n + 1) * c)
                cols = slice(g * gw, (g + 1) * gw)
                mixed = _dot(wg, vn[rows, cols]) + bcol
                o_ref[0, rows, cols] = (u[rows, cols] * mixed).astype(o_ref.dtype)
    if conv:
        o_ref = out_refs[0]
        for t0 in range(0, halo_sc.shape[1], DN_DK):
            cols = slice(t0, t0 + DN_DK)
            acc = halo_sc[CONV_HALO:, cols] * cw_ref[CONV_WIDTH - 1:CONV_WIDTH, cols]
            for dd in range(1, CONV_WIDTH):
                acc = acc + halo_sc[pl.ds(CONV_HALO - dd, tm), cols] * cw_ref[CONV_WIDTH - 1 - dd:CONV_WIDTH - dd, cols]
            halo_sc[0:CONV_HALO, cols] = halo_sc[tm:tm + CONV_HALO, cols]
            yj = _silu(acc)
            if t0 < 2 * DN_HEADS * DN_DK:
                yj = yj * lax.rsqrt(jnp.sum(yj * yj, axis=-1, keepdims=True) + EPS)
            if t0 < DN_HEADS * DN_DK:
                yj = yj * (DN_DK ** -0.5)
            o_ref[0, :, cols] = yj.astype(o_ref.dtype)


def _inproj(x, g, scale, shift, w, col_splits, out_dtypes, acts, tm, pending=None, conv_w=None, sgu=None):
    b, s, d = x.shape
    n = w.shape[1]
    out_cols = list(col_splits) if sgu is None else list(col_splits[:-2]) + [col_splits[-2]]
    outs = tuple(jax.ShapeDtypeStruct((b, s, c1 - c0), dt) for (c0, c1), dt in zip(out_cols, out_dtypes))
    row = lambda i, j: (i, 0, 0)
    tok = lambda i, j: (i, j, 0)
    const = lambda i, j: (0, 0)
    in_specs = [pl.BlockSpec((1, tm, d), tok)]
    out_specs = tuple(pl.BlockSpec((1, tm, c1 - c0), tok) for (c0, c1) in out_cols)
    args = (x,)
    if pending is not None:
        in_specs += [pl.BlockSpec((1, tm, d // 2), tok), pl.BlockSpec((1, 1, d), row)]
        out_specs = (pl.BlockSpec((1, tm, d), tok),) + out_specs
        outs = (jax.ShapeDtypeStruct((b, s, d), F32),) + outs
        args += tuple(pending)
    in_specs += [pl.BlockSpec((1, d), const), pl.BlockSpec((1, 1, d), row), pl.BlockSpec((1, 1, d), row),
                 pl.BlockSpec((d, n), const)]
    args += (g.reshape(1, d), scale, shift, w)
    scratch = []
    if conv_w is not None:
        in_specs.append(pl.BlockSpec(conv_w.shape, const))
        args += (conv_w,)
        scratch.append(pltpu.VMEM((CONV_HALO + tm, conv_w.shape[1]), F32))
    if w.dtype != BF16:
        scratch.append(pltpu.VMEM((d, n), BF16))
    if sgu is not None:
        g_norm, w_s, b_s = sgu
        in_specs += [pl.BlockSpec((1, g_norm.shape[0]), const), pl.BlockSpec(w_s.shape, lambda i, j: (0, 0, 0)),
                     pl.BlockSpec((b_s.shape[1], b_s.shape[0]), const)]
        args += (g_norm.reshape(1, -1), w_s, b_s.T)
    return pl.pallas_call(
        functools.partial(_inproj_kernel, col_splits=col_splits, acts=acts, pending=pending is not None,
                          conv=conv_w is not None, sgu=sgu is not None, cast_w=w.dtype != BF16),
        grid=(b, s // tm),
        in_specs=in_specs,
        out_specs=out_specs,
        out_shape=outs,
        scratch_shapes=scratch,
        compiler_params=_cparams(("arbitrary", "arbitrary")),
        name="inproj",
    )(*args)


def _gates_kernel(sm_ref, smt_ref, pc_ref, pr_ref, col_ref, row_ref):
    s = sm_ref.shape[1]
    blk = LANES
    ri = lax.broadcasted_iota(jnp.int32, (blk, blk), 0)
    ci = lax.broadcasted_iota(jnp.int32, (blk, blk), 1)
    same_chunk = (ri // DN_CHUNK) == (ci // DN_CHUNK)
    tril = jnp.where(ri >= ci, 1.0, 0.0).astype(BF16)
    tril_loc = jnp.where(same_chunk & (ri >= ci), 1.0, 0.0).astype(BF16)
    triu_loc = jnp.where(same_chunk & (ri <= ci), 1.0, 0.0).astype(BF16)
    lane = lax.broadcasted_iota(jnp.int32, (blk, LANES), 1)
    carry = jnp.zeros((1, LANES), F32)
    for j in range(s // blk):
        rows = slice(j * blk, (j + 1) * blk)
        xc = sm_ref[0, rows, :] + pc_ref[1:2, :]
        dec = pc_ref[0:1, :] * _softplus(jnp.where(lane < 4, xc, -xc))
        cum_glb = _dot_ones(tril, dec) + carry
        col_ref[0, rows, :] = jnp.where(lane < 4, _dot_ones(tril_loc, dec), jnp.where(lane < 8, _sigmoid(xc), cum_glb))
        carry = cum_glb[blk - 1:blk, :]
        decr = pr_ref[:, 0:1] * _softplus(smt_ref[0, :, rows] + pr_ref[:, 1:2])
        row_ref[0, :, rows] = _dot_ones(decr, triu_loc)


def _gates(small, small_t, pc, pr):
    b, s, _ = small.shape
    return pl.pallas_call(
        _gates_kernel,
        grid=(b,),
        in_specs=[pl.BlockSpec((1, s, LANES), lambda i: (i, 0, 0)),
                  pl.BlockSpec((1, 16, s), lambda i: (i, 0, 0)),
                  pl.BlockSpec((2, LANES), lambda i: (0, 0)),
                  pl.BlockSpec((16, 2), lambda i: (0, 0))],
        out_specs=(pl.BlockSpec((1, s, LANES), lambda i: (i, 0, 0)),
                   pl.BlockSpec((1, 16, s), lambda i: (i, 0, 0))),
        out_shape=(jax.ShapeDtypeStruct((b, s, LANES), F32), jax.ShapeDtypeStruct((b, 16, s), F32)),
        compiler_params=_cparams(("arbitrary",)),
        name="gates",
    )(small, small_t, pc, pr)


DN_PACK = 4
DN_GROUP = DN_PACK * DN_CHUNK
DN_ITER_GROUPS = 2


def _blockdiag(p):
    c, wide = p.shape
    t = jnp.concatenate([p] * (wide // c), axis=0)
    rb = lax.broadcasted_iota(jnp.int32, (wide, wide), 0) // c
    cb = lax.broadcasted_iota(jnp.int32, (wide, wide), 1) // c
    return jnp.where(rb == cb, t, 0.0).astype(BF16)


def _diag_blocks(m, c):
    wide = m.shape[1]
    cb = lax.broadcasted_iota(jnp.int32, (c, wide), 1) // c
    out = m[:c]
    for j in range(1, wide // c):
        out = jnp.where(cb == j, m[j * c:(j + 1) * c], out)
    return out


def _rows_to_blocks(col, c, wide):
    cb = lax.broadcasted_iota(jnp.int32, (c, wide), 1) // c
    out = jnp.broadcast_to(col[:c], (c, wide))
    for j in range(1, wide // c):
        out = jnp.where(cb == j, col[j * c:(j + 1) * c], out)
    return out


def _packed_unit_lower_inverse(lows):
    c, wide = lows[0].shape
    ri = lax.broadcasted_iota(jnp.int32, (c, wide), 0)
    ci = lax.broadcasted_iota(jnp.int32, (c, wide), 1) % c
    eye = jnp.where(ri == ci, 1.0, 0.0)
    n = range(len(lows))
    ds = [jnp.where((ri // 16) == (ci // 16), lo, 0.0) for lo in lows]
    xs = [eye - d for d in ds]
    ps = ds
    bds = [_blockdiag(p) for p in ps]
    for _ in range(3):
        ps = [_dot(ps[i].astype(BF16), bds[i]) for i in n]
        bds = [_blockdiag(p) for p in ps]
        xs = [xs[i] + _dot(xs[i].astype(BF16), bds[i]) for i in n]
    for width in (16, 32):
        sel = ((ri // (2 * width)) == (ci // (2 * width))) & ((ri // width) != (ci // width))
        offs = [_blockdiag(jnp.where(sel, lo, 0.0)) for lo in lows]
        ts = [_dot(xs[i].astype(BF16), offs[i]) for i in n]
        bdx = [_blockdiag(x) for x in xs]
        xs = [xs[i] - _dot(ts[i].astype(BF16), bdx[i]) for i in n]
    return xs


def _deltanet_kernel(dn_ref, z_ref, col_ref, row_ref, og_ref, o_ref, st_sc):
    s = dn_ref.shape[1]
    c = DN_CHUNK
    gt = DN_GROUP
    dk = DN_DK
    hd = DN_HEADS * dk
    heads = range(DN_HEADS)
    st_sc[...] = jnp.zeros_like(st_sc)

    ri = lax.broadcasted_iota(jnp.int32, (c, gt), 0)
    ci = lax.broadcasted_iota(jnp.int32, (c, gt), 1) % c
    og = og_ref[...]

    per_iter = DN_ITER_GROUPS if (s // gt) % DN_ITER_GROUPS == 0 else 1
    insts = [(gg, h) for gg in range(per_iter) for h in heads]
    ins = range(len(insts))

    def group(g, carry):
        r0 = [pl.multiple_of((g * per_iter + gg) * gt, gt) for gg in range(per_iter)]
        colg = [col_ref[0, pl.ds(r0[gg], gt), :] for gg in range(per_iter)]
        rowg = [row_ref[0, g * per_iter + gg] for gg in range(per_iter)]
        hs = [slice(h * dk, (h + 1) * dk) for h in heads]
        q = [dn_ref[0, pl.ds(r0[gg], gt), h * dk:(h + 1) * dk].astype(F32) for gg, h in insts]
        k = [dn_ref[0, pl.ds(r0[gg], gt), hd + h * dk:hd + (h + 1) * dk].astype(F32) for gg, h in insts]
        v = [dn_ref[0, pl.ds(r0[gg], gt), 2 * hd + h * dk:2 * hd + (h + 1) * dk].astype(F32) for gg, h in insts]
        gc = [colg[gg][:, h:h + 1] for gg, h in insts]
        beta = [colg[gg][:, DN_HEADS + h:DN_HEADS + h + 1] for gg, h in insts]
        grow = [rowg[gg][h:h + 1, :] for gg, h in insts]
        glast = [jnp.concatenate([jnp.broadcast_to(gc[i][(j + 1) * c - 1:(j + 1) * c], (c, 1))
                                  for j in range(DN_PACK)], axis=0) for i in ins]
        eg = [jnp.exp(gc[i]) for i in ins]
        kbf = [k[i].astype(BF16) for i in ins]
        both = [_dot_nt(jnp.concatenate([(k[i] * beta[i]).astype(BF16), q[i].astype(BF16)], axis=0), kbf[i])
                for i in ins]
        decay = [jnp.exp(jnp.where(ri >= ci, _rows_to_blocks(gc[i], c, gt) - grow[i], NEG_INF)) for i in ins]
        kk = [_diag_blocks(both[i][:gt], c) * decay[i] for i in ins]
        qk = [_blockdiag(_diag_blocks(both[i][gt:], c) * decay[i]) for i in ins]
        t_inv = _packed_unit_lower_inverse([jnp.where(ri > ci, kk[i], 0.0) for i in ins])
        rhs = [jnp.concatenate([k[i] * (beta[i] * eg[i]), v[i] * beta[i]], axis=1).astype(BF16) for i in ins]
        wu = [_dot(_blockdiag(t_inv[i]), rhs[i]).astype(BF16) for i in ins]
        qwu = [_dot(qk[i], wu[i]) for i in ins]
        qp = [(q[i] * eg[i] - qwu[i][:, :dk]).astype(BF16) for i in ins]
        kdec = [(k[i] * jnp.exp(glast[i] - gc[i])).astype(BF16) for i in ins]
        mb = [[_dot_tn(kdec[i][j * c:(j + 1) * c], wu[i][j * c:(j + 1) * c]) for i in ins]
              for j in range(DN_PACK)]
        for gg in range(per_iter):
            mine = [gg * DN_HEADS + h for h in heads]
            outs = [[] for _ in heads]
            for j in range(DN_PACK):
                rows = slice(j * c, (j + 1) * c)
                state = [st_sc[h] for h in heads]
                lhs = [jnp.concatenate([qp[i][rows], mb[j][i][:, :dk].astype(BF16)], axis=0) for i in mine]
                r = [_dot(lhs[h], state[h].astype(BF16)) for h in heads]
                for h, i in zip(heads, mine):
                    gl = jnp.exp(glast[i][j * c:j * c + 1])
                    st_sc[h] = state[h] * gl - r[h][c:] + mb[j][i][:, dk:]
                    outs[h].append(r[h][:c] + qwu[i][rows, dk:])
            for h in heads:
                o = jnp.concatenate(outs[h], axis=0)
                on = o * lax.rsqrt(jnp.mean(o * o, axis=-1, keepdims=True) + EPS) * og
                zz = z_ref[0, pl.ds(r0[gg], gt), hs[h]].astype(F32)
                o_ref[0, pl.ds(r0[gg], gt), hs[h]] = (on * _silu(zz)).astype(o_ref.dtype)
        return carry

    lax.fori_loop(0, s // (gt * per_iter), group, 0)


def _deltanet(dn, z, col, row, onorm_g):
    b, s, w3 = dn.shape
    hd = DN_HEADS * DN_DK
    return pl.pallas_call(
        _deltanet_kernel,
        grid=(b,),
        in_specs=[pl.BlockSpec((1, s, w3), lambda i: (i, 0, 0)),
                  pl.BlockSpec((1, s, hd), lambda i: (i, 0, 0)),
                  pl.BlockSpec((1, s, LANES), lambda i: (i, 0, 0)),
                  pl.BlockSpec((1, s // DN_GROUP, DN_HEADS, DN_GROUP), lambda i: (i, 0, 0, 0)),
                  pl.BlockSpec((1, DN_DK), lambda i: (0, 0))],
        out_specs=pl.BlockSpec((1, s, hd), lambda i: (i, 0, 0)),
        out_shape=jax.ShapeDtypeStruct((b, s, hd), BF16),
        scratch_shapes=[pltpu.VMEM((DN_HEADS, DN_DK, DN_DK), F32)],
        compiler_params=_cparams(("arbitrary",)),
        name="deltanet",
    )(dn, z, col, row, onorm_g.reshape(1, DN_DK))


def _split3(x):
    hi = x.astype(BF16).astype(F32)
    r = x - hi
    mid = r.astype(BF16).astype(F32)
    return hi, mid, r - mid


def _fox_kernel(q_ref, k_ref, v_ref, colq_ref, colk_ref, qg_ref, kg_ref, o_ref, ka_sc, va_sc, m_sc, acc_sc, *, tq):
    qi = pl.program_id(1)
    s = k_ref.shape[1]
    dh = FOX_DH

    def bias_lanes(col):
        lane = lax.broadcasted_iota(jnp.int32, col.shape, 1)
        hi, mid, lo = _split3(col)
        return jnp.where((lane >= 8) & (lane < 8 + FOX_HEADS), hi,
                         jnp.where((lane >= 24) & (lane < 24 + FOX_HEADS), pltpu.roll(mid, 16, axis=1),
                                   jnp.where((lane >= 40) & (lane < 40 + FOX_HEADS), pltpu.roll(lo, 32, axis=1), 0.0)))

    @pl.when(qi == 0)
    def _():
        lane = lax.broadcasted_iota(jnp.int32, (s, LANES), 1)
        ones_col = jnp.where(lane == 0, 1.0, 0.0).astype(BF16)
        ones_k = (lane >= 56) & (lane < 96) & ((lane % 16) >= 8) & ((lane % 16) < 8 + FOX_HEADS)
        ext_k = (jnp.where(ones_k, 1.0, 0.0) - bias_lanes(colk_ref[0])).astype(BF16)
        for h in range(FOX_HEADS):
            cols = slice(h * dh, (h + 1) * dh)
            kf = k_ref[0, :, cols].astype(F32)
            kn = kf * lax.rsqrt(jnp.mean(kf * kf, axis=-1, keepdims=True) + EPS) * kg_ref[...]
            ka_sc[h, :, :dh] = kn.astype(BF16)
            ka_sc[h, :, dh:] = ext_k
            va_sc[h, :, :dh] = v_ref[0, :, cols]
            va_sc[h, :, dh:] = ones_col

    lane = lax.broadcasted_iota(jnp.int32, (tq, LANES), 1)
    cq = pltpu.roll(bias_lanes(colq_ref[0]), 48, axis=1)
    qa = []
    for h in range(FOX_HEADS):
        cols = slice(h * dh, (h + 1) * dh)
        qf = q_ref[0, :, cols].astype(F32)
        qn = qf * lax.rsqrt(jnp.mean(qf * qf, axis=-1, keepdims=True) + EPS) * qg_ref[...] * (dh ** -0.5)
        mine = ((lane % 16) == 8 + h) & (lane < 96)
        ext = jnp.where(mine, jnp.where(lane < 48, 1.0, cq), 0.0)
        qa.append(jnp.concatenate([qn.astype(BF16), ext.astype(BF16)], axis=1))
    m_sc[...] = jnp.full(m_sc.shape, NEG_INF, F32)
    acc_sc[...] = jnp.zeros_like(acc_sc)
    causal = lax.broadcasted_iota(jnp.int32, (tq, tq), 0) >= lax.broadcasted_iota(jnp.int32, (tq, tq), 1)

    def step(k0, masked):
        heads = range(FOX_HEADS)
        logits = [_dot_nt(qa[h], ka_sc[h, pl.ds(k0, tq), :]) for h in heads]
        if masked:
            logits = [jnp.where(causal, lg, NEG_INF) for lg in logits]
        ps, alphas = [], []
        for h in heads:
            m_old = m_sc[h]
            m_new = jnp.maximum(m_old, jnp.max(logits[h], axis=1, keepdims=True))
            m_sc[h] = m_new
            alphas.append(jnp.exp(m_old - m_new))
            ps.append(jnp.exp(logits[h] - jnp.concatenate([m_new] * (tq // LANES), axis=1)).astype(BF16))
        for h in heads:
            pv = _dot(ps[h], va_sc[h, pl.ds(k0, tq), :])
            acc_sc[h] = acc_sc[h] * jnp.concatenate([alphas[h], alphas[h]], axis=1) + pv

    def body(j, carry):
        step(pl.multiple_of(j * tq, tq), False)
        return carry

    lax.fori_loop(0, qi, body, 0)
    step(pl.multiple_of(qi * tq, tq), True)
    for h in range(FOX_HEADS):
        acc = acc_sc[h]
        o_ref[0, :, h * dh:(h + 1) * dh] = (acc[:, :dh] / acc[:, dh:dh + 1]).astype(o_ref.dtype)


def _fox(fox, col, qg, kg, tq):
    b, s, _ = fox.shape
    hd = FOX_HEADS * FOX_DH
    return pl.pallas_call(
        functools.partial(_fox_kernel, tq=tq),
        grid=(b, s // tq),
        in_specs=[pl.BlockSpec((1, tq, hd), lambda i, j: (i, j, 0)),
                  pl.BlockSpec((1, s, hd), lambda i, j: (i, 0, 1)),
                  pl.BlockSpec((1, s, hd), lambda i, j: (i, 0, 2)),
                  pl.BlockSpec((1, tq, LANES), lambda i, j: (i, j, 0)),
                  pl.BlockSpec((1, s, LANES), lambda i, j: (i, 0, 0)),
                  pl.BlockSpec((1, FOX_DH), lambda i, j: (0, 0)),
                  pl.BlockSpec((1, FOX_DH), lambda i, j: (0, 0))],
        out_specs=pl.BlockSpec((1, tq, hd), lambda i, j: (i, j, 0)),
        out_shape=jax.ShapeDtypeStruct((b, s, hd), BF16),
        scratch_shapes=[pltpu.VMEM((FOX_HEADS, s, 2 * FOX_DH), BF16), pltpu.VMEM((FOX_HEADS, s, 2 * FOX_DH), BF16),
                        pltpu.VMEM((FOX_HEADS, tq, LANES), F32), pltpu.VMEM((FOX_HEADS, tq, 2 * FOX_DH), F32)],
        compiler_params=_cparams(("arbitrary", "arbitrary")),
        name="fox",
    )(fox, fox, fox, col, col, qg.reshape(1, FOX_DH), kg.reshape(1, FOX_DH))


def _lru_kernel(x_ref, gate_ref, cw_ref, cb_ref, wr_ref, br_ref, wi_ref, bi_ref, lam_ref, o_ref, a_sc, b_sc):
    s = x_ref.shape[1]
    x = _causal_conv(x_ref[0].astype(F32), cw_ref) + cb_ref[...]
    xb = x.astype(BF16)
    r = _sigmoid(_dot(xb, wr_ref[0].astype(BF16)) + br_ref[...])
    i = _sigmoid(_dot(xb, wi_ref[0].astype(BF16)) + bi_ref[...])
    log_a = (-LRU_C) * r * _softplus(-lam_ref[...])
    a = jnp.exp(log_a)
    bb = jnp.sqrt(1.0 - a * a) * (i * x)
    nt = s // SUBLANES
    a3 = a.reshape(nt, SUBLANES, a.shape[1])
    b3 = bb.reshape(nt, SUBLANES, a.shape[1])
    sub = lax.broadcasted_iota(jnp.int32, a3.shape, 1)
    d = 1
    while d < SUBLANES:
        keep = sub >= d
        b3 = a3 * jnp.where(keep, pltpu.roll(b3, d, axis=1), 0.0) + b3
        a3 = a3 * jnp.where(keep, pltpu.roll(a3, d, axis=1), 1.0)
        d *= 2
    a_sc[...] = a3.reshape(s, a.shape[1])
    b_sc[...] = b3.reshape(s, a.shape[1])
    at = a_sc[pl.ds(SUBLANES - 1, nt, stride=SUBLANES), :]
    bt = b_sc[pl.ds(SUBLANES - 1, nt, stride=SUBLANES), :]
    d = 1
    while d < nt:
        bt = at * _shift_rows(bt, d, 0.0) + bt
        at = at * _shift_rows(at, d, 1.0)
        d *= 2
    h_prev = _shift_rows(bt, 1, 0.0)
    bb = (a3 * h_prev[:, None, :] + b3).reshape(s, a.shape[1])
    o_ref[0] = (bb * gate_ref[0].astype(F32)).astype(o_ref.dtype)


def _lru(lx, lg, conv_w, conv_b, wr, br, wi, bi, lam):
    b, s, wd = lx.shape
    blk = wd // LRU_BLOCKS
    vec = lambda i, j: (0, j)
    return pl.pallas_call(
        _lru_kernel,
        grid=(b, LRU_BLOCKS),
        in_specs=[pl.BlockSpec((1, s, blk), lambda i, j: (i, 0, j)),
                  pl.BlockSpec((1, s, blk), lambda i, j: (i, 0, j)),
                  pl.BlockSpec((CONV_WIDTH, blk), vec),
                  pl.BlockSpec((1, blk), vec),
                  pl.BlockSpec((1, blk, blk), lambda i, j: (j, 0, 0)),
                  pl.BlockSpec((1, blk), vec),
                  pl.BlockSpec((1, blk, blk), lambda i, j: (j, 0, 0)),
                  pl.BlockSpec((1, blk), vec),
                  pl.BlockSpec((1, blk), vec)],
        out_specs=pl.BlockSpec((1, s, blk), lambda i, j: (i, 0, j)),
        out_shape=jax.ShapeDtypeStruct((b, s, wd), BF16),
        scratch_shapes=[pltpu.VMEM((s, blk), F32), pltpu.VMEM((s, blk), F32)],
        compiler_params=_cparams(("arbitrary", "arbitrary")),
        name="rglru",
    )(lx, lg, conv_w, conv_b.reshape(1, wd), wr, br.reshape(1, wd), wi, bi.reshape(1, wd), lam.reshape(1, wd))


def _routing(logits, rb):
    ne, tm = logits.shape
    row = lax.broadcasted_iota(jnp.int32, (ne, tm), 0)
    row_f = row.astype(F32)
    ex = jnp.exp(logits - jnp.max(logits, axis=0, keepdims=True))
    probs = ex / jnp.sum(ex, axis=0, keepdims=True)
    sel = probs + rb

    def top2(vals, idx):
        m1 = jnp.max(vals, axis=0, keepdims=True)
        i1 = jnp.min(jnp.where(vals == m1, idx, float(ne)), axis=0, keepdims=True)
        rest = jnp.where(idx == i1, NEG_INF, vals)
        m2 = jnp.max(rest, axis=0, keepdims=True)
        return m1, i1, m2, rest

    best = None
    g_idx = None
    grp = row // EXPERTS_PER_GROUP
    for g in range(ne // EXPERTS_PER_GROUP):
        m1, _, m2, _ = top2(jnp.where(grp == g, sel, NEG_INF), row_f)
        score = m1 + m2
        if g == 0:
            best, g_idx = score, jnp.zeros((1, tm), jnp.int32)
        else:
            upd = score > best
            best = jnp.where(upd, score, best)
            g_idx = jnp.where(upd, g, g_idx)
    _, i1, m2, rest = top2(jnp.where(grp == g_idx, sel, NEG_INF), row_f)
    i2 = jnp.min(jnp.where(rest == m2, row_f, float(ne)), axis=0, keepdims=True)
    p1 = jnp.sum(jnp.where(row_f == i1, probs, 0.0), axis=0, keepdims=True)
    p2 = jnp.sum(jnp.where(row_f == i2, probs, 0.0), axis=0, keepdims=True)
    den = p1 + p2
    first_lo = i1 < i2
    a = jnp.where(first_lo, i1, i2) - EXPERTS_PER_GROUP * g_idx.astype(F32)
    bhi = jnp.where(first_lo, i2, i1) - EXPERTS_PER_GROUP * g_idx.astype(F32)
    pair = a * (7.0 - a) * 0.5 + (bhi - a - 1.0)
    cls = PAIRS_PER_GROUP * g_idx.astype(F32) + pair
    w_lo = jnp.where(first_lo, p1, p2) / den
    w_hi = jnp.where(first_lo, p2, p1) / den
    return cls, w_lo, w_hi


PAIRS_PER_GROUP = EXPERTS_PER_GROUP * (EXPERTS_PER_GROUP - 1) // 2
N_CLASSES = (N_EXPERTS // EXPERTS_PER_GROUP) * PAIRS_PER_GROUP
_PAIRS = [(a, b) for a in range(EXPERTS_PER_GROUP) for b in range(a + 1, EXPERTS_PER_GROUP)]
CLASS_E_LO = [EXPERTS_PER_GROUP * (c // PAIRS_PER_GROUP) + _PAIRS[c % PAIRS_PER_GROUP][0] for c in range(N_CLASSES)]
CLASS_E_HI = [EXPERTS_PER_GROUP * (c // PAIRS_PER_GROUP) + _PAIRS[c % PAIRS_PER_GROUP][1] for c in range(N_CLASSES)]
CLASS_ROWS = 32
ROW_FEAT = 512
ROW_WORDS = 640


def _pack_halves(x):
    kk = x.shape[1] // 2
    lo = lax.bitcast_convert_type(x[:, :kk].astype(BF16).astype(F32), jnp.uint32)
    hi = lax.bitcast_convert_type(x[:, kk:].astype(BF16).astype(F32), jnp.uint32)
    return lax.bitcast_convert_type((lo >> 16) | hi, jnp.int32)


def _unpack_halves(p):
    u = lax.bitcast_convert_type(p, jnp.uint32)
    lo = lax.bitcast_convert_type(u << 16, F32)
    hi = lax.bitcast_convert_type(u & jnp.uint32(0xFFFF0000), F32)
    return jnp.concatenate([lo.astype(BF16), hi.astype(BF16)], axis=1)


def _outproj_kernel(a_ref, b_ref, x_ref, wf_ref, gt_ref, g2_ref, sc_ref, sh_ref, rwt_ref, rb_ref,
                    x1_ref, rows_ref, route_ref, cnt_ref, carry_sc, w_ref, tri_sc):
    first = (pl.program_id(0) == 0) & (pl.program_id(1) == 0)

    @pl.when(first)
    def _():
        carry_sc[...] = jnp.zeros_like(carry_sc)
        w_ref[...] = wf_ref[...].astype(BF16)
        earlier = (lax.broadcasted_iota(jnp.int32, tri_sc.shape, 0) < lax.broadcasted_iota(jnp.int32, tri_sc.shape, 1))
        tri_sc[...] = jnp.where(earlier, 1.0, 0.0).astype(BF16)

    half = a_ref.shape[2]
    tm = a_ref.shape[1]
    y = _dot(a_ref[0], w_ref[:half, :]) + _dot(b_ref[0], w_ref[half:, :])
    x1 = x_ref[0] + gt_ref[0] * y
    x1_ref[0] = x1
    h2 = _prenorm(x1, g2_ref[...], sc_ref[0], sh_ref[0])
    rows_ref[0, :, :ROW_FEAT] = _pack_halves(h2)
    cls, w_lo, w_hi = _routing(_dot_nt(rwt_ref[...], h2.astype(BF16)), rb_ref[...])
    r128 = lax.broadcasted_iota(jnp.int32, (LANES, tm), 0)
    wts = jnp.where(r128 == 0, w_lo, jnp.where(r128 == 1, w_hi, 0.0))
    rows_ref[0, :, ROW_FEAT:] = lax.bitcast_convert_type(wts.T[:, :ROW_WORDS - ROW_FEAT], jnp.int32)
    crow = lax.broadcasted_iota(jnp.int32, (carry_sc.shape[0], tm), 0).astype(F32)
    onehot = jnp.where(crow == cls, 1.0, 0.0)
    prefix = _dot(onehot.astype(BF16), tri_sc[...]) + carry_sc[:, 0:1]
    rank = jnp.sum(onehot * prefix, axis=0, keepdims=True)
    r8 = lax.broadcasted_iota(jnp.int32, (8, tm), 0)
    route_ref[0] = jnp.where(r8 == 0, cls, jnp.where(r8 == 1, rank, 0.0))
    carry_sc[...] = carry_sc[...] + jnp.sum(onehot, axis=1, keepdims=True)
    cnt_ref[...] = carry_sc[...]


def _outproj(a, bb, x, w, gt, g2, sc2, sh2, rw, rb, tm):
    b, s, d = x.shape
    half = a.shape[2]
    row = lambda i, j: (i, 0, 0)
    tok = lambda i, j: (i, j, 0)
    const = lambda i, j: (0, 0)
    return pl.pallas_call(
        _outproj_kernel,
        grid=(b, s // tm),
        in_specs=[pl.BlockSpec((1, tm, half), tok),
                  pl.BlockSpec((1, tm, half), tok),
                  pl.BlockSpec((1, tm, d), tok),
                  pl.BlockSpec((2 * half, d), const),
                  pl.BlockSpec((1, 1, d), row),
                  pl.BlockSpec((1, d), const),
                  pl.BlockSpec((1, 1, d), row),
                  pl.BlockSpec((1, 1, d), row),
                  pl.BlockSpec((N_EXPERTS, d), const),
                  pl.BlockSpec((N_EXPERTS, 1), const)],
        out_specs=(pl.BlockSpec((1, tm, d), tok),
                   pl.BlockSpec((1, tm, ROW_WORDS), tok),
                   pl.BlockSpec((1, 8, tm), lambda i, j: (i, 0, j)),
                   pl.BlockSpec((CLASS_ROWS, LANES), const)),
        out_shape=(jax.ShapeDtypeStruct((b, s, d), F32),
                   jax.ShapeDtypeStruct((b, s, ROW_WORDS), jnp.int32),
                   jax.ShapeDtypeStruct((b, 8, s), F32),
                   jax.ShapeDtypeStruct((CLASS_ROWS, LANES), F32)),
        scratch_shapes=[pltpu.VMEM((CLASS_ROWS, LANES), F32), pltpu.VMEM((2 * half, d), BF16),
                        pltpu.VMEM((tm, tm), BF16)],
        compiler_params=_cparams(("arbitrary", "arbitrary")),
        name="outproj_router",
    )(a, bb, x, w, gt, g2.reshape(1, d), sc2, sh2, rw, rb)


MOE_ROWS = 512
SC_CHUNK = 128


def _sc_workers():
    info = plsc.get_sparse_core_info()
    return info.num_cores, info.num_cores * info.num_subcores


def _sc_scatter_rows(rows, idx3, n_out):
    nw, k, ch = idx3.shape
    width = rows.shape[1]
    nc, _ = _sc_workers()
    mesh = plsc.VectorSubcoreMesh(core_axis_name="c", subcore_axis_name="s")

    @functools.partial(
        pl.kernel, mesh=mesh,
        out_type=jax.ShapeDtypeStruct((n_out, width), rows.dtype),
        scratch_types=[pltpu.VMEM((k, ch), jnp.int32), pltpu.VMEM((ch, width), rows.dtype), pltpu.SemaphoreType.DMA],
        name="moe_dispatch")
    def kern(rows_hbm, idx_hbm, out_hbm, idx_v, rows_v, sem):
        wid = lax.axis_index("s") * nc + lax.axis_index("c")
        pltpu.sync_copy(idx_hbm.at[wid], idx_v)

        @pl.loop(0, k)
        def _(j):
            pltpu.sync_copy(rows_hbm.at[pl.ds((wid * k + j) * ch, ch)], rows_v)
            pltpu.async_copy(rows_v, out_hbm.at[idx_v.at[j]], sem).wait()

    return kern(rows, idx3)


def _sc_gather_rows(table, idx3):
    nw, k, ch = idx3.shape
    width = table.shape[1]
    nc, _ = _sc_workers()
    mesh = plsc.VectorSubcoreMesh(core_axis_name="c", subcore_axis_name="s")

    @functools.partial(
        pl.kernel, mesh=mesh,
        out_type=jax.ShapeDtypeStruct((nw * k * ch, width), table.dtype),
        scratch_types=[pltpu.VMEM((k, ch), jnp.int32), pltpu.VMEM((ch, width), table.dtype), pltpu.SemaphoreType.DMA],
        name="moe_combine")
    def kern(table_hbm, idx_hbm, out_hbm, idx_v, rows_v, sem):
        wid = lax.axis_index("s") * nc + lax.axis_index("c")
        pltpu.sync_copy(idx_hbm.at[wid], idx_v)

        @pl.loop(0, k)
        def _(j):
            pltpu.async_copy(table_hbm.at[idx_v.at[j]], rows_v, sem).wait()
            pltpu.sync_copy(rows_v, out_hbm.at[pl.ds((wid * k + j) * ch, ch)])

    return kern(table, idx3)


def _experts_kernel(elo_ref, ehi_ref, nused_ref, x_ref, wg0_ref, wu0_ref, wd0_ref, wg1_ref, wu1_ref, wd1_ref, y_ref):
    @pl.when(pl.program_id(0) < nused_ref[0])
    def _():
        blk = x_ref[...]
        feat = ROW_FEAT
        h = _unpack_halves(blk[:, :feat])
        wts = lax.bitcast_convert_type(blk[:, feat:], F32)
        y = None
        for col, (wg_ref, wu_ref, wd_ref) in enumerate(((wg0_ref, wu0_ref, wd0_ref), (wg1_ref, wu1_ref, wd1_ref))):
            act = (_silu(_dot(h, wg_ref[0, 0].astype(BF16))) * _dot(h, wu_ref[0, 0].astype(BF16))
                   * wts[:, col:col + 1])
            part = _dot(act.astype(BF16), wd_ref[0, 0].astype(BF16))
            y = part if y is None else y + part
        y_ref[...] = _pack_halves(y)


def _experts(xs, blk_elo, blk_ehi, nused, wg, wu, wd, layer):
    n_rows = xs.shape[0]
    _, ne, d, f = wg.shape
    nblk = n_rows // MOE_ROWS
    rows = lambda i, elo, ehi, nu: (jnp.minimum(i, nu[0] - 1), 0)
    lo = lambda i, elo, ehi, nu: (layer, elo[i], 0, 0)
    hi = lambda i, elo, ehi, nu: (layer, ehi[i], 0, 0)
    return pl.pallas_call(
        _experts_kernel,
        grid_spec=pltpu.PrefetchScalarGridSpec(
            num_scalar_prefetch=3,
            grid=(nblk,),
            in_specs=[pl.BlockSpec((MOE_ROWS, ROW_WORDS), rows),
                      pl.BlockSpec((1, 1, d, f), lo), pl.BlockSpec((1, 1, d, f), lo), pl.BlockSpec((1, 1, f, d), lo),
                      pl.BlockSpec((1, 1, d, f), hi), pl.BlockSpec((1, 1, d, f), hi), pl.BlockSpec((1, 1, f, d), hi)],
            out_specs=pl.BlockSpec((MOE_ROWS, d // 2), lambda i, elo, ehi, nu: (i, 0))),
        out_shape=jax.ShapeDtypeStruct((n_rows, d // 2), jnp.int32),
        compiler_params=_cparams(("arbitrary",)),
        name="moe_experts",
    )(blk_elo, blk_ehi, nused, xs, wg, wu, wd, wg, wu, wd)


def _residual_kernel(x_ref, m_ref, gt_ref, o_ref):
    o_ref[0] = x_ref[0] + gt_ref[0] * _unpack_halves(m_ref[0]).astype(F32)


def _residual(x1, moe_p, gt, tm):
    b, s, d = x1.shape
    tok = lambda i, j: (i, j, 0)
    return pl.pallas_call(
        _residual_kernel,
        grid=(b, s // tm),
        in_specs=[pl.BlockSpec((1, tm, d), tok), pl.BlockSpec((1, tm, d // 2), tok),
                  pl.BlockSpec((1, 1, d), lambda i, j: (i, 0, 0))],
        out_specs=pl.BlockSpec((1, tm, d), tok),
        out_shape=jax.ShapeDtypeStruct((b, s, d), F32),
        compiler_params=_cparams(("arbitrary", "arbitrary")),
        name="moe_residual",
    )(x1, moe_p, gt)


def _moe(rows, route, counts, wg, wu, wd, layer):
    b, s, _ = rows.shape
    d = wg.shape[2]
    t = b * s
    _, nw = _sc_workers()
    cnt = counts[:N_CLASSES, 0].astype(jnp.int32)
    padded = ((cnt + MOE_ROWS - 1) // MOE_ROWS) * MOE_ROWS
    ends = jnp.cumsum(padded)
    n_rows = t + N_CLASSES * MOE_ROWS
    nblk = n_rows // MOE_ROWS
    cls = route[:, 0, :].reshape(t).astype(jnp.int32)
    rank = route[:, 1, :].reshape(t).astype(jnp.int32)
    pos = rank + jnp.sum(jnp.where(cls[:, None] > jnp.arange(N_CLASSES, dtype=jnp.int32)[None, :], padded[None, :], 0),
                         axis=1)
    idx3 = pos.reshape(nw, t // (nw * SC_CHUNK), SC_CHUNK)
    nused = (ends[-1] // MOE_ROWS).reshape(1)
    blk_cls = jnp.sum((jnp.arange(nblk, dtype=jnp.int32)[:, None] * MOE_ROWS) >= ends[None, :], axis=1)
    blk_cls = jnp.minimum(blk_cls, blk_cls[jnp.maximum(nused[0] - 1, 0)])
    blk_elo = jnp.asarray(CLASS_E_LO, jnp.int32)[blk_cls]
    blk_ehi = jnp.asarray(CLASS_E_HI, jnp.int32)[blk_cls]
    xs = _sc_scatter_rows(rows.reshape(t, ROW_WORDS), idx3, n_rows)
    ys = _experts(xs, blk_elo, blk_ehi, nused, wg, wu, wd, layer)
    return _sc_gather_rows(ys, idx3).reshape(b, s, d // 2)


def kernel(x, c, ada_w, ada_b, norm1_g, norm2_g, ev_w_in, ev_conv_w, ev_dn_a_log, ev_dn_dt_bias, ev_dn_onorm_g, ev_fox_f_bias, ev_fox_qnorm_g, ev_fox_knorm_g, ev_w_out, od_w_in, od_conv_w, od_conv_b, od_lru_wr, od_lru_br, od_lru_wi, od_lru_bi, od_lru_lambda, od_sgu_norm_g, od_sgu_w, od_sgu_b, od_w_out, router_w, router_b, moe_w_gate, moe_w_up, moe_w_down):
    b, s, d = x.shape
    depth = ada_w.shape[0]
    tm = min(512, s)
    mod = _adaln(c, ada_w, ada_b).reshape(depth, b, 6, 1, d)
    rw = router_w.T.astype(BF16)
    rb = router_b.reshape(N_EXPERTS, 1)

    pending = None
    for layer in range(depth):
        sh1, sc1, gt1, sh2, sc2, gt2 = (mod[layer, :, k] for k in range(6))
        i = layer // 2
        if layer % 2 == 0:
            w = ev_w_in[i]
            nq = 3 * DN_HEADS * DN_DK
            nz = DN_HEADS * DN_DK
            nf = 3 * FOX_HEADS * FOX_DH
            o_a = nq + nz
            o_f = o_a + 2 * DN_HEADS
            o_ff = o_f + nf
            small_w = jnp.concatenate([w[:, o_a:o_f], w[:, o_ff:o_ff + FOX_HEADS]], axis=1)
            small_w = jnp.pad(small_w, ((0, 0), (0, LANES - small_w.shape[1])))
            w_all = jnp.concatenate([w[:, :o_a], w[:, o_f:o_ff], small_w], axis=1).astype(BF16)
            splits = ((0, nq), (nq, o_a), (o_a, o_a + nf), (o_a + nf, o_a + nf + LANES))
            res = _inproj(x, norm1_g[layer], sc1, sh1, w_all, splits,
                          (BF16, BF16, BF16, F32), (None,) * 4, tm, pending, ev_conv_w[i])
            if pending is not None:
                x, res = res[0], res[1:]
            dn, z, fox, small = res
            small_t = jnp.swapaxes(small[:, :, :16], 1, 2)
            zeros4 = jnp.zeros((4,), F32)
            mul = jnp.concatenate([-jnp.exp(ev_dn_a_log[i]), zeros4, -jnp.ones((4,), F32), zeros4])
            bias = jnp.concatenate([ev_dn_dt_bias[i], zeros4, ev_fox_f_bias[i], zeros4])
            pr = jnp.stack([mul, bias], axis=1)
            pc = jnp.pad(jnp.stack([mul, bias], axis=0), ((0, 0), (0, LANES - 16)))
            col, row = _gates(small, small_t, pc, pr)
            row_dn = row[:, :DN_HEADS].reshape(b, DN_HEADS, s // DN_GROUP, DN_GROUP).transpose(0, 2, 1, 3)
            o_dn = _deltanet(dn, z, col, row_dn, ev_dn_onorm_g[i])
            o_fox = _fox(fox, col, ev_fox_qnorm_g[i], ev_fox_knorm_g[i], min(512, s))
            mix_a, mix_b, w_out = o_dn, o_fox, ev_w_out[i]
        else:
            lw = od_lru_wr.shape[-1] * LRU_BLOCKS
            splits = ((0, lw), (lw, 2 * lw), (2 * lw, 2 * lw + od_sgu_w.shape[-1] * SGU_GROUPS),
                      (2 * lw + od_sgu_w.shape[-1] * SGU_GROUPS, od_w_in.shape[-1]))
            res = _inproj(x, norm1_g[layer], sc1, sh1, od_w_in[i], splits,
                          (BF16,) * 3, (None, "gelu"), tm, pending,
                          sgu=(od_sgu_norm_g[i], od_sgu_w[i], od_sgu_b[i]))
            if pending is not None:
                x, res = res[0], res[1:]
            lx, lg, o_sgu = res
            o_lru = _lru(lx, lg, od_conv_w[i], od_conv_b[i], od_lru_wr[i], od_lru_br[i], od_lru_wi[i],
                         od_lru_bi[i], od_lru_lambda[i])
            mix_a, mix_b, w_out = o_lru, o_sgu, od_w_out[i]
        x1, rows, route, counts = _outproj(mix_a, mix_b, x, w_out, gt1, norm2_g[layer], sc2, sh2,
                                           rw, rb, tm)
        x = x1
        pending = (_moe(rows, route, counts, moe_w_gate, moe_w_up, moe_w_down, layer), gt2)
    return _residual(x, *pending, tm)
```

```python
import functools

import jax
import jax.numpy as jnp
from jax import lax
from jax.experimental import pallas as pl
from jax.experimental.pallas import tpu as pltpu
from jax.experimental.pallas import tpu_sc as plsc

F32 = jnp.float32
BF16 = jnp.bfloat16
EPS = 1e-6
NEG_INF = float("-inf")

DN_HEADS = 4
DN_DK = 128
DN_CHUNK = 64
CONV_WIDTH = 4
FOX_HEADS = 4
FOX_DH = 128
LRU_BLOCKS = 4
LRU_C = 8.0
SGU_GROUPS = 4
SGU_CHUNK = 128
N_EXPERTS = 16
EXPERTS_PER_GROUP = 4
LANES = 128
SUBLANES = 8

VMEM_LIMIT = 48 * 1024 * 1024
TOKEN_BLOCK = 512
FOX_QUERY_BLOCK = 512


def _cparams(sem):
    return pltpu.CompilerParams(dimension_semantics=sem, vmem_limit_bytes=VMEM_LIMIT)


def _dot(a, b):
    return jnp.dot(a, b, preferred_element_type=F32)


def _dot_nt(a, b):
    return lax.dot_general(a, b, (((1,), (1,)), ((), ())), preferred_element_type=F32)


def _dot_tn(a, b):
    return lax.dot_general(a, b, (((0,), (0,)), ((), ())), preferred_element_type=F32)


def _dot_ones(a, b):
    if a.dtype == BF16:
        return sum(_dot(a, t.astype(BF16)) for t in _split3(b))
    return sum(_dot(t.astype(BF16), b) for t in _split3(a))


def _sigmoid(x):
    return 0.5 + 0.5 * jnp.tanh(0.5 * x)


def _silu(x):
    hx = 0.5 * x
    return hx + hx * jnp.tanh(hx)


def _softplus(x):
    return jnp.maximum(x, 0.0) + jnp.log(1.0 + jnp.exp(-jnp.abs(x)))


def _gelu_tanh(x):
    c = 0.7978845608028654
    return 0.5 * x * (1.0 + jnp.tanh(c * (x + 0.044715 * (x * x * x))))


def _prenorm(x, g, scale, shift):
    ms = jnp.mean(x * x, axis=-1, keepdims=True)
    return (x * lax.rsqrt(ms + EPS) * g) * (1.0 + scale) + shift


def _shift_rows(x, d, fill=0.0):
    rows = lax.broadcasted_iota(jnp.int32, x.shape, 0)
    return jnp.where(rows >= d, pltpu.roll(x, d, axis=0), fill)


def _causal_conv(x, w_ref):
    k_w = w_ref.shape[0]
    acc = x * w_ref[k_w - 1:k_w, :]
    for d in range(1, k_w):
        acc = acc + _shift_rows(x, d) * w_ref[k_w - 1 - d:k_w - d, :]
    return acc


def _adaln_kernel(c_ref, w_ref, b_ref, o_ref):
    c = c_ref[...]
    ca = _silu(c).astype(BF16)
    o_ref[0] = _dot(ca, w_ref[0].astype(BF16)) + b_ref[0]


def _adaln(c, ada_w, ada_b):
    depth, d, n = ada_w.shape
    b = c.shape[0]
    tn = 1536
    return pl.pallas_call(
        _adaln_kernel,
        grid=(depth, n // tn),
        in_specs=[pl.BlockSpec((b, d), lambda l, j: (0, 0)),
                  pl.BlockSpec((1, d, tn), lambda l, j: (l, 0, j)),
                  pl.BlockSpec((1, 1, tn), lambda l, j: (l, 0, j))],
        out_specs=pl.BlockSpec((1, b, tn), lambda l, j: (l, 0, j)),
        out_shape=jax.ShapeDtypeStruct((depth, b, n), F32),
        compiler_params=_cparams(("arbitrary", "arbitrary")),
        name="adaln",
    )(c, ada_w, ada_b.reshape(depth, 1, n))


CONV_HALO = 8
CONV_SLAB = 512


def _inproj_kernel(*refs, col_splits, acts, pending, conv, sgu, cast_w):
    refs = list(refs)
    x_ref = refs.pop(0)
    if pending:
        m_ref, gtm_ref = refs.pop(0), refs.pop(0)
    g_ref, sc_ref, sh_ref, w_ref = (refs.pop(0) for _ in range(4))
    if cast_w:
        w_f32, w_ref = w_ref, refs.pop()

        @pl.when((pl.program_id(0) == 0) & (pl.program_id(1) == 0))
        def _():
            w_ref[...] = w_f32[...].astype(BF16)
    if conv:
        cw_ref = refs.pop(0)
        halo_sc = refs.pop()
    if sgu:
        sg_ref, sw_ref, sbt_ref = (refs.pop(0) for _ in range(3))
    if pending:
        xo_ref = refs.pop(0)
        x = x_ref[0] + gtm_ref[0] * _unpack_halves(m_ref[0]).astype(F32)
        xo_ref[0] = x
    else:
        x = x_ref[0]
    out_refs = refs
    h = _prenorm(x, g_ref[...], sc_ref[0], sh_ref[0]).astype(BF16)
    tm = x.shape[0]
    if conv:
        @pl.when(pl.program_id(1) == 0)
        def _():
            halo_sc[0:CONV_HALO, :] = jnp.zeros((CONV_HALO, halo_sc.shape[1]), F32)

        c0, c1 = col_splits[0]
        for s0 in range(c0, c1, CONV_SLAB):
            halo_sc[CONV_HALO:, s0 - c0:s0 - c0 + CONV_SLAB] = _dot(h, w_ref[:, s0:s0 + CONV_SLAB])
    n_plain = len(col_splits) - (2 if sgu else 0)
    for k, (o_ref, (c0, c1), act) in enumerate(zip(out_refs[:n_plain], col_splits[:n_plain], acts)):
        if conv and k == 0:
            continue
        p = _dot(h, w_ref[:, c0:c1])
        if act == "gelu":
            p = _gelu_tanh(p)
        o_ref[0] = p.astype(o_ref.dtype)
    if sgu:
        (u0, u1), (v0, v1) = col_splits[n_plain:]
        o_ref = out_refs[n_plain]
        u = _gelu_tanh(_dot(h, w_ref[:, u0:u1]))
        v = _gelu_tanh(_dot(h, w_ref[:, v0:v1]))
        vn = (v * lax.rsqrt(jnp.mean(v * v, axis=-1, keepdims=True) + EPS) * sg_ref[...]).astype(BF16)
        c = SGU_CHUNK
        gw = (u1 - u0) // SGU_GROUPS
        ri = lax.broadcasted_iota(jnp.int32, (c, c), 0)
        ci = lax.broadcasted_iota(jnp.int32, (c, c), 1)
        for g in range(SGU_GROUPS):
            wg = jnp.where(ri >= ci, sw_ref[g], 0.0).astype(BF16)
            bcol = sbt_ref[:, g:g + 1]
            for n in range(tm // c):
                rows = slice(n * c, (n + 1) * c)
                cols = slice(g * gw, (g + 1) * gw)
                mixed = _dot(wg, vn[rows, cols]) + bcol
                o_ref[0, rows, cols] = (u[rows, cols] * mixed).astype(o_ref.dtype)
    if conv:
        o_ref = out_refs[0]
        for t0 in range(0, halo_sc.shape[1], DN_DK):
            cols = slice(t0, t0 + DN_DK)
            acc = halo_sc[CONV_HALO:, cols] * cw_ref[CONV_WIDTH - 1:CONV_WIDTH, cols]
            for dd in range(1, CONV_WIDTH):
                acc = acc + halo_sc[pl.ds(CONV_HALO - dd, tm), cols] * cw_ref[CONV_WIDTH - 1 - dd:CONV_WIDTH - dd, cols]
            halo_sc[0:CONV_HALO, cols] = halo_sc[tm:tm + CONV_HALO, cols]
            yj = _silu(acc)
            if t0 < 2 * DN_HEADS * DN_DK:
                yj = yj * lax.rsqrt(jnp.sum(yj * yj, axis=-1, keepdims=True) + EPS)
            if t0 < DN_HEADS * DN_DK:
                yj = yj * (DN_DK ** -0.5)
            o_ref[0, :, cols] = yj.astype(o_ref.dtype)


def _inproj(x, g, scale, shift, w, col_splits, out_dtypes, acts, tm, pending=None, conv_w=None, sgu=None):
    b, s, d = x.shape
    n = w.shape[1]
    out_cols = list(col_splits) if sgu is None else list(col_splits[:-2]) + [col_splits[-2]]
    outs = tuple(jax.ShapeDtypeStruct((b, s, c1 - c0), dt) for (c0, c1), dt in zip(out_cols, out_dtypes))
    row = lambda i, j: (i, 0, 0)
    tok = lambda i, j: (i, j, 0)
    const = lambda i, j: (0, 0)
    in_specs = [pl.BlockSpec((1, tm, d), tok)]
    out_specs = tuple(pl.BlockSpec((1, tm, c1 - c0), tok) for (c0, c1) in out_cols)
    args = (x,)
    if pending is not None:
        in_specs += [pl.BlockSpec((1, tm, d // 2), tok), pl.BlockSpec((1, 1, d), row)]
        out_specs = (pl.BlockSpec((1, tm, d), tok),) + out_specs
        outs = (jax.ShapeDtypeStruct((b, s, d), F32),) + outs
        args += tuple(pending)
    in_specs += [pl.BlockSpec((1, d), const), pl.BlockSpec((1, 1, d), row), pl.BlockSpec((1, 1, d), row),
                 pl.BlockSpec((d, n), const)]
    args += (g.reshape(1, d), scale, shift, w)
    scratch = []
    if conv_w is not None:
        in_specs.append(pl.BlockSpec(conv_w.shape, const))
        args += (conv_w,)
        scratch.append(pltpu.VMEM((CONV_HALO + tm, conv_w.shape[1]), F32))
    if w.dtype != BF16:
        scratch.append(pltpu.VMEM((d, n), BF16))
    if sgu is not None:
        g_norm, w_s, b_s = sgu
        in_specs += [pl.BlockSpec((1, g_norm.shape[0]), const), pl.BlockSpec(w_s.shape, lambda i, j: (0, 0, 0)),
                     pl.BlockSpec((b_s.shape[1], b_s.shape[0]), const)]
        args += (g_norm.reshape(1, -1), w_s, b_s.T)
    return pl.pallas_call(
        functools.partial(_inproj_kernel, col_splits=col_splits, acts=acts, pending=pending is not None,
                          conv=conv_w is not None, sgu=sgu is not None, cast_w=w.dtype != BF16),
        grid=(b, s // tm),
        in_specs=in_specs,
        out_specs=out_specs,
        out_shape=outs,
        scratch_shapes=scratch,
        compiler_params=_cparams(("arbitrary", "arbitrary")),
        name="inproj",
    )(*args)


def _gates_kernel(sm_ref, smt_ref, pc_ref, pr_ref, col_ref, row_ref):
    s = sm_ref.shape[1]
    blk = LANES
    ri = lax.broadcasted_iota(jnp.int32, (blk, blk), 0)
    ci = lax.broadcasted_iota(jnp.int32, (blk, blk), 1)
    same_chunk = (ri // DN_CHUNK) == (ci // DN_CHUNK)
    tril = jnp.where(ri >= ci, 1.0, 0.0).astype(BF16)
    tril_loc = jnp.where(same_chunk & (ri >= ci), 1.0, 0.0).astype(BF16)
    triu_loc = jnp.where(same_chunk & (ri <= ci), 1.0, 0.0).astype(BF16)
    lane = lax.broadcasted_iota(jnp.int32, (blk, LANES), 1)
    carry = jnp.zeros((1, LANES), F32)
    for j in range(s // blk):
        rows = slice(j * blk, (j + 1) * blk)
        xc = sm_ref[0, rows, :] + pc_ref[1:2, :]
        dec = pc_ref[0:1, :] * _softplus(jnp.where(lane < 4, xc, -xc))
        cum_glb = _dot_ones(tril, dec) + carry
        col_ref[0, rows, :] = jnp.where(lane < 4, _dot_ones(tril_loc, dec), jnp.where(lane < 8, _sigmoid(xc), cum_glb))
        carry = cum_glb[blk - 1:blk, :]
        decr = pr_ref[:, 0:1] * _softplus(smt_ref[0, :, rows] + pr_ref[:, 1:2])
        row_ref[0, :, rows] = _dot_ones(decr, triu_loc)


def _gates(small, small_t, pc, pr):
    b, s, _ = small.shape
    return pl.pallas_call(
        _gates_kernel,
        grid=(b,),
        in_specs=[pl.BlockSpec((1, s, LANES), lambda i: (i, 0, 0)),
                  pl.BlockSpec((1, 16, s), lambda i: (i, 0, 0)),
                  pl.BlockSpec((2, LANES), lambda i: (0, 0)),
                  pl.BlockSpec((16, 2), lambda i: (0, 0))],
        out_specs=(pl.BlockSpec((1, s, LANES), lambda i: (i, 0, 0)),
                   pl.BlockSpec((1, 16, s), lambda i: (i, 0, 0))),
        out_shape=(jax.ShapeDtypeStruct((b, s, LANES), F32), jax.ShapeDtypeStruct((b, 16, s), F32)),
        compiler_params=_cparams(("arbitrary",)),
        name="gates",
    )(small, small_t, pc, pr)


DN_PACK = 4
DN_GROUP = DN_PACK * DN_CHUNK
DN_ITER_GROUPS = 2


def _blockdiag(p):
    c, wide = p.shape
    t = jnp.concatenate([p] * (wide // c), axis=0)
    rb = lax.broadcasted_iota(jnp.int32, (wide, wide), 0) // c
    cb = lax.broadcasted_iota(jnp.int32, (wide, wide), 1) // c
    return jnp.where(rb == cb, t, 0.0).astype(BF16)


def _diag_blocks(m, c):
    wide = m.shape[1]
    cb = lax.broadcasted_iota(jnp.int32, (c, wide), 1) // c
    out = m[:c]
    for j in range(1, wide // c):
        out = jnp.where(cb == j, m[j * c:(j + 1) * c], out)
    return out


def _rows_to_blocks(col, c, wide):
    cb = lax.broadcasted_iota(jnp.int32, (c, wide), 1) // c
    out = jnp.broadcast_to(col[:c], (c, wide))
    for j in range(1, wide // c):
        out = jnp.where(cb == j, col[j * c:(j + 1) * c], out)
    return out


def _packed_unit_lower_inverse(lows):
    c, wide = lows[0].shape
    ri = lax.broadcasted_iota(jnp.int32, (c, wide), 0)
    ci = lax.broadcasted_iota(jnp.int32, (c, wide), 1) % c
    eye = jnp.where(ri == ci, 1.0, 0.0)
    n = range(len(lows))
    ds = [jnp.where((ri // 16) == (ci // 16), lo, 0.0) for lo in lows]
    xs = [eye - d for d in ds]
    ps = ds
    bds = [_blockdiag(p) for p in ps]
    for _ in range(3):
        ps = [_dot(ps[i].astype(BF16), bds[i]) for i in n]
        bds = [_blockdiag(p) for p in ps]
        xs = [xs[i] + _dot(xs[i].astype(BF16), bds[i]) for i in n]
    for width in (16, 32):
        sel = ((ri // (2 * width)) == (ci // (2 * width))) & ((ri // width) != (ci // width))
        offs = [_blockdiag(jnp.where(sel, lo, 0.0)) for lo in lows]
        ts = [_dot(xs[i].astype(BF16), offs[i]) for i in n]
        bdx = [_blockdiag(x) for x in xs]
        xs = [xs[i] - _dot(ts[i].astype(BF16), bdx[i]) for i in n]
    return xs


def _deltanet_kernel(dn_ref, z_ref, col_ref, row_ref, og_ref, o_ref, st_sc):
    s = dn_ref.shape[1]
    c = DN_CHUNK
    gt = DN_GROUP
    dk = DN_DK
    hd = DN_HEADS * dk
    heads = range(DN_HEADS)
    st_sc[...] = jnp.zeros_like(st_sc)

    ri = lax.broadcasted_iota(jnp.int32, (c, gt), 0)
    ci = lax.broadcasted_iota(jnp.int32, (c, gt), 1) % c
    og = og_ref[...]

    per_iter = DN_ITER_GROUPS if (s // gt) % DN_ITER_GROUPS == 0 else 1
    insts = [(gg, h) for gg in range(per_iter) for h in heads]
    ins = range(len(insts))

    def group(g, carry):
        r0 = [pl.multiple_of((g * per_iter + gg) * gt, gt) for gg in range(per_iter)]
        colg = [col_ref[0, pl.ds(r0[gg], gt), :] for gg in range(per_iter)]
        rowg = [row_ref[0, g * per_iter + gg] for gg in range(per_iter)]
        hs = [slice(h * dk, (h + 1) * dk) for h in heads]
        q = [dn_ref[0, pl.ds(r0[gg], gt), h * dk:(h + 1) * dk].astype(F32) for gg, h in insts]
        k = [dn_ref[0, pl.ds(r0[gg], gt), hd + h * dk:hd + (h + 1) * dk].astype(F32) for gg, h in insts]
        v = [dn_ref[0, pl.ds(r0[gg], gt), 2 * hd + h * dk:2 * hd + (h + 1) * dk].astype(F32) for gg, h in insts]
        gc = [colg[gg][:, h:h + 1] for gg, h in insts]
        beta = [colg[gg][:, DN_HEADS + h:DN_HEADS + h + 1] for gg, h in insts]
        grow = [rowg[gg][h:h + 1, :] for gg, h in insts]
        glast = [jnp.concatenate([jnp.broadcast_to(gc[i][(j + 1) * c - 1:(j + 1) * c], (c, 1))
                                  for j in range(DN_PACK)], axis=0) for i in ins]
        eg = [jnp.exp(gc[i]) for i in ins]
        kbf = [k[i].astype(BF16) for i in ins]
        both = [_dot_nt(jnp.concatenate([(k[i] * beta[i]).astype(BF16), q[i].astype(BF16)], axis=0), kbf[i])
                for i in ins]
        decay = [jnp.exp(jnp.where(ri >= ci, _rows_to_blocks(gc[i], c, gt) - grow[i], NEG_INF)) for i in ins]
        kk = [_diag_blocks(both[i][:gt], c) * decay[i] for i in ins]
        qk = [_blockdiag(_diag_blocks(both[i][gt:], c) * decay[i]) for i in ins]
        t_inv = _packed_unit_lower_inverse([jnp.where(ri > ci, kk[i], 0.0) for i in ins])
        rhs = [jnp.concatenate([k[i] * (beta[i] * eg[i]), v[i] * beta[i]], axis=1).astype(BF16) for i in ins]
        wu = [_dot(_blockdiag(t_inv[i]), rhs[i]).astype(BF16) for i in ins]
        qwu = [_dot(qk[i], wu[i]) for i in ins]
        qp = [(q[i] * eg[i] - qwu[i][:, :dk]).astype(BF16) for i in ins]
        kdec = [(k[i] * jnp.exp(glast[i] - gc[i])).astype(BF16) for i in ins]
        mb = [[_dot_tn(kdec[i][j * c:(j + 1) * c], wu[i][j * c:(j + 1) * c]) for i in ins]
              for j in range(DN_PACK)]
        for gg in range(per_iter):
            mine = [gg * DN_HEADS + h for h in heads]
            outs = [[] for _ in heads]
            for j in range(DN_PACK):
                rows = slice(j * c, (j + 1) * c)
                state = [st_sc[h] for h in heads]
                lhs = [jnp.concatenate([qp[i][rows], mb[j][i][:, :dk].astype(BF16)], axis=0) for i in mine]
                r = [_dot(lhs[h], state[h].astype(BF16)) for h in heads]
                for h, i in zip(heads, mine):
                    gl = jnp.exp(glast[i][j * c:j * c + 1])
                    st_sc[h] = state[h] * gl - r[h][c:] + mb[j][i][:, dk:]
                    outs[h].append(r[h][:c] + qwu[i][rows, dk:])
            for h in heads:
                o = jnp.concatenate(outs[h], axis=0)
                on = o * lax.rsqrt(jnp.mean(o * o, axis=-1, keepdims=True) + EPS) * og
                zz = z_ref[0, pl.ds(r0[gg], gt), hs[h]].astype(F32)
                o_ref[0, pl.ds(r0[gg], gt), hs[h]] = (on * _silu(zz)).astype(o_ref.dtype)
        return carry

    lax.fori_loop(0, s // (gt * per_iter), group, 0)


def _deltanet(dn, z, col, row, onorm_g):
    b, s, w3 = dn.shape
    hd = DN_HEADS * DN_DK
    return pl.pallas_call(
        _deltanet_kernel,
        grid=(b,),
        in_specs=[pl.BlockSpec((1, s, w3), lambda i: (i, 0, 0)),
                  pl.BlockSpec((1, s, hd), lambda i: (i, 0, 0)),
                  pl.BlockSpec((1, s, LANES), lambda i: (i, 0, 0)),
                  pl.BlockSpec((1, s // DN_GROUP, DN_HEADS, DN_GROUP), lambda i: (i, 0, 0, 0)),
                  pl.BlockSpec((1, DN_DK), lambda i: (0, 0))],
        out_specs=pl.BlockSpec((1, s, hd), lambda i: (i, 0, 0)),
        out_shape=jax.ShapeDtypeStruct((b, s, hd), BF16),
        scratch_shapes=[pltpu.VMEM((DN_HEADS, DN_DK, DN_DK), F32)],
        compiler_params=_cparams(("arbitrary",)),
        name="deltanet",
    )(dn, z, col, row, onorm_g.reshape(1, DN_DK))


def _split3(x):
    hi = x.astype(BF16).astype(F32)
    r = x - hi
    mid = r.astype(BF16).astype(F32)
    return hi, mid, r - mid


def _fox_kernel(q_ref, k_ref, v_ref, colq_ref, colk_ref, qg_ref, kg_ref, o_ref, ka_sc, va_sc, m_sc, acc_sc, *, tq):
    qi = pl.program_id(1)
    s = k_ref.shape[1]
    dh = FOX_DH

    def bias_lanes(col):
        lane = lax.broadcasted_iota(jnp.int32, col.shape, 1)
        hi, mid, lo = _split3(col)
        return jnp.where((lane >= 8) & (lane < 8 + FOX_HEADS), hi,
                         jnp.where((lane >= 24) & (lane < 24 + FOX_HEADS), pltpu.roll(mid, 16, axis=1),
                                   jnp.where((lane >= 40) & (lane < 40 + FOX_HEADS), pltpu.roll(lo, 32, axis=1), 0.0)))

    @pl.when(qi == 0)
    def _():
        lane = lax.broadcasted_iota(jnp.int32, (s, LANES), 1)
        ones_col = jnp.where(lane == 0, 1.0, 0.0).astype(BF16)
        ones_k = (lane >= 56) & (lane < 96) & ((lane % 16) >= 8) & ((lane % 16) < 8 + FOX_HEADS)
        ext_k = (jnp.where(ones_k, 1.0, 0.0) - bias_lanes(colk_ref[0])).astype(BF16)
        for h in range(FOX_HEADS):
            cols = slice(h * dh, (h + 1) * dh)
            kf = k_ref[0, :, cols].astype(F32)
            kn = kf * lax.rsqrt(jnp.mean(kf * kf, axis=-1, keepdims=True) + EPS) * kg_ref[...]
            ka_sc[h, :, :dh] = kn.astype(BF16)
            ka_sc[h, :, dh:] = ext_k
            va_sc[h, :, :dh] = v_ref[0, :, cols]
            va_sc[h, :, dh:] = ones_col

    lane = lax.broadcasted_iota(jnp.int32, (tq, LANES), 1)
    cq = pltpu.roll(bias_lanes(colq_ref[0]), 48, axis=1)
    qa = []
    for h in range(FOX_HEADS):
        cols = slice(h * dh, (h + 1) * dh)
        qf = q_ref[0, :, cols].astype(F32)
        qn = qf * lax.rsqrt(jnp.mean(qf * qf, axis=-1, keepdims=True) + EPS) * qg_ref[...] * (dh ** -0.5)
        mine = ((lane % 16) == 8 + h) & (lane < 96)
        ext = jnp.where(mine, jnp.where(lane < 48, 1.0, cq), 0.0)
        qa.append(jnp.concatenate([qn.astype(BF16), ext.astype(BF16)], axis=1))
    m_sc[...] = jnp.full(m_sc.shape, NEG_INF, F32)
    acc_sc[...] = jnp.zeros_like(acc_sc)
    causal = lax.broadcasted_iota(jnp.int32, (tq, tq), 0) >= lax.broadcasted_iota(jnp.int32, (tq, tq), 1)

    def step(k0, masked):
        heads = range(FOX_HEADS)
        logits = [_dot_nt(qa[h], ka_sc[h, pl.ds(k0, tq), :]) for h in heads]
        if masked:
            logits = [jnp.where(causal, lg, NEG_INF) for lg in logits]
        ps, alphas = [], []
        for h in heads:
            m_old = m_sc[h]
            m_new = jnp.maximum(m_old, jnp.max(logits[h], axis=1, keepdims=True))
            m_sc[h] = m_new
            alphas.append(jnp.exp(m_old - m_new))
            ps.append(jnp.exp(logits[h] - jnp.concatenate([m_new] * (tq // LANES), axis=1)).astype(BF16))
        for h in heads:
            pv = _dot(ps[h], va_sc[h, pl.ds(k0, tq), :])
            acc_sc[h] = acc_sc[h] * jnp.concatenate([alphas[h], alphas[h]], axis=1) + pv

    def body(j, carry):
        step(pl.multiple_of(j * tq, tq), False)
        return carry

    lax.fori_loop(0, qi, body, 0)
    step(pl.multiple_of(qi * tq, tq), True)
    for h in range(FOX_HEADS):
        acc = acc_sc[h]
        o_ref[0, :, h * dh:(h + 1) * dh] = (acc[:, :dh] / acc[:, dh:dh + 1]).astype(o_ref.dtype)


def _fox(fox, col, qg, kg, tq):
    b, s, _ = fox.shape
    hd = FOX_HEADS * FOX_DH
    return pl.pallas_call(
        functools.partial(_fox_kernel, tq=tq),
        grid=(b, s // tq),
        in_specs=[pl.BlockSpec((1, tq, hd), lambda i, j: (i, j, 0)),
                  pl.BlockSpec((1, s, hd), lambda i, j: (i, 0, 1)),
                  pl.BlockSpec((1, s, hd), lambda i, j: (i, 0, 2)),
                  pl.BlockSpec((1, tq, LANES), lambda i, j: (i, j, 0)),
                  pl.BlockSpec((1, s, LANES), lambda i, j: (i, 0, 0)),
                  pl.BlockSpec((1, FOX_DH), lambda i, j: (0, 0)),
                  pl.BlockSpec((1, FOX_DH), lambda i, j: (0, 0))],
        out_specs=pl.BlockSpec((1, tq, hd), lambda i, j: (i, j, 0)),
        out_shape=jax.ShapeDtypeStruct((b, s, hd), BF16),
        scratch_shapes=[pltpu.VMEM((FOX_HEADS, s, 2 * FOX_DH), BF16), pltpu.VMEM((FOX_HEADS, s, 2 * FOX_DH), BF16),
                        pltpu.VMEM((FOX_HEADS, tq, LANES), F32), pltpu.VMEM((FOX_HEADS, tq, 2 * FOX_DH), F32)],
        compiler_params=_cparams(("arbitrary", "arbitrary")),
        name="fox",
    )(fox, fox, fox, col, col, qg.reshape(1, FOX_DH), kg.reshape(1, FOX_DH))


def _lru_kernel(x_ref, gate_ref, cw_ref, cb_ref, wr_ref, br_ref, wi_ref, bi_ref, lam_ref, o_ref, a_sc, b_sc):
    s = x_ref.shape[1]
    x = _causal_conv(x_ref[0].astype(F32), cw_ref) + cb_ref[...]
    xb = x.astype(BF16)
    r = _sigmoid(_dot(xb, wr_ref[0].astype(BF16)) + br_ref[...])
    i = _sigmoid(_dot(xb, wi_ref[0].astype(BF16)) + bi_ref[...])
    log_a = (-LRU_C) * r * _softplus(-lam_ref[...])
    a = jnp.exp(log_a)
    bb = jnp.sqrt(1.0 - a * a) * (i * x)
    nt = s // SUBLANES
    a3 = a.reshape(nt, SUBLANES, a.shape[1])
    b3 = bb.reshape(nt, SUBLANES, a.shape[1])
    sub = lax.broadcasted_iota(jnp.int32, a3.shape, 1)
    d = 1
    while d < SUBLANES:
        keep = sub >= d
        b3 = a3 * jnp.where(keep, pltpu.roll(b3, d, axis=1), 0.0) + b3
        a3 = a3 * jnp.where(keep, pltpu.roll(a3, d, axis=1), 1.0)
        d *= 2
    a_sc[...] = a3.reshape(s, a.shape[1])
    b_sc[...] = b3.reshape(s, a.shape[1])
    at = a_sc[pl.ds(SUBLANES - 1, nt, stride=SUBLANES), :]
    bt = b_sc[pl.ds(SUBLANES - 1, nt, stride=SUBLANES), :]
    d = 1
    while d < nt:
        bt = at * _shift_rows(bt, d, 0.0) + bt
        at = at * _shift_rows(at, d, 1.0)
        d *= 2
    h_prev = _shift_rows(bt, 1, 0.0)
    bb = (a3 * h_prev[:, None, :] + b3).reshape(s, a.shape[1])
    o_ref[0] = (bb * gate_ref[0].astype(F32)).astype(o_ref.dtype)


def _lru(lx, lg, conv_w, conv_b, wr, br, wi, bi, lam):
    b, s, wd = lx.shape
    blk = wd // LRU_BLOCKS
    vec = lambda i, j: (0, j)
    return pl.pallas_call(
        _lru_kernel,
        grid=(b, LRU_BLOCKS),
        in_specs=[pl.BlockSpec((1, s, blk), lambda i, j: (i, 0, j)),
                  pl.BlockSpec((1, s, blk), lambda i, j: (i, 0, j)),
                  pl.BlockSpec((CONV_WIDTH, blk), vec),
                  pl.BlockSpec((1, blk), vec),
                  pl.BlockSpec((1, blk, blk), lambda i, j: (j, 0, 0)),
                  pl.BlockSpec((1, blk), vec),
                  pl.BlockSpec((1, blk, blk), lambda i, j: (j, 0, 0)),
                  pl.BlockSpec((1, blk), vec),
                  pl.BlockSpec((1, blk), vec)],
        out_specs=pl.BlockSpec((1, s, blk), lambda i, j: (i, 0, j)),
        out_shape=jax.ShapeDtypeStruct((b, s, wd), BF16),
        scratch_shapes=[pltpu.VMEM((s, blk), F32), pltpu.VMEM((s, blk), F32)],
        compiler_params=_cparams(("arbitrary", "arbitrary")),
        name="rglru",
    )(lx, lg, conv_w, conv_b.reshape(1, wd), wr, br.reshape(1, wd), wi, bi.reshape(1, wd), lam.reshape(1, wd))


def _routing(logits, rb):
    ne, tm = logits.shape
    row = lax.broadcasted_iota(jnp.int32, (ne, tm), 0)
    row_f = row.astype(F32)
    ex = jnp.exp(logits - jnp.max(logits, axis=0, keepdims=True))
    probs = ex / jnp.sum(ex, axis=0, keepdims=True)
    sel = probs + rb

    def top2(vals, idx):
        m1 = jnp.max(vals, axis=0, keepdims=True)
        i1 = jnp.min(jnp.where(vals == m1, idx, float(ne)), axis=0, keepdims=True)
        rest = jnp.where(idx == i1, NEG_INF, vals)
        m2 = jnp.max(rest, axis=0, keepdims=True)
        return m1, i1, m2, rest

    best = None
    g_idx = None
    grp = row // EXPERTS_PER_GROUP
    for g in range(ne // EXPERTS_PER_GROUP):
        m1, _, m2, _ = top2(jnp.where(grp == g, sel, NEG_INF), row_f)
        score = m1 + m2
        if g == 0:
            best, g_idx = score, jnp.zeros((1, tm), jnp.int32)
        else:
            upd = score > best
            best = jnp.where(upd, score, best)
            g_idx = jnp.where(upd, g, g_idx)
    _, i1, m2, rest = top2(jnp.where(grp == g_idx, sel, NEG_INF), row_f)
    i2 = jnp.min(jnp.where(rest == m2, row_f, float(ne)), axis=0, keepdims=True)
    p1 = jnp.sum(jnp.where(row_f == i1, probs, 0.0), axis=0, keepdims=True)
    p2 = jnp.sum(jnp.where(row_f == i2, probs, 0.0), axis=0, keepdims=True)
    den = p1 + p2
    first_lo = i1 < i2
    a = jnp.where(first_lo, i1, i2) - EXPERTS_PER_GROUP * g_idx.astype(F32)
    bhi = jnp.where(first_lo, i2, i1) - EXPERTS_PER_GROUP * g_idx.astype(F32)
    pair = a * (7.0 - a) * 0.5 + (bhi - a - 1.0)
    pair = jnp.where(a == 1.0, 7.0 - pair, pair)
    cls = PAIRS_PER_GROUP * g_idx.astype(F32) + pair
    w_lo = jnp.where(first_lo, p1, p2) / den
    w_hi = jnp.where(first_lo, p2, p1) / den
    swap = pair == 5.0
    return cls, jnp.where(swap, w_hi, w_lo), jnp.where(swap, w_lo, w_hi)


PAIRS_PER_GROUP = EXPERTS_PER_GROUP * (EXPERTS_PER_GROUP - 1) // 2
N_CLASSES = (N_EXPERTS // EXPERTS_PER_GROUP) * PAIRS_PER_GROUP
PAIR_WALK = [(0, 1), (0, 2), (0, 3), (1, 3), (1, 2), (3, 2)]
assert EXPERTS_PER_GROUP == 4 and len(PAIR_WALK) == PAIRS_PER_GROUP
CLASS_E0 = [EXPERTS_PER_GROUP * (c // PAIRS_PER_GROUP) + PAIR_WALK[c % PAIRS_PER_GROUP][0] for c in range(N_CLASSES)]
CLASS_E1 = [EXPERTS_PER_GROUP * (c // PAIRS_PER_GROUP) + PAIR_WALK[c % PAIRS_PER_GROUP][1] for c in range(N_CLASSES)]
CLASS_ROWS = 32
ROW_FEAT = 512
ROW_WORDS = 640


def _pack_halves(x):
    kk = x.shape[1] // 2
    lo = lax.bitcast_convert_type(x[:, :kk].astype(BF16).astype(F32), jnp.uint32)
    hi = lax.bitcast_convert_type(x[:, kk:].astype(BF16).astype(F32), jnp.uint32)
    return lax.bitcast_convert_type((lo >> 16) | hi, jnp.int32)


def _unpack_halves(p):
    u = lax.bitcast_convert_type(p, jnp.uint32)
    lo = lax.bitcast_convert_type(u << 16, F32)
    hi = lax.bitcast_convert_type(u & jnp.uint32(0xFFFF0000), F32)
    return jnp.concatenate([lo.astype(BF16), hi.astype(BF16)], axis=1)


def _outproj_kernel(a_ref, b_ref, x_ref, wf_ref, gt_ref, g2_ref, sc_ref, sh_ref, rwt_ref, rb_ref,
                    x1_ref, rows_ref, route_ref, cnt_ref, carry_sc, w_ref, tri_sc):
    first = (pl.program_id(0) == 0) & (pl.program_id(1) == 0)

    @pl.when(first)
    def _():
        carry_sc[...] = jnp.zeros_like(carry_sc)
        w_ref[...] = wf_ref[...].astype(BF16)
        earlier = (lax.broadcasted_iota(jnp.int32, tri_sc.shape, 0) < lax.broadcasted_iota(jnp.int32, tri_sc.shape, 1))
        tri_sc[...] = jnp.where(earlier, 1.0, 0.0).astype(BF16)

    half = a_ref.shape[2]
    tm = a_ref.shape[1]
    y = _dot(a_ref[0], w_ref[:half, :]) + _dot(b_ref[0], w_ref[half:, :])
    x1 = x_ref[0] + gt_ref[0] * y
    x1_ref[0] = x1
    h2 = _prenorm(x1, g2_ref[...], sc_ref[0], sh_ref[0])
    rows_ref[0, :, :ROW_FEAT] = _pack_halves(h2)
    cls, w_lo, w_hi = _routing(_dot_nt(rwt_ref[...], h2.astype(BF16)), rb_ref[...])
    r128 = lax.broadcasted_iota(jnp.int32, (LANES, tm), 0)
    wts = jnp.where(r128 == 0, w_lo, jnp.where(r128 == 1, w_hi, 0.0))
    rows_ref[0, :, ROW_FEAT:] = lax.bitcast_convert_type(wts.T[:, :ROW_WORDS - ROW_FEAT], jnp.int32)
    crow = lax.broadcasted_iota(jnp.int32, (carry_sc.shape[0], tm), 0).astype(F32)
    onehot = jnp.where(crow == cls, 1.0, 0.0)
    prefix = _dot(onehot.astype(BF16), tri_sc[...]) + carry_sc[:, 0:1]
    rank = jnp.sum(onehot * prefix, axis=0, keepdims=True)
    r8 = lax.broadcasted_iota(jnp.int32, (8, tm), 0)
    route_ref[0] = jnp.where(r8 == 0, cls, jnp.where(r8 == 1, rank, 0.0))
    carry_sc[...] = carry_sc[...] + jnp.sum(onehot, axis=1, keepdims=True)
    cnt_ref[...] = carry_sc[...]


def _outproj(a, bb, x, w, gt, g2, sc2, sh2, rw, rb, tm):
    b, s, d = x.shape
    half = a.shape[2]
    row = lambda i, j: (i, 0, 0)
    tok = lambda i, j: (i, j, 0)
    const = lambda i, j: (0, 0)
    return pl.pallas_call(
        _outproj_kernel,
        grid=(b, s // tm),
        in_specs=[pl.BlockSpec((1, tm, half), tok),
                  pl.BlockSpec((1, tm, half), tok),
                  pl.BlockSpec((1, tm, d), tok),
                  pl.BlockSpec((2 * half, d), const),
                  pl.BlockSpec((1, 1, d), row),
                  pl.BlockSpec((1, d), const),
                  pl.BlockSpec((1, 1, d), row),
                  pl.BlockSpec((1, 1, d), row),
                  pl.BlockSpec((N_EXPERTS, d), const),
                  pl.BlockSpec((N_EXPERTS, 1), const)],
        out_specs=(pl.BlockSpec((1, tm, d), tok),
                   pl.BlockSpec((1, tm, ROW_WORDS), tok),
                   pl.BlockSpec((1, 8, tm), lambda i, j: (i, 0, j)),
                   pl.BlockSpec((CLASS_ROWS, LANES), const)),
        out_shape=(jax.ShapeDtypeStruct((b, s, d), F32),
                   jax.ShapeDtypeStruct((b, s, ROW_WORDS), jnp.int32),
                   jax.ShapeDtypeStruct((b, 8, s), F32),
                   jax.ShapeDtypeStruct((CLASS_ROWS, LANES), F32)),
        scratch_shapes=[pltpu.VMEM((CLASS_ROWS, LANES), F32), pltpu.VMEM((2 * half, d), BF16),
                        pltpu.VMEM((tm, tm), BF16)],
        compiler_params=_cparams(("arbitrary", "arbitrary")),
        name="outproj_router",
    )(a, bb, x, w, gt, g2.reshape(1, d), sc2, sh2, rw, rb)


MOE_ROWS = 512
SC_CHUNK = 128


def _sc_workers():
    info = plsc.get_sparse_core_info()
    return info.num_cores, info.num_cores * info.num_subcores


def _sc_scatter_rows(rows, idx3, n_out):
    nw, k, ch = idx3.shape
    width = rows.shape[1]
    nc, _ = _sc_workers()
    mesh = plsc.VectorSubcoreMesh(core_axis_name="c", subcore_axis_name="s")

    @functools.partial(
        pl.kernel, mesh=mesh,
        out_type=jax.ShapeDtypeStruct((n_out, width), rows.dtype),
        scratch_types=[pltpu.VMEM((k, ch), jnp.int32), pltpu.VMEM((ch, width), rows.dtype), pltpu.SemaphoreType.DMA],
        name="moe_dispatch")
    def kern(rows_hbm, idx_hbm, out_hbm, idx_v, rows_v, sem):
        wid = lax.axis_index("s") * nc + lax.axis_index("c")
        pltpu.sync_copy(idx_hbm.at[wid], idx_v)

        @pl.loop(0, k)
        def _(j):
            pltpu.sync_copy(rows_hbm.at[pl.ds((wid * k + j) * ch, ch)], rows_v)
            pltpu.async_copy(rows_v, out_hbm.at[idx_v.at[j]], sem).wait()

    return kern(rows, idx3)


def _sc_gather_rows(table, idx3):
    nw, k, ch = idx3.shape
    width = table.shape[1]
    nc, _ = _sc_workers()
    mesh = plsc.VectorSubcoreMesh(core_axis_name="c", subcore_axis_name="s")

    @functools.partial(
        pl.kernel, mesh=mesh,
        out_type=jax.ShapeDtypeStruct((nw * k * ch, width), table.dtype),
        scratch_types=[pltpu.VMEM((k, ch), jnp.int32), pltpu.VMEM((ch, width), table.dtype), pltpu.SemaphoreType.DMA],
        name="moe_combine")
    def kern(table_hbm, idx_hbm, out_hbm, idx_v, rows_v, sem):
        wid = lax.axis_index("s") * nc + lax.axis_index("c")
        pltpu.sync_copy(idx_hbm.at[wid], idx_v)

        @pl.loop(0, k)
        def _(j):
            pltpu.async_copy(table_hbm.at[idx_v.at[j]], rows_v, sem).wait()
            pltpu.sync_copy(rows_v, out_hbm.at[pl.ds((wid * k + j) * ch, ch)])

    return kern(table, idx3)


def _experts_kernel(e0_ref, e1_ref, nused_ref, x_ref, wg0_ref, wu0_ref, wd0_ref, wg1_ref, wu1_ref, wd1_ref, y_ref):
    @pl.when(pl.program_id(0) < nused_ref[0])
    def _():
        blk = x_ref[...]
        feat = ROW_FEAT
        h = _unpack_halves(blk[:, :feat])
        wts = lax.bitcast_convert_type(blk[:, feat:], F32)
        y = None
        for col, (wg_ref, wu_ref, wd_ref) in enumerate(((wg0_ref, wu0_ref, wd0_ref), (wg1_ref, wu1_ref, wd1_ref))):
            act = (_silu(_dot(h, wg_ref[0, 0].astype(BF16))) * _dot(h, wu_ref[0, 0].astype(BF16))
                   * wts[:, col:col + 1])
            part = _dot(act.astype(BF16), wd_ref[0, 0].astype(BF16))
            y = part if y is None else y + part
        y_ref[...] = _pack_halves(y)


def _experts(xs, blk_e0, blk_e1, nused, wg, wu, wd, layer):
    n_rows = xs.shape[0]
    _, ne, d, f = wg.shape
    nblk = n_rows // MOE_ROWS
    rows = lambda i, e0, e1, nu: (jnp.minimum(i, nu[0] - 1), 0)
    lo = lambda i, e0, e1, nu: (layer, e0[i], 0, 0)
    hi = lambda i, e0, e1, nu: (layer, e1[i], 0, 0)
    return pl.pallas_call(
        _experts_kernel,
        grid_spec=pltpu.PrefetchScalarGridSpec(
            num_scalar_prefetch=3,
            grid=(nblk,),
            in_specs=[pl.BlockSpec((MOE_ROWS, ROW_WORDS), rows),
                      pl.BlockSpec((1, 1, d, f), lo), pl.BlockSpec((1, 1, d, f), lo), pl.BlockSpec((1, 1, f, d), lo),
                      pl.BlockSpec((1, 1, d, f), hi), pl.BlockSpec((1, 1, d, f), hi), pl.BlockSpec((1, 1, f, d), hi)],
            out_specs=pl.BlockSpec((MOE_ROWS, d // 2), lambda i, e0, e1, nu: (i, 0))),
        out_shape=jax.ShapeDtypeStruct((n_rows, d // 2), jnp.int32),
        compiler_params=_cparams(("arbitrary",)),
        name="moe_experts",
    )(blk_e0, blk_e1, nused, xs, wg, wu, wd, wg, wu, wd)


def _residual_kernel(x_ref, m_ref, gt_ref, o_ref):
    o_ref[0] = x_ref[0] + gt_ref[0] * _unpack_halves(m_ref[0]).astype(F32)


def _residual(x1, moe_p, gt, tm):
    b, s, d = x1.shape
    tok = lambda i, j: (i, j, 0)
    return pl.pallas_call(
        _residual_kernel,
        grid=(b, s // tm),
        in_specs=[pl.BlockSpec((1, tm, d), tok), pl.BlockSpec((1, tm, d // 2), tok),
                  pl.BlockSpec((1, 1, d), lambda i, j: (i, 0, 0))],
        out_specs=pl.BlockSpec((1, tm, d), tok),
        out_shape=jax.ShapeDtypeStruct((b, s, d), F32),
        compiler_params=_cparams(("arbitrary", "arbitrary")),
        name="moe_residual",
    )(x1, moe_p, gt)


def _moe(rows, route, counts, wg, wu, wd, layer):
    b, s, _ = rows.shape
    d = wg.shape[2]
    t = b * s
    _, nw = _sc_workers()
    cnt = counts[:N_CLASSES, 0].astype(jnp.int32)
    padded = ((cnt + MOE_ROWS - 1) // MOE_ROWS) * MOE_ROWS
    ends = jnp.cumsum(padded)
    n_rows = t + N_CLASSES * MOE_ROWS
    nblk = n_rows // MOE_ROWS
    cls = route[:, 0, :].reshape(t).astype(jnp.int32)
    rank = route[:, 1, :].reshape(t).astype(jnp.int32)
    pos = rank + jnp.sum(jnp.where(cls[:, None] > jnp.arange(N_CLASSES, dtype=jnp.int32)[None, :], padded[None, :], 0),
                         axis=1)
    idx3 = pos.reshape(nw, t // (nw * SC_CHUNK), SC_CHUNK)
    nused = (ends[-1] // MOE_ROWS).reshape(1)
    blk_cls = jnp.sum((jnp.arange(nblk, dtype=jnp.int32)[:, None] * MOE_ROWS) >= ends[None, :], axis=1)
    blk_cls = jnp.minimum(blk_cls, blk_cls[jnp.maximum(nused[0] - 1, 0)])
    blk_e0 = jnp.asarray(CLASS_E0, jnp.int32)[blk_cls]
    blk_e1 = jnp.asarray(CLASS_E1, jnp.int32)[blk_cls]
    xs = _sc_scatter_rows(rows.reshape(t, ROW_WORDS), idx3, n_rows)
    ys = _experts(xs, blk_e0, blk_e1, nused, wg, wu, wd, layer)
    return _sc_gather_rows(ys, idx3).reshape(b, s, d // 2)


def kernel(x, c, ada_w, ada_b, norm1_g, norm2_g, ev_w_in, ev_conv_w, ev_dn_a_log, ev_dn_dt_bias, ev_dn_onorm_g, ev_fox_f_bias, ev_fox_qnorm_g, ev_fox_knorm_g, ev_w_out, od_w_in, od_conv_w, od_conv_b, od_lru_wr, od_lru_br, od_lru_wi, od_lru_bi, od_lru_lambda, od_sgu_norm_g, od_sgu_w, od_sgu_b, od_w_out, router_w, router_b, moe_w_gate, moe_w_up, moe_w_down):
    b, s, d = x.shape
    depth = ada_w.shape[0]
    tm = min(TOKEN_BLOCK, s)
    mod = _adaln(c, ada_w, ada_b).reshape(depth, b, 6, 1, d)
    rw = router_w.T.astype(BF16)
    rb = router_b.reshape(N_EXPERTS, 1)

    pending = None
    for layer in range(depth):
        sh1, sc1, gt1, sh2, sc2, gt2 = (mod[layer, :, k] for k in range(6))
        i = layer // 2
        if layer % 2 == 0:
            w = ev_w_in[i]
            nq = 3 * DN_HEADS * DN_DK
            nz = DN_HEADS * DN_DK
            nf = 3 * FOX_HEADS * FOX_DH
            o_a = nq + nz
            o_f = o_a + 2 * DN_HEADS
            o_ff = o_f + nf
            small_w = jnp.concatenate([w[:, o_a:o_f], w[:, o_ff:o_ff + FOX_HEADS]], axis=1)
            small_w = jnp.pad(small_w, ((0, 0), (0, LANES - small_w.shape[1])))
            w_all = jnp.concatenate([w[:, :o_a], w[:, o_f:o_ff], small_w], axis=1).astype(BF16)
            splits = ((0, nq), (nq, o_a), (o_a, o_a + nf), (o_a + nf, o_a + nf + LANES))
            res = _inproj(x, norm1_g[layer], sc1, sh1, w_all, splits,
                          (BF16, BF16, BF16, F32), (None,) * 4, tm, pending, ev_conv_w[i])
            if pending is not None:
                x, res = res[0], res[1:]
            dn, z, fox, small = res
            small_t = jnp.swapaxes(small[:, :, :16], 1, 2)
            zeros4 = jnp.zeros((4,), F32)
            mul = jnp.concatenate([-jnp.exp(ev_dn_a_log[i]), zeros4, -jnp.ones((4,), F32), zeros4])
            bias = jnp.concatenate([ev_dn_dt_bias[i], zeros4, ev_fox_f_bias[i], zeros4])
            pr = jnp.stack([mul, bias], axis=1)
            pc = jnp.pad(jnp.stack([mul, bias], axis=0), ((0, 0), (0, LANES - 16)))
            col, row = _gates(small, small_t, pc, pr)
            row_dn = row[:, :DN_HEADS].reshape(b, DN_HEADS, s // DN_GROUP, DN_GROUP).transpose(0, 2, 1, 3)
            o_dn = _deltanet(dn, z, col, row_dn, ev_dn_onorm_g[i])
            o_fox = _fox(fox, col, ev_fox_qnorm_g[i], ev_fox_knorm_g[i], min(FOX_QUERY_BLOCK, s))
            mix_a, mix_b, w_out = o_dn, o_fox, ev_w_out[i]
        else:
            lw = od_lru_wr.shape[-1] * LRU_BLOCKS
            splits = ((0, lw), (lw, 2 * lw), (2 * lw, 2 * lw + od_sgu_w.shape[-1] * SGU_GROUPS),
                      (2 * lw + od_sgu_w.shape[-1] * SGU_GROUPS, od_w_in.shape[-1]))
            res = _inproj(x, norm1_g[layer], sc1, sh1, od_w_in[i], splits,
                          (BF16,) * 3, (None, "gelu"), tm, pending,
                          sgu=(od_sgu_norm_g[i], od_sgu_w[i], od_sgu_b[i]))
            if pending is not None:
                x, res = res[0], res[1:]
            lx, lg, o_sgu = res
            o_lru = _lru(lx, lg, od_conv_w[i], od_conv_b[i], od_lru_wr[i], od_lru_br[i], od_lru_wi[i],
                         od_lru_bi[i], od_lru_lambda[i])
            mix_a, mix_b, w_out = o_lru, o_sgu, od_w_out[i]
        x1, rows, route, counts = _outproj(mix_a, mix_b, x, w_out, gt1, norm2_g[layer], sc2, sh2,
                                           rw, rb, tm)
        x = x1
        pending = (_moe(rows, route, counts, moe_w_gate, moe_w_up, moe_w_down, layer), gt2)
    return _residual(x, *pending, tm)
```

```python
import functools

import jax
import jax.numpy as jnp
from jax import lax
from jax.experimental import pallas as pl
from jax.experimental.pallas import tpu as pltpu
from jax.experimental.pallas import tpu_sc as plsc

F32 = jnp.float32
BF16 = jnp.bfloat16
EPS = 1e-6
NEG_INF = float("-inf")

DN_HEADS = 4
DN_DK = 128
DN_CHUNK = 64
CONV_WIDTH = 4
FOX_HEADS = 4
FOX_DH = 128
LRU_BLOCKS = 4
LRU_C = 8.0
SGU_GROUPS = 4
SGU_CHUNK = 128
N_EXPERTS = 16
EXPERTS_PER_GROUP = 4
LANES = 128
SUBLANES = 8

VMEM_LIMIT = 48 * 1024 * 1024
TOKEN_BLOCK = 512
FOX_QUERY_BLOCK = 512


def _cparams(sem):
    return pltpu.CompilerParams(dimension_semantics=sem, vmem_limit_bytes=VMEM_LIMIT)


def _dot(a, b):
    return jnp.dot(a, b, preferred_element_type=F32)


def _dot_nt(a, b):
    return lax.dot_general(a, b, (((1,), (1,)), ((), ())), preferred_element_type=F32)


def _dot_tn(a, b):
    return lax.dot_general(a, b, (((0,), (0,)), ((), ())), preferred_element_type=F32)


def _dot_ones(a, b):
    if a.dtype == BF16:
        return sum(_dot(a, t.astype(BF16)) for t in _split3(b))
    return sum(_dot(t.astype(BF16), b) for t in _split3(a))


def _sigmoid(x):
    return 0.5 + 0.5 * jnp.tanh(0.5 * x)


def _silu(x):
    hx = 0.5 * x
    return hx + hx * jnp.tanh(hx)


def _softplus(x):
    return jnp.maximum(x, 0.0) + jnp.log(1.0 + jnp.exp(-jnp.abs(x)))


def _gelu_tanh(x):
    c = 0.7978845608028654
    return 0.5 * x * (1.0 + jnp.tanh(c * (x + 0.044715 * (x * x * x))))


def _prenorm(x, g, scale, shift):
    ms = jnp.mean(x * x, axis=-1, keepdims=True)
    return (x * lax.rsqrt(ms + EPS) * g) * (1.0 + scale) + shift


def _shift_rows(x, d, fill=0.0):
    rows = lax.broadcasted_iota(jnp.int32, x.shape, 0)
    return jnp.where(rows >= d, pltpu.roll(x, d, axis=0), fill)


def _causal_conv(x, w_ref):
    k_w = w_ref.shape[0]
    acc = x * w_ref[k_w - 1:k_w, :]
    for d in range(1, k_w):
        acc = acc + _shift_rows(x, d) * w_ref[k_w - 1 - d:k_w - d, :]
    return acc


def _adaln_kernel(c_ref, w_ref, b_ref, o_ref):
    c = c_ref[...]
    ca = _silu(c).astype(BF16)
    o_ref[0] = _dot(ca, w_ref[0].astype(BF16)) + b_ref[0]


def _adaln(c, ada_w, ada_b):
    depth, d, n = ada_w.shape
    b = c.shape[0]
    tn = 1536
    return pl.pallas_call(
        _adaln_kernel,
        grid=(depth, n // tn),
        in_specs=[pl.BlockSpec((b, d), lambda l, j: (0, 0)),
                  pl.BlockSpec((1, d, tn), lambda l, j: (l, 0, j)),
                  pl.BlockSpec((1, 1, tn), lambda l, j: (l, 0, j))],
        out_specs=pl.BlockSpec((1, b, tn), lambda l, j: (l, 0, j)),
        out_shape=jax.ShapeDtypeStruct((depth, b, n), F32),
        compiler_params=_cparams(("arbitrary", "arbitrary")),
        name="adaln",
    )(c, ada_w, ada_b.reshape(depth, 1, n))


CONV_HALO = 8
CONV_SLAB = 512


def _inproj_kernel(*refs, col_splits, acts, pending, conv, sgu, cast_w):
    refs = list(refs)
    x_ref = refs.pop(0)
    if pending:
        m_ref, gtm_ref = refs.pop(0), refs.pop(0)
    g_ref, sc_ref, sh_ref, w_ref = (refs.pop(0) for _ in range(4))
    if cast_w:
        w_f32, w_ref = w_ref, refs.pop()

        @pl.when((pl.program_id(0) == 0) & (pl.program_id(1) == 0))
        def _():
            if sum(s1 - s0 for s0, s1, _ in cast_w) < w_ref.shape[1]:
                w_ref[...] = jnp.zeros_like(w_ref)
            for s0, s1, d0 in cast_w:
                w_ref[:, d0:d0 + s1 - s0] = w_f32[:, s0:s1].astype(BF16)
    if conv:
        cw_ref = refs.pop(0)
        halo_sc = refs.pop()
    if sgu:
        sg_ref, sw_ref, sbt_ref = (refs.pop(0) for _ in range(3))
    if pending:
        xo_ref = refs.pop(0)
        x = x_ref[0] + gtm_ref[0] * _unpack_halves(m_ref[0]).astype(F32)
        xo_ref[0] = x
    else:
        x = x_ref[0]
    out_refs = refs
    h = _prenorm(x, g_ref[...], sc_ref[0], sh_ref[0]).astype(BF16)
    tm = x.shape[0]
    if conv:
        @pl.when(pl.program_id(1) == 0)
        def _():
            halo_sc[0:CONV_HALO, :] = jnp.zeros((CONV_HALO, halo_sc.shape[1]), F32)

        c0, c1 = col_splits[0]
        for s0 in range(c0, c1, CONV_SLAB):
            halo_sc[CONV_HALO:, s0 - c0:s0 - c0 + CONV_SLAB] = _dot(h, w_ref[:, s0:s0 + CONV_SLAB])
    n_plain = len(col_splits) - (2 if sgu else 0)
    for k, (o_ref, (c0, c1), act) in enumerate(zip(out_refs[:n_plain], col_splits[:n_plain], acts)):
        if conv and k == 0:
            continue
        p = _dot(h, w_ref[:, c0:c1])
        if act == "gelu":
            p = _gelu_tanh(p)
        o_ref[0] = p.astype(o_ref.dtype)
    if sgu:
        (u0, u1), (v0, v1) = col_splits[n_plain:]
        o_ref = out_refs[n_plain]
        u = _gelu_tanh(_dot(h, w_ref[:, u0:u1]))
        v = _gelu_tanh(_dot(h, w_ref[:, v0:v1]))
        vn = (v * lax.rsqrt(jnp.mean(v * v, axis=-1, keepdims=True) + EPS) * sg_ref[...]).astype(BF16)
        c = SGU_CHUNK
        gw = (u1 - u0) // SGU_GROUPS
        ri = lax.broadcasted_iota(jnp.int32, (c, c), 0)
        ci = lax.broadcasted_iota(jnp.int32, (c, c), 1)
        for g in range(SGU_GROUPS):
            wg = jnp.where(ri >= ci, sw_ref[g], 0.0).astype(BF16)
            bcol = sbt_ref[:, g:g + 1]
            for n in range(tm // c):
                rows = slice(n * c, (n + 1) * c)
                cols = slice(g * gw, (g + 1) * gw)
                mixed = _dot(wg, vn[rows, cols]) + bcol
                o_ref[0, rows, cols] = (u[rows, cols] * mixed).astype(o_ref.dtype)
    if conv:
        o_ref = out_refs[0]
        for t0 in range(0, halo_sc.shape[1], DN_DK):
            cols = slice(t0, t0 + DN_DK)
            acc = halo_sc[CONV_HALO:, cols] * cw_ref[CONV_WIDTH - 1:CONV_WIDTH, cols]
            for dd in range(1, CONV_WIDTH):
                acc = acc + halo_sc[pl.ds(CONV_HALO - dd, tm), cols] * cw_ref[CONV_WIDTH - 1 - dd:CONV_WIDTH - dd, cols]
            halo_sc[0:CONV_HALO, cols] = halo_sc[tm:tm + CONV_HALO, cols]
            yj = _silu(acc)
            if t0 < 2 * DN_HEADS * DN_DK:
                yj = yj * lax.rsqrt(jnp.sum(yj * yj, axis=-1, keepdims=True) + EPS)
            if t0 < DN_HEADS * DN_DK:
                yj = yj * (DN_DK ** -0.5)
            o_ref[0, :, cols] = yj.astype(o_ref.dtype)


def _inproj(x, g, scale, shift, w, col_splits, out_dtypes, acts, tm, pending=None, conv_w=None, sgu=None,
            w_cols=None):
    b, s, d = x.shape
    n = w.shape[1] if w_cols is None else col_splits[-1][1]
    if w.dtype != BF16 and w_cols is None:
        w_cols = ((0, n, 0),)
    out_cols = list(col_splits) if sgu is None else list(col_splits[:-2]) + [col_splits[-2]]
    outs = tuple(jax.ShapeDtypeStruct((b, s, c1 - c0), dt) for (c0, c1), dt in zip(out_cols, out_dtypes))
    row = lambda i, j: (i, 0, 0)
    tok = lambda i, j: (i, j, 0)
    const = lambda i, j: (0, 0)
    in_specs = [pl.BlockSpec((1, tm, d), tok)]
    out_specs = tuple(pl.BlockSpec((1, tm, c1 - c0), tok) for (c0, c1) in out_cols)
    args = (x,)
    if pending is not None:
        in_specs += [pl.BlockSpec((1, tm, d // 2), tok), pl.BlockSpec((1, 1, d), row)]
        out_specs = (pl.BlockSpec((1, tm, d), tok),) + out_specs
        outs = (jax.ShapeDtypeStruct((b, s, d), F32),) + outs
        args += tuple(pending)
    in_specs += [pl.BlockSpec((1, d), const), pl.BlockSpec((1, 1, d), row), pl.BlockSpec((1, 1, d), row),
                 pl.BlockSpec(w.shape, const, pipeline_mode=pl.Buffered(1)) if w_cols else pl.BlockSpec((d, n), const)]
    args += (g.reshape(1, d), scale, shift, w)
    scratch = []
    if conv_w is not None:
        in_specs.append(pl.BlockSpec(conv_w.shape, const))
        args += (conv_w,)
        scratch.append(pltpu.VMEM((CONV_HALO + tm, conv_w.shape[1]), F32))
    if w_cols:
        scratch.append(pltpu.VMEM((d, n), BF16))
    if sgu is not None:
        g_norm, w_s, b_s = sgu
        in_specs += [pl.BlockSpec((1, g_norm.shape[0]), const), pl.BlockSpec(w_s.shape, lambda i, j: (0, 0, 0)),
                     pl.BlockSpec((b_s.shape[1], b_s.shape[0]), const)]
        args += (g_norm.reshape(1, -1), w_s, b_s.T)
    return pl.pallas_call(
        functools.partial(_inproj_kernel, col_splits=col_splits, acts=acts, pending=pending is not None,
                          conv=conv_w is not None, sgu=sgu is not None, cast_w=w_cols),
        grid=(b, s // tm),
        in_specs=in_specs,
        out_specs=out_specs,
        out_shape=outs,
        scratch_shapes=scratch,
        compiler_params=_cparams(("arbitrary", "arbitrary")),
        name="inproj",
    )(*args)


def _gates_kernel(sm_ref, smt_ref, pc_ref, pr_ref, col_ref, row_ref):
    s = sm_ref.shape[1]
    blk = LANES
    ri = lax.broadcasted_iota(jnp.int32, (blk, blk), 0)
    ci = lax.broadcasted_iota(jnp.int32, (blk, blk), 1)
    same_chunk = (ri // DN_CHUNK) == (ci // DN_CHUNK)
    tril = jnp.where(ri >= ci, 1.0, 0.0).astype(BF16)
    tril_loc = jnp.where(same_chunk & (ri >= ci), 1.0, 0.0).astype(BF16)
    triu_loc = jnp.where(same_chunk & (ri <= ci), 1.0, 0.0).astype(BF16)
    lane = lax.broadcasted_iota(jnp.int32, (blk, LANES), 1)
    carry = jnp.zeros((1, LANES), F32)
    for j in range(s // blk):
        rows = slice(j * blk, (j + 1) * blk)
        xc = sm_ref[0, rows, :] + pc_ref[1:2, :]
        dec = pc_ref[0:1, :] * _softplus(jnp.where(lane < 4, xc, -xc))
        cum_glb = _dot_ones(tril, dec) + carry
        col_ref[0, rows, :] = jnp.where(lane < 4, _dot_ones(tril_loc, dec), jnp.where(lane < 8, _sigmoid(xc), cum_glb))
        carry = cum_glb[blk - 1:blk, :]
        decr = pr_ref[:, 0:1] * _softplus(smt_ref[0, :, rows] + pr_ref[:, 1:2])
        row_ref[0, :, rows] = _dot_ones(decr, triu_loc)


def _gates(small, small_t, pc, pr):
    b, s, _ = small.shape
    return pl.pallas_call(
        _gates_kernel,
        grid=(b,),
        in_specs=[pl.BlockSpec((1, s, LANES), lambda i: (i, 0, 0)),
                  pl.BlockSpec((1, 16, s), lambda i: (i, 0, 0)),
                  pl.BlockSpec((2, LANES), lambda i: (0, 0)),
                  pl.BlockSpec((16, 2), lambda i: (0, 0))],
        out_specs=(pl.BlockSpec((1, s, LANES), lambda i: (i, 0, 0)),
                   pl.BlockSpec((1, 16, s), lambda i: (i, 0, 0))),
        out_shape=(jax.ShapeDtypeStruct((b, s, LANES), F32), jax.ShapeDtypeStruct((b, 16, s), F32)),
        compiler_params=_cparams(("arbitrary",)),
        name="gates",
    )(small, small_t, pc, pr)


DN_PACK = 4
DN_GROUP = DN_PACK * DN_CHUNK
DN_ITER_GROUPS = 2


def _blockdiag(p):
    c, wide = p.shape
    t = jnp.concatenate([p] * (wide // c), axis=0)
    rb = lax.broadcasted_iota(jnp.int32, (wide, wide), 0) // c
    cb = lax.broadcasted_iota(jnp.int32, (wide, wide), 1) // c
    return jnp.where(rb == cb, t, 0.0).astype(BF16)


def _diag_blocks(m, c):
    wide = m.shape[1]
    cb = lax.broadcasted_iota(jnp.int32, (c, wide), 1) // c
    out = m[:c]
    for j in range(1, wide // c):
        out = jnp.where(cb == j, m[j * c:(j + 1) * c], out)
    return out


def _rows_to_blocks(col, c, wide):
    cb = lax.broadcasted_iota(jnp.int32, (c, wide), 1) // c
    out = jnp.broadcast_to(col[:c], (c, wide))
    for j in range(1, wide // c):
        out = jnp.where(cb == j, col[j * c:(j + 1) * c], out)
    return out


def _packed_unit_lower_inverse(lows):
    c, wide = lows[0].shape
    ri = lax.broadcasted_iota(jnp.int32, (c, wide), 0)
    ci = lax.broadcasted_iota(jnp.int32, (c, wide), 1) % c
    eye = jnp.where(ri == ci, 1.0, 0.0)
    n = range(len(lows))
    ds = [jnp.where((ri // 16) == (ci // 16), lo, 0.0) for lo in lows]
    xs = [eye - d for d in ds]
    ps = ds
    bds = [_blockdiag(p) for p in ps]
    for _ in range(3):
        ps = [_dot(ps[i].astype(BF16), bds[i]) for i in n]
        bds = [_blockdiag(p) for p in ps]
        xs = [xs[i] + _dot(xs[i].astype(BF16), bds[i]) for i in n]
    for width in (16, 32):
        sel = ((ri // (2 * width)) == (ci // (2 * width))) & ((ri // width) != (ci // width))
        offs = [_blockdiag(jnp.where(sel, lo, 0.0)) for lo in lows]
        ts = [_dot(xs[i].astype(BF16), offs[i]) for i in n]
        bdx = [_blockdiag(x) for x in xs]
        xs = [xs[i] - _dot(ts[i].astype(BF16), bdx[i]) for i in n]
    return xs


def _deltanet_kernel(dn_ref, z_ref, col_ref, row_ref, og_ref, o_ref, st_sc):
    s = dn_ref.shape[1]
    c = DN_CHUNK
    gt = DN_GROUP
    dk = DN_DK
    hd = DN_HEADS * dk
    heads = range(DN_HEADS)
    st_sc[...] = jnp.zeros_like(st_sc)

    ri = lax.broadcasted_iota(jnp.int32, (c, gt), 0)
    ci = lax.broadcasted_iota(jnp.int32, (c, gt), 1) % c
    og = og_ref[...]

    per_iter = DN_ITER_GROUPS if (s // gt) % DN_ITER_GROUPS == 0 else 1
    insts = [(gg, h) for gg in range(per_iter) for h in heads]
    ins = range(len(insts))

    def group(g, carry):
        r0 = [pl.multiple_of((g * per_iter + gg) * gt, gt) for gg in range(per_iter)]
        colg = [col_ref[0, pl.ds(r0[gg], gt), :] for gg in range(per_iter)]
        rowg = [row_ref[0, g * per_iter + gg] for gg in range(per_iter)]
        hs = [slice(h * dk, (h + 1) * dk) for h in heads]
        q = [dn_ref[0, pl.ds(r0[gg], gt), h * dk:(h + 1) * dk].astype(F32) for gg, h in insts]
        k = [dn_ref[0, pl.ds(r0[gg], gt), hd + h * dk:hd + (h + 1) * dk].astype(F32) for gg, h in insts]
        v = [dn_ref[0, pl.ds(r0[gg], gt), 2 * hd + h * dk:2 * hd + (h + 1) * dk].astype(F32) for gg, h in insts]
        gc = [colg[gg][:, h:h + 1] for gg, h in insts]
        beta = [colg[gg][:, DN_HEADS + h:DN_HEADS + h + 1] for gg, h in insts]
        grow = [rowg[gg][h:h + 1, :] for gg, h in insts]
        glast = [jnp.concatenate([jnp.broadcast_to(gc[i][(j + 1) * c - 1:(j + 1) * c], (c, 1))
                                  for j in range(DN_PACK)], axis=0) for i in ins]
        eg = [jnp.exp(gc[i]) for i in ins]
        kbf = [k[i].astype(BF16) for i in ins]
        both = [_dot_nt(jnp.concatenate([(k[i] * beta[i]).astype(BF16), q[i].astype(BF16)], axis=0), kbf[i])
                for i in ins]
        decay = [jnp.exp(jnp.where(ri >= ci, _rows_to_blocks(gc[i], c, gt) - grow[i], NEG_INF)) for i in ins]
        kk = [_diag_blocks(both[i][:gt], c) * decay[i] for i in ins]
        qk = [_blockdiag(_diag_blocks(both[i][gt:], c) * decay[i]) for i in ins]
        t_inv = _packed_unit_lower_inverse([jnp.where(ri > ci, kk[i], 0.0) for i in ins])
        rhs = [jnp.concatenate([k[i] * (beta[i] * eg[i]), v[i] * beta[i]], axis=1).astype(BF16) for i in ins]
        wu = [_dot(_blockdiag(t_inv[i]), rhs[i]).astype(BF16) for i in ins]
        qwu = [_dot(qk[i], wu[i]) for i in ins]
        qp = [(q[i] * eg[i] - qwu[i][:, :dk]).astype(BF16) for i in ins]
        kdec = [(k[i] * jnp.exp(glast[i] - gc[i])).astype(BF16) for i in ins]
        mb = [[_dot_tn(kdec[i][j * c:(j + 1) * c], wu[i][j * c:(j + 1) * c]) for i in ins]
              for j in range(DN_PACK)]
        for gg in range(per_iter):
            mine = [gg * DN_HEADS + h for h in heads]
            outs = [[] for _ in heads]
            for j in range(DN_PACK):
                rows = slice(j * c, (j + 1) * c)
                state = [st_sc[h] for h in heads]
                lhs = [jnp.concatenate([qp[i][rows], mb[j][i][:, :dk].astype(BF16)], axis=0) for i in mine]
                r = [_dot(lhs[h], state[h].astype(BF16)) for h in heads]
                for h, i in zip(heads, mine):
                    gl = jnp.exp(glast[i][j * c:j * c + 1])
                    st_sc[h] = state[h] * gl - r[h][c:] + mb[j][i][:, dk:]
                    outs[h].append(r[h][:c] + qwu[i][rows, dk:])
            for h in heads:
                o = jnp.concatenate(outs[h], axis=0)
                on = o * lax.rsqrt(jnp.mean(o * o, axis=-1, keepdims=True) + EPS) * og
                zz = z_ref[0, pl.ds(r0[gg], gt), hs[h]].astype(F32)
                o_ref[0, pl.ds(r0[gg], gt), hs[h]] = (on * _silu(zz)).astype(o_ref.dtype)
        return carry

    lax.fori_loop(0, s // (gt * per_iter), group, 0)


def _deltanet(dn, z, col, row, onorm_g):
    b, s, w3 = dn.shape
    hd = DN_HEADS * DN_DK
    return pl.pallas_call(
        _deltanet_kernel,
        grid=(b,),
        in_specs=[pl.BlockSpec((1, s, w3), lambda i: (i, 0, 0)),
                  pl.BlockSpec((1, s, hd), lambda i: (i, 0, 0)),
                  pl.BlockSpec((1, s, LANES), lambda i: (i, 0, 0)),
                  pl.BlockSpec((1, s // DN_GROUP, DN_HEADS, DN_GROUP), lambda i: (i, 0, 0, 0)),
                  pl.BlockSpec((1, DN_DK), lambda i: (0, 0))],
        out_specs=pl.BlockSpec((1, s, hd), lambda i: (i, 0, 0)),
        out_shape=jax.ShapeDtypeStruct((b, s, hd), BF16),
        scratch_shapes=[pltpu.VMEM((DN_HEADS, DN_DK, DN_DK), F32)],
        compiler_params=_cparams(("arbitrary",)),
        name="deltanet",
    )(dn, z, col, row, onorm_g.reshape(1, DN_DK))


def _split3(x):
    hi = x.astype(BF16).astype(F32)
    r = x - hi
    mid = r.astype(BF16).astype(F32)
    return hi, mid, r - mid


def _fox_kernel(q_ref, k_ref, v_ref, colq_ref, colk_ref, qg_ref, kg_ref, o_ref, ka_sc, va_sc, m_sc, acc_sc, *, tq):
    qi = pl.program_id(1)
    s = k_ref.shape[1]
    dh = FOX_DH

    def bias_lanes(col):
        lane = lax.broadcasted_iota(jnp.int32, col.shape, 1)
        hi, mid, lo = _split3(col)
        return jnp.where((lane >= 8) & (lane < 8 + FOX_HEADS), hi,
                         jnp.where((lane >= 24) & (lane < 24 + FOX_HEADS), pltpu.roll(mid, 16, axis=1),
                                   jnp.where((lane >= 40) & (lane < 40 + FOX_HEADS), pltpu.roll(lo, 32, axis=1), 0.0)))

    @pl.when(qi == 0)
    def _():
        lane = lax.broadcasted_iota(jnp.int32, (s, LANES), 1)
        ones_col = jnp.where(lane == 0, 1.0, 0.0).astype(BF16)
        ones_k = (lane >= 56) & (lane < 96) & ((lane % 16) >= 8) & ((lane % 16) < 8 + FOX_HEADS)
        ext_k = (jnp.where(ones_k, 1.0, 0.0) - bias_lanes(colk_ref[0])).astype(BF16)
        for h in range(FOX_HEADS):
            cols = slice(h * dh, (h + 1) * dh)
            kf = k_ref[0, :, cols].astype(F32)
            kn = kf * lax.rsqrt(jnp.mean(kf * kf, axis=-1, keepdims=True) + EPS) * kg_ref[...]
            ka_sc[h, :, :dh] = kn.astype(BF16)
            ka_sc[h, :, dh:] = ext_k
            va_sc[h, :, :dh] = v_ref[0, :, cols]
            va_sc[h, :, dh:] = ones_col

    lane = lax.broadcasted_iota(jnp.int32, (tq, LANES), 1)
    cq = pltpu.roll(bias_lanes(colq_ref[0]), 48, axis=1)
    qa = []
    for h in range(FOX_HEADS):
        cols = slice(h * dh, (h + 1) * dh)
        qf = q_ref[0, :, cols].astype(F32)
        qn = qf * lax.rsqrt(jnp.mean(qf * qf, axis=-1, keepdims=True) + EPS) * qg_ref[...] * (dh ** -0.5)
        mine = ((lane % 16) == 8 + h) & (lane < 96)
        ext = jnp.where(mine, jnp.where(lane < 48, 1.0, cq), 0.0)
        qa.append(jnp.concatenate([qn.astype(BF16), ext.astype(BF16)], axis=1))
    m_sc[...] = jnp.full(m_sc.shape, NEG_INF, F32)
    acc_sc[...] = jnp.zeros_like(acc_sc)
    causal = lax.broadcasted_iota(jnp.int32, (tq, tq), 0) >= lax.broadcasted_iota(jnp.int32, (tq, tq), 1)

    def step(k0, masked):
        heads = range(FOX_HEADS)
        logits = [_dot_nt(qa[h], ka_sc[h, pl.ds(k0, tq), :]) for h in heads]
        if masked:
            logits = [jnp.where(causal, lg, NEG_INF) for lg in logits]
        ps, alphas = [], []
        for h in heads:
            m_old = m_sc[h]
            m_new = jnp.maximum(m_old, jnp.max(logits[h], axis=1, keepdims=True))
            m_sc[h] = m_new
            alphas.append(jnp.exp(m_old - m_new))
            ps.append(jnp.exp(logits[h] - jnp.concatenate([m_new] * (tq // LANES), axis=1)).astype(BF16))
        for h in heads:
            pv = _dot(ps[h], va_sc[h, pl.ds(k0, tq), :])
            acc_sc[h] = acc_sc[h] * jnp.concatenate([alphas[h], alphas[h]], axis=1) + pv

    def body(j, carry):
        step(pl.multiple_of(j * tq, tq), False)
        return carry

    lax.fori_loop(0, qi, body, 0)
    step(pl.multiple_of(qi * tq, tq), True)
    for h in range(FOX_HEADS):
        acc = acc_sc[h]
        o_ref[0, :, h * dh:(h + 1) * dh] = (acc[:, :dh] / acc[:, dh:dh + 1]).astype(o_ref.dtype)


def _fox(fox, col, qg, kg, tq):
    b, s, _ = fox.shape
    hd = FOX_HEADS * FOX_DH
    return pl.pallas_call(
        functools.partial(_fox_kernel, tq=tq),
        grid=(b, s // tq),
        in_specs=[pl.BlockSpec((1, tq, hd), lambda i, j: (i, j, 0)),
                  pl.BlockSpec((1, s, hd), lambda i, j: (i, 0, 1)),
                  pl.BlockSpec((1, s, hd), lambda i, j: (i, 0, 2)),
                  pl.BlockSpec((1, tq, LANES), lambda i, j: (i, j, 0)),
                  pl.BlockSpec((1, s, LANES), lambda i, j: (i, 0, 0)),
                  pl.BlockSpec((1, FOX_DH), lambda i, j: (0, 0)),
                  pl.BlockSpec((1, FOX_DH), lambda i, j: (0, 0))],
        out_specs=pl.BlockSpec((1, tq, hd), lambda i, j: (i, j, 0)),
        out_shape=jax.ShapeDtypeStruct((b, s, hd), BF16),
        scratch_shapes=[pltpu.VMEM((FOX_HEADS, s, 2 * FOX_DH), BF16), pltpu.VMEM((FOX_HEADS, s, 2 * FOX_DH), BF16),
                        pltpu.VMEM((FOX_HEADS, tq, LANES), F32), pltpu.VMEM((FOX_HEADS, tq, 2 * FOX_DH), F32)],
        compiler_params=_cparams(("arbitrary", "arbitrary")),
        name="fox",
    )(fox, fox, fox, col, col, qg.reshape(1, FOX_DH), kg.reshape(1, FOX_DH))


def _lru_kernel(x_ref, gate_ref, cw_ref, cb_ref, wr_ref, br_ref, wi_ref, bi_ref, lam_ref, o_ref, a_sc, b_sc):
    s = x_ref.shape[1]
    x = _causal_conv(x_ref[0].astype(F32), cw_ref) + cb_ref[...]
    xb = x.astype(BF16)
    r = _sigmoid(_dot(xb, wr_ref[0].astype(BF16)) + br_ref[...])
    i = _sigmoid(_dot(xb, wi_ref[0].astype(BF16)) + bi_ref[...])
    log_a = (-LRU_C) * r * _softplus(-lam_ref[...])
    a = jnp.exp(log_a)
    bb = jnp.sqrt(1.0 - a * a) * (i * x)
    nt = s // SUBLANES
    a3 = a.reshape(nt, SUBLANES, a.shape[1])
    b3 = bb.reshape(nt, SUBLANES, a.shape[1])
    sub = lax.broadcasted_iota(jnp.int32, a3.shape, 1)
    d = 1
    while d < SUBLANES:
        keep = sub >= d
        b3 = a3 * jnp.where(keep, pltpu.roll(b3, d, axis=1), 0.0) + b3
        a3 = a3 * jnp.where(keep, pltpu.roll(a3, d, axis=1), 1.0)
        d *= 2
    a_sc[...] = a3.reshape(s, a.shape[1])
    b_sc[...] = b3.reshape(s, a.shape[1])
    at = a_sc[pl.ds(SUBLANES - 1, nt, stride=SUBLANES), :]
    bt = b_sc[pl.ds(SUBLANES - 1, nt, stride=SUBLANES), :]
    d = 1
    while d < nt:
        bt = at * _shift_rows(bt, d, 0.0) + bt
        at = at * _shift_rows(at, d, 1.0)
        d *= 2
    h_prev = _shift_rows(bt, 1, 0.0)
    bb = (a3 * h_prev[:, None, :] + b3).reshape(s, a.shape[1])
    o_ref[0] = (bb * gate_ref[0].astype(F32)).astype(o_ref.dtype)


def _lru(lx, lg, conv_w, conv_b, wr, br, wi, bi, lam):
    b, s, wd = lx.shape
    blk = wd // LRU_BLOCKS
    vec = lambda i, j: (0, j)
    return pl.pallas_call(
        _lru_kernel,
        grid=(b, LRU_BLOCKS),
        in_specs=[pl.BlockSpec((1, s, blk), lambda i, j: (i, 0, j)),
                  pl.BlockSpec((1, s, blk), lambda i, j: (i, 0, j)),
                  pl.BlockSpec((CONV_WIDTH, blk), vec),
                  pl.BlockSpec((1, blk), vec),
                  pl.BlockSpec((1, blk, blk), lambda i, j: (j, 0, 0)),
                  pl.BlockSpec((1, blk), vec),
                  pl.BlockSpec((1, blk, blk), lambda i, j: (j, 0, 0)),
                  pl.BlockSpec((1, blk), vec),
                  pl.BlockSpec((1, blk), vec)],
        out_specs=pl.BlockSpec((1, s, blk), lambda i, j: (i, 0, j)),
        out_shape=jax.ShapeDtypeStruct((b, s, wd), BF16),
        scratch_shapes=[pltpu.VMEM((s, blk), F32), pltpu.VMEM((s, blk), F32)],
        compiler_params=_cparams(("arbitrary", "arbitrary")),
        name="rglru",
    )(lx, lg, conv_w, conv_b.reshape(1, wd), wr, br.reshape(1, wd), wi, bi.reshape(1, wd), lam.reshape(1, wd))


def _routing(logits, rb):
    ne, tm = logits.shape
    row = lax.broadcasted_iota(jnp.int32, (ne, tm), 0)
    row_f = row.astype(F32)
    ex = jnp.exp(logits - jnp.max(logits, axis=0, keepdims=True))
    probs = ex / jnp.sum(ex, axis=0, keepdims=True)
    sel = probs + rb

    def top2(vals, idx):
        m1 = jnp.max(vals, axis=0, keepdims=True)
        i1 = jnp.min(jnp.where(vals == m1, idx, float(ne)), axis=0, keepdims=True)
        rest = jnp.where(idx == i1, NEG_INF, vals)
        m2 = jnp.max(rest, axis=0, keepdims=True)
        return m1, i1, m2, rest

    best = None
    g_idx = None
    grp = row // EXPERTS_PER_GROUP
    for g in range(ne // EXPERTS_PER_GROUP):
        m1, _, m2, _ = top2(jnp.where(grp == g, sel, NEG_INF), row_f)
        score = m1 + m2
        if g == 0:
            best, g_idx = score, jnp.zeros((1, tm), jnp.int32)
        else:
            upd = score > best
            best = jnp.where(upd, score, best)
            g_idx = jnp.where(upd, g, g_idx)
    _, i1, m2, rest = top2(jnp.where(grp == g_idx, sel, NEG_INF), row_f)
    i2 = jnp.min(jnp.where(rest == m2, row_f, float(ne)), axis=0, keepdims=True)
    p1 = jnp.sum(jnp.where(row_f == i1, probs, 0.0), axis=0, keepdims=True)
    p2 = jnp.sum(jnp.where(row_f == i2, probs, 0.0), axis=0, keepdims=True)
    den = p1 + p2
    first_lo = i1 < i2
    a = jnp.where(first_lo, i1, i2) - EXPERTS_PER_GROUP * g_idx.astype(F32)
    bhi = jnp.where(first_lo, i2, i1) - EXPERTS_PER_GROUP * g_idx.astype(F32)
    pair = a * (7.0 - a) * 0.5 + (bhi - a - 1.0)
    pair = jnp.where(a == 1.0, 7.0 - pair, pair)
    cls = PAIRS_PER_GROUP * g_idx.astype(F32) + pair
    w_lo = jnp.where(first_lo, p1, p2) / den
    w_hi = jnp.where(first_lo, p2, p1) / den
    swap = pair == 5.0
    return cls, jnp.where(swap, w_hi, w_lo), jnp.where(swap, w_lo, w_hi)


PAIRS_PER_GROUP = EXPERTS_PER_GROUP * (EXPERTS_PER_GROUP - 1) // 2
N_CLASSES = (N_EXPERTS // EXPERTS_PER_GROUP) * PAIRS_PER_GROUP
PAIR_WALK = [(0, 1), (0, 2), (0, 3), (1, 3), (1, 2), (3, 2)]
assert EXPERTS_PER_GROUP == 4 and len(PAIR_WALK) == PAIRS_PER_GROUP
CLASS_E0 = [EXPERTS_PER_GROUP * (c // PAIRS_PER_GROUP) + PAIR_WALK[c % PAIRS_PER_GROUP][0] for c in range(N_CLASSES)]
CLASS_E1 = [EXPERTS_PER_GROUP * (c // PAIRS_PER_GROUP) + PAIR_WALK[c % PAIRS_PER_GROUP][1] for c in range(N_CLASSES)]
CLASS_ROWS = 32
ROW_FEAT = 512
ROW_WORDS = 640


def _pack_halves(x):
    kk = x.shape[1] // 2
    lo = lax.bitcast_convert_type(x[:, :kk].astype(BF16).astype(F32), jnp.uint32)
    hi = lax.bitcast_convert_type(x[:, kk:].astype(BF16).astype(F32), jnp.uint32)
    return lax.bitcast_convert_type((lo >> 16) | hi, jnp.int32)


def _unpack_halves(p):
    u = lax.bitcast_convert_type(p, jnp.uint32)
    lo = lax.bitcast_convert_type(u << 16, F32)
    hi = lax.bitcast_convert_type(u & jnp.uint32(0xFFFF0000), F32)
    return jnp.concatenate([lo.astype(BF16), hi.astype(BF16)], axis=1)


def _outproj_kernel(a_ref, b_ref, x_ref, wf_ref, gt_ref, g2_ref, sc_ref, sh_ref, rwt_ref, rb_ref,
                    x1_ref, rows_ref, route_ref, cnt_ref, carry_sc, w_ref, tri_sc):
    first = (pl.program_id(0) == 0) & (pl.program_id(1) == 0)

    @pl.when(first)
    def _():
        carry_sc[...] = jnp.zeros_like(carry_sc)
        w_ref[...] = wf_ref[...].astype(BF16)
        earlier = (lax.broadcasted_iota(jnp.int32, tri_sc.shape, 0) < lax.broadcasted_iota(jnp.int32, tri_sc.shape, 1))
        tri_sc[...] = jnp.where(earlier, 1.0, 0.0).astype(BF16)

    half = a_ref.shape[2]
    tm = a_ref.shape[1]
    y = _dot(a_ref[0], w_ref[:half, :]) + _dot(b_ref[0], w_ref[half:, :])
    x1 = x_ref[0] + gt_ref[0] * y
    x1_ref[0] = x1
    h2 = _prenorm(x1, g2_ref[...], sc_ref[0], sh_ref[0])
    rows_ref[0, :, :ROW_FEAT] = _pack_halves(h2)
    cls, w_lo, w_hi = _routing(_dot_nt(rwt_ref[...], h2.astype(BF16)), rb_ref[...])
    r128 = lax.broadcasted_iota(jnp.int32, (LANES, tm), 0)
    wts = jnp.where(r128 == 0, w_lo, jnp.where(r128 == 1, w_hi, 0.0))
    rows_ref[0, :, ROW_FEAT:] = lax.bitcast_convert_type(wts.T[:, :ROW_WORDS - ROW_FEAT], jnp.int32)
    crow = lax.broadcasted_iota(jnp.int32, (carry_sc.shape[0], tm), 0).astype(F32)
    onehot = jnp.where(crow == cls, 1.0, 0.0)
    prefix = _dot(onehot.astype(BF16), tri_sc[...]) + carry_sc[:, 0:1]
    rank = jnp.sum(onehot * prefix, axis=0, keepdims=True)
    r8 = lax.broadcasted_iota(jnp.int32, (8, tm), 0)
    route_ref[0] = jnp.where(r8 == 0, cls, jnp.where(r8 == 1, rank, 0.0))
    carry_sc[...] = carry_sc[...] + jnp.sum(onehot, axis=1, keepdims=True)
    cnt_ref[...] = carry_sc[...]


def _outproj(a, bb, x, w, gt, g2, sc2, sh2, rw, rb, tm):
    b, s, d = x.shape
    half = a.shape[2]
    row = lambda i, j: (i, 0, 0)
    tok = lambda i, j: (i, j, 0)
    const = lambda i, j: (0, 0)
    return pl.pallas_call(
        _outproj_kernel,
        grid=(b, s // tm),
        in_specs=[pl.BlockSpec((1, tm, half), tok),
                  pl.BlockSpec((1, tm, half), tok),
                  pl.BlockSpec((1, tm, d), tok),
                  pl.BlockSpec((2 * half, d), const),
                  pl.BlockSpec((1, 1, d), row),
                  pl.BlockSpec((1, d), const),
                  pl.BlockSpec((1, 1, d), row),
                  pl.BlockSpec((1, 1, d), row),
                  pl.BlockSpec((N_EXPERTS, d), const),
                  pl.BlockSpec((N_EXPERTS, 1), const)],
        out_specs=(pl.BlockSpec((1, tm, d), tok),
                   pl.BlockSpec((1, tm, ROW_WORDS), tok),
                   pl.BlockSpec((1, 8, tm), lambda i, j: (i, 0, j)),
                   pl.BlockSpec((CLASS_ROWS, LANES), const)),
        out_shape=(jax.ShapeDtypeStruct((b, s, d), F32),
                   jax.ShapeDtypeStruct((b, s, ROW_WORDS), jnp.int32),
                   jax.ShapeDtypeStruct((b, 8, s), F32),
                   jax.ShapeDtypeStruct((CLASS_ROWS, LANES), F32)),
        scratch_shapes=[pltpu.VMEM((CLASS_ROWS, LANES), F32), pltpu.VMEM((2 * half, d), BF16),
                        pltpu.VMEM((tm, tm), BF16)],
        compiler_params=_cparams(("arbitrary", "arbitrary")),
        name="outproj_router",
    )(a, bb, x, w, gt, g2.reshape(1, d), sc2, sh2, rw, rb)


MOE_ROWS = 512
SC_CHUNK = 128


def _sc_workers():
    info = plsc.get_sparse_core_info()
    return info.num_cores, info.num_cores * info.num_subcores


def _sc_scatter_rows(rows, idx3, n_out):
    nw, k, ch = idx3.shape
    width = rows.shape[1]
    nc, _ = _sc_workers()
    mesh = plsc.VectorSubcoreMesh(core_axis_name="c", subcore_axis_name="s")

    @functools.partial(
        pl.kernel, mesh=mesh,
        out_type=jax.ShapeDtypeStruct((n_out, width), rows.dtype),
        scratch_types=[pltpu.VMEM((k, ch), jnp.int32), pltpu.VMEM((ch, width), rows.dtype), pltpu.SemaphoreType.DMA],
        name="moe_dispatch")
    def kern(rows_hbm, idx_hbm, out_hbm, idx_v, rows_v, sem):
        wid = lax.axis_index("s") * nc + lax.axis_index("c")
        pltpu.sync_copy(idx_hbm.at[wid], idx_v)

        @pl.loop(0, k)
        def _(j):
            pltpu.sync_copy(rows_hbm.at[pl.ds((wid * k + j) * ch, ch)], rows_v)
            pltpu.async_copy(rows_v, out_hbm.at[idx_v.at[j]], sem).wait()

    return kern(rows, idx3)


def _sc_gather_rows(table, idx3):
    nw, k, ch = idx3.shape
    width = table.shape[1]
    nc, _ = _sc_workers()
    mesh = plsc.VectorSubcoreMesh(core_axis_name="c", subcore_axis_name="s")

    @functools.partial(
        pl.kernel, mesh=mesh,
        out_type=jax.ShapeDtypeStruct((nw * k * ch, width), table.dtype),
        scratch_types=[pltpu.VMEM((k, ch), jnp.int32), pltpu.VMEM((ch, width), table.dtype), pltpu.SemaphoreType.DMA],
        name="moe_combine")
    def kern(table_hbm, idx_hbm, out_hbm, idx_v, rows_v, sem):
        wid = lax.axis_index("s") * nc + lax.axis_index("c")
        pltpu.sync_copy(idx_hbm.at[wid], idx_v)

        @pl.loop(0, k)
        def _(j):
            pltpu.async_copy(table_hbm.at[idx_v.at[j]], rows_v, sem).wait()
            pltpu.sync_copy(rows_v, out_hbm.at[pl.ds((wid * k + j) * ch, ch)])

    return kern(table, idx3)


def _experts_kernel(e0_ref, e1_ref, nused_ref, x_ref, wg0_ref, wu0_ref, wd0_ref, wg1_ref, wu1_ref, wd1_ref, y_ref):
    @pl.when(pl.program_id(0) < nused_ref[0])
    def _():
        blk = x_ref[...]
        feat = ROW_FEAT
        h = _unpack_halves(blk[:, :feat])
        wts = lax.bitcast_convert_type(blk[:, feat:], F32)
        y = None
        for col, (wg_ref, wu_ref, wd_ref) in enumerate(((wg0_ref, wu0_ref, wd0_ref), (wg1_ref, wu1_ref, wd1_ref))):
            act = (_silu(_dot(h, wg_ref[0, 0].astype(BF16))) * _dot(h, wu_ref[0, 0].astype(BF16))
                   * wts[:, col:col + 1])
            part = _dot(act.astype(BF16), wd_ref[0, 0].astype(BF16))
            y = part if y is None else y + part
        y_ref[...] = _pack_halves(y)


def _experts(xs, blk_e0, blk_e1, nused, wg, wu, wd, layer):
    n_rows = xs.shape[0]
    _, ne, d, f = wg.shape
    nblk = n_rows // MOE_ROWS
    rows = lambda i, e0, e1, nu: (jnp.minimum(i, nu[0] - 1), 0)
    lo = lambda i, e0, e1, nu: (layer, e0[i], 0, 0)
    hi = lambda i, e0, e1, nu: (layer, e1[i], 0, 0)
    return pl.pallas_call(
        _experts_kernel,
        grid_spec=pltpu.PrefetchScalarGridSpec(
            num_scalar_prefetch=3,
            grid=(nblk,),
            in_specs=[pl.BlockSpec((MOE_ROWS, ROW_WORDS), rows),
                      pl.BlockSpec((1, 1, d, f), lo), pl.BlockSpec((1, 1, d, f), lo), pl.BlockSpec((1, 1, f, d), lo),
                      pl.BlockSpec((1, 1, d, f), hi), pl.BlockSpec((1, 1, d, f), hi), pl.BlockSpec((1, 1, f, d), hi)],
            out_specs=pl.BlockSpec((MOE_ROWS, d // 2), lambda i, e0, e1, nu: (i, 0))),
        out_shape=jax.ShapeDtypeStruct((n_rows, d // 2), jnp.int32),
        compiler_params=_cparams(("arbitrary",)),
        name="moe_experts",
    )(blk_e0, blk_e1, nused, xs, wg, wu, wd, wg, wu, wd)


def _residual_kernel(x_ref, m_ref, gt_ref, o_ref):
    o_ref[0] = x_ref[0] + gt_ref[0] * _unpack_halves(m_ref[0]).astype(F32)


def _residual(x1, moe_p, gt, tm):
    b, s, d = x1.shape
    tok = lambda i, j: (i, j, 0)
    return pl.pallas_call(
        _residual_kernel,
        grid=(b, s // tm),
        in_specs=[pl.BlockSpec((1, tm, d), tok), pl.BlockSpec((1, tm, d // 2), tok),
                  pl.BlockSpec((1, 1, d), lambda i, j: (i, 0, 0))],
        out_specs=pl.BlockSpec((1, tm, d), tok),
        out_shape=jax.ShapeDtypeStruct((b, s, d), F32),
        compiler_params=_cparams(("arbitrary", "arbitrary")),
        name="moe_residual",
    )(x1, moe_p, gt)


def _moe(rows, route, counts, wg, wu, wd, layer):
    b, s, _ = rows.shape
    d = wg.shape[2]
    t = b * s
    _, nw = _sc_workers()
    cnt = counts[:N_CLASSES, 0].astype(jnp.int32)
    padded = ((cnt + MOE_ROWS - 1) // MOE_ROWS) * MOE_ROWS
    ends = jnp.cumsum(padded)
    n_rows = t + N_CLASSES * MOE_ROWS
    nblk = n_rows // MOE_ROWS
    cls = route[:, 0, :].reshape(t).astype(jnp.int32)
    rank = route[:, 1, :].reshape(t).astype(jnp.int32)
    pos = rank + jnp.sum(jnp.where(cls[:, None] > jnp.arange(N_CLASSES, dtype=jnp.int32)[None, :], padded[None, :], 0),
                         axis=1)
    idx3 = pos.reshape(nw, t // (nw * SC_CHUNK), SC_CHUNK)
    nused = (ends[-1] // MOE_ROWS).reshape(1)
    blk_cls = jnp.sum((jnp.arange(nblk, dtype=jnp.int32)[:, None] * MOE_ROWS) >= ends[None, :], axis=1)
    blk_cls = jnp.minimum(blk_cls, blk_cls[jnp.maximum(nused[0] - 1, 0)])
    blk_e0 = jnp.asarray(CLASS_E0, jnp.int32)[blk_cls]
    blk_e1 = jnp.asarray(CLASS_E1, jnp.int32)[blk_cls]
    xs = _sc_scatter_rows(rows.reshape(t, ROW_WORDS), idx3, n_rows)
    ys = _experts(xs, blk_e0, blk_e1, nused, wg, wu, wd, layer)
    return _sc_gather_rows(ys, idx3).reshape(b, s, d // 2)


def kernel(x, c, ada_w, ada_b, norm1_g, norm2_g, ev_w_in, ev_conv_w, ev_dn_a_log, ev_dn_dt_bias, ev_dn_onorm_g, ev_fox_f_bias, ev_fox_qnorm_g, ev_fox_knorm_g, ev_w_out, od_w_in, od_conv_w, od_conv_b, od_lru_wr, od_lru_br, od_lru_wi, od_lru_bi, od_lru_lambda, od_sgu_norm_g, od_sgu_w, od_sgu_b, od_w_out, router_w, router_b, moe_w_gate, moe_w_up, moe_w_down):
    b, s, d = x.shape
    depth = ada_w.shape[0]
    tm = min(TOKEN_BLOCK, s)
    mod = _adaln(c, ada_w, ada_b).reshape(depth, b, 6, 1, d)
    rw = router_w.T.astype(BF16)
    rb = router_b.reshape(N_EXPERTS, 1)

    pending = None
    for layer in range(depth):
        sh1, sc1, gt1, sh2, sc2, gt2 = (mod[layer, :, k] for k in range(6))
        i = layer // 2
        if layer % 2 == 0:
            w = ev_w_in[i]
            nq = 3 * DN_HEADS * DN_DK
            nz = DN_HEADS * DN_DK
            nf = 3 * FOX_HEADS * FOX_DH
            o_a = nq + nz
            o_f = o_a + 2 * DN_HEADS
            o_ff = o_f + nf
            splits = ((0, nq), (nq, o_a), (o_a, o_a + nf), (o_a + nf, o_a + nf + LANES))
            w_cols = ((0, o_a, 0), (o_f, o_ff, o_a), (o_a, o_f, o_a + nf), (o_ff, o_ff + FOX_HEADS, o_a + nf + o_f - o_a))
            res = _inproj(x, norm1_g[layer], sc1, sh1, w, splits,
                          (BF16, BF16, BF16, F32), (None,) * 4, tm, pending, ev_conv_w[i], w_cols=w_cols)
            if pending is not None:
                x, res = res[0], res[1:]
            dn, z, fox, small = res
            small_t = jnp.swapaxes(small[:, :, :16], 1, 2)
            zeros4 = jnp.zeros((4,), F32)
            mul = jnp.concatenate([-jnp.exp(ev_dn_a_log[i]), zeros4, -jnp.ones((4,), F32), zeros4])
            bias = jnp.concatenate([ev_dn_dt_bias[i], zeros4, ev_fox_f_bias[i], zeros4])
            pr = jnp.stack([mul, bias], axis=1)
            pc = jnp.pad(jnp.stack([mul, bias], axis=0), ((0, 0), (0, LANES - 16)))
            col, row = _gates(small, small_t, pc, pr)
            row_dn = row[:, :DN_HEADS].reshape(b, DN_HEADS, s // DN_GROUP, DN_GROUP).transpose(0, 2, 1, 3)
            o_dn = _deltanet(dn, z, col, row_dn, ev_dn_onorm_g[i])
            o_fox = _fox(fox, col, ev_fox_qnorm_g[i], ev_fox_knorm_g[i], min(FOX_QUERY_BLOCK, s))
            mix_a, mix_b, w_out = o_dn, o_fox, ev_w_out[i]
        else:
            lw = od_lru_wr.shape[-1] * LRU_BLOCKS
            splits = ((0, lw), (lw, 2 * lw), (2 * lw, 2 * lw + od_sgu_w.shape[-1] * SGU_GROUPS),
                      (2 * lw + od_sgu_w.shape[-1] * SGU_GROUPS, od_w_in.shape[-1]))
            res = _inproj(x, norm1_g[layer], sc1, sh1, od_w_in[i], splits,
                          (BF16,) * 3, (None, "gelu"), tm, pending,
                          sgu=(od_sgu_norm_g[i], od_sgu_w[i], od_sgu_b[i]))
            if pending is not None:
                x, res = res[0], res[1:]
            lx, lg, o_sgu = res
            o_lru = _lru(lx, lg, od_conv_w[i], od_conv_b[i], od_lru_wr[i], od_lru_br[i], od_lru_wi[i],
                         od_lru_bi[i], od_lru_lambda[i])
            mix_a, mix_b, w_out = o_lru, o_sgu, od_w_out[i]
        x1, rows, route, counts = _outproj(mix_a, mix_b, x, w_out, gt1, norm2_g[layer], sc2, sh2,
                                           rw, rb, tm)
        x = x1
        pending = (_moe(rows, route, counts, moe_w_gate, moe_w_up, moe_w_down, layer), gt2)
    return _residual(x, *pending, tm)
```

```python
import functools

import jax
import jax.numpy as jnp
from jax import lax
from jax.experimental import pallas as pl
from jax.experimental.pallas import tpu as pltpu
from jax.experimental.pallas import tpu_sc as plsc

F32 = jnp.float32
BF16 = jnp.bfloat16
EPS = 1e-6
NEG_INF = float("-inf")

DN_HEADS = 4
DN_DK = 128
DN_CHUNK = 64
CONV_WIDTH = 4
FOX_HEADS = 4
FOX_DH = 128
LRU_BLOCKS = 4
LRU_C = 8.0
SGU_GROUPS = 4
SGU_CHUNK = 128
N_EXPERTS = 16
EXPERTS_PER_GROUP = 4
LANES = 128
SUBLANES = 8

VMEM_LIMIT = 48 * 1024 * 1024
TOKEN_BLOCK = 512
ROUTER_BLOCK = 1024
FOX_QUERY_BLOCK = 512


def _cparams(sem):
    return pltpu.CompilerParams(dimension_semantics=sem, vmem_limit_bytes=VMEM_LIMIT)


def _dot(a, b):
    return jnp.dot(a, b, preferred_element_type=F32)


def _dot_nt(a, b):
    return lax.dot_general(a, b, (((1,), (1,)), ((), ())), preferred_element_type=F32)


def _dot_tn(a, b):
    return lax.dot_general(a, b, (((0,), (0,)), ((), ())), preferred_element_type=F32)


def _dot_ones(a, b):
    if a.dtype == BF16:
        return sum(_dot(a, t.astype(BF16)) for t in _split3(b))
    return sum(_dot(t.astype(BF16), b) for t in _split3(a))


def _sigmoid(x):
    return 0.5 + 0.5 * jnp.tanh(0.5 * x)


def _silu(x):
    hx = 0.5 * x
    return hx + hx * jnp.tanh(hx)


def _softplus(x):
    return jnp.maximum(x, 0.0) + jnp.log(1.0 + jnp.exp(-jnp.abs(x)))


def _gelu_tanh(x):
    c = 0.7978845608028654
    return 0.5 * x * (1.0 + jnp.tanh(c * (x + 0.044715 * (x * x * x))))


def _prenorm(x, g, scale, shift):
    ms = jnp.mean(x * x, axis=-1, keepdims=True)
    return (x * lax.rsqrt(ms + EPS) * g) * (1.0 + scale) + shift


def _shift_rows(x, d, fill=0.0):
    rows = lax.broadcasted_iota(jnp.int32, x.shape, 0)
    return jnp.where(rows >= d, pltpu.roll(x, d, axis=0), fill)


def _causal_conv(x, w_ref):
    k_w = w_ref.shape[0]
    acc = x * w_ref[k_w - 1:k_w, :]
    for d in range(1, k_w):
        acc = acc + _shift_rows(x, d) * w_ref[k_w - 1 - d:k_w - d, :]
    return acc


def _adaln_kernel(c_ref, w_ref, b_ref, o_ref):
    c = c_ref[...]
    ca = _silu(c).astype(BF16)
    o_ref[0] = _dot(ca, w_ref[0].astype(BF16)) + b_ref[0]


def _adaln(c, ada_w, ada_b):
    depth, d, n = ada_w.shape
    b = c.shape[0]
    tn = 1536
    return pl.pallas_call(
        _adaln_kernel,
        grid=(depth, n // tn),
        in_specs=[pl.BlockSpec((b, d), lambda l, j: (0, 0)),
                  pl.BlockSpec((1, d, tn), lambda l, j: (l, 0, j)),
                  pl.BlockSpec((1, 1, tn), lambda l, j: (l, 0, j))],
        out_specs=pl.BlockSpec((1, b, tn), lambda l, j: (l, 0, j)),
        out_shape=jax.ShapeDtypeStruct((depth, b, n), F32),
        compiler_params=_cparams(("arbitrary", "arbitrary")),
        name="adaln",
    )(c, ada_w, ada_b.reshape(depth, 1, n))


CONV_HALO = 8
CONV_SLAB = 512


def _inproj_kernel(*refs, col_splits, acts, pending, conv, sgu, cast_w):
    refs = list(refs)
    x_ref = refs.pop(0)
    if pending:
        m_ref, gtm_ref = refs.pop(0), refs.pop(0)
    g_ref, sc_ref, sh_ref, w_ref = (refs.pop(0) for _ in range(4))
    if cast_w:
        w_f32, w_ref = w_ref, refs.pop()

        @pl.when((pl.program_id(0) == 0) & (pl.program_id(1) == 0))
        def _():
            if sum(s1 - s0 for s0, s1, _ in cast_w) < w_ref.shape[1]:
                w_ref[...] = jnp.zeros_like(w_ref)
            for s0, s1, d0 in cast_w:
                w_ref[:, d0:d0 + s1 - s0] = w_f32[:, s0:s1].astype(BF16)
    if conv:
        cw_ref = refs.pop(0)
        halo_sc = refs.pop()
    if sgu:
        sg_ref, sw_ref, sbt_ref = (refs.pop(0) for _ in range(3))
    if pending:
        xo_ref = refs.pop(0)
        x = x_ref[0] + gtm_ref[0] * _unpack_halves(m_ref[0]).astype(F32)
        xo_ref[0] = x
    else:
        x = x_ref[0]
    out_refs = refs
    h = _prenorm(x, g_ref[...], sc_ref[0], sh_ref[0]).astype(BF16)
    tm = x.shape[0]
    if conv:
        @pl.when(pl.program_id(1) == 0)
        def _():
            halo_sc[0:CONV_HALO, :] = jnp.zeros((CONV_HALO, halo_sc.shape[1]), F32)

        c0, c1 = col_splits[0]
        for s0 in range(c0, c1, CONV_SLAB):
            halo_sc[CONV_HALO:, s0 - c0:s0 - c0 + CONV_SLAB] = _dot(h, w_ref[:, s0:s0 + CONV_SLAB])
    n_plain = len(col_splits) - (2 if sgu else 0)
    for k, (o_ref, (c0, c1), act) in enumerate(zip(out_refs[:n_plain], col_splits[:n_plain], acts)):
        if conv and k == 0:
            continue
        p = _dot(h, w_ref[:, c0:c1])
        if act == "gelu":
            p = _gelu_tanh(p)
        o_ref[0] = p.astype(o_ref.dtype)
    if sgu:
        (u0, u1), (v0, v1) = col_splits[n_plain:]
        o_ref = out_refs[n_plain]
        u = _gelu_tanh(_dot(h, w_ref[:, u0:u1]))
        v = _gelu_tanh(_dot(h, w_ref[:, v0:v1]))
        vn = (v * lax.rsqrt(jnp.mean(v * v, axis=-1, keepdims=True) + EPS) * sg_ref[...]).astype(BF16)
        c = SGU_CHUNK
        gw = (u1 - u0) // SGU_GROUPS
        ri = lax.broadcasted_iota(jnp.int32, (c, c), 0)
        ci = lax.broadcasted_iota(jnp.int32, (c, c), 1)
        for g in range(SGU_GROUPS):
            wg = jnp.where(ri >= ci, sw_ref[g], 0.0).astype(BF16)
            bcol = sbt_ref[:, g:g + 1]
            for n in range(tm // c):
                rows = slice(n * c, (n + 1) * c)
                cols = slice(g * gw, (g + 1) * gw)
                mixed = _dot(wg, vn[rows, cols]) + bcol
                o_ref[0, rows, cols] = (u[rows, cols] * mixed).astype(o_ref.dtype)
    if conv:
        o_ref = out_refs[0]
        for t0 in range(0, halo_sc.shape[1], DN_DK):
            cols = slice(t0, t0 + DN_DK)
            acc = halo_sc[CONV_HALO:, cols] * cw_ref[CONV_WIDTH - 1:CONV_WIDTH, cols]
            for dd in range(1, CONV_WIDTH):
                acc = acc + halo_sc[pl.ds(CONV_HALO - dd, tm), cols] * cw_ref[CONV_WIDTH - 1 - dd:CONV_WIDTH - dd, cols]
            halo_sc[0:CONV_HALO, cols] = halo_sc[tm:tm + CONV_HALO, cols]
            yj = _silu(acc)
            if t0 < 2 * DN_HEADS * DN_DK:
                yj = yj * lax.rsqrt(jnp.sum(yj * yj, axis=-1, keepdims=True) + EPS)
            if t0 < DN_HEADS * DN_DK:
                yj = yj * (DN_DK ** -0.5)
            o_ref[0, :, cols] = yj.astype(o_ref.dtype)


def _inproj(x, g, scale, shift, w, col_splits, out_dtypes, acts, tm, pending=None, conv_w=None, sgu=None,
            w_cols=None):
    b, s, d = x.shape
    n = w.shape[1] if w_cols is None else col_splits[-1][1]
    if w.dtype != BF16 and w_cols is None:
        w_cols = ((0, n, 0),)
    out_cols = list(col_splits) if sgu is None else list(col_splits[:-2]) + [col_splits[-2]]
    outs = tuple(jax.ShapeDtypeStruct((b, s, c1 - c0), dt) for (c0, c1), dt in zip(out_cols, out_dtypes))
    row = lambda i, j: (i, 0, 0)
    tok = lambda i, j: (i, j, 0)
    const = lambda i, j: (0, 0)
    in_specs = [pl.BlockSpec((1, tm, d), tok)]
    out_specs = tuple(pl.BlockSpec((1, tm, c1 - c0), tok) for (c0, c1) in out_cols)
    args = (x,)
    if pending is not None:
        in_specs += [pl.BlockSpec((1, tm, d // 2), tok), pl.BlockSpec((1, 1, d), row)]
        out_specs = (pl.BlockSpec((1, tm, d), tok),) + out_specs
        outs = (jax.ShapeDtypeStruct((b, s, d), F32),) + outs
        args += tuple(pending)
    in_specs += [pl.BlockSpec((1, d), const), pl.BlockSpec((1, 1, d), row), pl.BlockSpec((1, 1, d), row),
                 pl.BlockSpec(w.shape, const, pipeline_mode=pl.Buffered(1)) if w_cols else pl.BlockSpec((d, n), const)]
    args += (g.reshape(1, d), scale, shift, w)
    scratch = []
    if conv_w is not None:
        in_specs.append(pl.BlockSpec(conv_w.shape, const))
        args += (conv_w,)
        scratch.append(pltpu.VMEM((CONV_HALO + tm, conv_w.shape[1]), F32))
    if w_cols:
        scratch.append(pltpu.VMEM((d, n), BF16))
    if sgu is not None:
        g_norm, w_s, b_s = sgu
        in_specs += [pl.BlockSpec((1, g_norm.shape[0]), const), pl.BlockSpec(w_s.shape, lambda i, j: (0, 0, 0)),
                     pl.BlockSpec((b_s.shape[1], b_s.shape[0]), const)]
        args += (g_norm.reshape(1, -1), w_s, b_s.T)
    return pl.pallas_call(
        functools.partial(_inproj_kernel, col_splits=col_splits, acts=acts, pending=pending is not None,
                          conv=conv_w is not None, sgu=sgu is not None, cast_w=w_cols),
        grid=(b, s // tm),
        in_specs=in_specs,
        out_specs=out_specs,
        out_shape=outs,
        scratch_shapes=scratch,
        compiler_params=_cparams(("arbitrary", "arbitrary")),
        name="inproj",
    )(*args)


def _gates_kernel(sm_ref, smt_ref, pc_ref, pr_ref, col_ref, row_ref):
    s = sm_ref.shape[1]
    blk = LANES
    ri = lax.broadcasted_iota(jnp.int32, (blk, blk), 0)
    ci = lax.broadcasted_iota(jnp.int32, (blk, blk), 1)
    same_chunk = (ri // DN_CHUNK) == (ci // DN_CHUNK)
    tril = jnp.where(ri >= ci, 1.0, 0.0).astype(BF16)
    tril_loc = jnp.where(same_chunk & (ri >= ci), 1.0, 0.0).astype(BF16)
    triu_loc = jnp.where(same_chunk & (ri <= ci), 1.0, 0.0).astype(BF16)
    lane = lax.broadcasted_iota(jnp.int32, (blk, LANES), 1)
    carry = jnp.zeros((1, LANES), F32)
    for j in range(s // blk):
        rows = slice(j * blk, (j + 1) * blk)
        xc = sm_ref[0, rows, :] + pc_ref[1:2, :]
        dec = pc_ref[0:1, :] * _softplus(jnp.where(lane < 4, xc, -xc))
        cum_glb = _dot_ones(tril, dec) + carry
        col_ref[0, rows, :] = jnp.where(lane < 4, _dot_ones(tril_loc, dec), jnp.where(lane < 8, _sigmoid(xc), cum_glb))
        carry = cum_glb[blk - 1:blk, :]
        decr = pr_ref[:, 0:1] * _softplus(smt_ref[0, :, rows] + pr_ref[:, 1:2])
        row_ref[0, :, rows] = _dot_ones(decr, triu_loc)


def _gates(small, small_t, pc, pr):
    b, s, _ = small.shape
    return pl.pallas_call(
        _gates_kernel,
        grid=(b,),
        in_specs=[pl.BlockSpec((1, s, LANES), lambda i: (i, 0, 0)),
                  pl.BlockSpec((1, 16, s), lambda i: (i, 0, 0)),
                  pl.BlockSpec((2, LANES), lambda i: (0, 0)),
                  pl.BlockSpec((16, 2), lambda i: (0, 0))],
        out_specs=(pl.BlockSpec((1, s, LANES), lambda i: (i, 0, 0)),
                   pl.BlockSpec((1, 16, s), lambda i: (i, 0, 0))),
        out_shape=(jax.ShapeDtypeStruct((b, s, LANES), F32), jax.ShapeDtypeStruct((b, 16, s), F32)),
        compiler_params=_cparams(("arbitrary",)),
        name="gates",
    )(small, small_t, pc, pr)


DN_PACK = 4
DN_GROUP = DN_PACK * DN_CHUNK
DN_ITER_GROUPS = 2


def _blockdiag(p):
    c, wide = p.shape
    t = jnp.concatenate([p] * (wide // c), axis=0)
    rb = lax.broadcasted_iota(jnp.int32, (wide, wide), 0) // c
    cb = lax.broadcasted_iota(jnp.int32, (wide, wide), 1) // c
    return jnp.where(rb == cb, t, 0.0).astype(BF16)


def _diag_blocks(m, c):
    wide = m.shape[1]
    cb = lax.broadcasted_iota(jnp.int32, (c, wide), 1) // c
    out = m[:c]
    for j in range(1, wide // c):
        out = jnp.where(cb == j, m[j * c:(j + 1) * c], out)
    return out


def _rows_to_blocks(col, c, wide):
    cb = lax.broadcasted_iota(jnp.int32, (c, wide), 1) // c
    out = jnp.broadcast_to(col[:c], (c, wide))
    for j in range(1, wide // c):
        out = jnp.where(cb == j, col[j * c:(j + 1) * c], out)
    return out


def _packed_unit_lower_inverse(lows):
    c, wide = lows[0].shape
    ri = lax.broadcasted_iota(jnp.int32, (c, wide), 0)
    ci = lax.broadcasted_iota(jnp.int32, (c, wide), 1) % c
    eye = jnp.where(ri == ci, 1.0, 0.0)
    n = range(len(lows))
    ds = [jnp.where((ri // 16) == (ci // 16), lo, 0.0) for lo in lows]
    xs = [eye - d for d in ds]
    ps = ds
    bds = [_blockdiag(p) for p in ps]
    for _ in range(3):
        ps = [_dot(ps[i].astype(BF16), bds[i]) for i in n]
        bds = [_blockdiag(p) for p in ps]
        xs = [xs[i] + _dot(xs[i].astype(BF16), bds[i]) for i in n]
    for width in (16, 32):
        sel = ((ri // (2 * width)) == (ci // (2 * width))) & ((ri // width) != (ci // width))
        offs = [_blockdiag(jnp.where(sel, lo, 0.0)) for lo in lows]
        ts = [_dot(xs[i].astype(BF16), offs[i]) for i in n]
        bdx = [_blockdiag(x) for x in xs]
        xs = [xs[i] - _dot(ts[i].astype(BF16), bdx[i]) for i in n]
    return xs


def _deltanet_kernel(dn_ref, z_ref, col_ref, row_ref, og_ref, o_ref, st_sc):
    s = dn_ref.shape[1]
    c = DN_CHUNK
    gt = DN_GROUP
    dk = DN_DK
    hd = DN_HEADS * dk
    heads = range(DN_HEADS)
    st_sc[...] = jnp.zeros_like(st_sc)

    ri = lax.broadcasted_iota(jnp.int32, (c, gt), 0)
    ci = lax.broadcasted_iota(jnp.int32, (c, gt), 1) % c
    og = og_ref[...]

    per_iter = DN_ITER_GROUPS if (s // gt) % DN_ITER_GROUPS == 0 else 1
    insts = [(gg, h) for gg in range(per_iter) for h in heads]
    ins = range(len(insts))

    def group(g, carry):
        r0 = [pl.multiple_of((g * per_iter + gg) * gt, gt) for gg in range(per_iter)]
        colg = [col_ref[0, pl.ds(r0[gg], gt), :] for gg in range(per_iter)]
        rowg = [row_ref[0, g * per_iter + gg] for gg in range(per_iter)]
        hs = [slice(h * dk, (h + 1) * dk) for h in heads]
        q = [dn_ref[0, pl.ds(r0[gg], gt), h * dk:(h + 1) * dk].astype(F32) for gg, h in insts]
        k = [dn_ref[0, pl.ds(r0[gg], gt), hd + h * dk:hd + (h + 1) * dk].astype(F32) for gg, h in insts]
        v = [dn_ref[0, pl.ds(r0[gg], gt), 2 * hd + h * dk:2 * hd + (h + 1) * dk].astype(F32) for gg, h in insts]
        gc = [colg[gg][:, h:h + 1] for gg, h in insts]
        beta = [colg[gg][:, DN_HEADS + h:DN_HEADS + h + 1] for gg, h in insts]
        grow = [rowg[gg][h:h + 1, :] for gg, h in insts]
        glast = [jnp.concatenate([jnp.broadcast_to(gc[i][(j + 1) * c - 1:(j + 1) * c], (c, 1))
                                  for j in range(DN_PACK)], axis=0) for i in ins]
        eg = [jnp.exp(gc[i]) for i in ins]
        kbf = [k[i].astype(BF16) for i in ins]
        both = [_dot_nt(jnp.concatenate([(k[i] * beta[i]).astype(BF16), q[i].astype(BF16)], axis=0), kbf[i])
                for i in ins]
        decay = [jnp.exp(jnp.where(ri >= ci, _rows_to_blocks(gc[i], c, gt) - grow[i], NEG_INF)) for i in ins]
        kk = [_diag_blocks(both[i][:gt], c) * decay[i] for i in ins]
        qk = [_blockdiag(_diag_blocks(both[i][gt:], c) * decay[i]) for i in ins]
        t_inv = _packed_unit_lower_inverse([jnp.where(ri > ci, kk[i], 0.0) for i in ins])
        rhs = [jnp.concatenate([k[i] * (beta[i] * eg[i]), v[i] * beta[i]], axis=1).astype(BF16) for i in ins]
        wu = [_dot(_blockdiag(t_inv[i]), rhs[i]).astype(BF16) for i in ins]
        qwu = [_dot(qk[i], wu[i]) for i in ins]
        qp = [(q[i] * eg[i] - qwu[i][:, :dk]).astype(BF16) for i in ins]
        kdec = [(k[i] * jnp.exp(glast[i] - gc[i])).astype(BF16) for i in ins]
        mb = [[_dot_tn(kdec[i][j * c:(j + 1) * c], wu[i][j * c:(j + 1) * c]) for i in ins]
              for j in range(DN_PACK)]
        for gg in range(per_iter):
            mine = [gg * DN_HEADS + h for h in heads]
            outs = [[] for _ in heads]
            for j in range(DN_PACK):
                rows = slice(j * c, (j + 1) * c)
                state = [st_sc[h] for h in heads]
                lhs = [jnp.concatenate([qp[i][rows], mb[j][i][:, :dk].astype(BF16)], axis=0) for i in mine]
                r = [_dot(lhs[h], state[h].astype(BF16)) for h in heads]
                for h, i in zip(heads, mine):
                    gl = jnp.exp(glast[i][j * c:j * c + 1])
                    st_sc[h] = state[h] * gl - r[h][c:] + mb[j][i][:, dk:]
                    outs[h].append(r[h][:c] + qwu[i][rows, dk:])
            for h in heads:
                o = jnp.concatenate(outs[h], axis=0)
                on = o * lax.rsqrt(jnp.mean(o * o, axis=-1, keepdims=True) + EPS) * og
                zz = z_ref[0, pl.ds(r0[gg], gt), hs[h]].astype(F32)
                o_ref[0, pl.ds(r0[gg], gt), hs[h]] = (on * _silu(zz)).astype(o_ref.dtype)
        return carry

    lax.fori_loop(0, s // (gt * per_iter), group, 0)


def _deltanet(dn, z, col, row, onorm_g):
    b, s, w3 = dn.shape
    hd = DN_HEADS * DN_DK
    return pl.pallas_call(
        _deltanet_kernel,
        grid=(b,),
        in_specs=[pl.BlockSpec((1, s, w3), lambda i: (i, 0, 0)),
                  pl.BlockSpec((1, s, hd), lambda i: (i, 0, 0)),
                  pl.BlockSpec((1, s, LANES), lambda i: (i, 0, 0)),
                  pl.BlockSpec((1, s // DN_GROUP, DN_HEADS, DN_GROUP), lambda i: (i, 0, 0, 0)),
                  pl.BlockSpec((1, DN_DK), lambda i: (0, 0))],
        out_specs=pl.BlockSpec((1, s, hd), lambda i: (i, 0, 0)),
        out_shape=jax.ShapeDtypeStruct((b, s, hd), BF16),
        scratch_shapes=[pltpu.VMEM((DN_HEADS, DN_DK, DN_DK), F32)],
        compiler_params=_cparams(("arbitrary",)),
        name="deltanet",
    )(dn, z, col, row, onorm_g.reshape(1, DN_DK))


def _split3(x):
    hi = x.astype(BF16).astype(F32)
    r = x - hi
    mid = r.astype(BF16).astype(F32)
    return hi, mid, r - mid


def _fox_kernel(q_ref, k_ref, v_ref, colq_ref, colk_ref, qg_ref, kg_ref, o_ref, ka_sc, va_sc, m_sc, acc_sc, *, tq):
    qi = pl.program_id(1)
    s = k_ref.shape[1]
    dh = FOX_DH

    def bias_lanes(col):
        lane = lax.broadcasted_iota(jnp.int32, col.shape, 1)
        hi, mid, lo = _split3(col)
        return jnp.where((lane >= 8) & (lane < 8 + FOX_HEADS), hi,
                         jnp.where((lane >= 24) & (lane < 24 + FOX_HEADS), pltpu.roll(mid, 16, axis=1),
                                   jnp.where((lane >= 40) & (lane < 40 + FOX_HEADS), pltpu.roll(lo, 32, axis=1), 0.0)))

    @pl.when(qi == 0)
    def _():
        lane = lax.broadcasted_iota(jnp.int32, (s, LANES), 1)
        ones_col = jnp.where(lane == 0, 1.0, 0.0).astype(BF16)
        ones_k = (lane >= 56) & (lane < 96) & ((lane % 16) >= 8) & ((lane % 16) < 8 + FOX_HEADS)
        ext_k = (jnp.where(ones_k, 1.0, 0.0) - bias_lanes(colk_ref[0])).astype(BF16)
        for h in range(FOX_HEADS):
            cols = slice(h * dh, (h + 1) * dh)
            kf = k_ref[0, :, cols].astype(F32)
            kn = kf * lax.rsqrt(jnp.mean(kf * kf, axis=-1, keepdims=True) + EPS) * kg_ref[...]
            ka_sc[h, :, :dh] = kn.astype(BF16)
            ka_sc[h, :, dh:] = ext_k
            va_sc[h, :, :dh] = v_ref[0, :, cols]
            va_sc[h, :, dh:] = ones_col

    lane = lax.broadcasted_iota(jnp.int32, (tq, LANES), 1)
    cq = pltpu.roll(bias_lanes(colq_ref[0]), 48, axis=1)
    qa = []
    for h in range(FOX_HEADS):
        cols = slice(h * dh, (h + 1) * dh)
        qf = q_ref[0, :, cols].astype(F32)
        qn = qf * lax.rsqrt(jnp.mean(qf * qf, axis=-1, keepdims=True) + EPS) * qg_ref[...] * (dh ** -0.5)
        mine = ((lane % 16) == 8 + h) & (lane < 96)
        ext = jnp.where(mine, jnp.where(lane < 48, 1.0, cq), 0.0)
        qa.append(jnp.concatenate([qn.astype(BF16), ext.astype(BF16)], axis=1))
    m_sc[...] = jnp.full(m_sc.shape, NEG_INF, F32)
    acc_sc[...] = jnp.zeros_like(acc_sc)
    causal = lax.broadcasted_iota(jnp.int32, (tq, tq), 0) >= lax.broadcasted_iota(jnp.int32, (tq, tq), 1)

    def step(k0, masked):
        heads = range(FOX_HEADS)
        logits = [_dot_nt(qa[h], ka_sc[h, pl.ds(k0, tq), :]) for h in heads]
        if masked:
            logits = [jnp.where(causal, lg, NEG_INF) for lg in logits]
        ps, alphas = [], []
        for h in heads:
            m_old = m_sc[h]
            m_new = jnp.maximum(m_old, jnp.max(logits[h], axis=1, keepdims=True))
            m_sc[h] = m_new
            alphas.append(jnp.exp(m_old - m_new))
            ps.append(jnp.exp(logits[h] - jnp.concatenate([m_new] * (tq // LANES), axis=1)).astype(BF16))
        for h in heads:
            pv = _dot(ps[h], va_sc[h, pl.ds(k0, tq), :])
            acc_sc[h] = acc_sc[h] * jnp.concatenate([alphas[h], alphas[h]], axis=1) + pv

    def body(j, carry):
        step(pl.multiple_of(j * tq, tq), False)
        return carry

    lax.fori_loop(0, qi, body, 0)
    step(pl.multiple_of(qi * tq, tq), True)
    for h in range(FOX_HEADS):
        acc = acc_sc[h]
        o_ref[0, :, h * dh:(h + 1) * dh] = (acc[:, :dh] / acc[:, dh:dh + 1]).astype(o_ref.dtype)


def _fox(fox, col, qg, kg, tq):
    b, s, _ = fox.shape
    hd = FOX_HEADS * FOX_DH
    return pl.pallas_call(
        functools.partial(_fox_kernel, tq=tq),
        grid=(b, s // tq),
        in_specs=[pl.BlockSpec((1, tq, hd), lambda i, j: (i, j, 0)),
                  pl.BlockSpec((1, s, hd), lambda i, j: (i, 0, 1)),
                  pl.BlockSpec((1, s, hd), lambda i, j: (i, 0, 2)),
                  pl.BlockSpec((1, tq, LANES), lambda i, j: (i, j, 0)),
                  pl.BlockSpec((1, s, LANES), lambda i, j: (i, 0, 0)),
                  pl.BlockSpec((1, FOX_DH), lambda i, j: (0, 0)),
                  pl.BlockSpec((1, FOX_DH), lambda i, j: (0, 0))],
        out_specs=pl.BlockSpec((1, tq, hd), lambda i, j: (i, j, 0)),
        out_shape=jax.ShapeDtypeStruct((b, s, hd), BF16),
        scratch_shapes=[pltpu.VMEM((FOX_HEADS, s, 2 * FOX_DH), BF16), pltpu.VMEM((FOX_HEADS, s, 2 * FOX_DH), BF16),
                        pltpu.VMEM((FOX_HEADS, tq, LANES), F32), pltpu.VMEM((FOX_HEADS, tq, 2 * FOX_DH), F32)],
        compiler_params=_cparams(("arbitrary", "arbitrary")),
        name="fox",
    )(fox, fox, fox, col, col, qg.reshape(1, FOX_DH), kg.reshape(1, FOX_DH))


def _lru_kernel(x_ref, gate_ref, cw_ref, cb_ref, wr_ref, br_ref, wi_ref, bi_ref, lam_ref, o_ref, a_sc, b_sc):
    s = x_ref.shape[1]
    x = _causal_conv(x_ref[0].astype(F32), cw_ref) + cb_ref[...]
    xb = x.astype(BF16)
    r = _sigmoid(_dot(xb, wr_ref[0].astype(BF16)) + br_ref[...])
    i = _sigmoid(_dot(xb, wi_ref[0].astype(BF16)) + bi_ref[...])
    log_a = (-LRU_C) * r * _softplus(-lam_ref[...])
    a = jnp.exp(log_a)
    bb = jnp.sqrt(1.0 - a * a) * (i * x)
    nt = s // SUBLANES
    a3 = a.reshape(nt, SUBLANES, a.shape[1])
    b3 = bb.reshape(nt, SUBLANES, a.shape[1])
    sub = lax.broadcasted_iota(jnp.int32, a3.shape, 1)
    d = 1
    while d < SUBLANES:
        keep = sub >= d
        b3 = a3 * jnp.where(keep, pltpu.roll(b3, d, axis=1), 0.0) + b3
        a3 = a3 * jnp.where(keep, pltpu.roll(a3, d, axis=1), 1.0)
        d *= 2
    a_sc[...] = a3.reshape(s, a.shape[1])
    b_sc[...] = b3.reshape(s, a.shape[1])
    at = a_sc[pl.ds(SUBLANES - 1, nt, stride=SUBLANES), :]
    bt = b_sc[pl.ds(SUBLANES - 1, nt, stride=SUBLANES), :]
    d = 1
    while d < nt:
        bt = at * _shift_rows(bt, d, 0.0) + bt
        at = at * _shift_rows(at, d, 1.0)
        d *= 2
    h_prev = _shift_rows(bt, 1, 0.0)
    bb = (a3 * h_prev[:, None, :] + b3).reshape(s, a.shape[1])
    o_ref[0] = (bb * gate_ref[0].astype(F32)).astype(o_ref.dtype)


def _lru(lx, lg, conv_w, conv_b, wr, br, wi, bi, lam):
    b, s, wd = lx.shape
    blk = wd // LRU_BLOCKS
    vec = lambda i, j: (0, j)
    return pl.pallas_call(
        _lru_kernel,
        grid=(b, LRU_BLOCKS),
        in_specs=[pl.BlockSpec((1, s, blk), lambda i, j: (i, 0, j)),
                  pl.BlockSpec((1, s, blk), lambda i, j: (i, 0, j)),
                  pl.BlockSpec((CONV_WIDTH, blk), vec),
                  pl.BlockSpec((1, blk), vec),
                  pl.BlockSpec((1, blk, blk), lambda i, j: (j, 0, 0)),
                  pl.BlockSpec((1, blk), vec),
                  pl.BlockSpec((1, blk, blk), lambda i, j: (j, 0, 0)),
                  pl.BlockSpec((1, blk), vec),
                  pl.BlockSpec((1, blk), vec)],
        out_specs=pl.BlockSpec((1, s, blk), lambda i, j: (i, 0, j)),
        out_shape=jax.ShapeDtypeStruct((b, s, wd), BF16),
        scratch_shapes=[pltpu.VMEM((s, blk), F32), pltpu.VMEM((s, blk), F32)],
        compiler_params=_cparams(("arbitrary", "arbitrary")),
        name="rglru",
    )(lx, lg, conv_w, conv_b.reshape(1, wd), wr, br.reshape(1, wd), wi, bi.reshape(1, wd), lam.reshape(1, wd))


def _routing(logits, rb):
    ne, tm = logits.shape
    row = lax.broadcasted_iota(jnp.int32, (ne, tm), 0)
    row_f = row.astype(F32)
    ex = jnp.exp(logits - jnp.max(logits, axis=0, keepdims=True))
    probs = ex / jnp.sum(ex, axis=0, keepdims=True)
    sel = probs + rb

    def top2(vals, idx):
        m1 = jnp.max(vals, axis=0, keepdims=True)
        i1 = jnp.min(jnp.where(vals == m1, idx, float(ne)), axis=0, keepdims=True)
        rest = jnp.where(idx == i1, NEG_INF, vals)
        m2 = jnp.max(rest, axis=0, keepdims=True)
        return m1, i1, m2, rest

    best = None
    g_idx = None
    grp = row // EXPERTS_PER_GROUP
    for g in range(ne // EXPERTS_PER_GROUP):
        m1, _, m2, _ = top2(jnp.where(grp == g, sel, NEG_INF), row_f)
        score = m1 + m2
        if g == 0:
            best, g_idx = score, jnp.zeros((1, tm), jnp.int32)
        else:
            upd = score > best
            best = jnp.where(upd, score, best)
            g_idx = jnp.where(upd, g, g_idx)
    _, i1, m2, rest = top2(jnp.where(grp == g_idx, sel, NEG_INF), row_f)
    i2 = jnp.min(jnp.where(rest == m2, row_f, float(ne)), axis=0, keepdims=True)
    p1 = jnp.sum(jnp.where(row_f == i1, probs, 0.0), axis=0, keepdims=True)
    p2 = jnp.sum(jnp.where(row_f == i2, probs, 0.0), axis=0, keepdims=True)
    den = p1 + p2
    first_lo = i1 < i2
    a = jnp.where(first_lo, i1, i2) - EXPERTS_PER_GROUP * g_idx.astype(F32)
    bhi = jnp.where(first_lo, i2, i1) - EXPERTS_PER_GROUP * g_idx.astype(F32)
    pair = a * (7.0 - a) * 0.5 + (bhi - a - 1.0)
    pair = jnp.where(a == 1.0, 7.0 - pair, pair)
    cls = PAIRS_PER_GROUP * g_idx.astype(F32) + pair
    w_lo = jnp.where(first_lo, p1, p2) / den
    w_hi = jnp.where(first_lo, p2, p1) / den
    swap = pair == 5.0
    return cls, jnp.where(swap, w_hi, w_lo), jnp.where(swap, w_lo, w_hi)


PAIRS_PER_GROUP = EXPERTS_PER_GROUP * (EXPERTS_PER_GROUP - 1) // 2
N_CLASSES = (N_EXPERTS // EXPERTS_PER_GROUP) * PAIRS_PER_GROUP
PAIR_WALK = [(0, 1), (0, 2), (0, 3), (1, 3), (1, 2), (3, 2)]
assert EXPERTS_PER_GROUP == 4 and len(PAIR_WALK) == PAIRS_PER_GROUP
CLASS_E0 = [EXPERTS_PER_GROUP * (c // PAIRS_PER_GROUP) + PAIR_WALK[c % PAIRS_PER_GROUP][0] for c in range(N_CLASSES)]
CLASS_E1 = [EXPERTS_PER_GROUP * (c // PAIRS_PER_GROUP) + PAIR_WALK[c % PAIRS_PER_GROUP][1] for c in range(N_CLASSES)]
CLASS_ROWS = 32
ROW_FEAT = 512
ROW_WORDS = 640


def _pack_halves(x):
    kk = x.shape[1] // 2
    lo = lax.bitcast_convert_type(x[:, :kk].astype(BF16).astype(F32), jnp.uint32)
    hi = lax.bitcast_convert_type(x[:, kk:].astype(BF16).astype(F32), jnp.uint32)
    return lax.bitcast_convert_type((lo >> 16) | hi, jnp.int32)


def _unpack_halves(p):
    u = lax.bitcast_convert_type(p, jnp.uint32)
    lo = lax.bitcast_convert_type(u << 16, F32)
    hi = lax.bitcast_convert_type(u & jnp.uint32(0xFFFF0000), F32)
    return jnp.concatenate([lo.astype(BF16), hi.astype(BF16)], axis=1)


def _outproj_kernel(a_ref, b_ref, x_ref, wf_ref, gt_ref, g2_ref, sc_ref, sh_ref, rwt_ref, rb_ref,
                    x1_ref, rows_ref, route_ref, cnt_ref, carry_sc, w_ref, tri_sc):
    first = (pl.program_id(0) == 0) & (pl.program_id(1) == 0)

    @pl.when(first)
    def _():
        carry_sc[...] = jnp.zeros_like(carry_sc)
        w_ref[...] = wf_ref[...].astype(BF16)
        earlier = (lax.broadcasted_iota(jnp.int32, tri_sc.shape, 0) < lax.broadcasted_iota(jnp.int32, tri_sc.shape, 1))
        tri_sc[...] = jnp.where(earlier, 1.0, 0.0).astype(BF16)

    half = a_ref.shape[2]
    tm = a_ref.shape[1]
    y = _dot(a_ref[0], w_ref[:half, :]) + _dot(b_ref[0], w_ref[half:, :])
    x1 = x_ref[0] + gt_ref[0] * y
    x1_ref[0] = x1
    h2 = _prenorm(x1, g2_ref[...], sc_ref[0], sh_ref[0])
    rows_ref[0, :, :ROW_FEAT] = _pack_halves(h2)
    cls, w_lo, w_hi = _routing(_dot_nt(rwt_ref[...], h2.astype(BF16)), rb_ref[...])
    r128 = lax.broadcasted_iota(jnp.int32, (LANES, tm), 0)
    wts = jnp.where(r128 == 0, w_lo, jnp.where(r128 == 1, w_hi, 0.0))
    rows_ref[0, :, ROW_FEAT:] = lax.bitcast_convert_type(wts.T[:, :ROW_WORDS - ROW_FEAT], jnp.int32)
    crow = lax.broadcasted_iota(jnp.int32, (carry_sc.shape[0], tm), 0).astype(F32)
    onehot = jnp.where(crow == cls, 1.0, 0.0)
    prefix = _dot(onehot.astype(BF16), tri_sc[...]) + carry_sc[:, 0:1]
    rank = jnp.sum(onehot * prefix, axis=0, keepdims=True)
    r8 = lax.broadcasted_iota(jnp.int32, (8, tm), 0)
    route_ref[0] = jnp.where(r8 == 0, cls, jnp.where(r8 == 1, rank, 0.0))
    carry_sc[...] = carry_sc[...] + jnp.sum(onehot, axis=1, keepdims=True)
    cnt_ref[...] = carry_sc[...]


def _outproj(a, bb, x, w, gt, g2, sc2, sh2, rw, rb, tm):
    b, s, d = x.shape
    half = a.shape[2]
    row = lambda i, j: (i, 0, 0)
    tok = lambda i, j: (i, j, 0)
    const = lambda i, j: (0, 0)
    return pl.pallas_call(
        _outproj_kernel,
        grid=(b, s // tm),
        in_specs=[pl.BlockSpec((1, tm, half), tok),
                  pl.BlockSpec((1, tm, half), tok),
                  pl.BlockSpec((1, tm, d), tok),
                  pl.BlockSpec((2 * half, d), const),
                  pl.BlockSpec((1, 1, d), row),
                  pl.BlockSpec((1, d), const),
                  pl.BlockSpec((1, 1, d), row),
                  pl.BlockSpec((1, 1, d), row),
                  pl.BlockSpec((N_EXPERTS, d), const),
                  pl.BlockSpec((N_EXPERTS, 1), const)],
        out_specs=(pl.BlockSpec((1, tm, d), tok),
                   pl.BlockSpec((1, tm, ROW_WORDS), tok),
                   pl.BlockSpec((1, 8, tm), lambda i, j: (i, 0, j)),
                   pl.BlockSpec((CLASS_ROWS, LANES), const)),
        out_shape=(jax.ShapeDtypeStruct((b, s, d), F32),
                   jax.ShapeDtypeStruct((b, s, ROW_WORDS), jnp.int32),
                   jax.ShapeDtypeStruct((b, 8, s), F32),
                   jax.ShapeDtypeStruct((CLASS_ROWS, LANES), F32)),
        scratch_shapes=[pltpu.VMEM((CLASS_ROWS, LANES), F32), pltpu.VMEM((2 * half, d), BF16),
                        pltpu.VMEM((tm, tm), BF16)],
        compiler_params=_cparams(("arbitrary", "arbitrary")),
        name="outproj_router",
    )(a, bb, x, w, gt, g2.reshape(1, d), sc2, sh2, rw, rb)


MOE_ROWS = 512
SC_CHUNK = 128


def _sc_workers():
    info = plsc.get_sparse_core_info()
    return info.num_cores, info.num_cores * info.num_subcores


def _sc_scatter_rows(rows, idx3, n_out):
    nw, k, ch = idx3.shape
    width = rows.shape[1]
    nc, _ = _sc_workers()
    mesh = plsc.VectorSubcoreMesh(core_axis_name="c", subcore_axis_name="s")

    @functools.partial(
        pl.kernel, mesh=mesh,
        out_type=jax.ShapeDtypeStruct((n_out, width), rows.dtype),
        scratch_types=[pltpu.VMEM((k, ch), jnp.int32), pltpu.VMEM((ch, width), rows.dtype), pltpu.SemaphoreType.DMA],
        name="moe_dispatch")
    def kern(rows_hbm, idx_hbm, out_hbm, idx_v, rows_v, sem):
        wid = lax.axis_index("s") * nc + lax.axis_index("c")
        pltpu.sync_copy(idx_hbm.at[wid], idx_v)

        @pl.loop(0, k)
        def _(j):
            pltpu.sync_copy(rows_hbm.at[pl.ds((wid * k + j) * ch, ch)], rows_v)
            pltpu.async_copy(rows_v, out_hbm.at[idx_v.at[j]], sem).wait()

    return kern(rows, idx3)


def _sc_gather_rows(table, idx3):
    nw, k, ch = idx3.shape
    width = table.shape[1]
    nc, _ = _sc_workers()
    mesh = plsc.VectorSubcoreMesh(core_axis_name="c", subcore_axis_name="s")

    @functools.partial(
        pl.kernel, mesh=mesh,
        out_type=jax.ShapeDtypeStruct((nw * k * ch, width), table.dtype),
        scratch_types=[pltpu.VMEM((k, ch), jnp.int32), pltpu.VMEM((ch, width), table.dtype), pltpu.SemaphoreType.DMA],
        name="moe_combine")
    def kern(table_hbm, idx_hbm, out_hbm, idx_v, rows_v, sem):
        wid = lax.axis_index("s") * nc + lax.axis_index("c")
        pltpu.sync_copy(idx_hbm.at[wid], idx_v)

        @pl.loop(0, k)
        def _(j):
            pltpu.async_copy(table_hbm.at[idx_v.at[j]], rows_v, sem).wait()
            pltpu.sync_copy(rows_v, out_hbm.at[pl.ds((wid * k + j) * ch, ch)])

    return kern(table, idx3)


def _experts_kernel(e0_ref, e1_ref, nused_ref, x_ref, wg0_ref, wu0_ref, wd0_ref, wg1_ref, wu1_ref, wd1_ref, y_ref):
    @pl.when(pl.program_id(0) < nused_ref[0])
    def _():
        blk = x_ref[...]
        feat = ROW_FEAT
        h = _unpack_halves(blk[:, :feat])
        wts = lax.bitcast_convert_type(blk[:, feat:], F32)
        y = None
        for col, (wg_ref, wu_ref, wd_ref) in enumerate(((wg0_ref, wu0_ref, wd0_ref), (wg1_ref, wu1_ref, wd1_ref))):
            act = (_silu(_dot(h, wg_ref[0, 0].astype(BF16))) * _dot(h, wu_ref[0, 0].astype(BF16))
                   * wts[:, col:col + 1])
            part = _dot(act.astype(BF16), wd_ref[0, 0].astype(BF16))
            y = part if y is None else y + part
        y_ref[...] = _pack_halves(y)


def _experts(xs, blk_e0, blk_e1, nused, wg, wu, wd, layer):
    n_rows = xs.shape[0]
    _, ne, d, f = wg.shape
    nblk = n_rows // MOE_ROWS
    rows = lambda i, e0, e1, nu: (jnp.minimum(i, nu[0] - 1), 0)
    lo = lambda i, e0, e1, nu: (layer, e0[i], 0, 0)
    hi = lambda i, e0, e1, nu: (layer, e1[i], 0, 0)
    return pl.pallas_call(
        _experts_kernel,
        grid_spec=pltpu.PrefetchScalarGridSpec(
            num_scalar_prefetch=3,
            grid=(nblk,),
            in_specs=[pl.BlockSpec((MOE_ROWS, ROW_WORDS), rows),
                      pl.BlockSpec((1, 1, d, f), lo), pl.BlockSpec((1, 1, d, f), lo), pl.BlockSpec((1, 1, f, d), lo),
                      pl.BlockSpec((1, 1, d, f), hi), pl.BlockSpec((1, 1, d, f), hi), pl.BlockSpec((1, 1, f, d), hi)],
            out_specs=pl.BlockSpec((MOE_ROWS, d // 2), lambda i, e0, e1, nu: (i, 0))),
        out_shape=jax.ShapeDtypeStruct((n_rows, d // 2), jnp.int32),
        compiler_params=_cparams(("arbitrary",)),
        name="moe_experts",
    )(blk_e0, blk_e1, nused, xs, wg, wu, wd, wg, wu, wd)


def _residual_kernel(x_ref, m_ref, gt_ref, o_ref):
    o_ref[0] = x_ref[0] + gt_ref[0] * _unpack_halves(m_ref[0]).astype(F32)


def _residual(x1, moe_p, gt, tm):
    b, s, d = x1.shape
    tok = lambda i, j: (i, j, 0)
    return pl.pallas_call(
        _residual_kernel,
        grid=(b, s // tm),
        in_specs=[pl.BlockSpec((1, tm, d), tok), pl.BlockSpec((1, tm, d // 2), tok),
                  pl.BlockSpec((1, 1, d), lambda i, j: (i, 0, 0))],
        out_specs=pl.BlockSpec((1, tm, d), tok),
        out_shape=jax.ShapeDtypeStruct((b, s, d), F32),
        compiler_params=_cparams(("arbitrary", "arbitrary")),
        name="moe_residual",
    )(x1, moe_p, gt)


def _moe(rows, route, counts, wg, wu, wd, layer):
    b, s, _ = rows.shape
    d = wg.shape[2]
    t = b * s
    _, nw = _sc_workers()
    cnt = counts[:N_CLASSES, 0].astype(jnp.int32)
    padded = ((cnt + MOE_ROWS - 1) // MOE_ROWS) * MOE_ROWS
    ends = jnp.cumsum(padded)
    n_rows = t + N_CLASSES * MOE_ROWS
    nblk = n_rows // MOE_ROWS
    cls = route[:, 0, :].reshape(t).astype(jnp.int32)
    rank = route[:, 1, :].reshape(t).astype(jnp.int32)
    pos = rank + jnp.sum(jnp.where(cls[:, None] > jnp.arange(N_CLASSES, dtype=jnp.int32)[None, :], padded[None, :], 0),
                         axis=1)
    idx3 = pos.reshape(nw, t // (nw * SC_CHUNK), SC_CHUNK)
    nused = (ends[-1] // MOE_ROWS).reshape(1)
    blk_cls = jnp.sum((jnp.arange(nblk, dtype=jnp.int32)[:, None] * MOE_ROWS) >= ends[None, :], axis=1)
    blk_cls = jnp.minimum(blk_cls, blk_cls[jnp.maximum(nused[0] - 1, 0)])
    blk_e0 = jnp.asarray(CLASS_E0, jnp.int32)[blk_cls]
    blk_e1 = jnp.asarray(CLASS_E1, jnp.int32)[blk_cls]
    xs = _sc_scatter_rows(rows.reshape(t, ROW_WORDS), idx3, n_rows)
    ys = _experts(xs, blk_e0, blk_e1, nused, wg, wu, wd, layer)
    return _sc_gather_rows(ys, idx3).reshape(b, s, d // 2)


def kernel(x, c, ada_w, ada_b, norm1_g, norm2_g, ev_w_in, ev_conv_w, ev_dn_a_log, ev_dn_dt_bias, ev_dn_onorm_g, ev_fox_f_bias, ev_fox_qnorm_g, ev_fox_knorm_g, ev_w_out, od_w_in, od_conv_w, od_conv_b, od_lru_wr, od_lru_br, od_lru_wi, od_lru_bi, od_lru_lambda, od_sgu_norm_g, od_sgu_w, od_sgu_b, od_w_out, router_w, router_b, moe_w_gate, moe_w_up, moe_w_down):
    b, s, d = x.shape
    depth = ada_w.shape[0]
    tm = min(TOKEN_BLOCK, s)
    mod = _adaln(c, ada_w, ada_b).reshape(depth, b, 6, 1, d)
    rw = router_w.T.astype(BF16)
    rb = router_b.reshape(N_EXPERTS, 1)

    pending = None
    for layer in range(depth):
        sh1, sc1, gt1, sh2, sc2, gt2 = (mod[layer, :, k] for k in range(6))
        i = layer // 2
        if layer % 2 == 0:
            w = ev_w_in[i]
            nq = 3 * DN_HEADS * DN_DK
            nz = DN_HEADS * DN_DK
            nf = 3 * FOX_HEADS * FOX_DH
            o_a = nq + nz
            o_f = o_a + 2 * DN_HEADS
            o_ff = o_f + nf
            splits = ((0, nq), (nq, o_a), (o_a, o_a + nf), (o_a + nf, o_a + nf + LANES))
            w_cols = ((0, o_a, 0), (o_f, o_ff, o_a), (o_a, o_f, o_a + nf), (o_ff, o_ff + FOX_HEADS, o_a + nf + o_f - o_a))
            res = _inproj(x, norm1_g[layer], sc1, sh1, w, splits,
                          (BF16, BF16, BF16, F32), (None,) * 4, tm, pending, ev_conv_w[i], w_cols=w_cols)
            if pending is not None:
                x, res = res[0], res[1:]
            dn, z, fox, small = res
            small_t = jnp.swapaxes(small[:, :, :16], 1, 2)
            zeros4 = jnp.zeros((4,), F32)
            mul = jnp.concatenate([-jnp.exp(ev_dn_a_log[i]), zeros4, -jnp.ones((4,), F32), zeros4])
            bias = jnp.concatenate([ev_dn_dt_bias[i], zeros4, ev_fox_f_bias[i], zeros4])
            pr = jnp.stack([mul, bias], axis=1)
            pc = jnp.pad(jnp.stack([mul, bias], axis=0), ((0, 0), (0, LANES - 16)))
            col, row = _gates(small, small_t, pc, pr)
            row_dn = row[:, :DN_HEADS].reshape(b, DN_HEADS, s // DN_GROUP, DN_GROUP).transpose(0, 2, 1, 3)
            o_dn = _deltanet(dn, z, col, row_dn, ev_dn_onorm_g[i])
            o_fox = _fox(fox, col, ev_fox_qnorm_g[i], ev_fox_knorm_g[i], min(FOX_QUERY_BLOCK, s))
            mix_a, mix_b, w_out = o_dn, o_fox, ev_w_out[i]
        else:
            lw = od_lru_wr.shape[-1] * LRU_BLOCKS
            splits = ((0, lw), (lw, 2 * lw), (2 * lw, 2 * lw + od_sgu_w.shape[-1] * SGU_GROUPS),
                      (2 * lw + od_sgu_w.shape[-1] * SGU_GROUPS, od_w_in.shape[-1]))
            res = _inproj(x, norm1_g[layer], sc1, sh1, od_w_in[i], splits,
                          (BF16,) * 3, (None, "gelu"), tm, pending,
                          sgu=(od_sgu_norm_g[i], od_sgu_w[i], od_sgu_b[i]))
            if pending is not None:
                x, res = res[0], res[1:]
            lx, lg, o_sgu = res
            o_lru = _lru(lx, lg, od_conv_w[i], od_conv_b[i], od_lru_wr[i], od_lru_br[i], od_lru_wi[i],
                         od_lru_bi[i], od_lru_lambda[i])
            mix_a, mix_b, w_out = o_lru, o_sgu, od_w_out[i]
        x1, rows, route, counts = _outproj(mix_a, mix_b, x, w_out, gt1, norm2_g[layer], sc2, sh2,
                                           rw, rb, min(ROUTER_BLOCK, s))
        x = x1
        pending = (_moe(rows, route, counts, moe_w_gate, moe_w_up, moe_w_down, layer), gt2)
    return _residual(x, *pending, min(ROUTER_BLOCK, s))
```

```python
import functools

import jax
import jax.numpy as jnp
from jax import lax
from jax.experimental import pallas as pl
from jax.experimental.pallas import tpu as pltpu
from jax.experimental.pallas import tpu_sc as plsc

F32 = jnp.float32
BF16 = jnp.bfloat16
EPS = 1e-6
NEG_INF = float("-inf")

DN_HEADS = 4
DN_DK = 128
DN_CHUNK = 64
CONV_WIDTH = 4
FOX_HEADS = 4
FOX_DH = 128
LRU_BLOCKS = 4
LRU_C = 8.0
SGU_GROUPS = 4
SGU_CHUNK = 128
N_EXPERTS = 16
EXPERTS_PER_GROUP = 4
LANES = 128
SUBLANES = 8

VMEM_LIMIT = 48 * 1024 * 1024
TOKEN_BLOCK = 512
ODD_TOKEN_BLOCK = 1024
ROUTER_BLOCK = 1024
FOX_QUERY_BLOCK = 512


def _cparams(sem):
    return pltpu.CompilerParams(dimension_semantics=sem, vmem_limit_bytes=VMEM_LIMIT)


def _dot(a, b):
    return jnp.dot(a, b, preferred_element_type=F32)


def _dot_nt(a, b):
    return lax.dot_general(a, b, (((1,), (1,)), ((), ())), preferred_element_type=F32)


def _dot_tn(a, b):
    return lax.dot_general(a, b, (((0,), (0,)), ((), ())), preferred_element_type=F32)


def _dot_ones(a, b):
    if a.dtype == BF16:
        return sum(_dot(a, t.astype(BF16)) for t in _split3(b))
    return sum(_dot(t.astype(BF16), b) for t in _split3(a))


def _sigmoid(x):
    return 0.5 + 0.5 * jnp.tanh(0.5 * x)


def _silu(x):
    hx = 0.5 * x
    return hx + hx * jnp.tanh(hx)


def _softplus(x):
    return jnp.maximum(x, 0.0) + jnp.log(1.0 + jnp.exp(-jnp.abs(x)))


def _gelu_tanh(x):
    c = 0.7978845608028654
    return 0.5 * x * (1.0 + jnp.tanh(c * (x + 0.044715 * (x * x * x))))


def _prenorm(x, g, scale, shift):
    ms = jnp.mean(x * x, axis=-1, keepdims=True)
    return (x * lax.rsqrt(ms + EPS) * g) * (1.0 + scale) + shift


def _shift_rows(x, d, fill=0.0):
    rows = lax.broadcasted_iota(jnp.int32, x.shape, 0)
    return jnp.where(rows >= d, pltpu.roll(x, d, axis=0), fill)


def _causal_conv(x, w_ref):
    k_w = w_ref.shape[0]
    acc = x * w_ref[k_w - 1:k_w, :]
    for d in range(1, k_w):
        acc = acc + _shift_rows(x, d) * w_ref[k_w - 1 - d:k_w - d, :]
    return acc


def _adaln_kernel(c_ref, w_ref, b_ref, o_ref):
    c = c_ref[...]
    ca = _silu(c).astype(BF16)
    o_ref[0] = _dot(ca, w_ref[0].astype(BF16)) + b_ref[0]


def _adaln(c, ada_w, ada_b):
    depth, d, n = ada_w.shape
    b = c.shape[0]
    tn = 1536
    return pl.pallas_call(
        _adaln_kernel,
        grid=(depth, n // tn),
        in_specs=[pl.BlockSpec((b, d), lambda l, j: (0, 0)),
                  pl.BlockSpec((1, d, tn), lambda l, j: (l, 0, j)),
                  pl.BlockSpec((1, 1, tn), lambda l, j: (l, 0, j))],
        out_specs=pl.BlockSpec((1, b, tn), lambda l, j: (l, 0, j)),
        out_shape=jax.ShapeDtypeStruct((depth, b, n), F32),
        compiler_params=_cparams(("arbitrary", "arbitrary")),
        name="adaln",
    )(c, ada_w, ada_b.reshape(depth, 1, n))


CONV_HALO = 8
CONV_SLAB = 512


def _inproj_kernel(*refs, col_splits, acts, pending, conv, sgu, cast_w):
    refs = list(refs)
    x_ref = refs.pop(0)
    if pending:
        m_ref, gtm_ref = refs.pop(0), refs.pop(0)
    g_ref, sc_ref, sh_ref, w_ref = (refs.pop(0) for _ in range(4))
    if cast_w:
        w_f32, w_ref = w_ref, refs.pop()

        @pl.when((pl.program_id(0) == 0) & (pl.program_id(1) == 0))
        def _():
            if sum(s1 - s0 for s0, s1, _ in cast_w) < w_ref.shape[1]:
                w_ref[...] = jnp.zeros_like(w_ref)
            for s0, s1, d0 in cast_w:
                w_ref[:, d0:d0 + s1 - s0] = w_f32[:, s0:s1].astype(BF16)
    if conv:
        cw_ref = refs.pop(0)
        halo_sc = refs.pop()
    if sgu:
        sg_ref, sw_ref, sbt_ref = (refs.pop(0) for _ in range(3))
    if pending:
        xo_ref = refs.pop(0)
        x = x_ref[0] + gtm_ref[0] * _unpack_halves(m_ref[0]).astype(F32)
        xo_ref[0] = x
    else:
        x = x_ref[0]
    out_refs = refs
    h = _prenorm(x, g_ref[...], sc_ref[0], sh_ref[0]).astype(BF16)
    tm = x.shape[0]
    if conv:
        @pl.when(pl.program_id(1) == 0)
        def _():
            halo_sc[0:CONV_HALO, :] = jnp.zeros((CONV_HALO, halo_sc.shape[1]), F32)

        c0, c1 = col_splits[0]
        for s0 in range(c0, c1, CONV_SLAB):
            halo_sc[CONV_HALO:, s0 - c0:s0 - c0 + CONV_SLAB] = _dot(h, w_ref[:, s0:s0 + CONV_SLAB])
    n_plain = len(col_splits) - (2 if sgu else 0)
    for k, (o_ref, (c0, c1), act) in enumerate(zip(out_refs[:n_plain], col_splits[:n_plain], acts)):
        if conv and k == 0:
            continue
        p = _dot(h, w_ref[:, c0:c1])
        if act == "gelu":
            p = _gelu_tanh(p)
        o_ref[0] = p.astype(o_ref.dtype)
    if sgu:
        (u0, u1), (v0, v1) = col_splits[n_plain:]
        o_ref = out_refs[n_plain]
        u = _gelu_tanh(_dot(h, w_ref[:, u0:u1]))
        v = _gelu_tanh(_dot(h, w_ref[:, v0:v1]))
        vn = (v * lax.rsqrt(jnp.mean(v * v, axis=-1, keepdims=True) + EPS) * sg_ref[...]).astype(BF16)
        c = SGU_CHUNK
        gw = (u1 - u0) // SGU_GROUPS
        ri = lax.broadcasted_iota(jnp.int32, (c, c), 0)
        ci = lax.broadcasted_iota(jnp.int32, (c, c), 1)
        for g in range(SGU_GROUPS):
            wg = jnp.where(ri >= ci, sw_ref[g], 0.0).astype(BF16)
            bcol = sbt_ref[:, g:g + 1]
            for n in range(tm // c):
                rows = slice(n * c, (n + 1) * c)
                cols = slice(g * gw, (g + 1) * gw)
                mixed = _dot(wg, vn[rows, cols]) + bcol
                o_ref[0, rows, cols] = (u[rows, cols] * mixed).astype(o_ref.dtype)
    if conv:
        o_ref = out_refs[0]
        for t0 in range(0, halo_sc.shape[1], DN_DK):
            cols = slice(t0, t0 + DN_DK)
            acc = halo_sc[CONV_HALO:, cols] * cw_ref[CONV_WIDTH - 1:CONV_WIDTH, cols]
            for dd in range(1, CONV_WIDTH):
                acc = acc + halo_sc[pl.ds(CONV_HALO - dd, tm), cols] * cw_ref[CONV_WIDTH - 1 - dd:CONV_WIDTH - dd, cols]
            halo_sc[0:CONV_HALO, cols] = halo_sc[tm:tm + CONV_HALO, cols]
            yj = _silu(acc)
            if t0 < 2 * DN_HEADS * DN_DK:
                yj = yj * lax.rsqrt(jnp.sum(yj * yj, axis=-1, keepdims=True) + EPS)
            if t0 < DN_HEADS * DN_DK:
                yj = yj * (DN_DK ** -0.5)
            o_ref[0, :, cols] = yj.astype(o_ref.dtype)


def _inproj(x, g, scale, shift, w, col_splits, out_dtypes, acts, tm, pending=None, conv_w=None, sgu=None,
            w_cols=None):
    b, s, d = x.shape
    n = w.shape[1] if w_cols is None else col_splits[-1][1]
    if w.dtype != BF16 and w_cols is None:
        w_cols = ((0, n, 0),)
    out_cols = list(col_splits) if sgu is None else list(col_splits[:-2]) + [col_splits[-2]]
    outs = tuple(jax.ShapeDtypeStruct((b, s, c1 - c0), dt) for (c0, c1), dt in zip(out_cols, out_dtypes))
    row = lambda i, j: (i, 0, 0)
    tok = lambda i, j: (i, j, 0)
    const = lambda i, j: (0, 0)
    in_specs = [pl.BlockSpec((1, tm, d), tok)]
    out_specs = tuple(pl.BlockSpec((1, tm, c1 - c0), tok) for (c0, c1) in out_cols)
    args = (x,)
    if pending is not None:
        in_specs += [pl.BlockSpec((1, tm, d // 2), tok), pl.BlockSpec((1, 1, d), row)]
        out_specs = (pl.BlockSpec((1, tm, d), tok),) + out_specs
        outs = (jax.ShapeDtypeStruct((b, s, d), F32),) + outs
        args += tuple(pending)
    in_specs += [pl.BlockSpec((1, d), const), pl.BlockSpec((1, 1, d), row), pl.BlockSpec((1, 1, d), row),
                 pl.BlockSpec(w.shape, const, pipeline_mode=pl.Buffered(1)) if w_cols else pl.BlockSpec((d, n), const)]
    args += (g.reshape(1, d), scale, shift, w)
    scratch = []
    if conv_w is not None:
        in_specs.append(pl.BlockSpec(conv_w.shape, const))
        args += (conv_w,)
        scratch.append(pltpu.VMEM((CONV_HALO + tm, conv_w.shape[1]), F32))
    if w_cols:
        scratch.append(pltpu.VMEM((d, n), BF16))
    if sgu is not None:
        g_norm, w_s, b_s = sgu
        in_specs += [pl.BlockSpec((1, g_norm.shape[0]), const), pl.BlockSpec(w_s.shape, lambda i, j: (0, 0, 0)),
                     pl.BlockSpec((b_s.shape[1], b_s.shape[0]), const)]
        args += (g_norm.reshape(1, -1), w_s, b_s.T)
    return pl.pallas_call(
        functools.partial(_inproj_kernel, col_splits=col_splits, acts=acts, pending=pending is not None,
                          conv=conv_w is not None, sgu=sgu is not None, cast_w=w_cols),
        grid=(b, s // tm),
        in_specs=in_specs,
        out_specs=out_specs,
        out_shape=outs,
        scratch_shapes=scratch,
        compiler_params=_cparams(("arbitrary", "arbitrary")),
        name="inproj",
    )(*args)


def _gates_kernel(sm_ref, smt_ref, pc_ref, pr_ref, col_ref, row_ref):
    s = sm_ref.shape[1]
    blk = LANES
    ri = lax.broadcasted_iota(jnp.int32, (blk, blk), 0)
    ci = lax.broadcasted_iota(jnp.int32, (blk, blk), 1)
    same_chunk = (ri // DN_CHUNK) == (ci // DN_CHUNK)
    tril = jnp.where(ri >= ci, 1.0, 0.0).astype(BF16)
    tril_loc = jnp.where(same_chunk & (ri >= ci), 1.0, 0.0).astype(BF16)
    triu_loc = jnp.where(same_chunk & (ri <= ci), 1.0, 0.0).astype(BF16)
    lane = lax.broadcasted_iota(jnp.int32, (blk, LANES), 1)
    carry = jnp.zeros((1, LANES), F32)
    for j in range(s // blk):
        rows = slice(j * blk, (j + 1) * blk)
        xc = sm_ref[0, rows, :] + pc_ref[1:2, :]
        dec = pc_ref[0:1, :] * _softplus(jnp.where(lane < 4, xc, -xc))
        cum_glb = _dot_ones(tril, dec) + carry
        col_ref[0, rows, :] = jnp.where(lane < 4, _dot_ones(tril_loc, dec), jnp.where(lane < 8, _sigmoid(xc), cum_glb))
        carry = cum_glb[blk - 1:blk, :]
        decr = pr_ref[:, 0:1] * _softplus(smt_ref[0, :, rows] + pr_ref[:, 1:2])
        row_ref[0, :, rows] = _dot_ones(decr, triu_loc)


def _gates(small, small_t, pc, pr):
    b, s, _ = small.shape
    return pl.pallas_call(
        _gates_kernel,
        grid=(b,),
        in_specs=[pl.BlockSpec((1, s, LANES), lambda i: (i, 0, 0)),
                  pl.BlockSpec((1, 16, s), lambda i: (i, 0, 0)),
                  pl.BlockSpec((2, LANES), lambda i: (0, 0)),
                  pl.BlockSpec((16, 2), lambda i: (0, 0))],
        out_specs=(pl.BlockSpec((1, s, LANES), lambda i: (i, 0, 0)),
                   pl.BlockSpec((1, 16, s), lambda i: (i, 0, 0))),
        out_shape=(jax.ShapeDtypeStruct((b, s, LANES), F32), jax.ShapeDtypeStruct((b, 16, s), F32)),
        compiler_params=_cparams(("arbitrary",)),
        name="gates",
    )(small, small_t, pc, pr)


DN_PACK = 4
DN_GROUP = DN_PACK * DN_CHUNK
DN_ITER_GROUPS = 2


def _blockdiag(p):
    c, wide = p.shape
    t = jnp.concatenate([p] * (wide // c), axis=0)
    rb = lax.broadcasted_iota(jnp.int32, (wide, wide), 0) // c
    cb = lax.broadcasted_iota(jnp.int32, (wide, wide), 1) // c
    return jnp.where(rb == cb, t, 0.0).astype(BF16)


def _diag_blocks(m, c):
    wide = m.shape[1]
    cb = lax.broadcasted_iota(jnp.int32, (c, wide), 1) // c
    out = m[:c]
    for j in range(1, wide // c):
        out = jnp.where(cb == j, m[j * c:(j + 1) * c], out)
    return out


def _rows_to_blocks(col, c, wide):
    cb = lax.broadcasted_iota(jnp.int32, (c, wide), 1) // c
    out = jnp.broadcast_to(col[:c], (c, wide))
    for j in range(1, wide // c):
        out = jnp.where(cb == j, col[j * c:(j + 1) * c], out)
    return out


def _packed_unit_lower_inverse(lows):
    c, wide = lows[0].shape
    ri = lax.broadcasted_iota(jnp.int32, (c, wide), 0)
    ci = lax.broadcasted_iota(jnp.int32, (c, wide), 1) % c
    eye = jnp.where(ri == ci, 1.0, 0.0)
    n = range(len(lows))
    ds = [jnp.where((ri // 16) == (ci // 16), lo, 0.0) for lo in lows]
    xs = [eye - d for d in ds]
    ps = ds
    bds = [_blockdiag(p) for p in ps]
    for _ in range(3):
        ps = [_dot(ps[i].astype(BF16), bds[i]) for i in n]
        bds = [_blockdiag(p) for p in ps]
        xs = [xs[i] + _dot(xs[i].astype(BF16), bds[i]) for i in n]
    for width in (16, 32):
        sel = ((ri // (2 * width)) == (ci // (2 * width))) & ((ri // width) != (ci // width))
        offs = [_blockdiag(jnp.where(sel, lo, 0.0)) for lo in lows]
        ts = [_dot(xs[i].astype(BF16), offs[i]) for i in n]
        bdx = [_blockdiag(x) for x in xs]
        xs = [xs[i] - _dot(ts[i].astype(BF16), bdx[i]) for i in n]
    return xs


def _deltanet_kernel(dn_ref, z_ref, col_ref, row_ref, og_ref, o_ref, st_sc):
    s = dn_ref.shape[1]
    c = DN_CHUNK
    gt = DN_GROUP
    dk = DN_DK
    hd = DN_HEADS * dk
    heads = range(DN_HEADS)
    st_sc[...] = jnp.zeros_like(st_sc)

    ri = lax.broadcasted_iota(jnp.int32, (c, gt), 0)
    ci = lax.broadcasted_iota(jnp.int32, (c, gt), 1) % c
    og = og_ref[...]

    per_iter = DN_ITER_GROUPS if (s // gt) % DN_ITER_GROUPS == 0 else 1
    insts = [(gg, h) for gg in range(per_iter) for h in heads]
    ins = range(len(insts))

    def group(g, carry):
        r0 = [pl.multiple_of((g * per_iter + gg) * gt, gt) for gg in range(per_iter)]
        colg = [col_ref[0, pl.ds(r0[gg], gt), :] for gg in range(per_iter)]
        rowg = [row_ref[0, g * per_iter + gg] for gg in range(per_iter)]
        hs = [slice(h * dk, (h + 1) * dk) for h in heads]
        q = [dn_ref[0, pl.ds(r0[gg], gt), h * dk:(h + 1) * dk].astype(F32) for gg, h in insts]
        k = [dn_ref[0, pl.ds(r0[gg], gt), hd + h * dk:hd + (h + 1) * dk].astype(F32) for gg, h in insts]
        v = [dn_ref[0, pl.ds(r0[gg], gt), 2 * hd + h * dk:2 * hd + (h + 1) * dk].astype(F32) for gg, h in insts]
        gc = [colg[gg][:, h:h + 1] for gg, h in insts]
        beta = [colg[gg][:, DN_HEADS + h:DN_HEADS + h + 1] for gg, h in insts]
        grow = [rowg[gg][h:h + 1, :] for gg, h in insts]
        glast = [jnp.concatenate([jnp.broadcast_to(gc[i][(j + 1) * c - 1:(j + 1) * c], (c, 1))
                                  for j in range(DN_PACK)], axis=0) for i in ins]
        eg = [jnp.exp(gc[i]) for i in ins]
        kbf = [k[i].astype(BF16) for i in ins]
        both = [_dot_nt(jnp.concatenate([(k[i] * beta[i]).astype(BF16), q[i].astype(BF16)], axis=0), kbf[i])
                for i in ins]
        decay = [jnp.exp(jnp.where(ri >= ci, _rows_to_blocks(gc[i], c, gt) - grow[i], NEG_INF)) for i in ins]
        kk = [_diag_blocks(both[i][:gt], c) * decay[i] for i in ins]
        qk = [_blockdiag(_diag_blocks(both[i][gt:], c) * decay[i]) for i in ins]
        t_inv = _packed_unit_lower_inverse([jnp.where(ri > ci, kk[i], 0.0) for i in ins])
        rhs = [jnp.concatenate([k[i] * (beta[i] * eg[i]), v[i] * beta[i]], axis=1).astype(BF16) for i in ins]
        wu = [_dot(_blockdiag(t_inv[i]), rhs[i]).astype(BF16) for i in ins]
        qwu = [_dot(qk[i], wu[i]) for i in ins]
        qp = [(q[i] * eg[i] - qwu[i][:, :dk]).astype(BF16) for i in ins]
        kdec = [(k[i] * jnp.exp(glast[i] - gc[i])).astype(BF16) for i in ins]
        mb = [[_dot_tn(kdec[i][j * c:(j + 1) * c], wu[i][j * c:(j + 1) * c]) for i in ins]
              for j in range(DN_PACK)]
        for gg in range(per_iter):
            mine = [gg * DN_HEADS + h for h in heads]
            outs = [[] for _ in heads]
            for j in range(DN_PACK):
                rows = slice(j * c, (j + 1) * c)
                state = [st_sc[h] for h in heads]
                lhs = [jnp.concatenate([qp[i][rows], mb[j][i][:, :dk].astype(BF16)], axis=0) for i in mine]
                r = [_dot(lhs[h], state[h].astype(BF16)) for h in heads]
                for h, i in zip(heads, mine):
                    gl = jnp.exp(glast[i][j * c:j * c + 1])
                    st_sc[h] = state[h] * gl - r[h][c:] + mb[j][i][:, dk:]
                    outs[h].append(r[h][:c] + qwu[i][rows, dk:])
            for h in heads:
                o = jnp.concatenate(outs[h], axis=0)
                on = o * lax.rsqrt(jnp.mean(o * o, axis=-1, keepdims=True) + EPS) * og
                zz = z_ref[0, pl.ds(r0[gg], gt), hs[h]].astype(F32)
                o_ref[0, pl.ds(r0[gg], gt), hs[h]] = (on * _silu(zz)).astype(o_ref.dtype)
        return carry

    lax.fori_loop(0, s // (gt * per_iter), group, 0)


def _deltanet(dn, z, col, row, onorm_g):
    b, s, w3 = dn.shape
    hd = DN_HEADS * DN_DK
    return pl.pallas_call(
        _deltanet_kernel,
        grid=(b,),
        in_specs=[pl.BlockSpec((1, s, w3), lambda i: (i, 0, 0)),
                  pl.BlockSpec((1, s, hd), lambda i: (i, 0, 0)),
                  pl.BlockSpec((1, s, LANES), lambda i: (i, 0, 0)),
                  pl.BlockSpec((1, s // DN_GROUP, DN_HEADS, DN_GROUP), lambda i: (i, 0, 0, 0)),
                  pl.BlockSpec((1, DN_DK), lambda i: (0, 0))],
        out_specs=pl.BlockSpec((1, s, hd), lambda i: (i, 0, 0)),
        out_shape=jax.ShapeDtypeStruct((b, s, hd), BF16),
        scratch_shapes=[pltpu.VMEM((DN_HEADS, DN_DK, DN_DK), F32)],
        compiler_params=_cparams(("arbitrary",)),
        name="deltanet",
    )(dn, z, col, row, onorm_g.reshape(1, DN_DK))


def _split3(x):
    hi = x.astype(BF16).astype(F32)
    r = x - hi
    mid = r.astype(BF16).astype(F32)
    return hi, mid, r - mid


def _fox_kernel(q_ref, k_ref, v_ref, colq_ref, colk_ref, qg_ref, kg_ref, o_ref, ka_sc, va_sc, m_sc, acc_sc, *, tq):
    qi = pl.program_id(1)
    s = k_ref.shape[1]
    dh = FOX_DH

    def bias_lanes(col):
        lane = lax.broadcasted_iota(jnp.int32, col.shape, 1)
        hi, mid, lo = _split3(col)
        return jnp.where((lane >= 8) & (lane < 8 + FOX_HEADS), hi,
                         jnp.where((lane >= 24) & (lane < 24 + FOX_HEADS), pltpu.roll(mid, 16, axis=1),
                                   jnp.where((lane >= 40) & (lane < 40 + FOX_HEADS), pltpu.roll(lo, 32, axis=1), 0.0)))

    @pl.when(qi == 0)
    def _():
        lane = lax.broadcasted_iota(jnp.int32, (s, LANES), 1)
        ones_col = jnp.where(lane == 0, 1.0, 0.0).astype(BF16)
        ones_k = (lane >= 56) & (lane < 96) & ((lane % 16) >= 8) & ((lane % 16) < 8 + FOX_HEADS)
        ext_k = (jnp.where(ones_k, 1.0, 0.0) - bias_lanes(colk_ref[0])).astype(BF16)
        for h in range(FOX_HEADS):
            cols = slice(h * dh, (h + 1) * dh)
            kf = k_ref[0, :, cols].astype(F32)
            kn = kf * lax.rsqrt(jnp.mean(kf * kf, axis=-1, keepdims=True) + EPS) * kg_ref[...]
            ka_sc[h, :, :dh] = kn.astype(BF16)
            ka_sc[h, :, dh:] = ext_k
            va_sc[h, :, :dh] = v_ref[0, :, cols]
            va_sc[h, :, dh:] = ones_col

    lane = lax.broadcasted_iota(jnp.int32, (tq, LANES), 1)
    cq = pltpu.roll(bias_lanes(colq_ref[0]), 48, axis=1)
    qa = []
    for h in range(FOX_HEADS):
        cols = slice(h * dh, (h + 1) * dh)
        qf = q_ref[0, :, cols].astype(F32)
        qn = qf * lax.rsqrt(jnp.mean(qf * qf, axis=-1, keepdims=True) + EPS) * qg_ref[...] * (dh ** -0.5)
        mine = ((lane % 16) == 8 + h) & (lane < 96)
        ext = jnp.where(mine, jnp.where(lane < 48, 1.0, cq), 0.0)
        qa.append(jnp.concatenate([qn.astype(BF16), ext.astype(BF16)], axis=1))
    m_sc[...] = jnp.full(m_sc.shape, NEG_INF, F32)
    acc_sc[...] = jnp.zeros_like(acc_sc)
    causal = lax.broadcasted_iota(jnp.int32, (tq, tq), 0) >= lax.broadcasted_iota(jnp.int32, (tq, tq), 1)

    def step(k0, masked):
        heads = range(FOX_HEADS)
        logits = [_dot_nt(qa[h], ka_sc[h, pl.ds(k0, tq), :]) for h in heads]
        if masked:
            logits = [jnp.where(causal, lg, NEG_INF) for lg in logits]
        ps, alphas = [], []
        for h in heads:
            m_old = m_sc[h]
            m_new = jnp.maximum(m_old, jnp.max(logits[h], axis=1, keepdims=True))
            m_sc[h] = m_new
            alphas.append(jnp.exp(m_old - m_new))
            ps.append(jnp.exp(logits[h] - jnp.concatenate([m_new] * (tq // LANES), axis=1)).astype(BF16))
        for h in heads:
            pv = _dot(ps[h], va_sc[h, pl.ds(k0, tq), :])
            acc_sc[h] = acc_sc[h] * jnp.concatenate([alphas[h], alphas[h]], axis=1) + pv

    def body(j, carry):
        step(pl.multiple_of(j * tq, tq), False)
        return carry

    lax.fori_loop(0, qi, body, 0)
    step(pl.multiple_of(qi * tq, tq), True)
    for h in range(FOX_HEADS):
        acc = acc_sc[h]
        o_ref[0, :, h * dh:(h + 1) * dh] = (acc[:, :dh] / acc[:, dh:dh + 1]).astype(o_ref.dtype)


def _fox(fox, col, qg, kg, tq):
    b, s, _ = fox.shape
    hd = FOX_HEADS * FOX_DH
    return pl.pallas_call(
        functools.partial(_fox_kernel, tq=tq),
        grid=(b, s // tq),
        in_specs=[pl.BlockSpec((1, tq, hd), lambda i, j: (i, j, 0)),
                  pl.BlockSpec((1, s, hd), lambda i, j: (i, 0, 1)),
                  pl.BlockSpec((1, s, hd), lambda i, j: (i, 0, 2)),
                  pl.BlockSpec((1, tq, LANES), lambda i, j: (i, j, 0)),
                  pl.BlockSpec((1, s, LANES), lambda i, j: (i, 0, 0)),
                  pl.BlockSpec((1, FOX_DH), lambda i, j: (0, 0)),
                  pl.BlockSpec((1, FOX_DH), lambda i, j: (0, 0))],
        out_specs=pl.BlockSpec((1, tq, hd), lambda i, j: (i, j, 0)),
        out_shape=jax.ShapeDtypeStruct((b, s, hd), BF16),
        scratch_shapes=[pltpu.VMEM((FOX_HEADS, s, 2 * FOX_DH), BF16), pltpu.VMEM((FOX_HEADS, s, 2 * FOX_DH), BF16),
                        pltpu.VMEM((FOX_HEADS, tq, LANES), F32), pltpu.VMEM((FOX_HEADS, tq, 2 * FOX_DH), F32)],
        compiler_params=_cparams(("arbitrary", "arbitrary")),
        name="fox",
    )(fox, fox, fox, col, col, qg.reshape(1, FOX_DH), kg.reshape(1, FOX_DH))


def _lru_kernel(x_ref, gate_ref, cw_ref, cb_ref, wr_ref, br_ref, wi_ref, bi_ref, lam_ref, o_ref, a_sc, b_sc):
    s = x_ref.shape[1]
    x = _causal_conv(x_ref[0].astype(F32), cw_ref) + cb_ref[...]
    xb = x.astype(BF16)
    r = _sigmoid(_dot(xb, wr_ref[0].astype(BF16)) + br_ref[...])
    i = _sigmoid(_dot(xb, wi_ref[0].astype(BF16)) + bi_ref[...])
    log_a = (-LRU_C) * r * _softplus(-lam_ref[...])
    a = jnp.exp(log_a)
    bb = jnp.sqrt(1.0 - a * a) * (i * x)
    nt = s // SUBLANES
    a3 = a.reshape(nt, SUBLANES, a.shape[1])
    b3 = bb.reshape(nt, SUBLANES, a.shape[1])
    sub = lax.broadcasted_iota(jnp.int32, a3.shape, 1)
    d = 1
    while d < SUBLANES:
        keep = sub >= d
        b3 = a3 * jnp.where(keep, pltpu.roll(b3, d, axis=1), 0.0) + b3
        a3 = a3 * jnp.where(keep, pltpu.roll(a3, d, axis=1), 1.0)
        d *= 2
    a_sc[...] = a3.reshape(s, a.shape[1])
    b_sc[...] = b3.reshape(s, a.shape[1])
    at = a_sc[pl.ds(SUBLANES - 1, nt, stride=SUBLANES), :]
    bt = b_sc[pl.ds(SUBLANES - 1, nt, stride=SUBLANES), :]
    d = 1
    while d < nt:
        bt = at * _shift_rows(bt, d, 0.0) + bt
        at = at * _shift_rows(at, d, 1.0)
        d *= 2
    h_prev = _shift_rows(bt, 1, 0.0)
    bb = (a3 * h_prev[:, None, :] + b3).reshape(s, a.shape[1])
    o_ref[0] = (bb * gate_ref[0].astype(F32)).astype(o_ref.dtype)


def _lru(lx, lg, conv_w, conv_b, wr, br, wi, bi, lam):
    b, s, wd = lx.shape
    blk = wd // LRU_BLOCKS
    vec = lambda i, j: (0, j)
    return pl.pallas_call(
        _lru_kernel,
        grid=(b, LRU_BLOCKS),
        in_specs=[pl.BlockSpec((1, s, blk), lambda i, j: (i, 0, j)),
                  pl.BlockSpec((1, s, blk), lambda i, j: (i, 0, j)),
                  pl.BlockSpec((CONV_WIDTH, blk), vec),
                  pl.BlockSpec((1, blk), vec),
                  pl.BlockSpec((1, blk, blk), lambda i, j: (j, 0, 0)),
                  pl.BlockSpec((1, blk), vec),
                  pl.BlockSpec((1, blk, blk), lambda i, j: (j, 0, 0)),
                  pl.BlockSpec((1, blk), vec),
                  pl.BlockSpec((1, blk), vec)],
        out_specs=pl.BlockSpec((1, s, blk), lambda i, j: (i, 0, j)),
        out_shape=jax.ShapeDtypeStruct((b, s, wd), BF16),
        scratch_shapes=[pltpu.VMEM((s, blk), F32), pltpu.VMEM((s, blk), F32)],
        compiler_params=_cparams(("arbitrary", "arbitrary")),
        name="rglru",
    )(lx, lg, conv_w, conv_b.reshape(1, wd), wr, br.reshape(1, wd), wi, bi.reshape(1, wd), lam.reshape(1, wd))


def _routing(logits, rb):
    ne, tm = logits.shape
    row = lax.broadcasted_iota(jnp.int32, (ne, tm), 0)
    row_f = row.astype(F32)
    ex = jnp.exp(logits - jnp.max(logits, axis=0, keepdims=True))
    probs = ex / jnp.sum(ex, axis=0, keepdims=True)
    sel = probs + rb

    def top2(vals, idx):
        m1 = jnp.max(vals, axis=0, keepdims=True)
        i1 = jnp.min(jnp.where(vals == m1, idx, float(ne)), axis=0, keepdims=True)
        rest = jnp.where(idx == i1, NEG_INF, vals)
        m2 = jnp.max(rest, axis=0, keepdims=True)
        return m1, i1, m2, rest

    best = None
    g_idx = None
    grp = row // EXPERTS_PER_GROUP
    for g in range(ne // EXPERTS_PER_GROUP):
        m1, _, m2, _ = top2(jnp.where(grp == g, sel, NEG_INF), row_f)
        score = m1 + m2
        if g == 0:
            best, g_idx = score, jnp.zeros((1, tm), jnp.int32)
        else:
            upd = score > best
            best = jnp.where(upd, score, best)
            g_idx = jnp.where(upd, g, g_idx)
    _, i1, m2, rest = top2(jnp.where(grp == g_idx, sel, NEG_INF), row_f)
    i2 = jnp.min(jnp.where(rest == m2, row_f, float(ne)), axis=0, keepdims=True)
    p1 = jnp.sum(jnp.where(row_f == i1, probs, 0.0), axis=0, keepdims=True)
    p2 = jnp.sum(jnp.where(row_f == i2, probs, 0.0), axis=0, keepdims=True)
    den = p1 + p2
    first_lo = i1 < i2
    a = jnp.where(first_lo, i1, i2) - EXPERTS_PER_GROUP * g_idx.astype(F32)
    bhi = jnp.where(first_lo, i2, i1) - EXPERTS_PER_GROUP * g_idx.astype(F32)
    pair = a * (7.0 - a) * 0.5 + (bhi - a - 1.0)
    pair = jnp.where(a == 1.0, 7.0 - pair, pair)
    cls = PAIRS_PER_GROUP * g_idx.astype(F32) + pair
    w_lo = jnp.where(first_lo, p1, p2) / den
    w_hi = jnp.where(first_lo, p2, p1) / den
    swap = pair == 5.0
    return cls, jnp.where(swap, w_hi, w_lo), jnp.where(swap, w_lo, w_hi)


PAIRS_PER_GROUP = EXPERTS_PER_GROUP * (EXPERTS_PER_GROUP - 1) // 2
N_CLASSES = (N_EXPERTS // EXPERTS_PER_GROUP) * PAIRS_PER_GROUP
PAIR_WALK = [(0, 1), (0, 2), (0, 3), (1, 3), (1, 2), (3, 2)]
assert EXPERTS_PER_GROUP == 4 and len(PAIR_WALK) == PAIRS_PER_GROUP
CLASS_E0 = [EXPERTS_PER_GROUP * (c // PAIRS_PER_GROUP) + PAIR_WALK[c % PAIRS_PER_GROUP][0] for c in range(N_CLASSES)]
CLASS_E1 = [EXPERTS_PER_GROUP * (c // PAIRS_PER_GROUP) + PAIR_WALK[c % PAIRS_PER_GROUP][1] for c in range(N_CLASSES)]
CLASS_ROWS = 32
ROW_FEAT = 512
ROW_WORDS = 640


def _pack_halves(x):
    kk = x.shape[1] // 2
    lo = lax.bitcast_convert_type(x[:, :kk].astype(BF16).astype(F32), jnp.uint32)
    hi = lax.bitcast_convert_type(x[:, kk:].astype(BF16).astype(F32), jnp.uint32)
    return lax.bitcast_convert_type((lo >> 16) | hi, jnp.int32)


def _unpack_halves(p):
    u = lax.bitcast_convert_type(p, jnp.uint32)
    lo = lax.bitcast_convert_type(u << 16, F32)
    hi = lax.bitcast_convert_type(u & jnp.uint32(0xFFFF0000), F32)
    return jnp.concatenate([lo.astype(BF16), hi.astype(BF16)], axis=1)


def _outproj_kernel(a_ref, b_ref, x_ref, wf_ref, gt_ref, g2_ref, sc_ref, sh_ref, rwt_ref, rb_ref,
                    x1_ref, rows_ref, route_ref, cnt_ref, carry_sc, w_ref, tri_sc):
    first = (pl.program_id(0) == 0) & (pl.program_id(1) == 0)

    @pl.when(first)
    def _():
        carry_sc[...] = jnp.zeros_like(carry_sc)
        w_ref[...] = wf_ref[...].astype(BF16)
        earlier = (lax.broadcasted_iota(jnp.int32, tri_sc.shape, 0) < lax.broadcasted_iota(jnp.int32, tri_sc.shape, 1))
        tri_sc[...] = jnp.where(earlier, 1.0, 0.0).astype(BF16)

    half = a_ref.shape[2]
    tm = a_ref.shape[1]
    y = _dot(a_ref[0], w_ref[:half, :]) + _dot(b_ref[0], w_ref[half:, :])
    x1 = x_ref[0] + gt_ref[0] * y
    x1_ref[0] = x1
    h2 = _prenorm(x1, g2_ref[...], sc_ref[0], sh_ref[0])
    rows_ref[0, :, :ROW_FEAT] = _pack_halves(h2)
    cls, w_lo, w_hi = _routing(_dot_nt(rwt_ref[...], h2.astype(BF16)), rb_ref[...])
    r128 = lax.broadcasted_iota(jnp.int32, (LANES, tm), 0)
    wts = jnp.where(r128 == 0, w_lo, jnp.where(r128 == 1, w_hi, 0.0))
    rows_ref[0, :, ROW_FEAT:] = lax.bitcast_convert_type(wts.T[:, :ROW_WORDS - ROW_FEAT], jnp.int32)
    crow = lax.broadcasted_iota(jnp.int32, (carry_sc.shape[0], tm), 0).astype(F32)
    onehot = jnp.where(crow == cls, 1.0, 0.0)
    prefix = _dot(onehot.astype(BF16), tri_sc[...]) + carry_sc[:, 0:1]
    rank = jnp.sum(onehot * prefix, axis=0, keepdims=True)
    r8 = lax.broadcasted_iota(jnp.int32, (8, tm), 0)
    route_ref[0] = jnp.where(r8 == 0, cls, jnp.where(r8 == 1, rank, 0.0))
    carry_sc[...] = carry_sc[...] + jnp.sum(onehot, axis=1, keepdims=True)
    cnt_ref[...] = carry_sc[...]


def _outproj(a, bb, x, w, gt, g2, sc2, sh2, rw, rb, tm):
    b, s, d = x.shape
    half = a.shape[2]
    row = lambda i, j: (i, 0, 0)
    tok = lambda i, j: (i, j, 0)
    const = lambda i, j: (0, 0)
    return pl.pallas_call(
        _outproj_kernel,
        grid=(b, s // tm),
        in_specs=[pl.BlockSpec((1, tm, half), tok),
                  pl.BlockSpec((1, tm, half), tok),
                  pl.BlockSpec((1, tm, d), tok),
                  pl.BlockSpec((2 * half, d), const),
                  pl.BlockSpec((1, 1, d), row),
                  pl.BlockSpec((1, d), const),
                  pl.BlockSpec((1, 1, d), row),
                  pl.BlockSpec((1, 1, d), row),
                  pl.BlockSpec((N_EXPERTS, d), const),
                  pl.BlockSpec((N_EXPERTS, 1), const)],
        out_specs=(pl.BlockSpec((1, tm, d), tok),
                   pl.BlockSpec((1, tm, ROW_WORDS), tok),
                   pl.BlockSpec((1, 8, tm), lambda i, j: (i, 0, j)),
                   pl.BlockSpec((CLASS_ROWS, LANES), const)),
        out_shape=(jax.ShapeDtypeStruct((b, s, d), F32),
                   jax.ShapeDtypeStruct((b, s, ROW_WORDS), jnp.int32),
                   jax.ShapeDtypeStruct((b, 8, s), F32),
                   jax.ShapeDtypeStruct((CLASS_ROWS, LANES), F32)),
        scratch_shapes=[pltpu.VMEM((CLASS_ROWS, LANES), F32), pltpu.VMEM((2 * half, d), BF16),
                        pltpu.VMEM((tm, tm), BF16)],
        compiler_params=_cparams(("arbitrary", "arbitrary")),
        name="outproj_router",
    )(a, bb, x, w, gt, g2.reshape(1, d), sc2, sh2, rw, rb)


MOE_ROWS = 512
SC_CHUNK = 128


def _sc_workers():
    info = plsc.get_sparse_core_info()
    return info.num_cores, info.num_cores * info.num_subcores


def _sc_scatter_rows(rows, idx3, n_out):
    nw, k, ch = idx3.shape
    width = rows.shape[1]
    nc, _ = _sc_workers()
    mesh = plsc.VectorSubcoreMesh(core_axis_name="c", subcore_axis_name="s")

    @functools.partial(
        pl.kernel, mesh=mesh,
        out_type=jax.ShapeDtypeStruct((n_out, width), rows.dtype),
        scratch_types=[pltpu.VMEM((k, ch), jnp.int32), pltpu.VMEM((ch, width), rows.dtype), pltpu.SemaphoreType.DMA],
        name="moe_dispatch")
    def kern(rows_hbm, idx_hbm, out_hbm, idx_v, rows_v, sem):
        wid = lax.axis_index("s") * nc + lax.axis_index("c")
        pltpu.sync_copy(idx_hbm.at[wid], idx_v)

        @pl.loop(0, k)
        def _(j):
            pltpu.sync_copy(rows_hbm.at[pl.ds((wid * k + j) * ch, ch)], rows_v)
            pltpu.async_copy(rows_v, out_hbm.at[idx_v.at[j]], sem).wait()

    return kern(rows, idx3)


def _sc_gather_rows(table, idx3):
    nw, k, ch = idx3.shape
    width = table.shape[1]
    nc, _ = _sc_workers()
    mesh = plsc.VectorSubcoreMesh(core_axis_name="c", subcore_axis_name="s")

    @functools.partial(
        pl.kernel, mesh=mesh,
        out_type=jax.ShapeDtypeStruct((nw * k * ch, width), table.dtype),
        scratch_types=[pltpu.VMEM((k, ch), jnp.int32), pltpu.VMEM((ch, width), table.dtype), pltpu.SemaphoreType.DMA],
        name="moe_combine")
    def kern(table_hbm, idx_hbm, out_hbm, idx_v, rows_v, sem):
        wid = lax.axis_index("s") * nc + lax.axis_index("c")
        pltpu.sync_copy(idx_hbm.at[wid], idx_v)

        @pl.loop(0, k)
        def _(j):
            pltpu.async_copy(table_hbm.at[idx_v.at[j]], rows_v, sem).wait()
            pltpu.sync_copy(rows_v, out_hbm.at[pl.ds((wid * k + j) * ch, ch)])

    return kern(table, idx3)


def _experts_kernel(e0_ref, e1_ref, nused_ref, x_ref, wg0_ref, wu0_ref, wd0_ref, wg1_ref, wu1_ref, wd1_ref, y_ref):
    @pl.when(pl.program_id(0) < nused_ref[0])
    def _():
        blk = x_ref[...]
        feat = ROW_FEAT
        h = _unpack_halves(blk[:, :feat])
        wts = lax.bitcast_convert_type(blk[:, feat:], F32)
        y = None
        for col, (wg_ref, wu_ref, wd_ref) in enumerate(((wg0_ref, wu0_ref, wd0_ref), (wg1_ref, wu1_ref, wd1_ref))):
            act = (_silu(_dot(h, wg_ref[0, 0].astype(BF16))) * _dot(h, wu_ref[0, 0].astype(BF16))
                   * wts[:, col:col + 1])
            part = _dot(act.astype(BF16), wd_ref[0, 0].astype(BF16))
            y = part if y is None else y + part
        y_ref[...] = _pack_halves(y)


def _experts(xs, blk_e0, blk_e1, nused, wg, wu, wd, layer):
    n_rows = xs.shape[0]
    _, ne, d, f = wg.shape
    nblk = n_rows // MOE_ROWS
    rows = lambda i, e0, e1, nu: (jnp.minimum(i, nu[0] - 1), 0)
    lo = lambda i, e0, e1, nu: (layer, e0[i], 0, 0)
    hi = lambda i, e0, e1, nu: (layer, e1[i], 0, 0)
    return pl.pallas_call(
        _experts_kernel,
        grid_spec=pltpu.PrefetchScalarGridSpec(
            num_scalar_prefetch=3,
            grid=(nblk,),
            in_specs=[pl.BlockSpec((MOE_ROWS, ROW_WORDS), rows),
                      pl.BlockSpec((1, 1, d, f), lo), pl.BlockSpec((1, 1, d, f), lo), pl.BlockSpec((1, 1, f, d), lo),
                      pl.BlockSpec((1, 1, d, f), hi), pl.BlockSpec((1, 1, d, f), hi), pl.BlockSpec((1, 1, f, d), hi)],
            out_specs=pl.BlockSpec((MOE_ROWS, d // 2), lambda i, e0, e1, nu: (i, 0))),
        out_shape=jax.ShapeDtypeStruct((n_rows, d // 2), jnp.int32),
        compiler_params=_cparams(("arbitrary",)),
        name="moe_experts",
    )(blk_e0, blk_e1, nused, xs, wg, wu, wd, wg, wu, wd)


def _residual_kernel(x_ref, m_ref, gt_ref, o_ref):
    o_ref[0] = x_ref[0] + gt_ref[0] * _unpack_halves(m_ref[0]).astype(F32)


def _residual(x1, moe_p, gt, tm):
    b, s, d = x1.shape
    tok = lambda i, j: (i, j, 0)
    return pl.pallas_call(
        _residual_kernel,
        grid=(b, s // tm),
        in_specs=[pl.BlockSpec((1, tm, d), tok), pl.BlockSpec((1, tm, d // 2), tok),
                  pl.BlockSpec((1, 1, d), lambda i, j: (i, 0, 0))],
        out_specs=pl.BlockSpec((1, tm, d), tok),
        out_shape=jax.ShapeDtypeStruct((b, s, d), F32),
        compiler_params=_cparams(("arbitrary", "arbitrary")),
        name="moe_residual",
    )(x1, moe_p, gt)


def _moe(rows, route, counts, wg, wu, wd, layer):
    b, s, _ = rows.shape
    d = wg.shape[2]
    t = b * s
    _, nw = _sc_workers()
    cnt = counts[:N_CLASSES, 0].astype(jnp.int32)
    padded = ((cnt + MOE_ROWS - 1) // MOE_ROWS) * MOE_ROWS
    ends = jnp.cumsum(padded)
    n_rows = t + N_CLASSES * MOE_ROWS
    nblk = n_rows // MOE_ROWS
    cls = route[:, 0, :].reshape(t).astype(jnp.int32)
    rank = route[:, 1, :].reshape(t).astype(jnp.int32)
    pos = rank + jnp.sum(jnp.where(cls[:, None] > jnp.arange(N_CLASSES, dtype=jnp.int32)[None, :], padded[None, :], 0),
                         axis=1)
    idx3 = pos.reshape(nw, t // (nw * SC_CHUNK), SC_CHUNK)
    nused = (ends[-1] // MOE_ROWS).reshape(1)
    blk_cls = jnp.sum((jnp.arange(nblk, dtype=jnp.int32)[:, None] * MOE_ROWS) >= ends[None, :], axis=1)
    blk_cls = jnp.minimum(blk_cls, blk_cls[jnp.maximum(nused[0] - 1, 0)])
    blk_e0 = jnp.asarray(CLASS_E0, jnp.int32)[blk_cls]
    blk_e1 = jnp.asarray(CLASS_E1, jnp.int32)[blk_cls]
    xs = _sc_scatter_rows(rows.reshape(t, ROW_WORDS), idx3, n_rows)
    ys = _experts(xs, blk_e0, blk_e1, nused, wg, wu, wd, layer)
    return _sc_gather_rows(ys, idx3).reshape(b, s, d // 2)


def kernel(x, c, ada_w, ada_b, norm1_g, norm2_g, ev_w_in, ev_conv_w, ev_dn_a_log, ev_dn_dt_bias, ev_dn_onorm_g, ev_fox_f_bias, ev_fox_qnorm_g, ev_fox_knorm_g, ev_w_out, od_w_in, od_conv_w, od_conv_b, od_lru_wr, od_lru_br, od_lru_wi, od_lru_bi, od_lru_lambda, od_sgu_norm_g, od_sgu_w, od_sgu_b, od_w_out, router_w, router_b, moe_w_gate, moe_w_up, moe_w_down):
    b, s, d = x.shape
    depth = ada_w.shape[0]
    tm = min(TOKEN_BLOCK, s)
    mod = _adaln(c, ada_w, ada_b).reshape(depth, b, 6, 1, d)
    rw = router_w.T.astype(BF16)
    rb = router_b.reshape(N_EXPERTS, 1)

    pending = None
    for layer in range(depth):
        sh1, sc1, gt1, sh2, sc2, gt2 = (mod[layer, :, k] for k in range(6))
        i = layer // 2
        if layer % 2 == 0:
            w = ev_w_in[i]
            nq = 3 * DN_HEADS * DN_DK
            nz = DN_HEADS * DN_DK
            nf = 3 * FOX_HEADS * FOX_DH
            o_a = nq + nz
            o_f = o_a + 2 * DN_HEADS
            o_ff = o_f + nf
            splits = ((0, nq), (nq, o_a), (o_a, o_a + nf), (o_a + nf, o_a + nf + LANES))
            w_cols = ((0, o_a, 0), (o_f, o_ff, o_a), (o_a, o_f, o_a + nf), (o_ff, o_ff + FOX_HEADS, o_a + nf + o_f - o_a))
            res = _inproj(x, norm1_g[layer], sc1, sh1, w, splits,
                          (BF16, BF16, BF16, F32), (None,) * 4, tm, pending, ev_conv_w[i], w_cols=w_cols)
            if pending is not None:
                x, res = res[0], res[1:]
            dn, z, fox, small = res
            small_t = jnp.swapaxes(small[:, :, :16], 1, 2)
            zeros4 = jnp.zeros((4,), F32)
            mul = jnp.concatenate([-jnp.exp(ev_dn_a_log[i]), zeros4, -jnp.ones((4,), F32), zeros4])
            bias = jnp.concatenate([ev_dn_dt_bias[i], zeros4, ev_fox_f_bias[i], zeros4])
            pr = jnp.stack([mul, bias], axis=1)
            pc = jnp.pad(jnp.stack([mul, bias], axis=0), ((0, 0), (0, LANES - 16)))
            col, row = _gates(small, small_t, pc, pr)
            row_dn = row[:, :DN_HEADS].reshape(b, DN_HEADS, s // DN_GROUP, DN_GROUP).transpose(0, 2, 1, 3)
            o_dn = _deltanet(dn, z, col, row_dn, ev_dn_onorm_g[i])
            o_fox = _fox(fox, col, ev_fox_qnorm_g[i], ev_fox_knorm_g[i], min(FOX_QUERY_BLOCK, s))
            mix_a, mix_b, w_out = o_dn, o_fox, ev_w_out[i]
        else:
            lw = od_lru_wr.shape[-1] * LRU_BLOCKS
            splits = ((0, lw), (lw, 2 * lw), (2 * lw, 2 * lw + od_sgu_w.shape[-1] * SGU_GROUPS),
                      (2 * lw + od_sgu_w.shape[-1] * SGU_GROUPS, od_w_in.shape[-1]))
            res = _inproj(x, norm1_g[layer], sc1, sh1, od_w_in[i], splits,
                          (BF16,) * 3, (None, "gelu"), min(ODD_TOKEN_BLOCK, s), pending,
                          sgu=(od_sgu_norm_g[i], od_sgu_w[i], od_sgu_b[i]))
            if pending is not None:
                x, res = res[0], res[1:]
            lx, lg, o_sgu = res
            o_lru = _lru(lx, lg, od_conv_w[i], od_conv_b[i], od_lru_wr[i], od_lru_br[i], od_lru_wi[i],
                         od_lru_bi[i], od_lru_lambda[i])
            mix_a, mix_b, w_out = o_lru, o_sgu, od_w_out[i]
        x1, rows, route, counts = _outproj(mix_a, mix_b, x, w_out, gt1, norm2_g[layer], sc2, sh2,
                                           rw, rb, min(ROUTER_BLOCK, s))
        x = x1
        pending = (_moe(rows, route, counts, moe_w_gate, moe_w_up, moe_w_down, layer), gt2)
    return _residual(x, *pending, min(ROUTER_BLOCK, s))
```

```python
import functools

import jax
import jax.numpy as jnp
from jax import lax
from jax.experimental import pallas as pl
from jax.experimental.pallas import tpu as pltpu
from jax.experimental.pallas import tpu_sc as plsc

F32 = jnp.float32
BF16 = jnp.bfloat16
EPS = 1e-6
NEG_INF = float("-inf")

DN_HEADS = 4
DN_DK = 128
DN_CHUNK = 64
CONV_WIDTH = 4
FOX_HEADS = 4
FOX_DH = 128
LRU_BLOCKS = 4
LRU_C = 8.0
SGU_GROUPS = 4
SGU_CHUNK = 128
N_EXPERTS = 16
EXPERTS_PER_GROUP = 4
LANES = 128
SUBLANES = 8

VMEM_LIMIT = 48 * 1024 * 1024
TOKEN_BLOCK = 512
ROUTER_BLOCK = 1024
FOX_QUERY_BLOCK = 512


def _cparams(sem):
    return pltpu.CompilerParams(dimension_semantics=sem, vmem_limit_bytes=VMEM_LIMIT)


def _dot(a, b):
    return jnp.dot(a, b, preferred_element_type=F32)


def _dot_nt(a, b):
    return lax.dot_general(a, b, (((1,), (1,)), ((), ())), preferred_element_type=F32)


def _dot_tn(a, b):
    return lax.dot_general(a, b, (((0,), (0,)), ((), ())), preferred_element_type=F32)


def _dot_ones(a, b):
    if a.dtype == BF16:
        return sum(_dot(a, t.astype(BF16)) for t in _split3(b))
    return sum(_dot(t.astype(BF16), b) for t in _split3(a))


def _sigmoid(x):
    return 0.5 + 0.5 * jnp.tanh(0.5 * x)


def _silu(x):
    hx = 0.5 * x
    return hx + hx * jnp.tanh(hx)


def _softplus(x):
    return jnp.maximum(x, 0.0) + jnp.log(1.0 + jnp.exp(-jnp.abs(x)))


def _gelu_tanh(x):
    c = 0.7978845608028654
    return 0.5 * x * (1.0 + jnp.tanh(c * (x + 0.044715 * (x * x * x))))


def _prenorm(x, g, scale, shift):
    ms = jnp.mean(x * x, axis=-1, keepdims=True)
    return (x * lax.rsqrt(ms + EPS) * g) * (1.0 + scale) + shift


def _shift_rows(x, d, fill=0.0):
    rows = lax.broadcasted_iota(jnp.int32, x.shape, 0)
    return jnp.where(rows >= d, pltpu.roll(x, d, axis=0), fill)


def _causal_conv(x, w_ref):
    k_w = w_ref.shape[0]
    acc = x * w_ref[k_w - 1:k_w, :]
    for d in range(1, k_w):
        acc = acc + _shift_rows(x, d) * w_ref[k_w - 1 - d:k_w - d, :]
    return acc


def _adaln_kernel(c_ref, w_ref, b_ref, o_ref):
    c = c_ref[...]
    ca = _silu(c).astype(BF16)
    o_ref[0] = _dot(ca, w_ref[0].astype(BF16)) + b_ref[0]


def _adaln(c, ada_w, ada_b):
    depth, d, n = ada_w.shape
    b = c.shape[0]
    tn = 1536
    return pl.pallas_call(
        _adaln_kernel,
        grid=(depth, n // tn),
        in_specs=[pl.BlockSpec((b, d), lambda l, j: (0, 0)),
                  pl.BlockSpec((1, d, tn), lambda l, j: (l, 0, j)),
                  pl.BlockSpec((1, 1, tn), lambda l, j: (l, 0, j))],
        out_specs=pl.BlockSpec((1, b, tn), lambda l, j: (l, 0, j)),
        out_shape=jax.ShapeDtypeStruct((depth, b, n), F32),
        compiler_params=_cparams(("arbitrary", "arbitrary")),
        name="adaln",
    )(c, ada_w, ada_b.reshape(depth, 1, n))


CONV_HALO = 8
CONV_SLAB = 512


def _inproj_kernel(*refs, col_splits, acts, pending, conv, sgu, cast_w):
    refs = list(refs)
    x_ref = refs.pop(0)
    if pending:
        m_ref, gtm_ref = refs.pop(0), refs.pop(0)
    g_ref, sc_ref, sh_ref, w_ref = (refs.pop(0) for _ in range(4))
    if cast_w:
        w_f32, w_ref = w_ref, refs.pop()

        @pl.when((pl.program_id(0) == 0) & (pl.program_id(1) == 0))
        def _():
            if sum(s1 - s0 for s0, s1, _ in cast_w) < w_ref.shape[1]:
                w_ref[...] = jnp.zeros_like(w_ref)
            for s0, s1, d0 in cast_w:
                w_ref[:, d0:d0 + s1 - s0] = w_f32[:, s0:s1].astype(BF16)
    if conv:
        cw_ref = refs.pop(0)
        halo_sc = refs.pop()
    if sgu:
        sg_ref, sw_ref, sbt_ref = (refs.pop(0) for _ in range(3))
    if pending:
        xo_ref = refs.pop(0)
        x = x_ref[0] + gtm_ref[0] * _unpack_halves(m_ref[0]).astype(F32)
        xo_ref[0] = x
    else:
        x = x_ref[0]
    out_refs = refs
    h = _prenorm(x, g_ref[...], sc_ref[0], sh_ref[0]).astype(BF16)
    tm = x.shape[0]
    if conv:
        @pl.when(pl.program_id(1) == 0)
        def _():
            halo_sc[0:CONV_HALO, :] = jnp.zeros((CONV_HALO, halo_sc.shape[1]), F32)

        c0, c1 = col_splits[0]
        for s0 in range(c0, c1, CONV_SLAB):
            halo_sc[CONV_HALO:, s0 - c0:s0 - c0 + CONV_SLAB] = _dot(h, w_ref[:, s0:s0 + CONV_SLAB])
    n_plain = len(col_splits) - (2 if sgu else 0)
    for k, (o_ref, (c0, c1), act) in enumerate(zip(out_refs[:n_plain], col_splits[:n_plain], acts)):
        if conv and k == 0:
            continue
        p = _dot(h, w_ref[:, c0:c1])
        if act == "gelu":
            p = _gelu_tanh(p)
        o_ref[0] = p.astype(o_ref.dtype)
    if sgu:
        (u0, u1), (v0, v1) = col_splits[n_plain:]
        o_ref = out_refs[n_plain]
        u = _gelu_tanh(_dot(h, w_ref[:, u0:u1]))
        v = _gelu_tanh(_dot(h, w_ref[:, v0:v1]))
        vn = (v * lax.rsqrt(jnp.mean(v * v, axis=-1, keepdims=True) + EPS) * sg_ref[...]).astype(BF16)
        c = SGU_CHUNK
        gw = (u1 - u0) // SGU_GROUPS
        ri = lax.broadcasted_iota(jnp.int32, (c, c), 0)
        ci = lax.broadcasted_iota(jnp.int32, (c, c), 1)
        for g in range(SGU_GROUPS):
            wg = jnp.where(ri >= ci, sw_ref[g], 0.0).astype(BF16)
            bcol = sbt_ref[:, g:g + 1]
            for n in range(tm // c):
                rows = slice(n * c, (n + 1) * c)
                cols = slice(g * gw, (g + 1) * gw)
                mixed = _dot(wg, vn[rows, cols]) + bcol
                o_ref[0, rows, cols] = (u[rows, cols] * mixed).astype(o_ref.dtype)
    if conv:
        o_ref = out_refs[0]
        for t0 in range(0, halo_sc.shape[1], DN_DK):
            cols = slice(t0, t0 + DN_DK)
            acc = halo_sc[CONV_HALO:, cols] * cw_ref[CONV_WIDTH - 1:CONV_WIDTH, cols]
            for dd in range(1, CONV_WIDTH):
                acc = acc + halo_sc[pl.ds(CONV_HALO - dd, tm), cols] * cw_ref[CONV_WIDTH - 1 - dd:CONV_WIDTH - dd, cols]
            halo_sc[0:CONV_HALO, cols] = halo_sc[tm:tm + CONV_HALO, cols]
            yj = _silu(acc)
            if t0 < 2 * DN_HEADS * DN_DK:
                yj = yj * lax.rsqrt(jnp.sum(yj * yj, axis=-1, keepdims=True) + EPS)
            if t0 < DN_HEADS * DN_DK:
                yj = yj * (DN_DK ** -0.5)
            o_ref[0, :, cols] = yj.astype(o_ref.dtype)


def _inproj(x, g, scale, shift, w, col_splits, out_dtypes, acts, tm, pending=None, conv_w=None, sgu=None,
            w_cols=None):
    b, s, d = x.shape
    n = w.shape[1] if w_cols is None else col_splits[-1][1]
    if w.dtype != BF16 and w_cols is None:
        w_cols = ((0, n, 0),)
    out_cols = list(col_splits) if sgu is None else list(col_splits[:-2]) + [col_splits[-2]]
    outs = tuple(jax.ShapeDtypeStruct((b, s, c1 - c0), dt) for (c0, c1), dt in zip(out_cols, out_dtypes))
    row = lambda i, j: (i, 0, 0)
    tok = lambda i, j: (i, j, 0)
    const = lambda i, j: (0, 0)
    in_specs = [pl.BlockSpec((1, tm, d), tok)]
    out_specs = tuple(pl.BlockSpec((1, tm, c1 - c0), tok) for (c0, c1) in out_cols)
    args = (x,)
    if pending is not None:
        in_specs += [pl.BlockSpec((1, tm, d // 2), tok), pl.BlockSpec((1, 1, d), row)]
        out_specs = (pl.BlockSpec((1, tm, d), tok),) + out_specs
        outs = (jax.ShapeDtypeStruct((b, s, d), F32),) + outs
        args += tuple(pending)
    in_specs += [pl.BlockSpec((1, d), const), pl.BlockSpec((1, 1, d), row), pl.BlockSpec((1, 1, d), row),
                 pl.BlockSpec(w.shape, const, pipeline_mode=pl.Buffered(1)) if w_cols else pl.BlockSpec((d, n), const)]
    args += (g.reshape(1, d), scale, shift, w)
    scratch = []
    if conv_w is not None:
        in_specs.append(pl.BlockSpec(conv_w.shape, const))
        args += (conv_w,)
        scratch.append(pltpu.VMEM((CONV_HALO + tm, conv_w.shape[1]), F32))
    if w_cols:
        scratch.append(pltpu.VMEM((d, n), BF16))
    if sgu is not None:
        g_norm, w_s, b_s = sgu
        in_specs += [pl.BlockSpec((1, g_norm.shape[0]), const), pl.BlockSpec(w_s.shape, lambda i, j: (0, 0, 0)),
                     pl.BlockSpec((b_s.shape[1], b_s.shape[0]), const)]
        args += (g_norm.reshape(1, -1), w_s, b_s.T)
    return pl.pallas_call(
        functools.partial(_inproj_kernel, col_splits=col_splits, acts=acts, pending=pending is not None,
                          conv=conv_w is not None, sgu=sgu is not None, cast_w=w_cols),
        grid=(b, s // tm),
        in_specs=in_specs,
        out_specs=out_specs,
        out_shape=outs,
        scratch_shapes=scratch,
        compiler_params=_cparams(("arbitrary", "arbitrary")),
        name="inproj",
    )(*args)


def _gates_kernel(sm_ref, smt_ref, pc_ref, pr_ref, col_ref, row_ref):
    s = sm_ref.shape[1]
    blk = LANES
    ri = lax.broadcasted_iota(jnp.int32, (blk, blk), 0)
    ci = lax.broadcasted_iota(jnp.int32, (blk, blk), 1)
    same_chunk = (ri // DN_CHUNK) == (ci // DN_CHUNK)
    tril = jnp.where(ri >= ci, 1.0, 0.0).astype(BF16)
    tril_loc = jnp.where(same_chunk & (ri >= ci), 1.0, 0.0).astype(BF16)
    triu_loc = jnp.where(same_chunk & (ri <= ci), 1.0, 0.0).astype(BF16)
    lane = lax.broadcasted_iota(jnp.int32, (blk, LANES), 1)
    carry = jnp.zeros((1, LANES), F32)
    for j in range(s // blk):
        rows = slice(j * blk, (j + 1) * blk)
        xc = sm_ref[0, rows, :] + pc_ref[1:2, :]
        dec = pc_ref[0:1, :] * _softplus(jnp.where(lane < 4, xc, -xc))
        cum_glb = _dot_ones(tril, dec) + carry
        col_ref[0, rows, :] = jnp.where(lane < 4, _dot_ones(tril_loc, dec), jnp.where(lane < 8, _sigmoid(xc), cum_glb))
        carry = cum_glb[blk - 1:blk, :]
        decr = pr_ref[:, 0:1] * _softplus(smt_ref[0, :, rows] + pr_ref[:, 1:2])
        row_ref[0, :, rows] = _dot_ones(decr, triu_loc)


def _gates(small, small_t, pc, pr):
    b, s, _ = small.shape
    return pl.pallas_call(
        _gates_kernel,
        grid=(b,),
        in_specs=[pl.BlockSpec((1, s, LANES), lambda i: (i, 0, 0)),
                  pl.BlockSpec((1, 16, s), lambda i: (i, 0, 0)),
                  pl.BlockSpec((2, LANES), lambda i: (0, 0)),
                  pl.BlockSpec((16, 2), lambda i: (0, 0))],
        out_specs=(pl.BlockSpec((1, s, LANES), lambda i: (i, 0, 0)),
                   pl.BlockSpec((1, 16, s), lambda i: (i, 0, 0))),
        out_shape=(jax.ShapeDtypeStruct((b, s, LANES), F32), jax.ShapeDtypeStruct((b, 16, s), F32)),
        compiler_params=_cparams(("arbitrary",)),
        name="gates",
    )(small, small_t, pc, pr)


DN_PACK = 4
DN_GROUP = DN_PACK * DN_CHUNK
DN_ITER_GROUPS = 4


def _blockdiag(p):
    c, wide = p.shape
    t = jnp.concatenate([p] * (wide // c), axis=0)
    rb = lax.broadcasted_iota(jnp.int32, (wide, wide), 0) // c
    cb = lax.broadcasted_iota(jnp.int32, (wide, wide), 1) // c
    return jnp.where(rb == cb, t, 0.0).astype(BF16)


def _diag_blocks(m, c):
    wide = m.shape[1]
    cb = lax.broadcasted_iota(jnp.int32, (c, wide), 1) // c
    out = m[:c]
    for j in range(1, wide // c):
        out = jnp.where(cb == j, m[j * c:(j + 1) * c], out)
    return out


def _rows_to_blocks(col, c, wide):
    cb = lax.broadcasted_iota(jnp.int32, (c, wide), 1) // c
    out = jnp.broadcast_to(col[:c], (c, wide))
    for j in range(1, wide // c):
        out = jnp.where(cb == j, col[j * c:(j + 1) * c], out)
    return out


def _packed_unit_lower_inverse(lows):
    c, wide = lows[0].shape
    ri = lax.broadcasted_iota(jnp.int32, (c, wide), 0)
    ci = lax.broadcasted_iota(jnp.int32, (c, wide), 1) % c
    eye = jnp.where(ri == ci, 1.0, 0.0)
    n = range(len(lows))
    ds = [jnp.where((ri // 16) == (ci // 16), lo, 0.0) for lo in lows]
    xs = [eye - d for d in ds]
    ps = ds
    bds = [_blockdiag(p) for p in ps]
    for _ in range(3):
        ps = [_dot(ps[i].astype(BF16), bds[i]) for i in n]
        bds = [_blockdiag(p) for p in ps]
        xs = [xs[i] + _dot(xs[i].astype(BF16), bds[i]) for i in n]
    for width in (16, 32):
        sel = ((ri // (2 * width)) == (ci // (2 * width))) & ((ri // width) != (ci // width))
        offs = [_blockdiag(jnp.where(sel, lo, 0.0)) for lo in lows]
        ts = [_dot(xs[i].astype(BF16), offs[i]) for i in n]
        bdx = [_blockdiag(x) for x in xs]
        xs = [xs[i] - _dot(ts[i].astype(BF16), bdx[i]) for i in n]
    return xs


def _deltanet_kernel(dn_ref, z_ref, col_ref, row_ref, og_ref, o_ref, st_sc):
    s = dn_ref.shape[1]
    c = DN_CHUNK
    gt = DN_GROUP
    dk = DN_DK
    hd = DN_HEADS * dk
    heads = range(DN_HEADS)
    st_sc[...] = jnp.zeros_like(st_sc)

    ri = lax.broadcasted_iota(jnp.int32, (c, gt), 0)
    ci = lax.broadcasted_iota(jnp.int32, (c, gt), 1) % c
    og = og_ref[...]

    per_iter = DN_ITER_GROUPS if (s // gt) % DN_ITER_GROUPS == 0 else 1
    insts = [(gg, h) for gg in range(per_iter) for h in heads]
    ins = range(len(insts))

    def group(g, carry):
        r0 = [pl.multiple_of((g * per_iter + gg) * gt, gt) for gg in range(per_iter)]
        colg = [col_ref[0, pl.ds(r0[gg], gt), :] for gg in range(per_iter)]
        rowg = [row_ref[0, g * per_iter + gg] for gg in range(per_iter)]
        hs = [slice(h * dk, (h + 1) * dk) for h in heads]
        q = [dn_ref[0, pl.ds(r0[gg], gt), h * dk:(h + 1) * dk].astype(F32) for gg, h in insts]
        k = [dn_ref[0, pl.ds(r0[gg], gt), hd + h * dk:hd + (h + 1) * dk].astype(F32) for gg, h in insts]
        v = [dn_ref[0, pl.ds(r0[gg], gt), 2 * hd + h * dk:2 * hd + (h + 1) * dk].astype(F32) for gg, h in insts]
        gc = [colg[gg][:, h:h + 1] for gg, h in insts]
        beta = [colg[gg][:, DN_HEADS + h:DN_HEADS + h + 1] for gg, h in insts]
        grow = [rowg[gg][h:h + 1, :] for gg, h in insts]
        glast = [jnp.concatenate([jnp.broadcast_to(gc[i][(j + 1) * c - 1:(j + 1) * c], (c, 1))
                                  for j in range(DN_PACK)], axis=0) for i in ins]
        eg = [jnp.exp(gc[i]) for i in ins]
        kbf = [k[i].astype(BF16) for i in ins]
        both = [_dot_nt(jnp.concatenate([(k[i] * beta[i]).astype(BF16), q[i].astype(BF16)], axis=0), kbf[i])
                for i in ins]
        decay = [jnp.exp(jnp.where(ri >= ci, _rows_to_blocks(gc[i], c, gt) - grow[i], NEG_INF)) for i in ins]
        kk = [_diag_blocks(both[i][:gt], c) * decay[i] for i in ins]
        qk = [_blockdiag(_diag_blocks(both[i][gt:], c) * decay[i]) for i in ins]
        t_inv = _packed_unit_lower_inverse([jnp.where(ri > ci, kk[i], 0.0) for i in ins])
        rhs = [jnp.concatenate([k[i] * (beta[i] * eg[i]), v[i] * beta[i]], axis=1).astype(BF16) for i in ins]
        wu = [_dot(_blockdiag(t_inv[i]), rhs[i]).astype(BF16) for i in ins]
        qwu = [_dot(qk[i], wu[i]) for i in ins]
        qp = [(q[i] * eg[i] - qwu[i][:, :dk]).astype(BF16) for i in ins]
        kdec = [(k[i] * jnp.exp(glast[i] - gc[i])).astype(BF16) for i in ins]
        mb = [[_dot_tn(kdec[i][j * c:(j + 1) * c], wu[i][j * c:(j + 1) * c]) for i in ins]
              for j in range(DN_PACK)]
        for gg in range(per_iter):
            mine = [gg * DN_HEADS + h for h in heads]
            outs = [[] for _ in heads]
            for j in range(DN_PACK):
                rows = slice(j * c, (j + 1) * c)
                state = [st_sc[h] for h in heads]
                lhs = [jnp.concatenate([qp[i][rows], mb[j][i][:, :dk].astype(BF16)], axis=0) for i in mine]
                r = [_dot(lhs[h], state[h].astype(BF16)) for h in heads]
                for h, i in zip(heads, mine):
                    gl = jnp.exp(glast[i][j * c:j * c + 1])
                    st_sc[h] = state[h] * gl - r[h][c:] + mb[j][i][:, dk:]
                    outs[h].append(r[h][:c] + qwu[i][rows, dk:])
            for h in heads:
                o = jnp.concatenate(outs[h], axis=0)
                on = o * lax.rsqrt(jnp.mean(o * o, axis=-1, keepdims=True) + EPS) * og
                zz = z_ref[0, pl.ds(r0[gg], gt), hs[h]].astype(F32)
                o_ref[0, pl.ds(r0[gg], gt), hs[h]] = (on * _silu(zz)).astype(o_ref.dtype)
        return carry

    lax.fori_loop(0, s // (gt * per_iter), group, 0)


def _deltanet(dn, z, col, row, onorm_g):
    b, s, w3 = dn.shape
    hd = DN_HEADS * DN_DK
    return pl.pallas_call(
        _deltanet_kernel,
        grid=(b,),
        in_specs=[pl.BlockSpec((1, s, w3), lambda i: (i, 0, 0)),
                  pl.BlockSpec((1, s, hd), lambda i: (i, 0, 0)),
                  pl.BlockSpec((1, s, LANES), lambda i: (i, 0, 0)),
                  pl.BlockSpec((1, s // DN_GROUP, DN_HEADS, DN_GROUP), lambda i: (i, 0, 0, 0)),
                  pl.BlockSpec((1, DN_DK), lambda i: (0, 0))],
        out_specs=pl.BlockSpec((1, s, hd), lambda i: (i, 0, 0)),
        out_shape=jax.ShapeDtypeStruct((b, s, hd), BF16),
        scratch_shapes=[pltpu.VMEM((DN_HEADS, DN_DK, DN_DK), F32)],
        compiler_params=_cparams(("arbitrary",)),
        name="deltanet",
    )(dn, z, col, row, onorm_g.reshape(1, DN_DK))


def _split3(x):
    hi = x.astype(BF16).astype(F32)
    r = x - hi
    mid = r.astype(BF16).astype(F32)
    return hi, mid, r - mid


def _fox_kernel(q_ref, k_ref, v_ref, colq_ref, colk_ref, qg_ref, kg_ref, o_ref, ka_sc, va_sc, m_sc, acc_sc, *, tq):
    qi = pl.program_id(1)
    s = k_ref.shape[1]
    dh = FOX_DH

    def bias_lanes(col):
        lane = lax.broadcasted_iota(jnp.int32, col.shape, 1)
        hi, mid, lo = _split3(col)
        return jnp.where((lane >= 8) & (lane < 8 + FOX_HEADS), hi,
                         jnp.where((lane >= 24) & (lane < 24 + FOX_HEADS), pltpu.roll(mid, 16, axis=1),
                                   jnp.where((lane >= 40) & (lane < 40 + FOX_HEADS), pltpu.roll(lo, 32, axis=1), 0.0)))

    @pl.when(qi == 0)
    def _():
        lane = lax.broadcasted_iota(jnp.int32, (s, LANES), 1)
        ones_col = jnp.where(lane == 0, 1.0, 0.0).astype(BF16)
        ones_k = (lane >= 56) & (lane < 96) & ((lane % 16) >= 8) & ((lane % 16) < 8 + FOX_HEADS)
        ext_k = (jnp.where(ones_k, 1.0, 0.0) - bias_lanes(colk_ref[0])).astype(BF16)
        for h in range(FOX_HEADS):
            cols = slice(h * dh, (h + 1) * dh)
            kf = k_ref[0, :, cols].astype(F32)
            kn = kf * lax.rsqrt(jnp.mean(kf * kf, axis=-1, keepdims=True) + EPS) * kg_ref[...]
            ka_sc[h, :, :dh] = kn.astype(BF16)
            ka_sc[h, :, dh:] = ext_k
            va_sc[h, :, :dh] = v_ref[0, :, cols]
            va_sc[h, :, dh:] = ones_col

    lane = lax.broadcasted_iota(jnp.int32, (tq, LANES), 1)
    cq = pltpu.roll(bias_lanes(colq_ref[0]), 48, axis=1)
    qa = []
    for h in range(FOX_HEADS):
        cols = slice(h * dh, (h + 1) * dh)
        qf = q_ref[0, :, cols].astype(F32)
        qn = qf * lax.rsqrt(jnp.mean(qf * qf, axis=-1, keepdims=True) + EPS) * qg_ref[...] * (dh ** -0.5)
        mine = ((lane % 16) == 8 + h) & (lane < 96)
        ext = jnp.where(mine, jnp.where(lane < 48, 1.0, cq), 0.0)
        qa.append(jnp.concatenate([qn.astype(BF16), ext.astype(BF16)], axis=1))
    m_sc[...] = jnp.full(m_sc.shape, NEG_INF, F32)
    acc_sc[...] = jnp.zeros_like(acc_sc)
    causal = lax.broadcasted_iota(jnp.int32, (tq, tq), 0) >= lax.broadcasted_iota(jnp.int32, (tq, tq), 1)

    def step(k0, masked):
        heads = range(FOX_HEADS)
        logits = [_dot_nt(qa[h], ka_sc[h, pl.ds(k0, tq), :]) for h in heads]
        if masked:
            logits = [jnp.where(causal, lg, NEG_INF) for lg in logits]
        ps, alphas = [], []
        for h in heads:
            m_old = m_sc[h]
            m_new = jnp.maximum(m_old, jnp.max(logits[h], axis=1, keepdims=True))
            m_sc[h] = m_new
            alphas.append(jnp.exp(m_old - m_new))
            ps.append(jnp.exp(logits[h] - jnp.concatenate([m_new] * (tq // LANES), axis=1)).astype(BF16))
        for h in heads:
            pv = _dot(ps[h], va_sc[h, pl.ds(k0, tq), :])
            acc_sc[h] = acc_sc[h] * jnp.concatenate([alphas[h], alphas[h]], axis=1) + pv

    def body(j, carry):
        step(pl.multiple_of(j * tq, tq), False)
        return carry

    lax.fori_loop(0, qi, body, 0)
    step(pl.multiple_of(qi * tq, tq), True)
    for h in range(FOX_HEADS):
        acc = acc_sc[h]
        o_ref[0, :, h * dh:(h + 1) * dh] = (acc[:, :dh] / acc[:, dh:dh + 1]).astype(o_ref.dtype)


def _fox(fox, col, qg, kg, tq):
    b, s, _ = fox.shape
    hd = FOX_HEADS * FOX_DH
    return pl.pallas_call(
        functools.partial(_fox_kernel, tq=tq),
        grid=(b, s // tq),
        in_specs=[pl.BlockSpec((1, tq, hd), lambda i, j: (i, j, 0)),
                  pl.BlockSpec((1, s, hd), lambda i, j: (i, 0, 1)),
                  pl.BlockSpec((1, s, hd), lambda i, j: (i, 0, 2)),
                  pl.BlockSpec((1, tq, LANES), lambda i, j: (i, j, 0)),
                  pl.BlockSpec((1, s, LANES), lambda i, j: (i, 0, 0)),
                  pl.BlockSpec((1, FOX_DH), lambda i, j: (0, 0)),
                  pl.BlockSpec((1, FOX_DH), lambda i, j: (0, 0))],
        out_specs=pl.BlockSpec((1, tq, hd), lambda i, j: (i, j, 0)),
        out_shape=jax.ShapeDtypeStruct((b, s, hd), BF16),
        scratch_shapes=[pltpu.VMEM((FOX_HEADS, s, 2 * FOX_DH), BF16), pltpu.VMEM((FOX_HEADS, s, 2 * FOX_DH), BF16),
                        pltpu.VMEM((FOX_HEADS, tq, LANES), F32), pltpu.VMEM((FOX_HEADS, tq, 2 * FOX_DH), F32)],
        compiler_params=_cparams(("arbitrary", "arbitrary")),
        name="fox",
    )(fox, fox, fox, col, col, qg.reshape(1, FOX_DH), kg.reshape(1, FOX_DH))


def _lru_kernel(x_ref, gate_ref, cw_ref, cb_ref, wr_ref, br_ref, wi_ref, bi_ref, lam_ref, o_ref, a_sc, b_sc):
    s = x_ref.shape[1]
    x = _causal_conv(x_ref[0].astype(F32), cw_ref) + cb_ref[...]
    xb = x.astype(BF16)
    r = _sigmoid(_dot(xb, wr_ref[0].astype(BF16)) + br_ref[...])
    i = _sigmoid(_dot(xb, wi_ref[0].astype(BF16)) + bi_ref[...])
    log_a = (-LRU_C) * r * _softplus(-lam_ref[...])
    a = jnp.exp(log_a)
    bb = jnp.sqrt(1.0 - a * a) * (i * x)
    nt = s // SUBLANES
    a3 = a.reshape(nt, SUBLANES, a.shape[1])
    b3 = bb.reshape(nt, SUBLANES, a.shape[1])
    sub = lax.broadcasted_iota(jnp.int32, a3.shape, 1)
    d = 1
    while d < SUBLANES:
        keep = sub >= d
        b3 = a3 * jnp.where(keep, pltpu.roll(b3, d, axis=1), 0.0) + b3
        a3 = a3 * jnp.where(keep, pltpu.roll(a3, d, axis=1), 1.0)
        d *= 2
    a_sc[...] = a3.reshape(s, a.shape[1])
    b_sc[...] = b3.reshape(s, a.shape[1])
    at = a_sc[pl.ds(SUBLANES - 1, nt, stride=SUBLANES), :]
    bt = b_sc[pl.ds(SUBLANES - 1, nt, stride=SUBLANES), :]
    d = 1
    while d < nt:
        bt = at * _shift_rows(bt, d, 0.0) + bt
        at = at * _shift_rows(at, d, 1.0)
        d *= 2
    h_prev = _shift_rows(bt, 1, 0.0)
    bb = (a3 * h_prev[:, None, :] + b3).reshape(s, a.shape[1])
    o_ref[0] = (bb * gate_ref[0].astype(F32)).astype(o_ref.dtype)


def _lru(lx, lg, conv_w, conv_b, wr, br, wi, bi, lam):
    b, s, wd = lx.shape
    blk = wd // LRU_BLOCKS
    vec = lambda i, j: (0, j)
    return pl.pallas_call(
        _lru_kernel,
        grid=(b, LRU_BLOCKS),
        in_specs=[pl.BlockSpec((1, s, blk), lambda i, j: (i, 0, j)),
                  pl.BlockSpec((1, s, blk), lambda i, j: (i, 0, j)),
                  pl.BlockSpec((CONV_WIDTH, blk), vec),
                  pl.BlockSpec((1, blk), vec),
                  pl.BlockSpec((1, blk, blk), lambda i, j: (j, 0, 0)),
                  pl.BlockSpec((1, blk), vec),
                  pl.BlockSpec((1, blk, blk), lambda i, j: (j, 0, 0)),
                  pl.BlockSpec((1, blk), vec),
                  pl.BlockSpec((1, blk), vec)],
        out_specs=pl.BlockSpec((1, s, blk), lambda i, j: (i, 0, j)),
        out_shape=jax.ShapeDtypeStruct((b, s, wd), BF16),
        scratch_shapes=[pltpu.VMEM((s, blk), F32), pltpu.VMEM((s, blk), F32)],
        compiler_params=_cparams(("arbitrary", "arbitrary")),
        name="rglru",
    )(lx, lg, conv_w, conv_b.reshape(1, wd), wr, br.reshape(1, wd), wi, bi.reshape(1, wd), lam.reshape(1, wd))


def _routing(logits, rb):
    ne, tm = logits.shape
    row = lax.broadcasted_iota(jnp.int32, (ne, tm), 0)
    row_f = row.astype(F32)
    ex = jnp.exp(logits - jnp.max(logits, axis=0, keepdims=True))
    probs = ex / jnp.sum(ex, axis=0, keepdims=True)
    sel = probs + rb

    def top2(vals, idx):
        m1 = jnp.max(vals, axis=0, keepdims=True)
        i1 = jnp.min(jnp.where(vals == m1, idx, float(ne)), axis=0, keepdims=True)
        rest = jnp.where(idx == i1, NEG_INF, vals)
        m2 = jnp.max(rest, axis=0, keepdims=True)
        return m1, i1, m2, rest

    best = None
    g_idx = None
    grp = row // EXPERTS_PER_GROUP
    for g in range(ne // EXPERTS_PER_GROUP):
        m1, _, m2, _ = top2(jnp.where(grp == g, sel, NEG_INF), row_f)
        score = m1 + m2
        if g == 0:
            best, g_idx = score, jnp.zeros((1, tm), jnp.int32)
        else:
            upd = score > best
            best = jnp.where(upd, score, best)
            g_idx = jnp.where(upd, g, g_idx)
    _, i1, m2, rest = top2(jnp.where(grp == g_idx, sel, NEG_INF), row_f)
    i2 = jnp.min(jnp.where(rest == m2, row_f, float(ne)), axis=0, keepdims=True)
    p1 = jnp.sum(jnp.where(row_f == i1, probs, 0.0), axis=0, keepdims=True)
    p2 = jnp.sum(jnp.where(row_f == i2, probs, 0.0), axis=0, keepdims=True)
    den = p1 + p2
    first_lo = i1 < i2
    a = jnp.where(first_lo, i1, i2) - EXPERTS_PER_GROUP * g_idx.astype(F32)
    bhi = jnp.where(first_lo, i2, i1) - EXPERTS_PER_GROUP * g_idx.astype(F32)
    pair = a * (7.0 - a) * 0.5 + (bhi - a - 1.0)
    pair = jnp.where(a == 1.0, 7.0 - pair, pair)
    cls = PAIRS_PER_GROUP * g_idx.astype(F32) + pair
    w_lo = jnp.where(first_lo, p1, p2) / den
    w_hi = jnp.where(first_lo, p2, p1) / den
    swap = pair == 5.0
    return cls, jnp.where(swap, w_hi, w_lo), jnp.where(swap, w_lo, w_hi)


PAIRS_PER_GROUP = EXPERTS_PER_GROUP * (EXPERTS_PER_GROUP - 1) // 2
N_CLASSES = (N_EXPERTS // EXPERTS_PER_GROUP) * PAIRS_PER_GROUP
PAIR_WALK = [(0, 1), (0, 2), (0, 3), (1, 3), (1, 2), (3, 2)]
assert EXPERTS_PER_GROUP == 4 and len(PAIR_WALK) == PAIRS_PER_GROUP
CLASS_E0 = [EXPERTS_PER_GROUP * (c // PAIRS_PER_GROUP) + PAIR_WALK[c % PAIRS_PER_GROUP][0] for c in range(N_CLASSES)]
CLASS_E1 = [EXPERTS_PER_GROUP * (c // PAIRS_PER_GROUP) + PAIR_WALK[c % PAIRS_PER_GROUP][1] for c in range(N_CLASSES)]
CLASS_ROWS = 32
ROW_FEAT = 512
ROW_WORDS = 640


def _pack_halves(x):
    kk = x.shape[1] // 2
    lo = lax.bitcast_convert_type(x[:, :kk].astype(BF16).astype(F32), jnp.uint32)
    hi = lax.bitcast_convert_type(x[:, kk:].astype(BF16).astype(F32), jnp.uint32)
    return lax.bitcast_convert_type((lo >> 16) | hi, jnp.int32)


def _unpack_halves(p):
    u = lax.bitcast_convert_type(p, jnp.uint32)
    lo = lax.bitcast_convert_type(u << 16, F32)
    hi = lax.bitcast_convert_type(u & jnp.uint32(0xFFFF0000), F32)
    return jnp.concatenate([lo.astype(BF16), hi.astype(BF16)], axis=1)


def _outproj_kernel(a_ref, b_ref, x_ref, wf_ref, gt_ref, g2_ref, sc_ref, sh_ref, rwt_ref, rb_ref,
                    x1_ref, rows_ref, route_ref, cnt_ref, carry_sc, w_ref, tri_sc):
    first = (pl.program_id(0) == 0) & (pl.program_id(1) == 0)

    @pl.when(first)
    def _():
        carry_sc[...] = jnp.zeros_like(carry_sc)
        w_ref[...] = wf_ref[...].astype(BF16)
        earlier = (lax.broadcasted_iota(jnp.int32, tri_sc.shape, 0) < lax.broadcasted_iota(jnp.int32, tri_sc.shape, 1))
        tri_sc[...] = jnp.where(earlier, 1.0, 0.0).astype(BF16)

    half = a_ref.shape[2]
    tm = a_ref.shape[1]
    y = _dot(a_ref[0], w_ref[:half, :]) + _dot(b_ref[0], w_ref[half:, :])
    x1 = x_ref[0] + gt_ref[0] * y
    x1_ref[0] = x1
    h2 = _prenorm(x1, g2_ref[...], sc_ref[0], sh_ref[0])
    rows_ref[0, :, :ROW_FEAT] = _pack_halves(h2)
    cls, w_lo, w_hi = _routing(_dot_nt(rwt_ref[...], h2.astype(BF16)), rb_ref[...])
    r128 = lax.broadcasted_iota(jnp.int32, (LANES, tm), 0)
    wts = jnp.where(r128 == 0, w_lo, jnp.where(r128 == 1, w_hi, 0.0))
    rows_ref[0, :, ROW_FEAT:] = lax.bitcast_convert_type(wts.T[:, :ROW_WORDS - ROW_FEAT], jnp.int32)
    crow = lax.broadcasted_iota(jnp.int32, (carry_sc.shape[0], tm), 0).astype(F32)
    onehot = jnp.where(crow == cls, 1.0, 0.0)
    prefix = _dot(onehot.astype(BF16), tri_sc[...]) + carry_sc[:, 0:1]
    rank = jnp.sum(onehot * prefix, axis=0, keepdims=True)
    r8 = lax.broadcasted_iota(jnp.int32, (8, tm), 0)
    route_ref[0] = jnp.where(r8 == 0, cls, jnp.where(r8 == 1, rank, 0.0))
    carry_sc[...] = carry_sc[...] + jnp.sum(onehot, axis=1, keepdims=True)
    cnt_ref[...] = carry_sc[...]


def _outproj(a, bb, x, w, gt, g2, sc2, sh2, rw, rb, tm):
    b, s, d = x.shape
    half = a.shape[2]
    row = lambda i, j: (i, 0, 0)
    tok = lambda i, j: (i, j, 0)
    const = lambda i, j: (0, 0)
    return pl.pallas_call(
        _outproj_kernel,
        grid=(b, s // tm),
        in_specs=[pl.BlockSpec((1, tm, half), tok),
                  pl.BlockSpec((1, tm, half), tok),
                  pl.BlockSpec((1, tm, d), tok),
                  pl.BlockSpec((2 * half, d), const),
                  pl.BlockSpec((1, 1, d), row),
                  pl.BlockSpec((1, d), const),
                  pl.BlockSpec((1, 1, d), row),
                  pl.BlockSpec((1, 1, d), row),
                  pl.BlockSpec((N_EXPERTS, d), const),
                  pl.BlockSpec((N_EXPERTS, 1), const)],
        out_specs=(pl.BlockSpec((1, tm, d), tok),
                   pl.BlockSpec((1, tm, ROW_WORDS), tok),
                   pl.BlockSpec((1, 8, tm), lambda i, j: (i, 0, j)),
                   pl.BlockSpec((CLASS_ROWS, LANES), const)),
        out_shape=(jax.ShapeDtypeStruct((b, s, d), F32),
                   jax.ShapeDtypeStruct((b, s, ROW_WORDS), jnp.int32),
                   jax.ShapeDtypeStruct((b, 8, s), F32),
                   jax.ShapeDtypeStruct((CLASS_ROWS, LANES), F32)),
        scratch_shapes=[pltpu.VMEM((CLASS_ROWS, LANES), F32), pltpu.VMEM((2 * half, d), BF16),
                        pltpu.VMEM((tm, tm), BF16)],
        compiler_params=_cparams(("arbitrary", "arbitrary")),
        name="outproj_router",
    )(a, bb, x, w, gt, g2.reshape(1, d), sc2, sh2, rw, rb)


MOE_ROWS = 512
SC_CHUNK = 128


def _sc_workers():
    info = plsc.get_sparse_core_info()
    return info.num_cores, info.num_cores * info.num_subcores


def _sc_scatter_rows(rows, idx3, n_out):
    nw, k, ch = idx3.shape
    width = rows.shape[1]
    nc, _ = _sc_workers()
    mesh = plsc.VectorSubcoreMesh(core_axis_name="c", subcore_axis_name="s")

    @functools.partial(
        pl.kernel, mesh=mesh,
        out_type=jax.ShapeDtypeStruct((n_out, width), rows.dtype),
        scratch_types=[pltpu.VMEM((k, ch), jnp.int32), pltpu.VMEM((ch, width), rows.dtype), pltpu.SemaphoreType.DMA],
        name="moe_dispatch")
    def kern(rows_hbm, idx_hbm, out_hbm, idx_v, rows_v, sem):
        wid = lax.axis_index("s") * nc + lax.axis_index("c")
        pltpu.sync_copy(idx_hbm.at[wid], idx_v)

        @pl.loop(0, k)
        def _(j):
            pltpu.sync_copy(rows_hbm.at[pl.ds((wid * k + j) * ch, ch)], rows_v)
            pltpu.async_copy(rows_v, out_hbm.at[idx_v.at[j]], sem).wait()

    return kern(rows, idx3)


def _sc_gather_rows(table, idx3):
    nw, k, ch = idx3.shape
    width = table.shape[1]
    nc, _ = _sc_workers()
    mesh = plsc.VectorSubcoreMesh(core_axis_name="c", subcore_axis_name="s")

    @functools.partial(
        pl.kernel, mesh=mesh,
        out_type=jax.ShapeDtypeStruct((nw * k * ch, width), table.dtype),
        scratch_types=[pltpu.VMEM((k, ch), jnp.int32), pltpu.VMEM((ch, width), table.dtype), pltpu.SemaphoreType.DMA],
        name="moe_combine")
    def kern(table_hbm, idx_hbm, out_hbm, idx_v, rows_v, sem):
        wid = lax.axis_index("s") * nc + lax.axis_index("c")
        pltpu.sync_copy(idx_hbm.at[wid], idx_v)

        @pl.loop(0, k)
        def _(j):
            pltpu.async_copy(table_hbm.at[idx_v.at[j]], rows_v, sem).wait()
            pltpu.sync_copy(rows_v, out_hbm.at[pl.ds((wid * k + j) * ch, ch)])

    return kern(table, idx3)


def _experts_kernel(e0_ref, e1_ref, nused_ref, x_ref, wg0_ref, wu0_ref, wd0_ref, wg1_ref, wu1_ref, wd1_ref, y_ref):
    @pl.when(pl.program_id(0) < nused_ref[0])
    def _():
        blk = x_ref[...]
        feat = ROW_FEAT
        h = _unpack_halves(blk[:, :feat])
        wts = lax.bitcast_convert_type(blk[:, feat:], F32)
        y = None
        for col, (wg_ref, wu_ref, wd_ref) in enumerate(((wg0_ref, wu0_ref, wd0_ref), (wg1_ref, wu1_ref, wd1_ref))):
            act = (_silu(_dot(h, wg_ref[0, 0].astype(BF16))) * _dot(h, wu_ref[0, 0].astype(BF16))
                   * wts[:, col:col + 1])
            part = _dot(act.astype(BF16), wd_ref[0, 0].astype(BF16))
            y = part if y is None else y + part
        y_ref[...] = _pack_halves(y)


def _experts(xs, blk_e0, blk_e1, nused, wg, wu, wd, layer):
    n_rows = xs.shape[0]
    _, ne, d, f = wg.shape
    nblk = n_rows // MOE_ROWS
    rows = lambda i, e0, e1, nu: (jnp.minimum(i, nu[0] - 1), 0)
    lo = lambda i, e0, e1, nu: (layer, e0[i], 0, 0)
    hi = lambda i, e0, e1, nu: (layer, e1[i], 0, 0)
    return pl.pallas_call(
        _experts_kernel,
        grid_spec=pltpu.PrefetchScalarGridSpec(
            num_scalar_prefetch=3,
            grid=(nblk,),
            in_specs=[pl.BlockSpec((MOE_ROWS, ROW_WORDS), rows),
                      pl.BlockSpec((1, 1, d, f), lo), pl.BlockSpec((1, 1, d, f), lo), pl.BlockSpec((1, 1, f, d), lo),
                      pl.BlockSpec((1, 1, d, f), hi), pl.BlockSpec((1, 1, d, f), hi), pl.BlockSpec((1, 1, f, d), hi)],
            out_specs=pl.BlockSpec((MOE_ROWS, d // 2), lambda i, e0, e1, nu: (i, 0))),
        out_shape=jax.ShapeDtypeStruct((n_rows, d // 2), jnp.int32),
        compiler_params=_cparams(("arbitrary",)),
        name="moe_experts",
    )(blk_e0, blk_e1, nused, xs, wg, wu, wd, wg, wu, wd)


def _residual_kernel(x_ref, m_ref, gt_ref, o_ref):
    o_ref[0] = x_ref[0] + gt_ref[0] * _unpack_halves(m_ref[0]).astype(F32)


def _residual(x1, moe_p, gt, tm):
    b, s, d = x1.shape
    tok = lambda i, j: (i, j, 0)
    return pl.pallas_call(
        _residual_kernel,
        grid=(b, s // tm),
        in_specs=[pl.BlockSpec((1, tm, d), tok), pl.BlockSpec((1, tm, d // 2), tok),
                  pl.BlockSpec((1, 1, d), lambda i, j: (i, 0, 0))],
        out_specs=pl.BlockSpec((1, tm, d), tok),
        out_shape=jax.ShapeDtypeStruct((b, s, d), F32),
        compiler_params=_cparams(("arbitrary", "arbitrary")),
        name="moe_residual",
    )(x1, moe_p, gt)


def _moe(rows, route, counts, wg, wu, wd, layer):
    b, s, _ = rows.shape
    d = wg.shape[2]
    t = b * s
    _, nw = _sc_workers()
    cnt = counts[:N_CLASSES, 0].astype(jnp.int32)
    padded = ((cnt + MOE_ROWS - 1) // MOE_ROWS) * MOE_ROWS
    ends = jnp.cumsum(padded)
    n_rows = t + N_CLASSES * MOE_ROWS
    nblk = n_rows // MOE_ROWS
    cls = route[:, 0, :].reshape(t).astype(jnp.int32)
    rank = route[:, 1, :].reshape(t).astype(jnp.int32)
    pos = rank + jnp.sum(jnp.where(cls[:, None] > jnp.arange(N_CLASSES, dtype=jnp.int32)[None, :], padded[None, :], 0),
                         axis=1)
    idx3 = pos.reshape(nw, t // (nw * SC_CHUNK), SC_CHUNK)
    nused = (ends[-1] // MOE_ROWS).reshape(1)
    blk_cls = jnp.sum((jnp.arange(nblk, dtype=jnp.int32)[:, None] * MOE_ROWS) >= ends[None, :], axis=1)
    blk_cls = jnp.minimum(blk_cls, blk_cls[jnp.maximum(nused[0] - 1, 0)])
    blk_e0 = jnp.asarray(CLASS_E0, jnp.int32)[blk_cls]
    blk_e1 = jnp.asarray(CLASS_E1, jnp.int32)[blk_cls]
    xs = _sc_scatter_rows(rows.reshape(t, ROW_WORDS), idx3, n_rows)
    ys = _experts(xs, blk_e0, blk_e1, nused, wg, wu, wd, layer)
    return _sc_gather_rows(ys, idx3).reshape(b, s, d // 2)


def kernel(x, c, ada_w, ada_b, norm1_g, norm2_g, ev_w_in, ev_conv_w, ev_dn_a_log, ev_dn_dt_bias, ev_dn_onorm_g, ev_fox_f_bias, ev_fox_qnorm_g, ev_fox_knorm_g, ev_w_out, od_w_in, od_conv_w, od_conv_b, od_lru_wr, od_lru_br, od_lru_wi, od_lru_bi, od_lru_lambda, od_sgu_norm_g, od_sgu_w, od_sgu_b, od_w_out, router_w, router_b, moe_w_gate, moe_w_up, moe_w_down):
    b, s, d = x.shape
    depth = ada_w.shape[0]
    tm = min(TOKEN_BLOCK, s)
    mod = _adaln(c, ada_w, ada_b).reshape(depth, b, 6, 1, d)
    rw = router_w.T.astype(BF16)
    rb = router_b.reshape(N_EXPERTS, 1)

    pending = None
    for layer in range(depth):
        sh1, sc1, gt1, sh2, sc2, gt2 = (mod[layer, :, k] for k in range(6))
        i = layer // 2
        if layer % 2 == 0:
            w = ev_w_in[i]
            nq = 3 * DN_HEADS * DN_DK
            nz = DN_HEADS * DN_DK
            nf = 3 * FOX_HEADS * FOX_DH
            o_a = nq + nz
            o_f = o_a + 2 * DN_HEADS
            o_ff = o_f + nf
            splits = ((0, nq), (nq, o_a), (o_a, o_a + nf), (o_a + nf, o_a + nf + LANES))
            w_cols = ((0, o_a, 0), (o_f, o_ff, o_a), (o_a, o_f, o_a + nf), (o_ff, o_ff + FOX_HEADS, o_a + nf + o_f - o_a))
            res = _inproj(x, norm1_g[layer], sc1, sh1, w, splits,
                          (BF16, BF16, BF16, F32), (None,) * 4, tm, pending, ev_conv_w[i], w_cols=w_cols)
            if pending is not None:
                x, res = res[0], res[1:]
            dn, z, fox, small = res
            small_t = jnp.swapaxes(small[:, :, :16], 1, 2)
            zeros4 = jnp.zeros((4,), F32)
            mul = jnp.concatenate([-jnp.exp(ev_dn_a_log[i]), zeros4, -jnp.ones((4,), F32), zeros4])
            bias = jnp.concatenate([ev_dn_dt_bias[i], zeros4, ev_fox_f_bias[i], zeros4])
            pr = jnp.stack([mul, bias], axis=1)
            pc = jnp.pad(jnp.stack([mul, bias], axis=0), ((0, 0), (0, LANES - 16)))
            col, row = _gates(small, small_t, pc, pr)
            row_dn = row[:, :DN_HEADS].reshape(b, DN_HEADS, s // DN_GROUP, DN_GROUP).transpose(0, 2, 1, 3)
            o_dn = _deltanet(dn, z, col, row_dn, ev_dn_onorm_g[i])
            o_fox = _fox(fox, col, ev_fox_qnorm_g[i], ev_fox_knorm_g[i], min(FOX_QUERY_BLOCK, s))
            mix_a, mix_b, w_out = o_dn, o_fox, ev_w_out[i]
        else:
            lw = od_lru_wr.shape[-1] * LRU_BLOCKS
            splits = ((0, lw), (lw, 2 * lw), (2 * lw, 2 * lw + od_sgu_w.shape[-1] * SGU_GROUPS),
                      (2 * lw + od_sgu_w.shape[-1] * SGU_GROUPS, od_w_in.shape[-1]))
            res = _inproj(x, norm1_g[layer], sc1, sh1, od_w_in[i], splits,
                          (BF16,) * 3, (None, "gelu"), tm, pending,
                          sgu=(od_sgu_norm_g[i], od_sgu_w[i], od_sgu_b[i]))
            if pending is not None:
                x, res = res[0], res[1:]
            lx, lg, o_sgu = res
            o_lru = _lru(lx, lg, od_conv_w[i], od_conv_b[i], od_lru_wr[i], od_lru_br[i], od_lru_wi[i],
                         od_lru_bi[i], od_lru_lambda[i])
            mix_a, mix_b, w_out = o_lru, o_sgu, od_w_out[i]
        x1, rows, route, counts = _outproj(mix_a, mix_b, x, w_out, gt1, norm2_g[layer], sc2, sh2,
                                           rw, rb, min(ROUTER_BLOCK, s))
        x = x1
        pending = (_moe(rows, route, counts, moe_w_gate, moe_w_up, moe_w_down, layer), gt2)
    return _residual(x, *pending, min(ROUTER_BLOCK, s))
```

```python
import functools

import jax
import jax.numpy as jnp
from jax import lax
from jax.experimental import pallas as pl
from jax.experimental.pallas import tpu as pltpu
from jax.experimental.pallas import tpu_sc as plsc

F32 = jnp.float32
BF16 = jnp.bfloat16
EPS = 1e-6
NEG_INF = float("-inf")

DN_HEADS = 4
DN_DK = 128
DN_CHUNK = 64
CONV_WIDTH = 4
FOX_HEADS = 4
FOX_DH = 128
LRU_BLOCKS = 4
LRU_C = 8.0
SGU_GROUPS = 4
SGU_CHUNK = 128
N_EXPERTS = 16
EXPERTS_PER_GROUP = 4
LANES = 128
SUBLANES = 8

VMEM_LIMIT = 48 * 1024 * 1024
TOKEN_BLOCK = 512
ODD_TOKEN_BLOCK = 1024
ROUTER_BLOCK = 1024
FOX_QUERY_BLOCK = 512


def _cparams(sem):
    return pltpu.CompilerParams(dimension_semantics=sem, vmem_limit_bytes=VMEM_LIMIT)


def _dot(a, b):
    return jnp.dot(a, b, preferred_element_type=F32)


def _dot_nt(a, b):
    return lax.dot_general(a, b, (((1,), (1,)), ((), ())), preferred_element_type=F32)


def _dot_tn(a, b):
    return lax.dot_general(a, b, (((0,), (0,)), ((), ())), preferred_element_type=F32)


def _dot_ones(a, b):
    if a.dtype == BF16:
        return sum(_dot(a, t.astype(BF16)) for t in _split3(b))
    return sum(_dot(t.astype(BF16), b) for t in _split3(a))


def _sigmoid(x):
    return 0.5 + 0.5 * jnp.tanh(0.5 * x)


def _silu(x):
    hx = 0.5 * x
    return hx + hx * jnp.tanh(hx)


def _softplus(x):
    return jnp.maximum(x, 0.0) + jnp.log(1.0 + jnp.exp(-jnp.abs(x)))


def _gelu_tanh(x):
    c = 0.7978845608028654
    return 0.5 * x * (1.0 + jnp.tanh(c * (x + 0.044715 * (x * x * x))))


def _prenorm(x, g, scale, shift):
    ms = jnp.mean(x * x, axis=-1, keepdims=True)
    return (x * lax.rsqrt(ms + EPS) * g) * (1.0 + scale) + shift


def _shift_rows(x, d, fill=0.0):
    rows = lax.broadcasted_iota(jnp.int32, x.shape, 0)
    return jnp.where(rows >= d, pltpu.roll(x, d, axis=0), fill)


def _causal_conv(x, w_ref):
    k_w = w_ref.shape[0]

    def taps(v, shift):
        acc = v * w_ref[k_w - 1:k_w, :]
        for d in range(1, k_w):
            acc = acc + shift(v, d) * w_ref[k_w - 1 - d:k_w - d, :]
        return acc

    body = taps(x, lambda v, d: pltpu.roll(v, d, axis=0))
    return jnp.concatenate([taps(x[:SUBLANES], _shift_rows), body[SUBLANES:]], axis=0)


def _adaln_kernel(c_ref, w_ref, b_ref, o_ref):
    c = c_ref[...]
    ca = _silu(c).astype(BF16)
    o_ref[0] = _dot(ca, w_ref[0].astype(BF16)) + b_ref[0]


def _adaln(c, ada_w, ada_b):
    depth, d, n = ada_w.shape
    b = c.shape[0]
    tn = 1536
    return pl.pallas_call(
        _adaln_kernel,
        grid=(depth, n // tn),
        in_specs=[pl.BlockSpec((b, d), lambda l, j: (0, 0)),
                  pl.BlockSpec((1, d, tn), lambda l, j: (l, 0, j)),
                  pl.BlockSpec((1, 1, tn), lambda l, j: (l, 0, j))],
        out_specs=pl.BlockSpec((1, b, tn), lambda l, j: (l, 0, j)),
        out_shape=jax.ShapeDtypeStruct((depth, b, n), F32),
        compiler_params=_cparams(("arbitrary", "arbitrary")),
        name="adaln",
    )(c, ada_w, ada_b.reshape(depth, 1, n))


CONV_HALO = 8
CONV_SLAB = 512


def _inproj_kernel(*refs, col_splits, acts, pending, conv, sgu, cast_w):
    refs = list(refs)
    x_ref = refs.pop(0)
    if pending:
        m_ref, gtm_ref = refs.pop(0), refs.pop(0)
    g_ref, sc_ref, sh_ref, w_ref = (refs.pop(0) for _ in range(4))
    if cast_w:
        w_f32, w_ref = w_ref, refs.pop()

        @pl.when((pl.program_id(0) == 0) & (pl.program_id(1) == 0))
        def _():
            if sum(s1 - s0 for s0, s1, _ in cast_w) < w_ref.shape[1]:
                w_ref[...] = jnp.zeros_like(w_ref)
            for s0, s1, d0 in cast_w:
                w_ref[:, d0:d0 + s1 - s0] = w_f32[:, s0:s1].astype(BF16)
    if conv:
        cw_ref = refs.pop(0)
        halo_sc = refs.pop()
    if sgu:
        sg_ref, sw_ref, sbt_ref = (refs.pop(0) for _ in range(3))
    if pending:
        xo_ref = refs.pop(0)
        x = x_ref[0] + gtm_ref[0] * _unpack_halves(m_ref[0]).astype(F32)
        xo_ref[0] = x
    else:
        x = x_ref[0]
    out_refs = refs
    h = _prenorm(x, g_ref[...], sc_ref[0], sh_ref[0]).astype(BF16)
    tm = x.shape[0]
    if conv:
        @pl.when(pl.program_id(1) == 0)
        def _():
            halo_sc[0:CONV_HALO, :] = jnp.zeros((CONV_HALO, halo_sc.shape[1]), F32)

        c0, c1 = col_splits[0]
        for s0 in range(c0, c1, CONV_SLAB):
            halo_sc[CONV_HALO:, s0 - c0:s0 - c0 + CONV_SLAB] = _dot(h, w_ref[:, s0:s0 + CONV_SLAB])
    n_plain = len(col_splits) - (2 if sgu else 0)
    for k, (o_ref, (c0, c1), act) in enumerate(zip(out_refs[:n_plain], col_splits[:n_plain], acts)):
        if conv and k == 0:
            continue
        p = _dot(h, w_ref[:, c0:c1])
        if act == "gelu":
            p = _gelu_tanh(p)
        o_ref[0] = p.astype(o_ref.dtype)
    if sgu:
        (u0, u1), (v0, v1) = col_splits[n_plain:]
        o_ref = out_refs[n_plain]
        u = _gelu_tanh(_dot(h, w_ref[:, u0:u1]))
        v = _gelu_tanh(_dot(h, w_ref[:, v0:v1]))
        vn = (v * lax.rsqrt(jnp.mean(v * v, axis=-1, keepdims=True) + EPS) * sg_ref[...]).astype(BF16)
        c = SGU_CHUNK
        gw = (u1 - u0) // SGU_GROUPS
        ri = lax.broadcasted_iota(jnp.int32, (c, c), 0)
        ci = lax.broadcasted_iota(jnp.int32, (c, c), 1)
        for g in range(SGU_GROUPS):
            wg = jnp.where(ri >= ci, sw_ref[g], 0.0).astype(BF16)
            bcol = sbt_ref[:, g:g + 1]
            for n in range(tm // c):
                rows = slice(n * c, (n + 1) * c)
                cols = slice(g * gw, (g + 1) * gw)
                mixed = _dot(wg, vn[rows, cols]) + bcol
                o_ref[0, rows, cols] = (u[rows, cols] * mixed).astype(o_ref.dtype)
    if conv:
        o_ref = out_refs[0]
        for t0 in range(0, halo_sc.shape[1], DN_DK):
            cols = slice(t0, t0 + DN_DK)
            acc = halo_sc[CONV_HALO:, cols] * cw_ref[CONV_WIDTH - 1:CONV_WIDTH, cols]
            for dd in range(1, CONV_WIDTH):
                acc = acc + halo_sc[pl.ds(CONV_HALO - dd, tm), cols] * cw_ref[CONV_WIDTH - 1 - dd:CONV_WIDTH - dd, cols]
            halo_sc[0:CONV_HALO, cols] = halo_sc[tm:tm + CONV_HALO, cols]
            yj = _silu(acc)
            if t0 < 2 * DN_HEADS * DN_DK:
                yj = yj * lax.rsqrt(jnp.sum(yj * yj, axis=-1, keepdims=True) + EPS)
            if t0 < DN_HEADS * DN_DK:
                yj = yj * (DN_DK ** -0.5)
            o_ref[0, :, cols] = yj.astype(o_ref.dtype)


def _inproj(x, g, scale, shift, w, col_splits, out_dtypes, acts, tm, pending=None, conv_w=None, sgu=None,
            w_cols=None):
    b, s, d = x.shape
    n = w.shape[1] if w_cols is None else col_splits[-1][1]
    if w.dtype != BF16 and w_cols is None:
        w_cols = ((0, n, 0),)
    out_cols = list(col_splits) if sgu is None else list(col_splits[:-2]) + [col_splits[-2]]
    outs = tuple(jax.ShapeDtypeStruct((b, s, c1 - c0), dt) for (c0, c1), dt in zip(out_cols, out_dtypes))
    row = lambda i, j: (i, 0, 0)
    tok = lambda i, j: (i, j, 0)
    const = lambda i, j: (0, 0)
    in_specs = [pl.BlockSpec((1, tm, d), tok)]
    out_specs = tuple(pl.BlockSpec((1, tm, c1 - c0), tok) for (c0, c1) in out_cols)
    args = (x,)
    if pending is not None:
        in_specs += [pl.BlockSpec((1, tm, d // 2), tok), pl.BlockSpec((1, 1, d), row)]
        out_specs = (pl.BlockSpec((1, tm, d), tok),) + out_specs
        outs = (jax.ShapeDtypeStruct((b, s, d), F32),) + outs
        args += tuple(pending)
    in_specs += [pl.BlockSpec((1, d), const), pl.BlockSpec((1, 1, d), row), pl.BlockSpec((1, 1, d), row),
                 pl.BlockSpec(w.shape, const, pipeline_mode=pl.Buffered(1)) if w_cols else pl.BlockSpec((d, n), const)]
    args += (g.reshape(1, d), scale, shift, w)
    scratch = []
    if conv_w is not None:
        in_specs.append(pl.BlockSpec(conv_w.shape, const))
        args += (conv_w,)
        scratch.append(pltpu.VMEM((CONV_HALO + tm, conv_w.shape[1]), F32))
    if w_cols:
        scratch.append(pltpu.VMEM((d, n), BF16))
    if sgu is not None:
        g_norm, w_s, b_s = sgu
        in_specs += [pl.BlockSpec((1, g_norm.shape[0]), const), pl.BlockSpec(w_s.shape, lambda i, j: (0, 0, 0)),
                     pl.BlockSpec((b_s.shape[1], b_s.shape[0]), const)]
        args += (g_norm.reshape(1, -1), w_s, b_s.T)
    return pl.pallas_call(
        functools.partial(_inproj_kernel, col_splits=col_splits, acts=acts, pending=pending is not None,
                          conv=conv_w is not None, sgu=sgu is not None, cast_w=w_cols),
        grid=(b, s // tm),
        in_specs=in_specs,
        out_specs=out_specs,
        out_shape=outs,
        scratch_shapes=scratch,
        compiler_params=_cparams(("arbitrary", "arbitrary")),
        name="inproj",
    )(*args)


def _gates_kernel(sm_ref, smt_ref, pc_ref, pr_ref, col_ref, row_ref):
    s = sm_ref.shape[1]
    blk = LANES
    ri = lax.broadcasted_iota(jnp.int32, (blk, blk), 0)
    ci = lax.broadcasted_iota(jnp.int32, (blk, blk), 1)
    same_chunk = (ri // DN_CHUNK) == (ci // DN_CHUNK)
    tril = jnp.where(ri >= ci, 1.0, 0.0).astype(BF16)
    tril_loc = jnp.where(same_chunk & (ri >= ci), 1.0, 0.0).astype(BF16)
    triu_loc = jnp.where(same_chunk & (ri <= ci), 1.0, 0.0).astype(BF16)
    lane = lax.broadcasted_iota(jnp.int32, (blk, LANES), 1)
    carry = jnp.zeros((1, LANES), F32)
    for j in range(s // blk):
        rows = slice(j * blk, (j + 1) * blk)
        xc = sm_ref[0, rows, :] + pc_ref[1:2, :]
        dec = pc_ref[0:1, :] * _softplus(jnp.where(lane < 4, xc, -xc))
        cum_glb = _dot_ones(tril, dec) + carry
        col_ref[0, rows, :] = jnp.where(lane < 4, _dot_ones(tril_loc, dec), jnp.where(lane < 8, _sigmoid(xc), cum_glb))
        carry = cum_glb[blk - 1:blk, :]
        decr = pr_ref[:, 0:1] * _softplus(smt_ref[0, :, rows] + pr_ref[:, 1:2])
        row_ref[0, :, rows] = _dot_ones(decr, triu_loc)


def _gates(small, small_t, pc, pr):
    b, s, _ = small.shape
    return pl.pallas_call(
        _gates_kernel,
        grid=(b,),
        in_specs=[pl.BlockSpec((1, s, LANES), lambda i: (i, 0, 0)),
                  pl.BlockSpec((1, 16, s), lambda i: (i, 0, 0)),
                  pl.BlockSpec((2, LANES), lambda i: (0, 0)),
                  pl.BlockSpec((16, 2), lambda i: (0, 0))],
        out_specs=(pl.BlockSpec((1, s, LANES), lambda i: (i, 0, 0)),
                   pl.BlockSpec((1, 16, s), lambda i: (i, 0, 0))),
        out_shape=(jax.ShapeDtypeStruct((b, s, LANES), F32), jax.ShapeDtypeStruct((b, 16, s), F32)),
        compiler_params=_cparams(("arbitrary",)),
        name="gates",
    )(small, small_t, pc, pr)


DN_PACK = 4
DN_GROUP = DN_PACK * DN_CHUNK
DN_ITER_GROUPS = 4


def _blockdiag(p):
    c, wide = p.shape
    t = jnp.concatenate([p] * (wide // c), axis=0)
    rb = lax.broadcasted_iota(jnp.int32, (wide, wide), 0) // c
    cb = lax.broadcasted_iota(jnp.int32, (wide, wide), 1) // c
    return jnp.where(rb == cb, t, 0.0).astype(BF16)


def _diag_blocks(m, c):
    wide = m.shape[1]
    cb = lax.broadcasted_iota(jnp.int32, (c, wide), 1) // c
    out = m[:c]
    for j in range(1, wide // c):
        out = jnp.where(cb == j, m[j * c:(j + 1) * c], out)
    return out


def _rows_to_blocks(col, c, wide):
    cb = lax.broadcasted_iota(jnp.int32, (c, wide), 1) // c
    out = jnp.broadcast_to(col[:c], (c, wide))
    for j in range(1, wide // c):
        out = jnp.where(cb == j, col[j * c:(j + 1) * c], out)
    return out


def _packed_unit_lower_inverse(lows):
    c, wide = lows[0].shape
    ri = lax.broadcasted_iota(jnp.int32, (c, wide), 0)
    ci = lax.broadcasted_iota(jnp.int32, (c, wide), 1) % c
    eye = jnp.where(ri == ci, 1.0, 0.0)
    n = range(len(lows))
    ds = [jnp.where((ri // 16) == (ci // 16), lo, 0.0) for lo in lows]
    xs = [eye - d for d in ds]
    ps = ds
    bds = [_blockdiag(p) for p in ps]
    for _ in range(3):
        ps = [_dot(ps[i].astype(BF16), bds[i]) for i in n]
        bds = [_blockdiag(p) for p in ps]
        xs = [xs[i] + _dot(xs[i].astype(BF16), bds[i]) for i in n]
    for width in (16, 32):
        sel = ((ri // (2 * width)) == (ci // (2 * width))) & ((ri // width) != (ci // width))
        offs = [_blockdiag(jnp.where(sel, lo, 0.0)) for lo in lows]
        ts = [_dot(xs[i].astype(BF16), offs[i]) for i in n]
        bdx = [_blockdiag(x) for x in xs]
        xs = [xs[i] - _dot(ts[i].astype(BF16), bdx[i]) for i in n]
    return xs


def _deltanet_kernel(dn_ref, z_ref, col_ref, row_ref, og_ref, o_ref, st_sc):
    s = dn_ref.shape[1]
    c = DN_CHUNK
    gt = DN_GROUP
    dk = DN_DK
    hd = DN_HEADS * dk
    heads = range(DN_HEADS)
    st_sc[...] = jnp.zeros_like(st_sc)

    ri = lax.broadcasted_iota(jnp.int32, (c, gt), 0)
    ci = lax.broadcasted_iota(jnp.int32, (c, gt), 1) % c
    og = og_ref[...]

    per_iter = DN_ITER_GROUPS if (s // gt) % DN_ITER_GROUPS == 0 else 1
    insts = [(gg, h) for gg in range(per_iter) for h in heads]
    ins = range(len(insts))

    def group(g, carry):
        r0 = [pl.multiple_of((g * per_iter + gg) * gt, gt) for gg in range(per_iter)]
        colg = [col_ref[0, pl.ds(r0[gg], gt), :] for gg in range(per_iter)]
        rowg = [row_ref[0, g * per_iter + gg] for gg in range(per_iter)]
        hs = [slice(h * dk, (h + 1) * dk) for h in heads]
        q = [dn_ref[0, pl.ds(r0[gg], gt), h * dk:(h + 1) * dk].astype(F32) for gg, h in insts]
        k = [dn_ref[0, pl.ds(r0[gg], gt), hd + h * dk:hd + (h + 1) * dk].astype(F32) for gg, h in insts]
        v = [dn_ref[0, pl.ds(r0[gg], gt), 2 * hd + h * dk:2 * hd + (h + 1) * dk].astype(F32) for gg, h in insts]
        gc = [colg[gg][:, h:h + 1] for gg, h in insts]
        beta = [colg[gg][:, DN_HEADS + h:DN_HEADS + h + 1] for gg, h in insts]
        grow = [rowg[gg][h:h + 1, :] for gg, h in insts]
        glast = [jnp.concatenate([jnp.broadcast_to(gc[i][(j + 1) * c - 1:(j + 1) * c], (c, 1))
                                  for j in range(DN_PACK)], axis=0) for i in ins]
        eg = [jnp.exp(gc[i]) for i in ins]
        kbf = [k[i].astype(BF16) for i in ins]
        both = [_dot_nt(jnp.concatenate([(k[i] * beta[i]).astype(BF16), q[i].astype(BF16)], axis=0), kbf[i])
                for i in ins]
        decay = [jnp.exp(jnp.where(ri >= ci, _rows_to_blocks(gc[i], c, gt) - grow[i], NEG_INF)) for i in ins]
        kk = [_diag_blocks(both[i][:gt], c) * decay[i] for i in ins]
        qk = [_blockdiag(_diag_blocks(both[i][gt:], c) * decay[i]) for i in ins]
        t_inv = _packed_unit_lower_inverse([jnp.where(ri > ci, kk[i], 0.0) for i in ins])
        rhs = [jnp.concatenate([k[i] * (beta[i] * eg[i]), v[i] * beta[i]], axis=1).astype(BF16) for i in ins]
        wu = [_dot(_blockdiag(t_inv[i]), rhs[i]).astype(BF16) for i in ins]
        qwu = [_dot(qk[i], wu[i]) for i in ins]
        qp = [(q[i] * eg[i] - qwu[i][:, :dk]).astype(BF16) for i in ins]
        kdec = [(k[i] * jnp.exp(glast[i] - gc[i])).astype(BF16) for i in ins]
        mb = [[_dot_tn(kdec[i][j * c:(j + 1) * c], wu[i][j * c:(j + 1) * c]) for i in ins]
              for j in range(DN_PACK)]
        for gg in range(per_iter):
            mine = [gg * DN_HEADS + h for h in heads]
            outs = [[] for _ in heads]
            for j in range(DN_PACK):
                rows = slice(j * c, (j + 1) * c)
                state = [st_sc[h] for h in heads]
                lhs = [jnp.concatenate([qp[i][rows], mb[j][i][:, :dk].astype(BF16)], axis=0) for i in mine]
                r = [_dot(lhs[h], state[h].astype(BF16)) for h in heads]
                for h, i in zip(heads, mine):
                    gl = jnp.exp(glast[i][j * c:j * c + 1])
                    st_sc[h] = state[h] * gl - r[h][c:] + mb[j][i][:, dk:]
                    outs[h].append(r[h][:c] + qwu[i][rows, dk:])
            for h in heads:
                o = jnp.concatenate(outs[h], axis=0)
                on = o * lax.rsqrt(jnp.mean(o * o, axis=-1, keepdims=True) + EPS) * og
                zz = z_ref[0, pl.ds(r0[gg], gt), hs[h]].astype(F32)
                o_ref[0, pl.ds(r0[gg], gt), hs[h]] = (on * _silu(zz)).astype(o_ref.dtype)
        return carry

    lax.fori_loop(0, s // (gt * per_iter), group, 0)


def _deltanet(dn, z, col, row, onorm_g):
    b, s, w3 = dn.shape
    hd = DN_HEADS * DN_DK
    return pl.pallas_call(
        _deltanet_kernel,
        grid=(b,),
        in_specs=[pl.BlockSpec((1, s, w3), lambda i: (i, 0, 0)),
                  pl.BlockSpec((1, s, hd), lambda i: (i, 0, 0)),
                  pl.BlockSpec((1, s, LANES), lambda i: (i, 0, 0)),
                  pl.BlockSpec((1, s // DN_GROUP, DN_HEADS, DN_GROUP), lambda i: (i, 0, 0, 0)),
                  pl.BlockSpec((1, DN_DK), lambda i: (0, 0))],
        out_specs=pl.BlockSpec((1, s, hd), lambda i: (i, 0, 0)),
        out_shape=jax.ShapeDtypeStruct((b, s, hd), BF16),
        scratch_shapes=[pltpu.VMEM((DN_HEADS, DN_DK, DN_DK), F32)],
        compiler_params=_cparams(("arbitrary",)),
        name="deltanet",
    )(dn, z, col, row, onorm_g.reshape(1, DN_DK))


def _split3(x):
    hi = x.astype(BF16).astype(F32)
    r = x - hi
    mid = r.astype(BF16).astype(F32)
    return hi, mid, r - mid


def _fox_kernel(q_ref, k_ref, v_ref, colq_ref, colk_ref, qg_ref, kg_ref, o_ref, ka_sc, va_sc, m_sc, acc_sc, *, tq):
    qi = pl.program_id(1)
    s = k_ref.shape[1]
    dh = FOX_DH

    def bias_lanes(col):
        lane = lax.broadcasted_iota(jnp.int32, col.shape, 1)
        hi, mid, lo = _split3(col)
        return jnp.where((lane >= 8) & (lane < 8 + FOX_HEADS), hi,
                         jnp.where((lane >= 24) & (lane < 24 + FOX_HEADS), pltpu.roll(mid, 16, axis=1),
                                   jnp.where((lane >= 40) & (lane < 40 + FOX_HEADS), pltpu.roll(lo, 32, axis=1), 0.0)))

    @pl.when(qi == 0)
    def _():
        lane = lax.broadcasted_iota(jnp.int32, (s, LANES), 1)
        ones_col = jnp.where(lane == 0, 1.0, 0.0).astype(BF16)
        ones_k = (lane >= 56) & (lane < 96) & ((lane % 16) >= 8) & ((lane % 16) < 8 + FOX_HEADS)
        ext_k = (jnp.where(ones_k, 1.0, 0.0) - bias_lanes(colk_ref[0])).astype(BF16)
        for h in range(FOX_HEADS):
            cols = slice(h * dh, (h + 1) * dh)
            kf = k_ref[0, :, cols].astype(F32)
            kn = kf * lax.rsqrt(jnp.mean(kf * kf, axis=-1, keepdims=True) + EPS) * kg_ref[...]
            ka_sc[h, :, :dh] = kn.astype(BF16)
            ka_sc[h, :, dh:] = ext_k
            va_sc[h, :, :dh] = v_ref[0, :, cols]
            va_sc[h, :, dh:] = ones_col

    lane = lax.broadcasted_iota(jnp.int32, (tq, LANES), 1)
    cq = pltpu.roll(bias_lanes(colq_ref[0]), 48, axis=1)
    qa = []
    for h in range(FOX_HEADS):
        cols = slice(h * dh, (h + 1) * dh)
        qf = q_ref[0, :, cols].astype(F32)
        qn = qf * lax.rsqrt(jnp.mean(qf * qf, axis=-1, keepdims=True) + EPS) * qg_ref[...] * (dh ** -0.5)
        mine = ((lane % 16) == 8 + h) & (lane < 96)
        ext = jnp.where(mine, jnp.where(lane < 48, 1.0, cq), 0.0)
        qa.append(jnp.concatenate([qn.astype(BF16), ext.astype(BF16)], axis=1))
    m_sc[...] = jnp.full(m_sc.shape, NEG_INF, F32)
    acc_sc[...] = jnp.zeros_like(acc_sc)
    causal = lax.broadcasted_iota(jnp.int32, (tq, tq), 0) >= lax.broadcasted_iota(jnp.int32, (tq, tq), 1)

    def step(k0, masked):
        heads = range(FOX_HEADS)
        logits = [_dot_nt(qa[h], ka_sc[h, pl.ds(k0, tq), :]) for h in heads]
        if masked:
            logits = [jnp.where(causal, lg, NEG_INF) for lg in logits]
        ps, alphas = [], []
        for h in heads:
            m_old = m_sc[h]
            m_new = jnp.maximum(m_old, jnp.max(logits[h], axis=1, keepdims=True))
            m_sc[h] = m_new
            alphas.append(jnp.exp(m_old - m_new))
            ps.append(jnp.exp(logits[h] - jnp.concatenate([m_new] * (tq // LANES), axis=1)).astype(BF16))
        for h in heads:
            pv = _dot(ps[h], va_sc[h, pl.ds(k0, tq), :])
            acc_sc[h] = acc_sc[h] * jnp.concatenate([alphas[h], alphas[h]], axis=1) + pv

    def body(j, carry):
        step(pl.multiple_of(j * tq, tq), False)
        return carry

    lax.fori_loop(0, qi, body, 0)
    step(pl.multiple_of(qi * tq, tq), True)
    for h in range(FOX_HEADS):
        acc = acc_sc[h]
        o_ref[0, :, h * dh:(h + 1) * dh] = (acc[:, :dh] / acc[:, dh:dh + 1]).astype(o_ref.dtype)


def _fox(fox, col, qg, kg, tq):
    b, s, _ = fox.shape
    hd = FOX_HEADS * FOX_DH
    return pl.pallas_call(
        functools.partial(_fox_kernel, tq=tq),
        grid=(b, s // tq),
        in_specs=[pl.BlockSpec((1, tq, hd), lambda i, j: (i, j, 0)),
                  pl.BlockSpec((1, s, hd), lambda i, j: (i, 0, 1)),
                  pl.BlockSpec((1, s, hd), lambda i, j: (i, 0, 2)),
                  pl.BlockSpec((1, tq, LANES), lambda i, j: (i, j, 0)),
                  pl.BlockSpec((1, s, LANES), lambda i, j: (i, 0, 0)),
                  pl.BlockSpec((1, FOX_DH), lambda i, j: (0, 0)),
                  pl.BlockSpec((1, FOX_DH), lambda i, j: (0, 0))],
        out_specs=pl.BlockSpec((1, tq, hd), lambda i, j: (i, j, 0)),
        out_shape=jax.ShapeDtypeStruct((b, s, hd), BF16),
        scratch_shapes=[pltpu.VMEM((FOX_HEADS, s, 2 * FOX_DH), BF16), pltpu.VMEM((FOX_HEADS, s, 2 * FOX_DH), BF16),
                        pltpu.VMEM((FOX_HEADS, tq, LANES), F32), pltpu.VMEM((FOX_HEADS, tq, 2 * FOX_DH), F32)],
        compiler_params=_cparams(("arbitrary", "arbitrary")),
        name="fox",
    )(fox, fox, fox, col, col, qg.reshape(1, FOX_DH), kg.reshape(1, FOX_DH))


def _lru_kernel(x_ref, gate_ref, cw_ref, cb_ref, wr_ref, br_ref, wi_ref, bi_ref, lam_ref, o_ref, a_sc, b_sc):
    s = x_ref.shape[1]
    x = _causal_conv(x_ref[0].astype(F32), cw_ref) + cb_ref[...]
    xb = x.astype(BF16)
    r = _sigmoid(_dot(xb, wr_ref[0].astype(BF16)) + br_ref[...])
    i = _sigmoid(_dot(xb, wi_ref[0].astype(BF16)) + bi_ref[...])
    log_a = (-LRU_C) * r * _softplus(-lam_ref[...])
    a = jnp.exp(log_a)
    one_m = 1.0 - a * a
    bb = jnp.where(one_m > 0.0, one_m * lax.rsqrt(one_m), 0.0) * (i * x)
    nt = s // SUBLANES
    a3 = a.reshape(nt, SUBLANES, a.shape[1])
    b3 = bb.reshape(nt, SUBLANES, a.shape[1])
    sub = lax.broadcasted_iota(jnp.int32, a3.shape, 1)
    d = 1
    while d < SUBLANES:
        keep = sub >= d
        b3 = a3 * jnp.where(keep, pltpu.roll(b3, d, axis=1), 0.0) + b3
        a3 = a3 * jnp.where(keep, pltpu.roll(a3, d, axis=1), 1.0)
        d *= 2
    a_sc[...] = a3.reshape(s, a.shape[1])
    b_sc[...] = b3.reshape(s, a.shape[1])
    at = a_sc[pl.ds(SUBLANES - 1, nt, stride=SUBLANES), :]
    bt = b_sc[pl.ds(SUBLANES - 1, nt, stride=SUBLANES), :]
    d = 1
    while d < nt:
        bt = at * _shift_rows(bt, d, 0.0) + bt
        at = at * _shift_rows(at, d, 1.0)
        d *= 2
    h_prev = _shift_rows(bt, 1, 0.0)
    bb = (a3 * h_prev[:, None, :] + b3).reshape(s, a.shape[1])
    o_ref[0] = (bb * gate_ref[0].astype(F32)).astype(o_ref.dtype)


def _lru(lx, lg, conv_w, conv_b, wr, br, wi, bi, lam):
    b, s, wd = lx.shape
    blk = wd // LRU_BLOCKS
    vec = lambda i, j: (0, j)
    return pl.pallas_call(
        _lru_kernel,
        grid=(b, LRU_BLOCKS),
        in_specs=[pl.BlockSpec((1, s, blk), lambda i, j: (i, 0, j)),
                  pl.BlockSpec((1, s, blk), lambda i, j: (i, 0, j)),
                  pl.BlockSpec((CONV_WIDTH, blk), vec),
                  pl.BlockSpec((1, blk), vec),
                  pl.BlockSpec((1, blk, blk), lambda i, j: (j, 0, 0)),
                  pl.BlockSpec((1, blk), vec),
                  pl.BlockSpec((1, blk, blk), lambda i, j: (j, 0, 0)),
                  pl.BlockSpec((1, blk), vec),
                  pl.BlockSpec((1, blk), vec)],
        out_specs=pl.BlockSpec((1, s, blk), lambda i, j: (i, 0, j)),
        out_shape=jax.ShapeDtypeStruct((b, s, wd), BF16),
        scratch_shapes=[pltpu.VMEM((s, blk), F32), pltpu.VMEM((s, blk), F32)],
        compiler_params=_cparams(("arbitrary", "arbitrary")),
        name="rglru",
    )(lx, lg, conv_w, conv_b.reshape(1, wd), wr, br.reshape(1, wd), wi, bi.reshape(1, wd), lam.reshape(1, wd))


def _routing(logits, rb):
    ne, tm = logits.shape
    row = lax.broadcasted_iota(jnp.int32, (ne, tm), 0)
    row_f = row.astype(F32)
    ex = jnp.exp(logits - jnp.max(logits, axis=0, keepdims=True))
    probs = ex / jnp.sum(ex, axis=0, keepdims=True)
    sel = probs + rb

    def top2(vals, idx):
        m1 = jnp.max(vals, axis=0, keepdims=True)
        i1 = jnp.min(jnp.where(vals == m1, idx, float(ne)), axis=0, keepdims=True)
        rest = jnp.where(idx == i1, NEG_INF, vals)
        m2 = jnp.max(rest, axis=0, keepdims=True)
        return m1, i1, m2, rest

    best = None
    g_idx = None
    grp = row // EXPERTS_PER_GROUP
    for g in range(ne // EXPERTS_PER_GROUP):
        m1, _, m2, _ = top2(jnp.where(grp == g, sel, NEG_INF), row_f)
        score = m1 + m2
        if g == 0:
            best, g_idx = score, jnp.zeros((1, tm), jnp.int32)
        else:
            upd = score > best
            best = jnp.where(upd, score, best)
            g_idx = jnp.where(upd, g, g_idx)
    _, i1, m2, rest = top2(jnp.where(grp == g_idx, sel, NEG_INF), row_f)
    i2 = jnp.min(jnp.where(rest == m2, row_f, float(ne)), axis=0, keepdims=True)
    p1 = jnp.sum(jnp.where(row_f == i1, probs, 0.0), axis=0, keepdims=True)
    p2 = jnp.sum(jnp.where(row_f == i2, probs, 0.0), axis=0, keepdims=True)
    den = p1 + p2
    first_lo = i1 < i2
    a = jnp.where(first_lo, i1, i2) - EXPERTS_PER_GROUP * g_idx.astype(F32)
    bhi = jnp.where(first_lo, i2, i1) - EXPERTS_PER_GROUP * g_idx.astype(F32)
    pair = a * (7.0 - a) * 0.5 + (bhi - a - 1.0)
    pair = jnp.where(a == 1.0, 7.0 - pair, pair)
    cls = PAIRS_PER_GROUP * g_idx.astype(F32) + pair
    w_lo = jnp.where(first_lo, p1, p2) / den
    w_hi = jnp.where(first_lo, p2, p1) / den
    swap = pair == 5.0
    return cls, jnp.where(swap, w_hi, w_lo), jnp.where(swap, w_lo, w_hi)


PAIRS_PER_GROUP = EXPERTS_PER_GROUP * (EXPERTS_PER_GROUP - 1) // 2
N_CLASSES = (N_EXPERTS // EXPERTS_PER_GROUP) * PAIRS_PER_GROUP
PAIR_WALK = [(0, 1), (0, 2), (0, 3), (1, 3), (1, 2), (3, 2)]
assert EXPERTS_PER_GROUP == 4 and len(PAIR_WALK) == PAIRS_PER_GROUP
CLASS_E0 = [EXPERTS_PER_GROUP * (c // PAIRS_PER_GROUP) + PAIR_WALK[c % PAIRS_PER_GROUP][0] for c in range(N_CLASSES)]
CLASS_E1 = [EXPERTS_PER_GROUP * (c // PAIRS_PER_GROUP) + PAIR_WALK[c % PAIRS_PER_GROUP][1] for c in range(N_CLASSES)]
CLASS_ROWS = 32
ROW_FEAT = 512
ROW_WORDS = 640


def _pack_halves(x):
    kk = x.shape[1] // 2
    lo = lax.bitcast_convert_type(x[:, :kk].astype(BF16).astype(F32), jnp.uint32)
    hi = lax.bitcast_convert_type(x[:, kk:].astype(BF16).astype(F32), jnp.uint32)
    return lax.bitcast_convert_type((lo >> 16) | hi, jnp.int32)


def _unpack_halves(p):
    u = lax.bitcast_convert_type(p, jnp.uint32)
    lo = lax.bitcast_convert_type(u << 16, F32)
    hi = lax.bitcast_convert_type(u & jnp.uint32(0xFFFF0000), F32)
    return jnp.concatenate([lo.astype(BF16), hi.astype(BF16)], axis=1)


def _outproj_kernel(a_ref, b_ref, x_ref, wf_ref, gt_ref, g2_ref, sc_ref, sh_ref, rwt_ref, rb_ref,
                    x1_ref, rows_ref, route_ref, cnt_ref, carry_sc, w_ref, tri_sc):
    first = (pl.program_id(0) == 0) & (pl.program_id(1) == 0)

    @pl.when(first)
    def _():
        carry_sc[...] = jnp.zeros_like(carry_sc)
        w_ref[...] = wf_ref[...].astype(BF16)
        earlier = (lax.broadcasted_iota(jnp.int32, tri_sc.shape, 0) < lax.broadcasted_iota(jnp.int32, tri_sc.shape, 1))
        tri_sc[...] = jnp.where(earlier, 1.0, 0.0).astype(BF16)

    half = a_ref.shape[2]
    tm = a_ref.shape[1]
    y = _dot(a_ref[0], w_ref[:half, :]) + _dot(b_ref[0], w_ref[half:, :])
    x1 = x_ref[0] + gt_ref[0] * y
    x1_ref[0] = x1
    h2 = _prenorm(x1, g2_ref[...], sc_ref[0], sh_ref[0])
    rows_ref[0, :, :ROW_FEAT] = _pack_halves(h2)
    cls, w_lo, w_hi = _routing(_dot_nt(rwt_ref[...], h2.astype(BF16)), rb_ref[...])
    r128 = lax.broadcasted_iota(jnp.int32, (LANES, tm), 0)
    wts = jnp.where(r128 == 0, w_lo, jnp.where(r128 == 1, w_hi, 0.0))
    rows_ref[0, :, ROW_FEAT:] = lax.bitcast_convert_type(wts.T[:, :ROW_WORDS - ROW_FEAT], jnp.int32)
    crow = lax.broadcasted_iota(jnp.int32, (carry_sc.shape[0], tm), 0).astype(F32)
    onehot = jnp.where(crow == cls, 1.0, 0.0)
    prefix = _dot(onehot.astype(BF16), tri_sc[...]) + carry_sc[:, 0:1]
    rank = jnp.sum(onehot * prefix, axis=0, keepdims=True)
    r8 = lax.broadcasted_iota(jnp.int32, (8, tm), 0)
    route_ref[0] = jnp.where(r8 == 0, cls, jnp.where(r8 == 1, rank, 0.0))
    carry_sc[...] = carry_sc[...] + jnp.sum(onehot, axis=1, keepdims=True)
    cnt_ref[...] = carry_sc[...]


def _outproj(a, bb, x, w, gt, g2, sc2, sh2, rw, rb, tm):
    b, s, d = x.shape
    half = a.shape[2]
    row = lambda i, j: (i, 0, 0)
    tok = lambda i, j: (i, j, 0)
    const = lambda i, j: (0, 0)
    return pl.pallas_call(
        _outproj_kernel,
        grid=(b, s // tm),
        in_specs=[pl.BlockSpec((1, tm, half), tok),
                  pl.BlockSpec((1, tm, half), tok),
                  pl.BlockSpec((1, tm, d), tok),
                  pl.BlockSpec((2 * half, d), const),
                  pl.BlockSpec((1, 1, d), row),
                  pl.BlockSpec((1, d), const),
                  pl.BlockSpec((1, 1, d), row),
                  pl.BlockSpec((1, 1, d), row),
                  pl.BlockSpec((N_EXPERTS, d), const),
                  pl.BlockSpec((N_EXPERTS, 1), const)],
        out_specs=(pl.BlockSpec((1, tm, d), tok),
                   pl.BlockSpec((1, tm, ROW_WORDS), tok),
                   pl.BlockSpec((1, 8, tm), lambda i, j: (i, 0, j)),
                   pl.BlockSpec((CLASS_ROWS, LANES), const)),
        out_shape=(jax.ShapeDtypeStruct((b, s, d), F32),
                   jax.ShapeDtypeStruct((b, s, ROW_WORDS), jnp.int32),
                   jax.ShapeDtypeStruct((b, 8, s), F32),
                   jax.ShapeDtypeStruct((CLASS_ROWS, LANES), F32)),
        scratch_shapes=[pltpu.VMEM((CLASS_ROWS, LANES), F32), pltpu.VMEM((2 * half, d), BF16),
                        pltpu.VMEM((tm, tm), BF16)],
        compiler_params=_cparams(("arbitrary", "arbitrary")),
        name="outproj_router",
    )(a, bb, x, w, gt, g2.reshape(1, d), sc2, sh2, rw, rb)


MOE_ROWS = 512
SC_CHUNK = 128


def _sc_workers():
    info = plsc.get_sparse_core_info()
    return info.num_cores, info.num_cores * info.num_subcores


def _sc_scatter_rows(rows, idx3, n_out):
    nw, k, ch = idx3.shape
    width = rows.shape[1]
    nc, _ = _sc_workers()
    mesh = plsc.VectorSubcoreMesh(core_axis_name="c", subcore_axis_name="s")

    @functools.partial(
        pl.kernel, mesh=mesh,
        out_type=jax.ShapeDtypeStruct((n_out, width), rows.dtype),
        scratch_types=[pltpu.VMEM((k, ch), jnp.int32), pltpu.VMEM((ch, width), rows.dtype), pltpu.SemaphoreType.DMA],
        name="moe_dispatch")
    def kern(rows_hbm, idx_hbm, out_hbm, idx_v, rows_v, sem):
        wid = lax.axis_index("s") * nc + lax.axis_index("c")
        pltpu.sync_copy(idx_hbm.at[wid], idx_v)

        @pl.loop(0, k)
        def _(j):
            pltpu.sync_copy(rows_hbm.at[pl.ds((wid * k + j) * ch, ch)], rows_v)
            pltpu.async_copy(rows_v, out_hbm.at[idx_v.at[j]], sem).wait()

    return kern(rows, idx3)


def _sc_gather_rows(table, idx3):
    nw, k, ch = idx3.shape
    width = table.shape[1]
    nc, _ = _sc_workers()
    mesh = plsc.VectorSubcoreMesh(core_axis_name="c", subcore_axis_name="s")

    @functools.partial(
        pl.kernel, mesh=mesh,
        out_type=jax.ShapeDtypeStruct((nw * k * ch, width), table.dtype),
        scratch_types=[pltpu.VMEM((k, ch), jnp.int32), pltpu.VMEM((ch, width), table.dtype), pltpu.SemaphoreType.DMA],
        name="moe_combine")
    def kern(table_hbm, idx_hbm, out_hbm, idx_v, rows_v, sem):
        wid = lax.axis_index("s") * nc + lax.axis_index("c")
        pltpu.sync_copy(idx_hbm.at[wid], idx_v)

        @pl.loop(0, k)
        def _(j):
            pltpu.async_copy(table_hbm.at[idx_v.at[j]], rows_v, sem).wait()
            pltpu.sync_copy(rows_v, out_hbm.at[pl.ds((wid * k + j) * ch, ch)])

    return kern(table, idx3)


def _experts_kernel(e0_ref, e1_ref, nused_ref, x_ref, wg0_ref, wu0_ref, wd0_ref, wg1_ref, wu1_ref, wd1_ref, y_ref):
    @pl.when(pl.program_id(0) < nused_ref[0])
    def _():
        blk = x_ref[...]
        feat = ROW_FEAT
        h = _unpack_halves(blk[:, :feat])
        wts = lax.bitcast_convert_type(blk[:, feat:], F32)
        y = None
        for col, (wg_ref, wu_ref, wd_ref) in enumerate(((wg0_ref, wu0_ref, wd0_ref), (wg1_ref, wu1_ref, wd1_ref))):
            act = (_silu(_dot(h, wg_ref[0, 0].astype(BF16))) * _dot(h, wu_ref[0, 0].astype(BF16))
                   * wts[:, col:col + 1])
            part = _dot(act.astype(BF16), wd_ref[0, 0].astype(BF16))
            y = part if y is None else y + part
        y_ref[...] = _pack_halves(y)


def _experts(xs, blk_e0, blk_e1, nused, wg, wu, wd, layer):
    n_rows = xs.shape[0]
    _, ne, d, f = wg.shape
    nblk = n_rows // MOE_ROWS
    rows = lambda i, e0, e1, nu: (jnp.minimum(i, nu[0] - 1), 0)
    lo = lambda i, e0, e1, nu: (layer, e0[i], 0, 0)
    hi = lambda i, e0, e1, nu: (layer, e1[i], 0, 0)
    return pl.pallas_call(
        _experts_kernel,
        grid_spec=pltpu.PrefetchScalarGridSpec(
            num_scalar_prefetch=3,
            grid=(nblk,),
            in_specs=[pl.BlockSpec((MOE_ROWS, ROW_WORDS), rows),
                      pl.BlockSpec((1, 1, d, f), lo), pl.BlockSpec((1, 1, d, f), lo), pl.BlockSpec((1, 1, f, d), lo),
                      pl.BlockSpec((1, 1, d, f), hi), pl.BlockSpec((1, 1, d, f), hi), pl.BlockSpec((1, 1, f, d), hi)],
            out_specs=pl.BlockSpec((MOE_ROWS, d // 2), lambda i, e0, e1, nu: (i, 0))),
        out_shape=jax.ShapeDtypeStruct((n_rows, d // 2), jnp.int32),
        compiler_params=_cparams(("arbitrary",)),
        name="moe_experts",
    )(blk_e0, blk_e1, nused, xs, wg, wu, wd, wg, wu, wd)


def _residual_kernel(x_ref, m_ref, gt_ref, o_ref):
    o_ref[0] = x_ref[0] + gt_ref[0] * _unpack_halves(m_ref[0]).astype(F32)


def _residual(x1, moe_p, gt, tm):
    b, s, d = x1.shape
    tok = lambda i, j: (i, j, 0)
    return pl.pallas_call(
        _residual_kernel,
        grid=(b, s // tm),
        in_specs=[pl.BlockSpec((1, tm, d), tok), pl.BlockSpec((1, tm, d // 2), tok),
                  pl.BlockSpec((1, 1, d), lambda i, j: (i, 0, 0))],
        out_specs=pl.BlockSpec((1, tm, d), tok),
        out_shape=jax.ShapeDtypeStruct((b, s, d), F32),
        compiler_params=_cparams(("arbitrary", "arbitrary")),
        name="moe_residual",
    )(x1, moe_p, gt)


def _moe(rows, route, counts, wg, wu, wd, layer):
    b, s, _ = rows.shape
    d = wg.shape[2]
    t = b * s
    _, nw = _sc_workers()
    cnt = counts[:N_CLASSES, 0].astype(jnp.int32)
    padded = ((cnt + MOE_ROWS - 1) // MOE_ROWS) * MOE_ROWS
    ends = jnp.cumsum(padded)
    n_rows = t + N_CLASSES * MOE_ROWS
    nblk = n_rows // MOE_ROWS
    cls = route[:, 0, :].reshape(t).astype(jnp.int32)
    rank = route[:, 1, :].reshape(t).astype(jnp.int32)
    pos = rank + jnp.sum(jnp.where(cls[:, None] > jnp.arange(N_CLASSES, dtype=jnp.int32)[None, :], padded[None, :], 0),
                         axis=1)
    idx3 = pos.reshape(nw, t // (nw * SC_CHUNK), SC_CHUNK)
    nused = (ends[-1] // MOE_ROWS).reshape(1)
    blk_cls = jnp.sum((jnp.arange(nblk, dtype=jnp.int32)[:, None] * MOE_ROWS) >= ends[None, :], axis=1)
    blk_cls = jnp.minimum(blk_cls, blk_cls[jnp.maximum(nused[0] - 1, 0)])
    blk_e0 = jnp.asarray(CLASS_E0, jnp.int32)[blk_cls]
    blk_e1 = jnp.asarray(CLASS_E1, jnp.int32)[blk_cls]
    xs = _sc_scatter_rows(rows.reshape(t, ROW_WORDS), idx3, n_rows)
    ys = _experts(xs, blk_e0, blk_e1, nused, wg, wu, wd, layer)
    return _sc_gather_rows(ys, idx3).reshape(b, s, d // 2)


def kernel(x, c, ada_w, ada_b, norm1_g, norm2_g, ev_w_in, ev_conv_w, ev_dn_a_log, ev_dn_dt_bias, ev_dn_onorm_g, ev_fox_f_bias, ev_fox_qnorm_g, ev_fox_knorm_g, ev_w_out, od_w_in, od_conv_w, od_conv_b, od_lru_wr, od_lru_br, od_lru_wi, od_lru_bi, od_lru_lambda, od_sgu_norm_g, od_sgu_w, od_sgu_b, od_w_out, router_w, router_b, moe_w_gate, moe_w_up, moe_w_down):
    b, s, d = x.shape
    depth = ada_w.shape[0]
    tm = min(TOKEN_BLOCK, s)
    mod = _adaln(c, ada_w, ada_b).reshape(depth, b, 6, 1, d)
    rw = router_w.T.astype(BF16)
    rb = router_b.reshape(N_EXPERTS, 1)

    pending = None
    for layer in range(depth):
        sh1, sc1, gt1, sh2, sc2, gt2 = (mod[layer, :, k] for k in range(6))
        i = layer // 2
        if layer % 2 == 0:
            w = ev_w_in[i]
            nq = 3 * DN_HEADS * DN_DK
            nz = DN_HEADS * DN_DK
            nf = 3 * FOX_HEADS * FOX_DH
            o_a = nq + nz
            o_f = o_a + 2 * DN_HEADS
            o_ff = o_f + nf
            splits = ((0, nq), (nq, o_a), (o_a, o_a + nf), (o_a + nf, o_a + nf + LANES))
            w_cols = ((0, o_a, 0), (o_f, o_ff, o_a), (o_a, o_f, o_a + nf), (o_ff, o_ff + FOX_HEADS, o_a + nf + o_f - o_a))
            res = _inproj(x, norm1_g[layer], sc1, sh1, w, splits,
                          (BF16, BF16, BF16, F32), (None,) * 4, tm, pending, ev_conv_w[i], w_cols=w_cols)
            if pending is not None:
                x, res = res[0], res[1:]
            dn, z, fox, small = res
            small_t = jnp.swapaxes(small[:, :, :16], 1, 2)
            zeros4 = jnp.zeros((4,), F32)
            mul = jnp.concatenate([-jnp.exp(ev_dn_a_log[i]), zeros4, -jnp.ones((4,), F32), zeros4])
            bias = jnp.concatenate([ev_dn_dt_bias[i], zeros4, ev_fox_f_bias[i], zeros4])
            pr = jnp.stack([mul, bias], axis=1)
            pc = jnp.pad(jnp.stack([mul, bias], axis=0), ((0, 0), (0, LANES - 16)))
            col, row = _gates(small, small_t, pc, pr)
            row_dn = row[:, :DN_HEADS].reshape(b, DN_HEADS, s // DN_GROUP, DN_GROUP).transpose(0, 2, 1, 3)
            o_dn = _deltanet(dn, z, col, row_dn, ev_dn_onorm_g[i])
            o_fox = _fox(fox, col, ev_fox_qnorm_g[i], ev_fox_knorm_g[i], min(FOX_QUERY_BLOCK, s))
            mix_a, mix_b, w_out = o_dn, o_fox, ev_w_out[i]
        else:
            lw = od_lru_wr.shape[-1] * LRU_BLOCKS
            splits = ((0, lw), (lw, 2 * lw), (2 * lw, 2 * lw + od_sgu_w.shape[-1] * SGU_GROUPS),
                      (2 * lw + od_sgu_w.shape[-1] * SGU_GROUPS, od_w_in.shape[-1]))
            res = _inproj(x, norm1_g[layer], sc1, sh1, od_w_in[i], splits,
                          (BF16,) * 3, (None, "gelu"), min(ODD_TOKEN_BLOCK, s), pending,
                          sgu=(od_sgu_norm_g[i], od_sgu_w[i], od_sgu_b[i]))
            if pending is not None:
                x, res = res[0], res[1:]
            lx, lg, o_sgu = res
            o_lru = _lru(lx, lg, od_conv_w[i], od_conv_b[i], od_lru_wr[i], od_lru_br[i], od_lru_wi[i],
                         od_lru_bi[i], od_lru_lambda[i])
            mix_a, mix_b, w_out = o_lru, o_sgu, od_w_out[i]
        x1, rows, route, counts = _outproj(mix_a, mix_b, x, w_out, gt1, norm2_g[layer], sc2, sh2,
                                           rw, rb, min(ROUTER_BLOCK, s))
        x = x1
        pending = (_moe(rows, route, counts, moe_w_gate, moe_w_up, moe_w_down, layer), gt2)
    return _residual(x, *pending, min(ROUTER_BLOCK, s))
```

```python
import functools

import jax
import jax.numpy as jnp
from jax import lax
from jax.experimental import pallas as pl
from jax.experimental.pallas import tpu as pltpu
from jax.experimental.pallas import tpu_sc as plsc

F32 = jnp.float32
BF16 = jnp.bfloat16
EPS = 1e-6
NEG_INF = float("-inf")

DN_HEADS = 4
DN_DK = 128
DN_CHUNK = 64
CONV_WIDTH = 4
FOX_HEADS = 4
FOX_DH = 128
LRU_BLOCKS = 4
LRU_C = 8.0
SGU_GROUPS = 4
SGU_CHUNK = 128
N_EXPERTS = 16
EXPERTS_PER_GROUP = 4
LANES = 128
SUBLANES = 8

VMEM_LIMIT = 48 * 1024 * 1024
TOKEN_BLOCK = 512
ODD_TOKEN_BLOCK = 1024
ROUTER_BLOCK = 1024
FOX_QUERY_BLOCK = 512


def _cparams(sem):
    return pltpu.CompilerParams(dimension_semantics=sem, vmem_limit_bytes=VMEM_LIMIT)


def _dot(a, b):
    return jnp.dot(a, b, preferred_element_type=F32)


def _dot_nt(a, b):
    return lax.dot_general(a, b, (((1,), (1,)), ((), ())), preferred_element_type=F32)


def _dot_tn(a, b):
    return lax.dot_general(a, b, (((0,), (0,)), ((), ())), preferred_element_type=F32)


def _dot_ones(a, b):
    if a.dtype == BF16:
        return sum(_dot(a, t.astype(BF16)) for t in _split3(b))
    return sum(_dot(t.astype(BF16), b) for t in _split3(a))


def _sigmoid(x):
    return 0.5 + 0.5 * jnp.tanh(0.5 * x)


def _silu(x):
    hx = 0.5 * x
    return hx + hx * jnp.tanh(hx)


def _softplus(x):
    return jnp.maximum(x, 0.0) + jnp.log(1.0 + jnp.exp(-jnp.abs(x)))


def _gelu_tanh(x):
    c = 0.7978845608028654
    return 0.5 * x * (1.0 + jnp.tanh(c * (x + 0.044715 * (x * x * x))))


def _prenorm(x, g, scale, shift):
    ms = jnp.mean(x * x, axis=-1, keepdims=True)
    return (x * lax.rsqrt(ms + EPS) * g) * (1.0 + scale) + shift


def _shift_rows(x, d, fill=0.0):
    rows = lax.broadcasted_iota(jnp.int32, x.shape, 0)
    return jnp.where(rows >= d, pltpu.roll(x, d, axis=0), fill)


def _causal_conv(x, w_ref):
    k_w = w_ref.shape[0]

    def taps(v, shift):
        acc = v * w_ref[k_w - 1:k_w, :]
        for d in range(1, k_w):
            acc = acc + shift(v, d) * w_ref[k_w - 1 - d:k_w - d, :]
        return acc

    body = taps(x, lambda v, d: pltpu.roll(v, d, axis=0))
    return jnp.concatenate([taps(x[:SUBLANES], _shift_rows), body[SUBLANES:]], axis=0)


def _adaln_kernel(c_ref, w_ref, b_ref, o_ref):
    c = c_ref[...]
    ca = _silu(c).astype(BF16)
    o_ref[0] = _dot(ca, w_ref[0].astype(BF16)) + b_ref[0]


def _adaln(c, ada_w, ada_b):
    depth, d, n = ada_w.shape
    b = c.shape[0]
    tn = 1536
    return pl.pallas_call(
        _adaln_kernel,
        grid=(depth, n // tn),
        in_specs=[pl.BlockSpec((b, d), lambda l, j: (0, 0)),
                  pl.BlockSpec((1, d, tn), lambda l, j: (l, 0, j)),
                  pl.BlockSpec((1, 1, tn), lambda l, j: (l, 0, j))],
        out_specs=pl.BlockSpec((1, b, tn), lambda l, j: (l, 0, j)),
        out_shape=jax.ShapeDtypeStruct((depth, b, n), F32),
        compiler_params=_cparams(("arbitrary", "arbitrary")),
        name="adaln",
    )(c, ada_w, ada_b.reshape(depth, 1, n))


CONV_HALO = 8
CONV_SLAB = 512


def _inproj_kernel(*refs, col_splits, acts, pending, conv, sgu, cast_w):
    refs = list(refs)
    x_ref = refs.pop(0)
    if pending:
        m_ref, gtm_ref = refs.pop(0), refs.pop(0)
    g_ref, sc_ref, sh_ref, w_ref = (refs.pop(0) for _ in range(4))
    if cast_w:
        w_f32, w_ref = w_ref, refs.pop()

        @pl.when((pl.program_id(0) == 0) & (pl.program_id(1) == 0))
        def _():
            if sum(s1 - s0 for s0, s1, _ in cast_w) < w_ref.shape[1]:
                w_ref[...] = jnp.zeros_like(w_ref)
            for s0, s1, d0 in cast_w:
                w_ref[:, d0:d0 + s1 - s0] = w_f32[:, s0:s1].astype(BF16)
    if conv:
        cw_ref = refs.pop(0)
        halo_sc = refs.pop()
    if sgu:
        sg_ref, sw_ref, sbt_ref = (refs.pop(0) for _ in range(3))
    if pending:
        xo_ref = refs.pop(0)
        x = x_ref[0] + gtm_ref[0] * _unpack_halves(m_ref[0]).astype(F32)
        xo_ref[0] = x
    else:
        x = x_ref[0]
    out_refs = refs
    h = _prenorm(x, g_ref[...], sc_ref[0], sh_ref[0]).astype(BF16)
    tm = x.shape[0]
    if conv:
        @pl.when(pl.program_id(1) == 0)
        def _():
            halo_sc[0:CONV_HALO, :] = jnp.zeros((CONV_HALO, halo_sc.shape[1]), F32)

        c0, c1 = col_splits[0]
        for s0 in range(c0, c1, CONV_SLAB):
            halo_sc[CONV_HALO:, s0 - c0:s0 - c0 + CONV_SLAB] = _dot(h, w_ref[:, s0:s0 + CONV_SLAB])
    n_plain = len(col_splits) - (2 if sgu else 0)
    for k, (o_ref, (c0, c1), act) in enumerate(zip(out_refs[:n_plain], col_splits[:n_plain], acts)):
        if conv and k == 0:
            continue
        p = _dot(h, w_ref[:, c0:c1])
        if act == "gelu":
            p = _gelu_tanh(p)
        o_ref[0] = p.astype(o_ref.dtype)
    if sgu:
        (u0, u1), (v0, v1) = col_splits[n_plain:]
        o_ref = out_refs[n_plain]
        u = _gelu_tanh(_dot(h, w_ref[:, u0:u1]))
        v = _gelu_tanh(_dot(h, w_ref[:, v0:v1]))
        vn = (v * lax.rsqrt(jnp.mean(v * v, axis=-1, keepdims=True) + EPS) * sg_ref[...]).astype(BF16)
        c = SGU_CHUNK
        gw = (u1 - u0) // SGU_GROUPS
        ri = lax.broadcasted_iota(jnp.int32, (c, c), 0)
        ci = lax.broadcasted_iota(jnp.int32, (c, c), 1)
        for g in range(SGU_GROUPS):
            wg = jnp.where(ri >= ci, sw_ref[g], 0.0).astype(BF16)
            bcol = sbt_ref[:, g:g + 1]
            for n in range(tm // c):
                rows = slice(n * c, (n + 1) * c)
                cols = slice(g * gw, (g + 1) * gw)
                mixed = _dot(wg, vn[rows, cols]) + bcol
                o_ref[0, rows, cols] = (u[rows, cols] * mixed).astype(o_ref.dtype)
    if conv:
        o_ref = out_refs[0]
        for t0 in range(0, halo_sc.shape[1], DN_DK):
            cols = slice(t0, t0 + DN_DK)
            acc = halo_sc[CONV_HALO:, cols] * cw_ref[CONV_WIDTH - 1:CONV_WIDTH, cols]
            for dd in range(1, CONV_WIDTH):
                acc = acc + halo_sc[pl.ds(CONV_HALO - dd, tm), cols] * cw_ref[CONV_WIDTH - 1 - dd:CONV_WIDTH - dd, cols]
            halo_sc[0:CONV_HALO, cols] = halo_sc[tm:tm + CONV_HALO, cols]
            yj = _silu(acc)
            if t0 < 2 * DN_HEADS * DN_DK:
                yj = yj * lax.rsqrt(jnp.sum(yj * yj, axis=-1, keepdims=True) + EPS)
            if t0 < DN_HEADS * DN_DK:
                yj = yj * (DN_DK ** -0.5)
            o_ref[0, :, cols] = yj.astype(o_ref.dtype)


def _inproj(x, g, scale, shift, w, col_splits, out_dtypes, acts, tm, pending=None, conv_w=None, sgu=None,
            w_cols=None):
    b, s, d = x.shape
    n = w.shape[1] if w_cols is None else col_splits[-1][1]
    if w.dtype != BF16 and w_cols is None:
        w_cols = ((0, n, 0),)
    out_cols = list(col_splits) if sgu is None else list(col_splits[:-2]) + [col_splits[-2]]
    outs = tuple(jax.ShapeDtypeStruct((b, s, c1 - c0), dt) for (c0, c1), dt in zip(out_cols, out_dtypes))
    row = lambda i, j: (i, 0, 0)
    tok = lambda i, j: (i, j, 0)
    const = lambda i, j: (0, 0)
    in_specs = [pl.BlockSpec((1, tm, d), tok)]
    out_specs = tuple(pl.BlockSpec((1, tm, c1 - c0), tok) for (c0, c1) in out_cols)
    args = (x,)
    if pending is not None:
        in_specs += [pl.BlockSpec((1, tm, d // 2), tok), pl.BlockSpec((1, 1, d), row)]
        out_specs = (pl.BlockSpec((1, tm, d), tok),) + out_specs
        outs = (jax.ShapeDtypeStruct((b, s, d), F32),) + outs
        args += tuple(pending)
    in_specs += [pl.BlockSpec((1, d), const), pl.BlockSpec((1, 1, d), row), pl.BlockSpec((1, 1, d), row),
                 pl.BlockSpec(w.shape, const, pipeline_mode=pl.Buffered(1)) if w_cols else pl.BlockSpec((d, n), const)]
    args += (g.reshape(1, d), scale, shift, w)
    scratch = []
    if conv_w is not None:
        in_specs.append(pl.BlockSpec(conv_w.shape, const))
        args += (conv_w,)
        scratch.append(pltpu.VMEM((CONV_HALO + tm, conv_w.shape[1]), F32))
    if w_cols:
        scratch.append(pltpu.VMEM((d, n), BF16))
    if sgu is not None:
        g_norm, w_s, b_s = sgu
        in_specs += [pl.BlockSpec((1, g_norm.shape[0]), const), pl.BlockSpec(w_s.shape, lambda i, j: (0, 0, 0)),
                     pl.BlockSpec((b_s.shape[1], b_s.shape[0]), const)]
        args += (g_norm.reshape(1, -1), w_s, b_s.T)
    return pl.pallas_call(
        functools.partial(_inproj_kernel, col_splits=col_splits, acts=acts, pending=pending is not None,
                          conv=conv_w is not None, sgu=sgu is not None, cast_w=w_cols),
        grid=(b, s // tm),
        in_specs=in_specs,
        out_specs=out_specs,
        out_shape=outs,
        scratch_shapes=scratch,
        compiler_params=_cparams(("arbitrary", "arbitrary")),
        name="inproj",
    )(*args)


def _gates_kernel(sm_ref, smt_ref, pc_ref, pr_ref, col_ref, row_ref):
    s = sm_ref.shape[1]
    blk = LANES
    ri = lax.broadcasted_iota(jnp.int32, (blk, blk), 0)
    ci = lax.broadcasted_iota(jnp.int32, (blk, blk), 1)
    same_chunk = (ri // DN_CHUNK) == (ci // DN_CHUNK)
    tril = jnp.where(ri >= ci, 1.0, 0.0).astype(BF16)
    tril_loc = jnp.where(same_chunk & (ri >= ci), 1.0, 0.0).astype(BF16)
    triu_loc = jnp.where(same_chunk & (ri <= ci), 1.0, 0.0).astype(BF16)
    lane = lax.broadcasted_iota(jnp.int32, (blk, LANES), 1)
    carry = jnp.zeros((1, LANES), F32)
    for j in range(s // blk):
        rows = slice(j * blk, (j + 1) * blk)
        xc = sm_ref[0, rows, :] + pc_ref[1:2, :]
        dec = pc_ref[0:1, :] * _softplus(jnp.where(lane < 4, xc, -xc))
        cum_glb = _dot_ones(tril, dec) + carry
        col_ref[0, rows, :] = jnp.where(lane < 4, _dot_ones(tril_loc, dec), jnp.where(lane < 8, _sigmoid(xc), cum_glb))
        carry = cum_glb[blk - 1:blk, :]
        decr = pr_ref[:, 0:1] * _softplus(smt_ref[0, :, rows] + pr_ref[:, 1:2])
        row_ref[0, :, rows] = _dot_ones(decr, triu_loc)


def _gates(small, small_t, pc, pr):
    b, s, _ = small.shape
    return pl.pallas_call(
        _gates_kernel,
        grid=(b,),
        in_specs=[pl.BlockSpec((1, s, LANES), lambda i: (i, 0, 0)),
                  pl.BlockSpec((1, 16, s), lambda i: (i, 0, 0)),
                  pl.BlockSpec((2, LANES), lambda i: (0, 0)),
                  pl.BlockSpec((16, 2), lambda i: (0, 0))],
        out_specs=(pl.BlockSpec((1, s, LANES), lambda i: (i, 0, 0)),
                   pl.BlockSpec((1, 16, s), lambda i: (i, 0, 0))),
        out_shape=(jax.ShapeDtypeStruct((b, s, LANES), F32), jax.ShapeDtypeStruct((b, 16, s), F32)),
        compiler_params=_cparams(("arbitrary",)),
        name="gates",
    )(small, small_t, pc, pr)


DN_PACK = 4
DN_GROUP = DN_PACK * DN_CHUNK
DN_ITER_GROUPS = 4


def _blockdiag(p):
    c, wide = p.shape
    t = jnp.concatenate([p] * (wide // c), axis=0)
    rb = lax.broadcasted_iota(jnp.int32, (wide, wide), 0) // c
    cb = lax.broadcasted_iota(jnp.int32, (wide, wide), 1) // c
    return jnp.where(rb == cb, t, 0.0).astype(BF16)


def _diag_blocks(m, c):
    wide = m.shape[1]
    cb = lax.broadcasted_iota(jnp.int32, (c, wide), 1) // c
    out = m[:c]
    for j in range(1, wide // c):
        out = jnp.where(cb == j, m[j * c:(j + 1) * c], out)
    return out


def _rows_to_blocks(col, c, wide):
    cb = lax.broadcasted_iota(jnp.int32, (c, wide), 1) // c
    out = jnp.broadcast_to(col[:c], (c, wide))
    for j in range(1, wide // c):
        out = jnp.where(cb == j, col[j * c:(j + 1) * c], out)
    return out


def _packed_unit_lower_inverse(lows):
    c, wide = lows[0].shape
    ri = lax.broadcasted_iota(jnp.int32, (c, wide), 0)
    ci = lax.broadcasted_iota(jnp.int32, (c, wide), 1) % c
    eye = jnp.where(ri == ci, 1.0, 0.0)
    n = range(len(lows))
    ds = [jnp.where((ri // 16) == (ci // 16), lo, 0.0) for lo in lows]
    xs = [eye - d for d in ds]
    ps = ds
    bds = [_blockdiag(p) for p in ps]
    for _ in range(3):
        ps = [_dot(ps[i].astype(BF16), bds[i]) for i in n]
        bds = [_blockdiag(p) for p in ps]
        xs = [xs[i] + _dot(xs[i].astype(BF16), bds[i]) for i in n]
    for width in (16, 32):
        sel = ((ri // (2 * width)) == (ci // (2 * width))) & ((ri // width) != (ci // width))
        offs = [_blockdiag(jnp.where(sel, lo, 0.0)) for lo in lows]
        ts = [_dot(xs[i].astype(BF16), offs[i]) for i in n]
        bdx = [_blockdiag(x) for x in xs]
        xs = [xs[i] - _dot(ts[i].astype(BF16), bdx[i]) for i in n]
    return xs


def _deltanet_kernel(dn_ref, z_ref, col_ref, row_ref, og_ref, o_ref, st_sc):
    s = dn_ref.shape[1]
    c = DN_CHUNK
    gt = DN_GROUP
    dk = DN_DK
    hd = DN_HEADS * dk
    heads = range(DN_HEADS)
    st_sc[...] = jnp.zeros_like(st_sc)

    ri = lax.broadcasted_iota(jnp.int32, (c, gt), 0)
    ci = lax.broadcasted_iota(jnp.int32, (c, gt), 1) % c
    og = og_ref[...]

    per_iter = DN_ITER_GROUPS if (s // gt) % DN_ITER_GROUPS == 0 else 1
    insts = [(gg, h) for gg in range(per_iter) for h in heads]
    ins = range(len(insts))

    def group(g, carry):
        r0 = [pl.multiple_of((g * per_iter + gg) * gt, gt) for gg in range(per_iter)]
        colg = [col_ref[0, pl.ds(r0[gg], gt), :] for gg in range(per_iter)]
        rowg = [row_ref[0, g * per_iter + gg] for gg in range(per_iter)]
        hs = [slice(h * dk, (h + 1) * dk) for h in heads]
        q = [dn_ref[0, pl.ds(r0[gg], gt), h * dk:(h + 1) * dk].astype(F32) for gg, h in insts]
        k = [dn_ref[0, pl.ds(r0[gg], gt), hd + h * dk:hd + (h + 1) * dk].astype(F32) for gg, h in insts]
        v = [dn_ref[0, pl.ds(r0[gg], gt), 2 * hd + h * dk:2 * hd + (h + 1) * dk].astype(F32) for gg, h in insts]
        gc = [colg[gg][:, h:h + 1] for gg, h in insts]
        beta = [colg[gg][:, DN_HEADS + h:DN_HEADS + h + 1] for gg, h in insts]
        grow = [rowg[gg][h:h + 1, :] for gg, h in insts]
        glast = [jnp.concatenate([jnp.broadcast_to(gc[i][(j + 1) * c - 1:(j + 1) * c], (c, 1))
                                  for j in range(DN_PACK)], axis=0) for i in ins]
        eg = [jnp.exp(gc[i]) for i in ins]
        kbf = [k[i].astype(BF16) for i in ins]
        both = [_dot_nt(jnp.concatenate([(k[i] * beta[i]).astype(BF16), q[i].astype(BF16)], axis=0), kbf[i])
                for i in ins]
        decay = [jnp.exp(jnp.where(ri >= ci, _rows_to_blocks(gc[i], c, gt) - grow[i], NEG_INF)) for i in ins]
        kk = [_diag_blocks(both[i][:gt], c) * decay[i] for i in ins]
        qk = [_blockdiag(_diag_blocks(both[i][gt:], c) * decay[i]) for i in ins]
        t_inv = _packed_unit_lower_inverse([jnp.where(ri > ci, kk[i], 0.0) for i in ins])
        rhs = [jnp.concatenate([k[i] * (beta[i] * eg[i]), v[i] * beta[i]], axis=1).astype(BF16) for i in ins]
        wu = [_dot(_blockdiag(t_inv[i]), rhs[i]).astype(BF16) for i in ins]
        qwu = [_dot(qk[i], wu[i]) for i in ins]
        qp = [(q[i] * eg[i] - qwu[i][:, :dk]).astype(BF16) for i in ins]
        kdec = [(k[i] * jnp.exp(glast[i] - gc[i])).astype(BF16) for i in ins]
        mb = [[_dot_tn(kdec[i][j * c:(j + 1) * c], wu[i][j * c:(j + 1) * c]) for i in ins]
              for j in range(DN_PACK)]
        for gg in range(per_iter):
            mine = [gg * DN_HEADS + h for h in heads]
            outs = [[] for _ in heads]
            for j in range(DN_PACK):
                rows = slice(j * c, (j + 1) * c)
                state = [st_sc[h] for h in heads]
                lhs = [jnp.concatenate([qp[i][rows], mb[j][i][:, :dk].astype(BF16)], axis=0) for i in mine]
                r = [_dot(lhs[h], state[h].astype(BF16)) for h in heads]
                for h, i in zip(heads, mine):
                    gl = jnp.exp(glast[i][j * c:j * c + 1])
                    st_sc[h] = state[h] * gl - r[h][c:] + mb[j][i][:, dk:]
                    outs[h].append(r[h][:c] + qwu[i][rows, dk:])
            for h in heads:
                o = jnp.concatenate(outs[h], axis=0)
                on = o * lax.rsqrt(jnp.mean(o * o, axis=-1, keepdims=True) + EPS) * og
                zz = z_ref[0, pl.ds(r0[gg], gt), hs[h]].astype(F32)
                o_ref[0, pl.ds(r0[gg], gt), hs[h]] = (on * _silu(zz)).astype(o_ref.dtype)
        return carry

    lax.fori_loop(0, s // (gt * per_iter), group, 0)


def _deltanet(dn, z, col, row, onorm_g):
    b, s, w3 = dn.shape
    hd = DN_HEADS * DN_DK
    return pl.pallas_call(
        _deltanet_kernel,
        grid=(b,),
        in_specs=[pl.BlockSpec((1, s, w3), lambda i: (i, 0, 0)),
                  pl.BlockSpec((1, s, hd), lambda i: (i, 0, 0)),
                  pl.BlockSpec((1, s, LANES), lambda i: (i, 0, 0)),
                  pl.BlockSpec((1, s // DN_GROUP, DN_HEADS, DN_GROUP), lambda i: (i, 0, 0, 0)),
                  pl.BlockSpec((1, DN_DK), lambda i: (0, 0))],
        out_specs=pl.BlockSpec((1, s, hd), lambda i: (i, 0, 0)),
        out_shape=jax.ShapeDtypeStruct((b, s, hd), BF16),
        scratch_shapes=[pltpu.VMEM((DN_HEADS, DN_DK, DN_DK), F32)],
        compiler_params=_cparams(("arbitrary",)),
        name="deltanet",
    )(dn, z, col, row, onorm_g.reshape(1, DN_DK))


def _split3(x):
    hi = x.astype(BF16).astype(F32)
    r = x - hi
    mid = r.astype(BF16).astype(F32)
    return hi, mid, r - mid


def _fox_kernel(q_ref, k_ref, v_ref, colq_ref, colk_ref, qg_ref, kg_ref, o_ref, ka_sc, va_sc, m_sc, acc_sc, *, tq):
    qi = pl.program_id(1)
    s = k_ref.shape[1]
    dh = FOX_DH

    def bias_lanes(col):
        lane = lax.broadcasted_iota(jnp.int32, col.shape, 1)
        hi, mid, lo = _split3(col)
        return jnp.where((lane >= 8) & (lane < 8 + FOX_HEADS), hi,
                         jnp.where((lane >= 24) & (lane < 24 + FOX_HEADS), pltpu.roll(mid, 16, axis=1),
                                   jnp.where((lane >= 40) & (lane < 40 + FOX_HEADS), pltpu.roll(lo, 32, axis=1), 0.0)))

    @pl.when(qi == 0)
    def _():
        lane = lax.broadcasted_iota(jnp.int32, (s, LANES), 1)
        ones_col = jnp.where(lane == 0, 1.0, 0.0).astype(BF16)
        ones_k = (lane >= 56) & (lane < 96) & ((lane % 16) >= 8) & ((lane % 16) < 8 + FOX_HEADS)
        ext_k = (jnp.where(ones_k, 1.0, 0.0) - bias_lanes(colk_ref[0])).astype(BF16)
        for h in range(FOX_HEADS):
            cols = slice(h * dh, (h + 1) * dh)
            kf = k_ref[0, :, cols].astype(F32)
            kn = kf * lax.rsqrt(jnp.mean(kf * kf, axis=-1, keepdims=True) + EPS) * kg_ref[...]
            ka_sc[h, :, :dh] = kn.astype(BF16)
            ka_sc[h, :, dh:] = ext_k
            va_sc[h, :, :dh] = v_ref[0, :, cols]
            va_sc[h, :, dh:] = ones_col

    lane = lax.broadcasted_iota(jnp.int32, (tq, LANES), 1)
    cq = pltpu.roll(bias_lanes(colq_ref[0]), 48, axis=1)
    qa = []
    for h in range(FOX_HEADS):
        cols = slice(h * dh, (h + 1) * dh)
        qf = q_ref[0, :, cols].astype(F32)
        qn = qf * lax.rsqrt(jnp.mean(qf * qf, axis=-1, keepdims=True) + EPS) * qg_ref[...] * (dh ** -0.5)
        mine = ((lane % 16) == 8 + h) & (lane < 96)
        ext = jnp.where(mine, jnp.where(lane < 48, 1.0, cq), 0.0)
        qa.append(jnp.concatenate([qn.astype(BF16), ext.astype(BF16)], axis=1))
    m_sc[...] = jnp.full(m_sc.shape, NEG_INF, F32)
    acc_sc[...] = jnp.zeros_like(acc_sc)
    causal = lax.broadcasted_iota(jnp.int32, (tq, tq), 0) >= lax.broadcasted_iota(jnp.int32, (tq, tq), 1)

    def step(k0, masked):
        heads = range(FOX_HEADS)
        logits = [_dot_nt(qa[h], ka_sc[h, pl.ds(k0, tq), :]) for h in heads]
        if masked:
            logits = [jnp.where(causal, lg, NEG_INF) for lg in logits]
        ps, alphas = [], []
        for h in heads:
            m_old = m_sc[h]
            m_new = jnp.maximum(m_old, jnp.max(logits[h], axis=1, keepdims=True))
            m_sc[h] = m_new
            alphas.append(jnp.exp(m_old - m_new))
            ps.append(jnp.exp(logits[h] - jnp.concatenate([m_new] * (tq // LANES), axis=1)).astype(BF16))
        for h in heads:
            pv = _dot(ps[h], va_sc[h, pl.ds(k0, tq), :])
            acc_sc[h] = acc_sc[h] * jnp.concatenate([alphas[h], alphas[h]], axis=1) + pv

    def body(j, carry):
        step(pl.multiple_of(j * tq, tq), False)
        return carry

    lax.fori_loop(0, qi, body, 0)
    step(pl.multiple_of(qi * tq, tq), True)
    for h in range(FOX_HEADS):
        acc = acc_sc[h]
        o_ref[0, :, h * dh:(h + 1) * dh] = (acc[:, :dh] / acc[:, dh:dh + 1]).astype(o_ref.dtype)


def _fox(fox, col, qg, kg, tq):
    b, s, _ = fox.shape
    hd = FOX_HEADS * FOX_DH
    return pl.pallas_call(
        functools.partial(_fox_kernel, tq=tq),
        grid=(b, s // tq),
        in_specs=[pl.BlockSpec((1, tq, hd), lambda i, j: (i, j, 0)),
                  pl.BlockSpec((1, s, hd), lambda i, j: (i, 0, 1)),
                  pl.BlockSpec((1, s, hd), lambda i, j: (i, 0, 2)),
                  pl.BlockSpec((1, tq, LANES), lambda i, j: (i, j, 0)),
                  pl.BlockSpec((1, s, LANES), lambda i, j: (i, 0, 0)),
                  pl.BlockSpec((1, FOX_DH), lambda i, j: (0, 0)),
                  pl.BlockSpec((1, FOX_DH), lambda i, j: (0, 0))],
        out_specs=pl.BlockSpec((1, tq, hd), lambda i, j: (i, j, 0)),
        out_shape=jax.ShapeDtypeStruct((b, s, hd), BF16),
        scratch_shapes=[pltpu.VMEM((FOX_HEADS, s, 2 * FOX_DH), BF16), pltpu.VMEM((FOX_HEADS, s, 2 * FOX_DH), BF16),
                        pltpu.VMEM((FOX_HEADS, tq, LANES), F32), pltpu.VMEM((FOX_HEADS, tq, 2 * FOX_DH), F32)],
        compiler_params=_cparams(("arbitrary", "arbitrary")),
        name="fox",
    )(fox, fox, fox, col, col, qg.reshape(1, FOX_DH), kg.reshape(1, FOX_DH))


def _lru_kernel(x_ref, gate_ref, cw_ref, cb_ref, wr_ref, br_ref, wi_ref, bi_ref, lam_ref, o_ref, a_sc, b_sc):
    s = x_ref.shape[1]
    x = _causal_conv(x_ref[0].astype(F32), cw_ref) + cb_ref[...]
    xb = x.astype(BF16)
    r = _sigmoid(_dot(xb, wr_ref[0].astype(BF16)) + br_ref[...])
    i = _sigmoid(_dot(xb, wi_ref[0].astype(BF16)) + bi_ref[...])
    log_a = (-LRU_C) * r * _softplus(-lam_ref[...])
    a = jnp.exp(log_a)
    one_m = 1.0 - a * a
    bb = jnp.where(one_m > 0.0, one_m * lax.rsqrt(one_m), 0.0) * (i * x)
    nt = s // SUBLANES
    a3 = a.reshape(nt, SUBLANES, a.shape[1])
    b3 = bb.reshape(nt, SUBLANES, a.shape[1])
    sub = lax.broadcasted_iota(jnp.int32, a3.shape, 1)
    d = 1
    while d < SUBLANES:
        keep = sub >= d
        b3 = a3 * jnp.where(keep, pltpu.roll(b3, d, axis=1), 0.0) + b3
        a3 = a3 * jnp.where(keep, pltpu.roll(a3, d, axis=1), 1.0)
        d *= 2
    a_sc[...] = a3.reshape(s, a.shape[1])
    b_sc[...] = b3.reshape(s, a.shape[1])
    at = a_sc[pl.ds(SUBLANES - 1, nt, stride=SUBLANES), :]
    bt = b_sc[pl.ds(SUBLANES - 1, nt, stride=SUBLANES), :]
    d = 1
    while d < nt:
        bt = at * _shift_rows(bt, d, 0.0) + bt
        at = at * _shift_rows(at, d, 1.0)
        d *= 2
    h_prev = _shift_rows(bt, 1, 0.0)
    bb = (a3 * h_prev[:, None, :] + b3).reshape(s, a.shape[1])
    o_ref[0] = (bb * gate_ref[0].astype(F32)).astype(o_ref.dtype)


def _lru(lx, lg, conv_w, conv_b, wr, br, wi, bi, lam):
    b, s, wd = lx.shape
    blk = wd // LRU_BLOCKS
    vec = lambda i, j: (0, j)
    return pl.pallas_call(
        _lru_kernel,
        grid=(b, LRU_BLOCKS),
        in_specs=[pl.BlockSpec((1, s, blk), lambda i, j: (i, 0, j)),
                  pl.BlockSpec((1, s, blk), lambda i, j: (i, 0, j)),
                  pl.BlockSpec((CONV_WIDTH, blk), vec),
                  pl.BlockSpec((1, blk), vec),
                  pl.BlockSpec((1, blk, blk), lambda i, j: (j, 0, 0)),
                  pl.BlockSpec((1, blk), vec),
                  pl.BlockSpec((1, blk, blk), lambda i, j: (j, 0, 0)),
                  pl.BlockSpec((1, blk), vec),
                  pl.BlockSpec((1, blk), vec)],
        out_specs=pl.BlockSpec((1, s, blk), lambda i, j: (i, 0, j)),
        out_shape=jax.ShapeDtypeStruct((b, s, wd), BF16),
        scratch_shapes=[pltpu.VMEM((s, blk), F32), pltpu.VMEM((s, blk), F32)],
        compiler_params=_cparams(("arbitrary", "arbitrary")),
        name="rglru",
    )(lx, lg, conv_w, conv_b.reshape(1, wd), wr, br.reshape(1, wd), wi, bi.reshape(1, wd), lam.reshape(1, wd))


def _routing(logits, rb):
    ne, tm = logits.shape
    row = lax.broadcasted_iota(jnp.int32, (ne, tm), 0)
    row_f = row.astype(F32)
    ex = jnp.exp(logits - jnp.max(logits, axis=0, keepdims=True))
    probs = ex / jnp.sum(ex, axis=0, keepdims=True)
    sel = probs + rb

    def top2(vals, idx):
        m1 = jnp.max(vals, axis=0, keepdims=True)
        i1 = jnp.min(jnp.where(vals == m1, idx, float(ne)), axis=0, keepdims=True)
        rest = jnp.where(idx == i1, NEG_INF, vals)
        m2 = jnp.max(rest, axis=0, keepdims=True)
        return m1, i1, m2, rest

    best = None
    g_idx = None
    grp = row // EXPERTS_PER_GROUP
    for g in range(ne // EXPERTS_PER_GROUP):
        m1, _, m2, _ = top2(jnp.where(grp == g, sel, NEG_INF), row_f)
        score = m1 + m2
        if g == 0:
            best, g_idx = score, jnp.zeros((1, tm), jnp.int32)
        else:
            upd = score > best
            best = jnp.where(upd, score, best)
            g_idx = jnp.where(upd, g, g_idx)
    _, i1, m2, rest = top2(jnp.where(grp == g_idx, sel, NEG_INF), row_f)
    i2 = jnp.min(jnp.where(rest == m2, row_f, float(ne)), axis=0, keepdims=True)
    p1 = jnp.sum(jnp.where(row_f == i1, probs, 0.0), axis=0, keepdims=True)
    p2 = jnp.sum(jnp.where(row_f == i2, probs, 0.0), axis=0, keepdims=True)
    den = p1 + p2
    first_lo = i1 < i2
    a = jnp.where(first_lo, i1, i2) - EXPERTS_PER_GROUP * g_idx.astype(F32)
    bhi = jnp.where(first_lo, i2, i1) - EXPERTS_PER_GROUP * g_idx.astype(F32)
    pair = a * (7.0 - a) * 0.5 + (bhi - a - 1.0)
    pair = jnp.where(a == 1.0, 7.0 - pair, pair)
    cls = PAIRS_PER_GROUP * g_idx.astype(F32) + pair
    w_lo = jnp.where(first_lo, p1, p2) / den
    w_hi = jnp.where(first_lo, p2, p1) / den
    swap = pair == 5.0
    return cls, jnp.where(swap, w_hi, w_lo), jnp.where(swap, w_lo, w_hi)


PAIRS_PER_GROUP = EXPERTS_PER_GROUP * (EXPERTS_PER_GROUP - 1) // 2
N_CLASSES = (N_EXPERTS // EXPERTS_PER_GROUP) * PAIRS_PER_GROUP
PAIR_WALK = [(0, 1), (0, 2), (0, 3), (1, 3), (1, 2), (3, 2)]
assert EXPERTS_PER_GROUP == 4 and len(PAIR_WALK) == PAIRS_PER_GROUP
CLASS_E0 = [EXPERTS_PER_GROUP * (c // PAIRS_PER_GROUP) + PAIR_WALK[c % PAIRS_PER_GROUP][0] for c in range(N_CLASSES)]
CLASS_E1 = [EXPERTS_PER_GROUP * (c // PAIRS_PER_GROUP) + PAIR_WALK[c % PAIRS_PER_GROUP][1] for c in range(N_CLASSES)]
CLASS_ROWS = 32
ROW_FEAT = 512
ROW_WORDS = 640


def _pack_halves(x):
    kk = x.shape[1] // 2
    lo = lax.bitcast_convert_type(x[:, :kk].astype(BF16).astype(F32), jnp.uint32)
    hi = lax.bitcast_convert_type(x[:, kk:].astype(BF16).astype(F32), jnp.uint32)
    return lax.bitcast_convert_type((lo >> 16) | hi, jnp.int32)


def _unpack_halves(p):
    u = lax.bitcast_convert_type(p, jnp.uint32)
    lo = lax.bitcast_convert_type(u << 16, F32)
    hi = lax.bitcast_convert_type(u & jnp.uint32(0xFFFF0000), F32)
    return jnp.concatenate([lo.astype(BF16), hi.astype(BF16)], axis=1)


X_RING = 3


def _outproj_kernel(a_ref, b_ref, x_hbm, wf_ref, gt_ref, g2_ref, sc_ref, sh_ref, rwt_ref, rb_ref,
                    x1_ref, rows_ref, route_ref, cnt_ref, carry_sc, w_ref, tri_sc, x_buf, x_sem):
    first = (pl.program_id(0) == 0) & (pl.program_id(1) == 0)
    nj = pl.num_programs(1)
    step = pl.program_id(0) * nj + pl.program_id(1)
    n_steps = pl.num_programs(0) * nj

    def x_copy(m):
        rows = pl.ds(pl.multiple_of((m % nj) * x_buf.shape[1], x_buf.shape[1]), x_buf.shape[1])
        return pltpu.make_async_copy(x_hbm.at[m // nj, rows, :], x_buf.at[m % X_RING], x_sem.at[m % X_RING])

    @pl.when(first)
    def _():
        for m in range(X_RING - 1):
            x_copy(m).start()

    @pl.when(step + X_RING - 1 < n_steps)
    def _():
        x_copy(step + X_RING - 1).start()

    @pl.when(first)
    def _():
        carry_sc[...] = jnp.zeros_like(carry_sc)
        w_ref[...] = wf_ref[...].astype(BF16)
        earlier = (lax.broadcasted_iota(jnp.int32, tri_sc.shape, 0) < lax.broadcasted_iota(jnp.int32, tri_sc.shape, 1))
        tri_sc[...] = jnp.where(earlier, 1.0, 0.0).astype(BF16)

    half = a_ref.shape[2]
    tm = a_ref.shape[1]
    y = _dot(a_ref[0], w_ref[:half, :]) + _dot(b_ref[0], w_ref[half:, :])
    x_copy(step).wait()
    x1 = x_buf[step % X_RING] + gt_ref[0] * y
    x1_ref[0] = x1
    h2 = _prenorm(x1, g2_ref[...], sc_ref[0], sh_ref[0])
    rows_ref[0, :, :ROW_FEAT] = _pack_halves(h2)
    cls, w_lo, w_hi = _routing(_dot_nt(rwt_ref[...], h2.astype(BF16)), rb_ref[...])
    r128 = lax.broadcasted_iota(jnp.int32, (LANES, tm), 0)
    wts = jnp.where(r128 == 0, w_lo, jnp.where(r128 == 1, w_hi, 0.0))
    rows_ref[0, :, ROW_FEAT:] = lax.bitcast_convert_type(wts.T[:, :ROW_WORDS - ROW_FEAT], jnp.int32)
    crow = lax.broadcasted_iota(jnp.int32, (carry_sc.shape[0], tm), 0).astype(F32)
    onehot = jnp.where(crow == cls, 1.0, 0.0)
    prefix = _dot(onehot.astype(BF16), tri_sc[...]) + carry_sc[:, 0:1]
    rank = jnp.sum(onehot * prefix, axis=0, keepdims=True)
    r8 = lax.broadcasted_iota(jnp.int32, (8, tm), 0)
    route_ref[0] = jnp.where(r8 == 0, cls, jnp.where(r8 == 1, rank, 0.0))
    carry_sc[...] = carry_sc[...] + jnp.sum(onehot, axis=1, keepdims=True)
    cnt_ref[...] = carry_sc[...]


def _outproj(a, bb, x, w, gt, g2, sc2, sh2, rw, rb, tm):
    b, s, d = x.shape
    half = a.shape[2]
    row = lambda i, j: (i, 0, 0)
    tok = lambda i, j: (i, j, 0)
    const = lambda i, j: (0, 0)
    return pl.pallas_call(
        _outproj_kernel,
        grid=(b, s // tm),
        in_specs=[pl.BlockSpec((1, tm, half), tok),
                  pl.BlockSpec((1, tm, half), tok),
                  pl.BlockSpec(memory_space=pl.ANY),
                  pl.BlockSpec((2 * half, d), const),
                  pl.BlockSpec((1, 1, d), row),
                  pl.BlockSpec((1, d), const),
                  pl.BlockSpec((1, 1, d), row),
                  pl.BlockSpec((1, 1, d), row),
                  pl.BlockSpec((N_EXPERTS, d), const),
                  pl.BlockSpec((N_EXPERTS, 1), const)],
        out_specs=(pl.BlockSpec((1, tm, d), tok),
                   pl.BlockSpec((1, tm, ROW_WORDS), tok),
                   pl.BlockSpec((1, 8, tm), lambda i, j: (i, 0, j)),
                   pl.BlockSpec((CLASS_ROWS, LANES), const)),
        out_shape=(jax.ShapeDtypeStruct((b, s, d), F32),
                   jax.ShapeDtypeStruct((b, s, ROW_WORDS), jnp.int32),
                   jax.ShapeDtypeStruct((b, 8, s), F32),
                   jax.ShapeDtypeStruct((CLASS_ROWS, LANES), F32)),
        scratch_shapes=[pltpu.VMEM((CLASS_ROWS, LANES), F32), pltpu.VMEM((2 * half, d), BF16),
                        pltpu.VMEM((tm, tm), BF16), pltpu.VMEM((X_RING, tm, d), F32),
                        pltpu.SemaphoreType.DMA((X_RING,))],
        compiler_params=_cparams(("arbitrary", "arbitrary")),
        name="outproj_router",
    )(a, bb, x, w, gt, g2.reshape(1, d), sc2, sh2, rw, rb)


MOE_ROWS = 512
SC_CHUNK = 128


def _sc_workers():
    info = plsc.get_sparse_core_info()
    return info.num_cores, info.num_cores * info.num_subcores


def _sc_scatter_rows(rows, idx3, n_out):
    nw, k, ch = idx3.shape
    width = rows.shape[1]
    nc, _ = _sc_workers()
    mesh = plsc.VectorSubcoreMesh(core_axis_name="c", subcore_axis_name="s")

    @functools.partial(
        pl.kernel, mesh=mesh,
        out_type=jax.ShapeDtypeStruct((n_out, width), rows.dtype),
        scratch_types=[pltpu.VMEM((k, ch), jnp.int32), pltpu.VMEM((ch, width), rows.dtype), pltpu.SemaphoreType.DMA],
        name="moe_dispatch")
    def kern(rows_hbm, idx_hbm, out_hbm, idx_v, rows_v, sem):
        wid = lax.axis_index("s") * nc + lax.axis_index("c")
        pltpu.sync_copy(idx_hbm.at[wid], idx_v)

        @pl.loop(0, k)
        def _(j):
            pltpu.sync_copy(rows_hbm.at[pl.ds((wid * k + j) * ch, ch)], rows_v)
            pltpu.async_copy(rows_v, out_hbm.at[idx_v.at[j]], sem).wait()

    return kern(rows, idx3)


def _sc_gather_rows(table, idx3):
    nw, k, ch = idx3.shape
    width = table.shape[1]
    nc, _ = _sc_workers()
    mesh = plsc.VectorSubcoreMesh(core_axis_name="c", subcore_axis_name="s")

    @functools.partial(
        pl.kernel, mesh=mesh,
        out_type=jax.ShapeDtypeStruct((nw * k * ch, width), table.dtype),
        scratch_types=[pltpu.VMEM((k, ch), jnp.int32), pltpu.VMEM((ch, width), table.dtype), pltpu.SemaphoreType.DMA],
        name="moe_combine")
    def kern(table_hbm, idx_hbm, out_hbm, idx_v, rows_v, sem):
        wid = lax.axis_index("s") * nc + lax.axis_index("c")
        pltpu.sync_copy(idx_hbm.at[wid], idx_v)

        @pl.loop(0, k)
        def _(j):
            pltpu.async_copy(table_hbm.at[idx_v.at[j]], rows_v, sem).wait()
            pltpu.sync_copy(rows_v, out_hbm.at[pl.ds((wid * k + j) * ch, ch)])

    return kern(table, idx3)


def _experts_kernel(e0_ref, e1_ref, nused_ref, x_ref, wg0_ref, wu0_ref, wd0_ref, wg1_ref, wu1_ref, wd1_ref, y_ref):
    @pl.when(pl.program_id(0) < nused_ref[0])
    def _():
        blk = x_ref[...]
        feat = ROW_FEAT
        h = _unpack_halves(blk[:, :feat])
        wts = lax.bitcast_convert_type(blk[:, feat:], F32)
        y = None
        for col, (wg_ref, wu_ref, wd_ref) in enumerate(((wg0_ref, wu0_ref, wd0_ref), (wg1_ref, wu1_ref, wd1_ref))):
            act = (_silu(_dot(h, wg_ref[0, 0].astype(BF16))) * _dot(h, wu_ref[0, 0].astype(BF16))
                   * wts[:, col:col + 1])
            part = _dot(act.astype(BF16), wd_ref[0, 0].astype(BF16))
            y = part if y is None else y + part
        y_ref[...] = _pack_halves(y)


def _experts(xs, blk_e0, blk_e1, nused, wg, wu, wd, layer):
    n_rows = xs.shape[0]
    _, ne, d, f = wg.shape
    nblk = n_rows // MOE_ROWS
    rows = lambda i, e0, e1, nu: (jnp.minimum(i, nu[0] - 1), 0)
    lo = lambda i, e0, e1, nu: (layer, e0[i], 0, 0)
    hi = lambda i, e0, e1, nu: (layer, e1[i], 0, 0)
    return pl.pallas_call(
        _experts_kernel,
        grid_spec=pltpu.PrefetchScalarGridSpec(
            num_scalar_prefetch=3,
            grid=(nblk,),
            in_specs=[pl.BlockSpec((MOE_ROWS, ROW_WORDS), rows),
                      pl.BlockSpec((1, 1, d, f), lo), pl.BlockSpec((1, 1, d, f), lo), pl.BlockSpec((1, 1, f, d), lo),
                      pl.BlockSpec((1, 1, d, f), hi), pl.BlockSpec((1, 1, d, f), hi), pl.BlockSpec((1, 1, f, d), hi)],
            out_specs=pl.BlockSpec((MOE_ROWS, d // 2), lambda i, e0, e1, nu: (i, 0))),
        out_shape=jax.ShapeDtypeStruct((n_rows, d // 2), jnp.int32),
        compiler_params=_cparams(("arbitrary",)),
        name="moe_experts",
    )(blk_e0, blk_e1, nused, xs, wg, wu, wd, wg, wu, wd)


def _residual_kernel(x_ref, m_ref, gt_ref, o_ref):
    o_ref[0] = x_ref[0] + gt_ref[0] * _unpack_halves(m_ref[0]).astype(F32)


def _residual(x1, moe_p, gt, tm):
    b, s, d = x1.shape
    tok = lambda i, j: (i, j, 0)
    return pl.pallas_call(
        _residual_kernel,
        grid=(b, s // tm),
        in_specs=[pl.BlockSpec((1, tm, d), tok), pl.BlockSpec((1, tm, d // 2), tok),
                  pl.BlockSpec((1, 1, d), lambda i, j: (i, 0, 0))],
        out_specs=pl.BlockSpec((1, tm, d), tok),
        out_shape=jax.ShapeDtypeStruct((b, s, d), F32),
        compiler_params=_cparams(("arbitrary", "arbitrary")),
        name="moe_residual",
    )(x1, moe_p, gt)


def _moe(rows, route, counts, wg, wu, wd, layer):
    b, s, _ = rows.shape
    d = wg.shape[2]
    t = b * s
    _, nw = _sc_workers()
    cnt = counts[:N_CLASSES, 0].astype(jnp.int32)
    padded = ((cnt + MOE_ROWS - 1) // MOE_ROWS) * MOE_ROWS
    ends = jnp.cumsum(padded)
    n_rows = t + N_CLASSES * MOE_ROWS
    nblk = n_rows // MOE_ROWS
    cls = route[:, 0, :].reshape(t).astype(jnp.int32)
    rank = route[:, 1, :].reshape(t).astype(jnp.int32)
    pos = rank + jnp.sum(jnp.where(cls[:, None] > jnp.arange(N_CLASSES, dtype=jnp.int32)[None, :], padded[None, :], 0),
                         axis=1)
    idx3 = pos.reshape(nw, t // (nw * SC_CHUNK), SC_CHUNK)
    nused = (ends[-1] // MOE_ROWS).reshape(1)
    blk_cls = jnp.sum((jnp.arange(nblk, dtype=jnp.int32)[:, None] * MOE_ROWS) >= ends[None, :], axis=1)
    blk_cls = jnp.minimum(blk_cls, blk_cls[jnp.maximum(nused[0] - 1, 0)])
    blk_e0 = jnp.asarray(CLASS_E0, jnp.int32)[blk_cls]
    blk_e1 = jnp.asarray(CLASS_E1, jnp.int32)[blk_cls]
    xs = _sc_scatter_rows(rows.reshape(t, ROW_WORDS), idx3, n_rows)
    ys = _experts(xs, blk_e0, blk_e1, nused, wg, wu, wd, layer)
    return _sc_gather_rows(ys, idx3).reshape(b, s, d // 2)


def kernel(x, c, ada_w, ada_b, norm1_g, norm2_g, ev_w_in, ev_conv_w, ev_dn_a_log, ev_dn_dt_bias, ev_dn_onorm_g, ev_fox_f_bias, ev_fox_qnorm_g, ev_fox_knorm_g, ev_w_out, od_w_in, od_conv_w, od_conv_b, od_lru_wr, od_lru_br, od_lru_wi, od_lru_bi, od_lru_lambda, od_sgu_norm_g, od_sgu_w, od_sgu_b, od_w_out, router_w, router_b, moe_w_gate, moe_w_up, moe_w_down):
    b, s, d = x.shape
    depth = ada_w.shape[0]
    tm = min(TOKEN_BLOCK, s)
    mod = _adaln(c, ada_w, ada_b).reshape(depth, b, 6, 1, d)
    rw = router_w.T.astype(BF16)
    rb = router_b.reshape(N_EXPERTS, 1)

    pending = None
    for layer in range(depth):
        sh1, sc1, gt1, sh2, sc2, gt2 = (mod[layer, :, k] for k in range(6))
        i = layer // 2
        if layer % 2 == 0:
            w = ev_w_in[i]
            nq = 3 * DN_HEADS * DN_DK
            nz = DN_HEADS * DN_DK
            nf = 3 * FOX_HEADS * FOX_DH
            o_a = nq + nz
            o_f = o_a + 2 * DN_HEADS
            o_ff = o_f + nf
            splits = ((0, nq), (nq, o_a), (o_a, o_a + nf), (o_a + nf, o_a + nf + LANES))
            w_cols = ((0, o_a, 0), (o_f, o_ff, o_a), (o_a, o_f, o_a + nf), (o_ff, o_ff + FOX_HEADS, o_a + nf + o_f - o_a))
            res = _inproj(x, norm1_g[layer], sc1, sh1, w, splits,
                          (BF16, BF16, BF16, F32), (None,) * 4, tm, pending, ev_conv_w[i], w_cols=w_cols)
            if pending is not None:
                x, res = res[0], res[1:]
            dn, z, fox, small = res
            small_t = jnp.swapaxes(small[:, :, :16], 1, 2)
            zeros4 = jnp.zeros((4,), F32)
            mul = jnp.concatenate([-jnp.exp(ev_dn_a_log[i]), zeros4, -jnp.ones((4,), F32), zeros4])
            bias = jnp.concatenate([ev_dn_dt_bias[i], zeros4, ev_fox_f_bias[i], zeros4])
            pr = jnp.stack([mul, bias], axis=1)
            pc = jnp.pad(jnp.stack([mul, bias], axis=0), ((0, 0), (0, LANES - 16)))
            col, row = _gates(small, small_t, pc, pr)
            row_dn = row[:, :DN_HEADS].reshape(b, DN_HEADS, s // DN_GROUP, DN_GROUP).transpose(0, 2, 1, 3)
            o_dn = _deltanet(dn, z, col, row_dn, ev_dn_onorm_g[i])
            o_fox = _fox(fox, col, ev_fox_qnorm_g[i], ev_fox_knorm_g[i], min(FOX_QUERY_BLOCK, s))
            mix_a, mix_b, w_out = o_dn, o_fox, ev_w_out[i]
        else:
            lw = od_lru_wr.shape[-1] * LRU_BLOCKS
            splits = ((0, lw), (lw, 2 * lw), (2 * lw, 2 * lw + od_sgu_w.shape[-1] * SGU_GROUPS),
                      (2 * lw + od_sgu_w.shape[-1] * SGU_GROUPS, od_w_in.shape[-1]))
            res = _inproj(x, norm1_g[layer], sc1, sh1, od_w_in[i], splits,
                          (BF16,) * 3, (None, "gelu"), min(ODD_TOKEN_BLOCK, s), pending,
                          sgu=(od_sgu_norm_g[i], od_sgu_w[i], od_sgu_b[i]))
            if pending is not None:
                x, res = res[0], res[1:]
            lx, lg, o_sgu = res
            o_lru = _lru(lx, lg, od_conv_w[i], od_conv_b[i], od_lru_wr[i], od_lru_br[i], od_lru_wi[i],
                         od_lru_bi[i], od_lru_lambda[i])
            mix_a, mix_b, w_out = o_lru, o_sgu, od_w_out[i]
        x1, rows, route, counts = _outproj(mix_a, mix_b, x, w_out, gt1, norm2_g[layer], sc2, sh2,
                                           rw, rb, min(ROUTER_BLOCK, s))
        x = x1
        pending = (_moe(rows, route, counts, moe_w_gate, moe_w_up, moe_w_down, layer), gt2)
    return _residual(x, *pending, min(ROUTER_BLOCK, s))
```
